```python
import math
import jax, jax.numpy as jnp
from jax import lax
import numpy as np

D_MODEL = 2048
BATCH = 8
SEQ = 4096
DEPTH = 2

SSD_HEADS = 32
SSD_HEAD_DIM = 64
SSD_WIDTH = SSD_HEADS * SSD_HEAD_DIM
SSD_STATE = 128
SSD_GROUPS = 4
SSD_CONV = 4
SSD_CHUNK = 128
SSD_CONV_DIM = SSD_WIDTH + 2 * SSD_GROUPS * SSD_STATE
DT_MIN = 0.001
DT_MAX = 0.1

SB_HEADS = 16
SB_HEAD_DIM = 128
SB_WIDTH = SB_HEADS * SB_HEAD_DIM
SB_BLOCK = 128

IN_DIM = SSD_WIDTH + SSD_CONV_DIM + SSD_HEADS + 3 * SB_WIDTH
MIX_WIDTH = SSD_WIDTH + SB_WIDTH

POOL_WINDOWS = (2, 4, 8, 16)
POOL_GROUP = D_MODEL // len(POOL_WINDOWS)

D_FF = 4 * D_MODEL
EPS = 1e-6

kernel_name = 'hybrid_ssd_stickbreak_pool_trunk'


def rms_norm(x, g):
    xf = x.astype(jnp.float32)
    y = xf * lax.rsqrt(jnp.mean(xf * xf, axis=-1, keepdims=True) + EPS)
    return (y * g.astype(jnp.float32)).astype(x.dtype)


def causal_depthwise_conv(x, w, b):
    k, c = w.shape
    y = lax.conv_general_dilated(
        x, w[:, None, :].astype(x.dtype), window_strides=(1,),
        padding=[(k - 1, 0)], dimension_numbers=('NWC', 'WIO', 'NWC'),
        feature_group_count=c)
    return y + b.astype(x.dtype)


def ssd_scan(x, dt, a, bmat, cmat):
    f32 = jnp.float32
    bsz, t, h, p = x.shape
    g, n = bmat.shape[2], bmat.shape[3]
    r, l = h // g, SSD_CHUNK
    nc = t // l
    xdt = (x.astype(f32) * dt[..., None]).reshape(bsz, nc, l, g, r, p)
    da = (dt * a).reshape(bsz, nc, l, g, r).transpose(0, 3, 4, 1, 2)
    bc = bmat.astype(f32).reshape(bsz, nc, l, g, n)
    cc = cmat.astype(f32).reshape(bsz, nc, l, g, n)
    a_cs = jnp.cumsum(da, axis=-1)
    causal = jnp.tril(jnp.ones((l, l), dtype=bool))
    seg = a_cs[..., :, None] - a_cs[..., None, :]
    decay = jnp.exp(jnp.where(causal, seg, -jnp.inf))
    cb = jnp.einsum('bclgn,bcsgn->bgcls', cc, bc)
    y_diag = jnp.einsum('bgrcls,bcsgrp->bclgrp', decay * cb[:, :, None], xdt)
    decay_to_end = jnp.exp(a_cs[..., -1:] - a_cs)
    chunk_states = jnp.einsum('bcsgn,bgrcs,bcsgrp->bcgrpn', bc, decay_to_end, xdt)
    chunk_decay = jnp.exp(a_cs[..., -1])

    def step(state, inp):
        s_c, d_c = inp
        return state * d_c[..., None, None] + s_c, state

    init = jnp.zeros((bsz, g, r, p, n), f32)
    _, prev = lax.scan(step, init, (jnp.moveaxis(chunk_states, 1, 0),
                                    jnp.moveaxis(chunk_decay, -1, 0)))
    y_off = jnp.einsum('bclgn,cbgrpn->bclgrp', cc, prev) * \
        jnp.exp(a_cs).transpose(0, 3, 4, 1, 2)[..., None]
    return (y_diag + y_off).reshape(bsz, t, h, p)


def stick_breaking_attention(q, k, v):
    bsz, h, t, d = q.shape
    scale = d ** -0.5
    outs = []
    for i in range(t // SB_BLOCK):
        q0 = i * SB_BLOCK
        end = q0 + SB_BLOCK
        z = jnp.einsum('bhqd,bhkd->bhqk', q[:, :, q0:end], k[:, :, :end]).astype(jnp.float32) * scale
        qpos = q0 + jnp.arange(SB_BLOCK)[:, None]
        kpos = jnp.arange(end)[None, :]
        mask = kpos < qpos
        log_beta = jax.nn.log_sigmoid(z)
        log_keep = jnp.where(mask, jax.nn.log_sigmoid(-z), 0.0)
        between = lax.cumsum(log_keep, axis=3, reverse=True) - log_keep
        w = jnp.where(mask, jnp.exp(log_beta + between), 0.0)
        outs.append(jnp.einsum('bhqk,bhkd->bhqd', w.astype(v.dtype), v[:, :, :end]))
    return jnp.concatenate(outs, axis=2)


def hybrid_mixer(h, w_in, conv_w, conv_b, dt_bias, a_log, d_skip, out_norm, q_norm, k_norm, w_out):
    bsz, t, _ = h.shape
    proj = h @ w_in
    cuts = [SSD_WIDTH, SSD_WIDTH + SSD_CONV_DIM, SSD_WIDTH + SSD_CONV_DIM + SSD_HEADS,
            SSD_WIDTH + SSD_CONV_DIM + SSD_HEADS + SB_WIDTH,
            SSD_WIDTH + SSD_CONV_DIM + SSD_HEADS + 2 * SB_WIDTH]
    z, xbc, dt_raw, q, k, v = jnp.split(proj, cuts, axis=-1)

    xbc = jax.nn.silu(causal_depthwise_conv(xbc, conv_w, conv_b))
    xs, bm, cm = jnp.split(xbc, [SSD_WIDTH, SSD_WIDTH + SSD_GROUPS * SSD_STATE], axis=-1)
    xs = xs.reshape(bsz, t, SSD_HEADS, SSD_HEAD_DIM)
    bm = bm.reshape(bsz, t, SSD_GROUPS, SSD_STATE)
    cm = cm.reshape(bsz, t, SSD_GROUPS, SSD_STATE)
    dt = jax.nn.softplus(dt_raw.astype(jnp.float32) + dt_bias.astype(jnp.float32))
    a = -jnp.exp(a_log.astype(jnp.float32))
    y = ssd_scan(xs, dt, a, bm, cm) + d_skip.astype(jnp.float32)[:, None] * xs.astype(jnp.float32)
    gated = y.reshape(bsz, t, SSD_WIDTH) * jax.nn.silu(z.astype(jnp.float32))
    gsz = SSD_WIDTH // SSD_GROUPS
    y_ssd = rms_norm(gated.reshape(bsz, t, SSD_GROUPS, gsz),
                     out_norm.reshape(SSD_GROUPS, gsz)).reshape(bsz, t, SSD_WIDTH)

    def heads(u):
        return u.reshape(bsz, t, SB_HEADS, SB_HEAD_DIM).transpose(0, 2, 1, 3)
    qh = rms_norm(heads(q), q_norm)
    kh = rms_norm(heads(k), k_norm)
    y_sb = stick_breaking_attention(qh, kh, heads(v))
    y_sb = y_sb.transpose(0, 2, 1, 3).reshape(bsz, t, SB_WIDTH)

    merged = jnp.concatenate([y_ssd.astype(h.dtype), y_sb.astype(h.dtype)], axis=-1)
    return merged @ w_out


def multiscale_pool(h, w, b, scale):
    bsz, t, _ = h.shape
    hf = h.astype(jnp.float32)
    cs = jnp.cumsum(hf, axis=1)
    count = jnp.arange(1, t + 1, dtype=jnp.float32)[:, None]
    diffs = []
    for gi, win in enumerate(POOL_WINDOWS):
        sl = slice(gi * POOL_GROUP, (gi + 1) * POOL_GROUP)
        c = cs[..., sl]
        lagged = jnp.pad(c, ((0, 0), (win, 0), (0, 0)))[:, :t]
        mean = (c - lagged) / jnp.minimum(count, float(win))
        diffs.append(mean - hf[..., sl])
    d = jnp.stack(diffs, axis=2).astype(h.dtype)
    y = jnp.einsum('btgc,gcd->btgd', d, w).reshape(bsz, t, D_MODEL) + b
    return y * scale


def sq_relu_mlp(h, w_up, w_down):
    u = jax.nn.relu(h @ w_up)
    return (u * u) @ w_down


def _fwd_setup_inputs(seed: int = 0) -> dict:
    key = jax.random.key(seed)
    ks = jax.random.split(key, 20)
    ne = (DEPTH + 1) // 2
    no = DEPTH // 2
    f32 = jnp.float32

    def normal(k, shape, s):
        return jax.random.normal(k, shape, f32) * s

    def gain(k, shape):
        return 1.0 + 0.02 * jax.random.normal(k, shape, f32)

    dt0 = jnp.exp(jax.random.uniform(ks[5], (ne, SSD_HEADS), f32, math.log(DT_MIN), math.log(DT_MAX)))
    return {
        'x': normal(ks[0], (BATCH, SEQ, D_MODEL), 1.0),
        'hyb_norm': gain(ks[1], (ne, D_MODEL)),
        'hyb_w_in': normal(ks[2], (ne, D_MODEL, IN_DIM), D_MODEL ** -0.5),
        'ssd_conv_w': normal(ks[3], (ne, SSD_CONV, SSD_CONV_DIM), SSD_CONV ** -0.5),
        'ssd_conv_b': normal(ks[4], (ne, SSD_CONV_DIM), 0.02),
        'ssd_dt_bias': dt0 + jnp.log(-jnp.expm1(-dt0)),
        'ssd_a_log': jnp.log(jax.random.uniform(ks[6], (ne, SSD_HEADS), f32, 1.0, 16.0)),
        'ssd_d': 1.0 + 0.1 * jax.random.normal(ks[7], (ne, SSD_HEADS), f32),
        'ssd_out_norm': gain(ks[8], (ne, SSD_WIDTH)),
        'sb_q_norm': gain(ks[9], (ne, SB_HEAD_DIM)),
        'sb_k_norm': gain(ks[10], (ne, SB_HEAD_DIM)),
        'hyb_w_out': normal(ks[11], (ne, MIX_WIDTH, D_MODEL), MIX_WIDTH ** -0.5),
        'pool_norm': gain(ks[12], (no, D_MODEL)),
        'pool_w': normal(ks[13], (no, len(POOL_WINDOWS), POOL_GROUP, POOL_GROUP), POOL_GROUP ** -0.5),
        'pool_b': normal(ks[14], (no, D_MODEL), 0.02),
        'pool_scale': gain(ks[15], (no, D_MODEL)),
        'mlp_norm': gain(ks[16], (DEPTH, D_MODEL)),
        'mlp_w_up': normal(ks[17], (DEPTH, D_MODEL, D_FF), D_MODEL ** -0.5),
        'mlp_w_down': normal(ks[18], (DEPTH, D_FF, D_MODEL), D_FF ** -0.5),
    }


def _fwd_reference(x, hyb_norm, hyb_w_in, ssd_conv_w, ssd_conv_b, ssd_dt_bias, ssd_a_log, ssd_d,
              ssd_out_norm, sb_q_norm, sb_k_norm, hyb_w_out, pool_norm, pool_w, pool_b,
              pool_scale, mlp_norm, mlp_w_up, mlp_w_down):
    for layer in range(DEPTH):
        i = layer // 2
        if layer % 2 == 0:
            mix = hybrid_mixer(rms_norm(x, hyb_norm[i]), hyb_w_in[i], ssd_conv_w[i], ssd_conv_b[i],
                               ssd_dt_bias[i], ssd_a_log[i], ssd_d[i], ssd_out_norm[i],
                               sb_q_norm[i], sb_k_norm[i], hyb_w_out[i])
        else:
            mix = multiscale_pool(rms_norm(x, pool_norm[i]), pool_w[i], pool_b[i], pool_scale[i])
        x = x + mix.astype(x.dtype)
        x = x + sq_relu_mlp(rms_norm(x, mlp_norm[layer]), mlp_w_up[layer], mlp_w_down[layer]).astype(x.dtype)
    return x


import jax as _jax
import jax.numpy as _jnp

TWIN_FORMAT = 'train_step'
FWD_PARAMS = ['x', 'hyb_norm', 'hyb_w_in', 'ssd_conv_w', 'ssd_conv_b', 'ssd_dt_bias', 'ssd_a_log', 'ssd_d', 'ssd_out_norm', 'sb_q_norm', 'sb_k_norm', 'hyb_w_out', 'pool_norm', 'pool_w', 'pool_b', 'pool_scale', 'mlp_norm', 'mlp_w_up', 'mlp_w_down']
TWIN_WEIGHTS = ['hyb_norm', 'hyb_w_in', 'ssd_conv_w', 'ssd_conv_b', 'ssd_dt_bias', 'ssd_a_log', 'ssd_d', 'ssd_out_norm', 'sb_q_norm', 'sb_k_norm', 'hyb_w_out', 'pool_norm', 'pool_w', 'pool_b', 'pool_scale', 'mlp_norm', 'mlp_w_up', 'mlp_w_down']
TWIN_DIFF_INPUT = 'x'
TWIN_INPUTS = ['x', 'hyb_norm', 'hyb_w_in', 'ssd_conv_w', 'ssd_conv_b', 'ssd_dt_bias', 'ssd_a_log', 'ssd_d', 'ssd_out_norm', 'sb_q_norm', 'sb_k_norm', 'hyb_w_out', 'pool_norm', 'pool_w', 'pool_b', 'pool_scale', 'mlp_norm', 'mlp_w_up', 'mlp_w_down', 'loss_target', 'm_hyb_norm', 'm_hyb_w_in', 'm_ssd_conv_w', 'm_ssd_conv_b', 'm_ssd_dt_bias', 'm_ssd_a_log', 'm_ssd_d', 'm_ssd_out_norm', 'm_sb_q_norm', 'm_sb_k_norm', 'm_hyb_w_out', 'm_pool_norm', 'm_pool_w', 'm_pool_b', 'm_pool_scale', 'm_mlp_norm', 'm_mlp_w_up', 'm_mlp_w_down', 'v_hyb_norm', 'v_hyb_w_in', 'v_ssd_conv_w', 'v_ssd_conv_b', 'v_ssd_dt_bias', 'v_ssd_a_log', 'v_ssd_d', 'v_ssd_out_norm', 'v_sb_q_norm', 'v_sb_k_norm', 'v_hyb_w_out', 'v_pool_norm', 'v_pool_w', 'v_pool_b', 'v_pool_scale', 'v_mlp_norm', 'v_mlp_w_up', 'v_mlp_w_down']
TWIN_OUTPUTS = ['loss', 'grad_x', 'grad_hyb_norm', 'grad_hyb_w_in', 'grad_ssd_conv_w', 'grad_ssd_conv_b', 'grad_ssd_dt_bias', 'grad_ssd_a_log', 'grad_ssd_d', 'grad_ssd_out_norm', 'grad_sb_q_norm', 'grad_sb_k_norm', 'grad_hyb_w_out', 'grad_pool_norm', 'grad_pool_w', 'grad_pool_b', 'grad_pool_scale', 'grad_mlp_norm', 'grad_mlp_w_up', 'grad_mlp_w_down', 'delta_hyb_norm', 'delta_hyb_w_in', 'delta_ssd_conv_w', 'delta_ssd_conv_b', 'delta_ssd_dt_bias', 'delta_ssd_a_log', 'delta_ssd_d', 'delta_ssd_out_norm', 'delta_sb_q_norm', 'delta_sb_k_norm', 'delta_hyb_w_out', 'delta_pool_norm', 'delta_pool_w', 'delta_pool_b', 'delta_pool_scale', 'delta_mlp_norm', 'delta_mlp_w_up', 'delta_mlp_w_down', 'new_m_hyb_norm', 'new_m_hyb_w_in', 'new_m_ssd_conv_w', 'new_m_ssd_conv_b', 'new_m_ssd_dt_bias', 'new_m_ssd_a_log', 'new_m_ssd_d', 'new_m_ssd_out_norm', 'new_m_sb_q_norm', 'new_m_sb_k_norm', 'new_m_hyb_w_out', 'new_m_pool_norm', 'new_m_pool_w', 'new_m_pool_b', 'new_m_pool_scale', 'new_m_mlp_norm', 'new_m_mlp_w_up', 'new_m_mlp_w_down', 'new_v_hyb_norm', 'new_v_hyb_w_in', 'new_v_ssd_conv_w', 'new_v_ssd_conv_b', 'new_v_ssd_dt_bias', 'new_v_ssd_a_log', 'new_v_ssd_d', 'new_v_ssd_out_norm', 'new_v_sb_q_norm', 'new_v_sb_k_norm', 'new_v_hyb_w_out', 'new_v_pool_norm', 'new_v_pool_w', 'new_v_pool_b', 'new_v_pool_scale', 'new_v_mlp_norm', 'new_v_mlp_w_up', 'new_v_mlp_w_down']
TWIN_LEAF_KINDS = {'loss': 'loss', 'grad_x': 'grad_x', 'grad_hyb_norm': 'grad_w', 'grad_hyb_w_in': 'grad_w', 'grad_ssd_conv_w': 'grad_w', 'grad_ssd_conv_b': 'grad_w', 'grad_ssd_dt_bias': 'grad_w', 'grad_ssd_a_log': 'grad_w', 'grad_ssd_d': 'grad_w', 'grad_ssd_out_norm': 'grad_w', 'grad_sb_q_norm': 'grad_w', 'grad_sb_k_norm': 'grad_w', 'grad_hyb_w_out': 'grad_w', 'grad_pool_norm': 'grad_w', 'grad_pool_w': 'grad_w', 'grad_pool_b': 'grad_w', 'grad_pool_scale': 'grad_w', 'grad_mlp_norm': 'grad_w', 'grad_mlp_w_up': 'grad_w', 'grad_mlp_w_down': 'grad_w', 'delta_hyb_norm': 'delta_w', 'delta_hyb_w_in': 'delta_w', 'delta_ssd_conv_w': 'delta_w', 'delta_ssd_conv_b': 'delta_w', 'delta_ssd_dt_bias': 'delta_w', 'delta_ssd_a_log': 'delta_w', 'delta_ssd_d': 'delta_w', 'delta_ssd_out_norm': 'delta_w', 'delta_sb_q_norm': 'delta_w', 'delta_sb_k_norm': 'delta_w', 'delta_hyb_w_out': 'delta_w', 'delta_pool_norm': 'delta_w', 'delta_pool_w': 'delta_w', 'delta_pool_b': 'delta_w', 'delta_pool_scale': 'delta_w', 'delta_mlp_norm': 'delta_w', 'delta_mlp_w_up': 'delta_w', 'delta_mlp_w_down': 'delta_w', 'new_m_hyb_norm': 'new_m', 'new_m_hyb_w_in': 'new_m', 'new_m_ssd_conv_w': 'new_m', 'new_m_ssd_conv_b': 'new_m', 'new_m_ssd_dt_bias': 'new_m', 'new_m_ssd_a_log': 'new_m', 'new_m_ssd_d': 'new_m', 'new_m_ssd_out_norm': 'new_m', 'new_m_sb_q_norm': 'new_m', 'new_m_sb_k_norm': 'new_m', 'new_m_hyb_w_out': 'new_m', 'new_m_pool_norm': 'new_m', 'new_m_pool_w': 'new_m', 'new_m_pool_b': 'new_m', 'new_m_pool_scale': 'new_m', 'new_m_mlp_norm': 'new_m', 'new_m_mlp_w_up': 'new_m', 'new_m_mlp_w_down': 'new_m', 'new_v_hyb_norm': 'new_v', 'new_v_hyb_w_in': 'new_v', 'new_v_ssd_conv_w': 'new_v', 'new_v_ssd_conv_b': 'new_v', 'new_v_ssd_dt_bias': 'new_v', 'new_v_ssd_a_log': 'new_v', 'new_v_ssd_d': 'new_v', 'new_v_ssd_out_norm': 'new_v', 'new_v_sb_q_norm': 'new_v', 'new_v_sb_k_norm': 'new_v', 'new_v_hyb_w_out': 'new_v', 'new_v_pool_norm': 'new_v', 'new_v_pool_w': 'new_v', 'new_v_pool_b': 'new_v', 'new_v_pool_scale': 'new_v', 'new_v_mlp_norm': 'new_v', 'new_v_mlp_w_up': 'new_v', 'new_v_mlp_w_down': 'new_v'}


def _forward(args):
    return _fwd_reference(*[args[k] for k in FWD_PARAMS])


def _output_shape():
    def fwd():
        inp = _fwd_setup_inputs(0)
        return _fwd_reference(*[inp[k] for k in FWD_PARAMS])
    out = _jax.eval_shape(fwd)
    return out.shape, out.dtype

N_MICROBATCH = 1
ADAM_LR = 0.001
ADAM_B1 = 0.9
ADAM_B2 = 0.999
ADAM_EPS = 1e-08
ADAM_WD = 0.01
ADAM_STEP = 10
PER_EXAMPLE_BATCH_AXIS = {'x': 0, 'loss_target': 0}
SHARED_INPUTS = []
_WEIGHT_DTYPES = {'hyb_norm': _jnp.float32, 'hyb_w_in': _jnp.float32, 'ssd_conv_w': _jnp.float32, 'ssd_conv_b': _jnp.float32, 'ssd_dt_bias': _jnp.float32, 'ssd_a_log': _jnp.float32, 'ssd_d': _jnp.float32, 'ssd_out_norm': _jnp.float32, 'sb_q_norm': _jnp.float32, 'sb_k_norm': _jnp.float32, 'hyb_w_out': _jnp.float32, 'pool_norm': _jnp.float32, 'pool_w': _jnp.float32, 'pool_b': _jnp.float32, 'pool_scale': _jnp.float32, 'mlp_norm': _jnp.float32, 'mlp_w_up': _jnp.float32, 'mlp_w_down': _jnp.float32}
MOMENT_SCALE = {'hyb_norm': 2.806498e+00, 'hyb_w_in': 2.905702e-01, 'ssd_conv_w': 1.411325e+00, 'ssd_conv_b': 4.799554e+00, 'ssd_dt_bias': 1.682187e+00, 'ssd_a_log': 6.100573e+00, 'ssd_d': 7.978319e+00, 'ssd_out_norm': 1.116224e+01, 'sb_q_norm': 8.212319e+00, 'sb_k_norm': 8.223153e+00, 'hyb_w_out': 2.558844e+00, 'pool_norm': 1.134300e+01, 'pool_w': 1.081434e+00, 'pool_b': 1.493185e+01, 'pool_scale': 1.136874e+01, 'mlp_norm': 4.746387e+01, 'mlp_w_up': 1.743580e+00, 'mlp_w_down': 6.813704e+00}


def _to_microbatches(a, axis):
    t = _jnp.moveaxis(a, axis, 0)
    t = t.reshape((N_MICROBATCH, t.shape[0] // N_MICROBATCH) + t.shape[1:])
    return _jnp.moveaxis(t, 1, axis + 1)


def setup_inputs(seed: int = 0) -> dict:
    inp = _fwd_setup_inputs(seed)
    key = _jax.random.fold_in(_jax.random.key(seed), 7919)
    shape, _ = _output_shape()
    out = dict(inp)
    out["loss_target"] = _jax.random.normal(_jax.random.fold_in(key, 0), shape, _jnp.float32)
    for i, name in enumerate(TWIN_WEIGHTS):
        w = inp[name].astype(_jnp.float32)
        if MOMENT_SCALE is None:
            s = _jnp.sqrt(_jnp.mean(_jnp.square(w)) + 1e-30)
        else:
            s = MOMENT_SCALE[name]
        km, kv = _jax.random.split(_jax.random.fold_in(key, i + 1))
        out[name] = w
        out["m_" + name] = s * _jax.random.normal(km, w.shape, _jnp.float32)
        out["v_" + name] = (s * s) * _jax.random.uniform(kv, w.shape, _jnp.float32, 0.5, 1.5)
    if N_MICROBATCH > 1:
        for name, axis in PER_EXAMPLE_BATCH_AXIS.items():
            out[name] = _to_microbatches(out[name], axis)
    return {'x': out['x'], 'hyb_norm': out['hyb_norm'], 'hyb_w_in': out['hyb_w_in'], 'ssd_conv_w': out['ssd_conv_w'], 'ssd_conv_b': out['ssd_conv_b'], 'ssd_dt_bias': out['ssd_dt_bias'], 'ssd_a_log': out['ssd_a_log'], 'ssd_d': out['ssd_d'], 'ssd_out_norm': out['ssd_out_norm'], 'sb_q_norm': out['sb_q_norm'], 'sb_k_norm': out['sb_k_norm'], 'hyb_w_out': out['hyb_w_out'], 'pool_norm': out['pool_norm'], 'pool_w': out['pool_w'], 'pool_b': out['pool_b'], 'pool_scale': out['pool_scale'], 'mlp_norm': out['mlp_norm'], 'mlp_w_up': out['mlp_w_up'], 'mlp_w_down': out['mlp_w_down'], 'loss_target': out['loss_target'], 'm_hyb_norm': out['m_hyb_norm'], 'm_hyb_w_in': out['m_hyb_w_in'], 'm_ssd_conv_w': out['m_ssd_conv_w'], 'm_ssd_conv_b': out['m_ssd_conv_b'], 'm_ssd_dt_bias': out['m_ssd_dt_bias'], 'm_ssd_a_log': out['m_ssd_a_log'], 'm_ssd_d': out['m_ssd_d'], 'm_ssd_out_norm': out['m_ssd_out_norm'], 'm_sb_q_norm': out['m_sb_q_norm'], 'm_sb_k_norm': out['m_sb_k_norm'], 'm_hyb_w_out': out['m_hyb_w_out'], 'm_pool_norm': out['m_pool_norm'], 'm_pool_w': out['m_pool_w'], 'm_pool_b': out['m_pool_b'], 'm_pool_scale': out['m_pool_scale'], 'm_mlp_norm': out['m_mlp_norm'], 'm_mlp_w_up': out['m_mlp_w_up'], 'm_mlp_w_down': out['m_mlp_w_down'], 'v_hyb_norm': out['v_hyb_norm'], 'v_hyb_w_in': out['v_hyb_w_in'], 'v_ssd_conv_w': out['v_ssd_conv_w'], 'v_ssd_conv_b': out['v_ssd_conv_b'], 'v_ssd_dt_bias': out['v_ssd_dt_bias'], 'v_ssd_a_log': out['v_ssd_a_log'], 'v_ssd_d': out['v_ssd_d'], 'v_ssd_out_norm': out['v_ssd_out_norm'], 'v_sb_q_norm': out['v_sb_q_norm'], 'v_sb_k_norm': out['v_sb_k_norm'], 'v_hyb_w_out': out['v_hyb_w_out'], 'v_pool_norm': out['v_pool_norm'], 'v_pool_w': out['v_pool_w'], 'v_pool_b': out['v_pool_b'], 'v_pool_scale': out['v_pool_scale'], 'v_mlp_norm': out['v_mlp_norm'], 'v_mlp_w_up': out['v_mlp_w_up'], 'v_mlp_w_down': out['v_mlp_w_down']}


def _loss(weights, diff, rest, loss_target):
    with _jax.named_scope("forward"):
        args = {**rest, TWIN_DIFF_INPUT: diff, **{k: w.astype(_WEIGHT_DTYPES[k]) for k, w in weights.items()}}
        y = _forward(args)
    with _jax.named_scope("loss_head"):
        err = _jnp.square(y.astype(_jnp.float32) - loss_target)
        return 0.5 * _jnp.sum(_jnp.mean(err, axis=-1)) if err.ndim else 0.5 * err


def _adamw(w, g, m, v):
    m = ADAM_B1 * m + (1.0 - ADAM_B1) * g
    v = ADAM_B2 * v + (1.0 - ADAM_B2) * _jnp.square(g)
    m_hat = m / (1.0 - ADAM_B1 ** ADAM_STEP)
    v_hat = v / (1.0 - ADAM_B2 ** ADAM_STEP)
    delta = -ADAM_LR * (m_hat / (_jnp.sqrt(v_hat) + ADAM_EPS) + ADAM_WD * w)
    return delta, m, v


def reference(x, hyb_norm, hyb_w_in, ssd_conv_w, ssd_conv_b, ssd_dt_bias, ssd_a_log, ssd_d, ssd_out_norm, sb_q_norm, sb_k_norm, hyb_w_out, pool_norm, pool_w, pool_b, pool_scale, mlp_norm, mlp_w_up, mlp_w_down, loss_target, m_hyb_norm, m_hyb_w_in, m_ssd_conv_w, m_ssd_conv_b, m_ssd_dt_bias, m_ssd_a_log, m_ssd_d, m_ssd_out_norm, m_sb_q_norm, m_sb_k_norm, m_hyb_w_out, m_pool_norm, m_pool_w, m_pool_b, m_pool_scale, m_mlp_norm, m_mlp_w_up, m_mlp_w_down, v_hyb_norm, v_hyb_w_in, v_ssd_conv_w, v_ssd_conv_b, v_ssd_dt_bias, v_ssd_a_log, v_ssd_d, v_ssd_out_norm, v_sb_q_norm, v_sb_k_norm, v_hyb_w_out, v_pool_norm, v_pool_w, v_pool_b, v_pool_scale, v_mlp_norm, v_mlp_w_up, v_mlp_w_down):
    given = dict(x=x, hyb_norm=hyb_norm, hyb_w_in=hyb_w_in, ssd_conv_w=ssd_conv_w, ssd_conv_b=ssd_conv_b, ssd_dt_bias=ssd_dt_bias, ssd_a_log=ssd_a_log, ssd_d=ssd_d, ssd_out_norm=ssd_out_norm, sb_q_norm=sb_q_norm, sb_k_norm=sb_k_norm, hyb_w_out=hyb_w_out, pool_norm=pool_norm, pool_w=pool_w, pool_b=pool_b, pool_scale=pool_scale, mlp_norm=mlp_norm, mlp_w_up=mlp_w_up, mlp_w_down=mlp_w_down, loss_target=loss_target, m_hyb_norm=m_hyb_norm, m_hyb_w_in=m_hyb_w_in, m_ssd_conv_w=m_ssd_conv_w, m_ssd_conv_b=m_ssd_conv_b, m_ssd_dt_bias=m_ssd_dt_bias, m_ssd_a_log=m_ssd_a_log, m_ssd_d=m_ssd_d, m_ssd_out_norm=m_ssd_out_norm, m_sb_q_norm=m_sb_q_norm, m_sb_k_norm=m_sb_k_norm, m_hyb_w_out=m_hyb_w_out, m_pool_norm=m_pool_norm, m_pool_w=m_pool_w, m_pool_b=m_pool_b, m_pool_scale=m_pool_scale, m_mlp_norm=m_mlp_norm, m_mlp_w_up=m_mlp_w_up, m_mlp_w_down=m_mlp_w_down, v_hyb_norm=v_hyb_norm, v_hyb_w_in=v_hyb_w_in, v_ssd_conv_w=v_ssd_conv_w, v_ssd_conv_b=v_ssd_conv_b, v_ssd_dt_bias=v_ssd_dt_bias, v_ssd_a_log=v_ssd_a_log, v_ssd_d=v_ssd_d, v_ssd_out_norm=v_ssd_out_norm, v_sb_q_norm=v_sb_q_norm, v_sb_k_norm=v_sb_k_norm, v_hyb_w_out=v_hyb_w_out, v_pool_norm=v_pool_norm, v_pool_w=v_pool_w, v_pool_b=v_pool_b, v_pool_scale=v_pool_scale, v_mlp_norm=v_mlp_norm, v_mlp_w_up=v_mlp_w_up, v_mlp_w_down=v_mlp_w_down)
    weights = {n: given[n] for n in TWIN_WEIGHTS}
    shared = {n: given[n] for n in SHARED_INPUTS}
    per_example = {n: given[n] for n in ['x']}
    grad_fn = _jax.value_and_grad(_loss, argnums=(0, 1))

    def one_microbatch(ex, loss_target):
        ex = dict(ex)
        diff = ex.pop(TWIN_DIFF_INPUT)
        return grad_fn(weights, diff, {**shared, **ex}, loss_target)

    if N_MICROBATCH == 1:
        loss, (grad_w, grad_x) = one_microbatch(per_example, given["loss_target"])
    else:
        def body(carry, xs):
            loss_sum, grad_sum = carry
            l_k, (gw_k, gx_k) = one_microbatch(xs[0], xs[1])
            with _jax.named_scope("update"):
                return (loss_sum + l_k, _jax.tree.map(_jnp.add, grad_sum, gw_k)), gx_k

        init = (_jnp.zeros((), _jnp.float32), _jax.tree.map(_jnp.zeros_like, weights))
        (loss, grad_w), grad_x = _jax.lax.scan(body, init, (per_example, given["loss_target"]))
    with _jax.named_scope("update"):
        delta_w, new_m, new_v = {}, {}, {}
        for n in TWIN_WEIGHTS:
            delta_w[n], new_m[n], new_v[n] = _adamw(weights[n], grad_w[n], given["m_" + n], given["v_" + n])
    return (loss, grad_x, *[grad_w[n] for n in TWIN_WEIGHTS], *[delta_w[n] for n in TWIN_WEIGHTS],
            *[new_m[n] for n in TWIN_WEIGHTS], *[new_v[n] for n in TWIN_WEIGHTS])
```

```python
import functools
import math

import jax
import jax.numpy as jnp
from jax import lax
from jax.experimental import pallas as pl
from jax.experimental.pallas import tpu as pltpu

F32 = jnp.float32
BF16 = jnp.bfloat16
EPS = 1e-6
V7X_VMEM_LIMIT = 56 * 1024 * 1024
MESH = pl.DeviceIdType.MESH
N_DEV = 8

SSD_HEADS = 32
SSD_HEAD_DIM = 64
SSD_STATE = 128
SSD_GROUPS = 4
SSD_CHUNK = 128
GROUP_W = SSD_HEADS * SSD_HEAD_DIM // SSD_GROUPS
HEADS_PER_GROUP = SSD_HEADS // SSD_GROUPS
SB_HEADS = 16
SB_DIM = 128
POOL_WINDOWS = (2, 4, 8, 16)
LANE = 128

ADAM_LR = 0.001
ADAM_B1 = 0.9
ADAM_B2 = 0.999
ADAM_EPS = 1e-08
ADAM_WD = 0.01
ADAM_STEP = 10

NN = (((1,), (0,)), ((), ()))
NT = (((1,), (1,)), ((), ()))
TN = (((0,), (0,)), ((), ()))

def _call(body, *, name, grid, in_specs, out_specs, out_shape, scratch=(), prefetch=0, **params):
    cp = pltpu.CompilerParams(vmem_limit_bytes=V7X_VMEM_LIMIT, **params)
    if prefetch:
        gs = pltpu.PrefetchScalarGridSpec(num_scalar_prefetch=prefetch, grid=grid, in_specs=in_specs,
                                          out_specs=out_specs, scratch_shapes=list(scratch))
        return pl.pallas_call(body, name=name, grid_spec=gs, out_shape=out_shape, compiler_params=cp)
    return pl.pallas_call(body, name=name, grid=grid, in_specs=in_specs, out_specs=out_specs,
                          out_shape=out_shape, scratch_shapes=list(scratch), compiler_params=cp)


def _dot(a, b, dims=NN):
    return lax.dot_general(a, b, dims, preferred_element_type=F32)


def _split3(x):
    hi = x.astype(BF16)
    r = x - hi.astype(F32)
    mid = r.astype(BF16)
    lo = (r - mid.astype(F32)).astype(BF16)
    return hi, mid, lo


def _dot3(x, m, dims=NN):
    hi, mid, lo = _split3(x)
    return _dot(hi, m, dims) + _dot(mid, m, dims) + _dot(lo, m, dims)


def _dot3l(m, x, dims=NN):
    hi, mid, lo = _split3(x)
    return _dot(m, hi, dims) + _dot(m, mid, dims) + _dot(m, lo, dims)


def _dot2(x, m):
    hi = x.astype(BF16)
    lo = (x - hi.astype(F32)).astype(BF16)
    return _dot(hi, m) + _dot(lo, m)


def _sigmoid(x):
    return 1.0 / (1.0 + jnp.exp(-x))


def _softplus(x):
    return jnp.maximum(x, 0.0) + jnp.log(1.0 + jnp.exp(-jnp.abs(x)))


def _iota(shape, dim):
    return lax.broadcasted_iota(jnp.int32, shape, dim)


def _matmul(a, b, *, mode, name, tm, tn, tk, extras=(), epilogue=None, out_dtypes=(F32,)):
    if mode == "tn":
        (K, M), N = a.shape, b.shape[1]
    else:
        (M, K), N = a.shape, b.shape[1 if mode == "nn" else 0]
    tm, tn, tk = min(tm, M), min(tn, N), min(tk, K)
    assert M % tm == 0 and N % tn == 0 and K % tk == 0, (name, M, N, K, tm, tn, tk)
    if mode == "nn":
        a_spec = pl.BlockSpec((tm, tk), lambda i, j, k: (i, k))
        b_spec = pl.BlockSpec((tk, tn), lambda i, j, k: (k, j))
        dims = NN
    elif mode == "nt":
        a_spec = pl.BlockSpec((tm, tk), lambda i, j, k: (i, k))
        b_spec = pl.BlockSpec((tn, tk), lambda i, j, k: (j, k))
        dims = NT
    else:
        a_spec = pl.BlockSpec((tk, tm), lambda i, j, k: (k, i))
        b_spec = pl.BlockSpec((tk, tn), lambda i, j, k: (k, j))
        dims = TN
    nk = K // tk
    ex_specs = []
    for e in extras:
        if e.shape[0] == 1:
            ex_specs.append(pl.BlockSpec((1, tn), lambda i, j, k: (0, j)))
        else:
            ex_specs.append(pl.BlockSpec((tm, tn), lambda i, j, k: (i, j)))
    n_ex, n_out = len(extras), len(out_dtypes)

    def body(*refs):
        a_ref, b_ref = refs[0], refs[1]
        ex_refs = refs[2:2 + n_ex]
        o_refs = refs[2 + n_ex:2 + n_ex + n_out]
        acc = refs[-1]
        k = pl.program_id(2)

        @pl.when(k == 0)
        def _():
            acc[...] = jnp.zeros_like(acc)

        acc[...] += _dot(a_ref[...].astype(BF16), b_ref[...].astype(BF16), dims)

        @pl.when(k == nk - 1)
        def _():
            r = acc[...]
            outs = (r,) if epilogue is None else epilogue(r, *[e[...] for e in ex_refs])
            for o_ref, o in zip(o_refs, outs):
                o_ref[...] = o.astype(o_ref.dtype)

    out_shape = [jax.ShapeDtypeStruct((M, N), d) for d in out_dtypes]
    out_specs = [pl.BlockSpec((tm, tn), lambda i, j, k: (i, j)) for _ in out_dtypes]
    res = _call(body, name=name, grid=(M // tm, N // tn, nk), in_specs=[a_spec, b_spec] + ex_specs,
                out_specs=out_specs, out_shape=out_shape, scratch=[pltpu.VMEM((tm, tn), F32)],
                dimension_semantics=("parallel", "parallel", "arbitrary"))(a, b, *extras)
    return res if n_out > 1 else res[0]


def _rowwise(fn, *, name, T, tm, tiles, vecs, out_tiles, out_vecs):
    n_t, n_v, n_ot, n_ov = len(tiles), len(vecs), len(out_tiles), len(out_vecs)

    def body(*refs):
        ins = [r[...] for r in refs[:n_t + n_v]]
        outs = fn(*ins)
        ot_refs = refs[n_t + n_v:n_t + n_v + n_ot]
        ov_refs = refs[n_t + n_v + n_ot:]
        for r, o in zip(ot_refs, outs[:n_ot]):
            r[...] = o.astype(r.dtype)
        if n_ov:
            first = pl.program_id(0) == 0

            @pl.when(first)
            def _():
                for r, o in zip(ov_refs, outs[n_ot:]):
                    r[...] = o

            @pl.when(jnp.logical_not(first))
            def _():
                for r, o in zip(ov_refs, outs[n_ot:]):
                    r[...] += o

    in_specs = [pl.BlockSpec((tm, w), functools.partial(lambda i, cb: (i, cb), cb=cb)) for _, w, cb in tiles]
    in_specs += [pl.BlockSpec(v.shape, lambda i: (0, 0)) for v in vecs]
    out_specs = [pl.BlockSpec((tm, w), lambda i: (i, 0)) for w, _ in out_tiles]
    out_specs += [pl.BlockSpec((r, w), lambda i: (0, 0)) for r, w in out_vecs]
    out_shape = [jax.ShapeDtypeStruct((T, w), d) for w, d in out_tiles]
    out_shape += [jax.ShapeDtypeStruct((r, w), F32) for r, w in out_vecs]
    return _call(body, name=name, grid=(T // tm,), in_specs=in_specs, out_specs=out_specs, out_shape=out_shape,
                 dimension_semantics=("arbitrary",))(*[t[0] for t in tiles], *vecs)


def _colsum(x):
    return jnp.sum(x, axis=0, keepdims=True)


def _rms_fwd(x, g):
    r = lax.rsqrt(jnp.mean(x * x, axis=-1, keepdims=True) + EPS)
    return x * r * g


def _rms_bwd(dh, x, g):
    r = lax.rsqrt(jnp.mean(x * x, axis=-1, keepdims=True) + EPS)
    xh = x * r
    dxh = dh * g
    dx = r * (dxh - xh * jnp.mean(dxh * xh, axis=-1, keepdims=True))
    return dx, _colsum(dh * xh)


def _rmsnorm(x, g, *, name, dtype):
    T, D = x.shape
    return _rowwise(lambda xv, gv: (_rms_fwd(xv, gv),), name=name, T=T, tm=256, tiles=[(x, D, 0)], vecs=[g],
                    out_tiles=[(D, dtype)], out_vecs=[])[0]


def _rmsnorm_bwd(dh, x, g, dres, *, name):
    T, D = x.shape

    def fn(dhv, xv, drv, gv):
        dx, dg = _rms_bwd(dhv, xv, gv)
        return drv + dx, dg

    return _rowwise(fn, name=name, T=T, tm=256, tiles=[(dh, D, 0), (x, D, 0), (dres, D, 0)], vecs=[g],
                    out_tiles=[(D, F32)], out_vecs=[(1, D)])


def _group_slices(width, group):
    return [slice(i, i + group) for i in range(0, width, group)]


def _gate_norm_fwd(ypre, proj, gain):
    T, W = ypre.shape

    def fn(y, z, g):
        gated = y * (z * _sigmoid(z))
        return (jnp.concatenate([_rms_fwd(gated[:, s], g[:, s]) for s in _group_slices(W, GROUP_W)], axis=1),)

    return _rowwise(fn, name="ssd_gate_norm", T=T, tm=256, tiles=[(ypre, W, 0), (proj, W, 0)], vecs=[gain],
                    out_tiles=[(W, BF16)], out_vecs=[])[0]


def _gate_norm_bwd(dmerged, ypre, proj, gain):
    T, W = ypre.shape

    def fn(do, y, z, g):
        sg = _sigmoid(z)
        sz = z * sg
        gated = y * sz
        parts = [_rms_bwd(do[:, s], gated[:, s], g[:, s]) for s in _group_slices(W, GROUP_W)]
        dgated = jnp.concatenate([p[0] for p in parts], axis=1)
        dgain = jnp.concatenate([p[1] for p in parts], axis=1)
        return dgated * sz, dgated * y * (sg * (1.0 + z * (1.0 - sg))), dgain

    return _rowwise(fn, name="ssd_gate_norm_bwd", T=T, tm=256, tiles=[(dmerged, W, 0), (ypre, W, 0), (proj, W, 0)],
                    vecs=[gain], out_tiles=[(W, F32), (W, BF16)], out_vecs=[(1, W)])


def _qk_norm_fwd(proj, qg, kg, W):
    T = proj.shape[0]

    def fn(q, k, gq, gk):
        sl = _group_slices(W, SB_DIM)
        return (jnp.concatenate([_rms_fwd(q[:, s], gq) for s in sl], axis=1),
                jnp.concatenate([_rms_fwd(k[:, s], gk) for s in sl], axis=1))

    return _rowwise(fn, name="sb_qk_norm", T=T, tm=256, tiles=[(proj, W, 1), (proj, W, 2)], vecs=[qg, kg],
                    out_tiles=[(W, BF16), (W, BF16)], out_vecs=[])


def _qk_norm_bwd(dqh, dkh, proj, qg, kg, W):
    T = proj.shape[0]

    def fn(dq, dk, q, k, gq, gk):
        sl = _group_slices(W, SB_DIM)
        pq = [_rms_bwd(dq[:, s], q[:, s], gq) for s in sl]
        pk = [_rms_bwd(dk[:, s], k[:, s], gk) for s in sl]
        return (jnp.concatenate([p[0] for p in pq], axis=1), jnp.concatenate([p[0] for p in pk], axis=1),
                sum(p[1] for p in pq), sum(p[1] for p in pk))

    return _rowwise(fn, name="sb_qk_norm_bwd", T=T, tm=256,
                    tiles=[(dqh, W, 0), (dkh, W, 0), (proj, W, 1), (proj, W, 2)], vecs=[qg, kg],
                    out_tiles=[(W, BF16), (W, BF16)], out_vecs=[(1, SB_DIM), (1, SB_DIM)])


def _loss_grad(y, target):
    T, D = y.shape

    def fn(yv, tv):
        err = yv - tv
        return err * (1.0 / D), _colsum(err * err)

    return _rowwise(fn, name="loss_grad", T=T, tm=256, tiles=[(y, D, 0), (target, D, 0)], vecs=[],
                    out_tiles=[(D, F32)], out_vecs=[(1, D)])


def _pool_scale_bwd(dx, ypre, scale):
    T, D = dx.shape

    def fn(d, yp, s):
        dpre = d * s
        return dpre, _colsum(d * yp), _colsum(dpre)

    return _rowwise(fn, name="pool_scale_bwd", T=T, tm=256, tiles=[(dx, D, 0), (ypre, D, 0)], vecs=[scale],
                    out_tiles=[(D, BF16)], out_vecs=[(1, D), (1, D)])


ROWS = 512


def _past(cur, prev, k):
    row = _iota(cur.shape, 0)
    rc = pltpu.roll(cur, k, 0)
    if prev is None:
        return jnp.where(row >= k, rc, 0.0)
    return jnp.where(row >= k, rc, pltpu.roll(prev, k, 0))


def _future(cur, nxt, k):
    n = cur.shape[0]
    row = _iota(cur.shape, 0)
    rc = pltpu.roll(cur, n - k, 0)
    if nxt is None:
        return jnp.where(row < n - k, rc, 0.0)
    return jnp.where(row < n - k, rc, pltpu.roll(nxt, n - k, 0))


def _chunk(ref, ci):
    return ref[ci * ROWS:(ci + 1) * ROWS, :]


def _conv_pre(x_ref, w, b, ci):
    cur = _chunk(x_ref, ci)
    prev = _chunk(x_ref, ci - 1) if ci > 0 else None
    taps = [_past(cur, prev, 3), _past(cur, prev, 2), _past(cur, prev, 1), cur]
    xc = b + sum(w[j:j + 1, :] * taps[j] for j in range(4))
    return xc, taps


CONV_COLS = 256


def _conv_silu_fwd(proj, col0, width, conv_w, conv_b):
    T = proj.shape[0]

    def body(x_ref, w_ref, b_ref, o_ref):
        w, b = w_ref[...], b_ref[...]
        for ci in range(T // ROWS):
            xc, _ = _conv_pre(x_ref, w, b, ci)
            o_ref[ci * ROWS:(ci + 1) * ROWS, :] = xc * _sigmoid(xc)

    cb0 = col0 // CONV_COLS
    return _call(body, name="ssd_conv_silu", grid=(width // CONV_COLS,),
                 in_specs=[pl.BlockSpec((T, CONV_COLS), lambda j: (0, cb0 + j)),
                           pl.BlockSpec((4, CONV_COLS), lambda j: (0, j)),
                           pl.BlockSpec((1, CONV_COLS), lambda j: (0, j))],
                 out_specs=pl.BlockSpec((T, CONV_COLS), lambda j: (0, j)),
                 out_shape=jax.ShapeDtypeStruct((T, width), F32))(proj, conv_w, conv_b)


def _conv_silu_bwd(dxa, proj, col0, width, conv_w, conv_b):
    T = proj.shape[0]
    nchunk = T // ROWS

    def body(d_ref, x_ref, w_ref, b_ref, dx_ref, dw_ref, db_ref, dxc_ref):
        w, b = w_ref[...], b_ref[...]
        dw = [jnp.zeros((1, CONV_COLS), F32) for _ in range(4)]
        db = jnp.zeros((1, CONV_COLS), F32)
        for ci in range(nchunk):
            xc, taps = _conv_pre(x_ref, w, b, ci)
            sg = _sigmoid(xc)
            dxc = _chunk(d_ref, ci) * (sg * (1.0 + xc * (1.0 - sg)))
            dxc_ref[ci * ROWS:(ci + 1) * ROWS, :] = dxc
            db = db + _colsum(dxc)
            dw = [dw[j] + _colsum(dxc * taps[j]) for j in range(4)]
        dw_ref[...] = jnp.concatenate(dw + [jnp.zeros((4, CONV_COLS), F32)], axis=0)
        db_ref[...] = db
        for ci in range(nchunk):
            cur = _chunk(dxc_ref, ci)
            nxt = _chunk(dxc_ref, ci + 1) if ci + 1 < nchunk else None
            dx = (w[3:4, :] * cur + w[2:3, :] * _future(cur, nxt, 1) + w[1:2, :] * _future(cur, nxt, 2)
                  + w[0:1, :] * _future(cur, nxt, 3))
            dx_ref[ci * ROWS:(ci + 1) * ROWS, :] = dx.astype(dx_ref.dtype)

    cb0 = col0 // CONV_COLS
    return _call(body, name="ssd_conv_silu_bwd", grid=(width // CONV_COLS,),
                 in_specs=[pl.BlockSpec((T, CONV_COLS), lambda j: (0, j)),
                           pl.BlockSpec((T, CONV_COLS), lambda j: (0, cb0 + j)),
                           pl.BlockSpec((4, CONV_COLS), lambda j: (0, j)),
                           pl.BlockSpec((1, CONV_COLS), lambda j: (0, j))],
                 out_specs=[pl.BlockSpec((T, CONV_COLS), lambda j: (0, j)),
                            pl.BlockSpec((8, CONV_COLS), lambda j: (0, j)),
                            pl.BlockSpec((1, CONV_COLS), lambda j: (0, j))],
                 out_shape=[jax.ShapeDtypeStruct((T, width), BF16), jax.ShapeDtypeStruct((8, width), F32),
                            jax.ShapeDtypeStruct((1, width), F32)],
                 scratch=[pltpu.VMEM((T, CONV_COLS), F32)])(dxa, proj, conv_w, conv_b)


def _window_count(ci, win, shape):
    t = (_iota(shape, 0) + ci * ROWS + 1).astype(F32)
    return jnp.minimum(t, float(win))


def _pool_diff_fwd(h):
    T, D = h.shape
    per_group = D // len(POOL_WINDOWS) // LANE

    def body(h_ref, o_ref):
        j = pl.program_id(0)
        for gi, win in enumerate(POOL_WINDOWS):
            @pl.when(j // per_group == gi)
            def _(win=win):
                for ci in range(T // ROWS):
                    cur = _chunk(h_ref, ci)
                    prev = _chunk(h_ref, ci - 1) if ci > 0 else None
                    s = cur
                    for k in range(1, win):
                        s = s + _past(cur, prev, k)
                    d = s / _window_count(ci, win, cur.shape) - cur
                    o_ref[ci * ROWS:(ci + 1) * ROWS, :] = d.astype(o_ref.dtype)

    return _call(body, name="pool_diff", grid=(D // LANE,), in_specs=[pl.BlockSpec((T, LANE), lambda j: (0, j))],
                 out_specs=pl.BlockSpec((T, LANE), lambda j: (0, j)),
                 out_shape=jax.ShapeDtypeStruct((T, D), BF16))(h)


def _pool_diff_bwd(dd):
    T, D = dd.shape
    per_group = D // len(POOL_WINDOWS) // LANE
    nchunk = T // ROWS

    def body(d_ref, o_ref):
        j = pl.program_id(0)
        for gi, win in enumerate(POOL_WINDOWS):
            @pl.when(j // per_group == gi)
            def _(win=win):
                for ci in range(nchunk):
                    cur = _chunk(d_ref, ci)
                    q = cur / _window_count(ci, win, cur.shape)
                    qn = None
                    if ci + 1 < nchunk:
                        qn = _chunk(d_ref, ci + 1) / _window_count(ci + 1, win, cur.shape)
                    s = q - cur
                    for k in range(1, win):
                        s = s + _future(q, qn, k)
                    o_ref[ci * ROWS:(ci + 1) * ROWS, :] = s

    return _call(body, name="pool_diff_bwd", grid=(D // LANE,), in_specs=[pl.BlockSpec((T, LANE), lambda j: (0, j))],
                 out_specs=pl.BlockSpec((T, LANE), lambda j: (0, j)),
                 out_shape=jax.ShapeDtypeStruct((T, D), F32))(dd)


def _pool_mm_fwd(d, w, b, scale, x):
    T, D = d.shape
    G = w.shape[1]
    tm = 512

    def body(d_ref, w_ref, b_ref, s_ref, x_ref, yp_ref, o_ref):
        yp = _dot(d_ref[...], w_ref[...]) + b_ref[...]
        yp_ref[...] = yp
        o_ref[...] = x_ref[...] + yp * s_ref[...]

    tile = pl.BlockSpec((tm, G), lambda i, g: (i, g))
    vec = pl.BlockSpec((1, G), lambda i, g: (0, g))
    return _call(body, name="pool_mm", grid=(T // tm, D // G),
                 in_specs=[tile, pl.BlockSpec((None, G, G), lambda i, g: (g, 0, 0)), vec, vec, tile],
                 out_specs=[tile, tile],
                 out_shape=[jax.ShapeDtypeStruct((T, D), F32), jax.ShapeDtypeStruct((T, D), F32)])(d, w, b, scale, x)


def _pool_mm_dx(dpre, w):
    T, D = dpre.shape
    G = w.shape[1]
    tm = 512

    def body(d_ref, w_ref, o_ref):
        o_ref[...] = _dot(d_ref[...], w_ref[...], NT)

    tile = pl.BlockSpec((tm, G), lambda i, g: (i, g))
    return _call(body, name="pool_mm_dx", grid=(T // tm, D // G),
                 in_specs=[tile, pl.BlockSpec((None, G, G), lambda i, g: (g, 0, 0))], out_specs=tile,
                 out_shape=jax.ShapeDtypeStruct((T, D), F32))(dpre, w)


def _pool_mm_dw(d, dpre):
    T, D = d.shape
    G = D // len(POOL_WINDOWS)
    tk = 512

    def body(d_ref, p_ref, o_ref):
        @pl.when(pl.program_id(1) == 0)
        def _():
            o_ref[...] = jnp.zeros_like(o_ref)

        o_ref[...] += _dot(d_ref[...], p_ref[...], TN)

    tile = pl.BlockSpec((tk, G), lambda g, k: (k, g))
    return _call(body, name="pool_mm_dw", grid=(D // G, T // tk), in_specs=[tile, tile],
                 out_specs=pl.BlockSpec((None, G, G), lambda g, k: (g, 0, 0)),
                 out_shape=jax.ShapeDtypeStruct((D // G, G, G), F32))(d, dpre)


def _ssd_consts(ssd_d, a_log, dt_bias):
    head = jnp.arange(LANE)[:, None]
    lane = jnp.arange(GROUP_W)[None, :]
    ex = jnp.stack([(head == g * HEADS_PER_GROUP + lane // SSD_HEAD_DIM) for g in range(SSD_GROUPS)])
    d_lanes = jnp.repeat(ssd_d.reshape(-1), SSD_HEAD_DIM).reshape(1, -1)
    pad = lambda v: jnp.pad(v.reshape(1, -1), ((0, 0), (0, LANE - SSD_HEADS)))
    return ex.astype(BF16), d_lanes, pad(a_log), pad(dt_bias)


def _ssd_chunk(xs, bm, cm, dtr, bias, alog, ex):
    L = SSD_CHUNK
    row, col = _iota((L, L), 0), _iota((L, L), 1)
    causal = col <= row
    ltri = causal.astype(BF16)
    a_row = -jnp.exp(alog)
    dt = _softplus(dtr + bias)
    da = dt * a_row
    dt_l = _dot3(dt, ex)
    da_l = _dot3(da, ex)
    acs_l = _dot3l(ltri, da_l)
    acs_r = _dot3(da, (row <= col).astype(BF16), TN)
    last_l = acs_l[L - 1:L, :]
    e_l = jnp.exp(last_l - acs_l)
    f_l = jnp.exp(acs_l)
    cd_l = jnp.exp(last_l)
    xdt = xs * dt_l
    cb = _dot(cm.astype(BF16), bm.astype(BF16), NT)
    return dict(causal=causal, dt=dt, da=da, dt_l=dt_l, acs_l=acs_l, acs_r=acs_r, e_l=e_l, f_l=f_l, cd_l=cd_l,
                xdt=xdt, cb=cb, a_row=a_row, ltri=ltri)


def _head_decay(q, acsrow_ref, g, r):
    colv = q["acs_l"][:, r * SSD_HEAD_DIM:r * SSD_HEAD_DIM + 1]
    rowv = acsrow_ref[pl.ds(g * HEADS_PER_GROUP + r, 1), :]
    return jnp.exp(jnp.where(q["causal"], colv - rowv, -1e30))


def _ssd_specs(T):
    L = SSD_CHUNK
    xs = pl.BlockSpec((L, GROUP_W), lambda g, c: (c, g))
    nb = SSD_HEADS * SSD_HEAD_DIM // SSD_STATE
    bm = pl.BlockSpec((L, SSD_STATE), lambda g, c: (c, nb + g))
    cm = pl.BlockSpec((L, SSD_STATE), lambda g, c: (c, nb + SSD_GROUPS + g))
    dtr = pl.BlockSpec((L, LANE), lambda g, c: (c, 0))
    vec = pl.BlockSpec((1, LANE), lambda g, c: (0, 0))
    ex = pl.BlockSpec((None, LANE, GROUP_W), lambda g, c: (g, 0, 0))
    dl = pl.BlockSpec((1, GROUP_W), lambda g, c: (0, g))
    return xs, bm, cm, dtr, vec, ex, dl


def _ssd_fwd(xbc, proj_dt, bias, alog, ex, d_lanes):
    T = xbc.shape[0]
    L, nc, W = SSD_CHUNK, T // SSD_CHUNK, SSD_HEADS * SSD_HEAD_DIM

    def body(xs_ref, b_ref, c_ref, dtr_ref, bias_ref, alog_ref, ex_ref, dl_ref, y_ref, st_ref, state, acsrow):
        g, c = pl.program_id(0), pl.program_id(1)

        @pl.when(c == 0)
        def _():
            state[...] = jnp.zeros_like(state)

        xs, bm, cm = xs_ref[...], b_ref[...], c_ref[...]
        q = _ssd_chunk(xs, bm, cm, dtr_ref[...], bias_ref[...], alog_ref[...], ex_ref[...])
        acsrow[...] = q["acs_r"]
        prev = state[...]
        st_ref[...] = prev
        xdt_b = q["xdt"].astype(BF16)
        yoff = q["f_l"] * _dot(cm.astype(BF16), prev.astype(BF16))
        lane = _iota((L, LANE), 1)
        for p in range(HEADS_PER_GROUP // 2):
            sl = slice(p * LANE, (p + 1) * LANE)
            ma = (_head_decay(q, acsrow, g, 2 * p) * q["cb"]).astype(BF16)
            mb = (_head_decay(q, acsrow, g, 2 * p + 1) * q["cb"]).astype(BF16)
            yd = jnp.where(lane < SSD_HEAD_DIM, _dot(ma, xdt_b[:, sl]), _dot(mb, xdt_b[:, sl]))
            y_ref[:, sl] = yd + yoff[:, sl] + dl_ref[:, sl] * xs[:, sl]
        st_new = _dot(bm.astype(BF16), (q["xdt"] * q["e_l"]).astype(BF16), TN)
        state[...] = q["cd_l"] * prev + st_new

    xs, bm, cm, dtr, vec, exs, dl = _ssd_specs(T)
    return _call(body, name="ssd_scan", grid=(SSD_GROUPS, nc), in_specs=[xs, bm, cm, dtr, vec, vec, exs, dl],
                 out_specs=[pl.BlockSpec((L, GROUP_W), lambda g, c: (c, g)),
                            pl.BlockSpec((None, None, SSD_STATE, GROUP_W), lambda g, c: (c, g, 0, 0))],
                 out_shape=[jax.ShapeDtypeStruct((T, W), F32),
                            jax.ShapeDtypeStruct((nc, SSD_GROUPS, SSD_STATE, GROUP_W), F32)],
                 scratch=[pltpu.VMEM((SSD_STATE, GROUP_W), F32), pltpu.VMEM((LANE, L), F32)],
                 dimension_semantics=("arbitrary", "arbitrary"))(xbc, xbc, xbc, proj_dt, bias, alog, ex, d_lanes)


def _ssd_bwd(dy, xbc, proj_dt, states, bias, alog, ex, d_lanes):
    T = xbc.shape[0]
    L, nc, W = SSD_CHUNK, T // SSD_CHUNK, SSD_HEADS * SSD_HEAD_DIM
    P = SSD_HEAD_DIM

    def body(dy_ref, xs_ref, b_ref, c_ref, dtr_ref, st_ref, bias_ref, alog_ref, ex_ref, dl_ref,
             dxs_ref, db_ref, dc_ref, ddt_ref, hv_ref, dstate, acsrow):
        g, c = pl.program_id(0), pl.program_id(1)

        @pl.when(c == 0)
        def _():
            dstate[...] = jnp.zeros_like(dstate)

        @pl.when(jnp.logical_and(g == 0, c == 0))
        def _():
            hv_ref[...] = jnp.zeros_like(hv_ref)

        xs, bm, cm, ex = xs_ref[...], b_ref[...], c_ref[...], ex_ref[...]
        dtr, bias = dtr_ref[...], bias_ref[...]
        q = _ssd_chunk(xs, bm, cm, dtr, bias, alog_ref[...], ex)
        acsrow[...] = q["acs_r"]
        dyv = dy_ref[...]
        prev = st_ref[...]
        dst = dstate[...]
        bm_b, cm_b = bm.astype(BF16), cm.astype(BF16)
        row128 = _iota((L, LANE), 0)
        lane = _iota((L, LANE), 1)
        row_w = _iota((L, GROUP_W), 0)

        dxs = dl_ref[...] * dyv
        d_dl = _colsum(dyv * xs)
        gmat = _dot(cm_b, prev.astype(BF16))
        dg_b = (dyv * q["f_l"]).astype(BF16)
        dacs = dyv * q["f_l"] * gmat
        dcm = _dot(dg_b, prev.astype(BF16), NT)
        dprev = _dot(cm_b, dg_b, TN)
        dcd = _colsum(dst * prev)
        dlast = dcd * q["cd_l"]
        xe = q["xdt"] * q["e_l"]
        dxe = _dot(bm_b, dst.astype(BF16))
        dbm = _dot(xe.astype(BF16), dst.astype(BF16), NT)
        dxdt = dxe * q["e_l"]
        t1 = dxe * xe
        dacs = dacs - t1
        dlast = dlast + _colsum(t1)
        dstate[...] = dprev + q["cd_l"] * dst
        xdt_b = q["xdt"].astype(BF16)
        dcb = jnp.zeros((L, L), F32)
        dacs_head = []
        dxdt_diag = []
        for p in range(HEADS_PER_GROUP // 2):
            sl = slice(p * LANE, (p + 1) * LANE)
            xp = xdt_b[:, sl]
            dyp = dyv[:, sl]
            vals, dx_parts = [], []
            for half in range(2):
                in_half = (lane < P) if half == 0 else (lane >= P)
                decay = _head_decay(q, acsrow, g, 2 * p + half)
                m = decay * q["cb"]
                dyh = jnp.where(in_half, dyp, 0.0).astype(BF16)
                dm = jnp.where(q["causal"], _dot(dyh, xp, NT), 0.0)
                dcb = dcb + dm * decay
                dseg = dm * m
                rs = jnp.sum(dseg, axis=1, keepdims=True)
                cs = jnp.broadcast_to(_colsum(dseg), (L, L)).T[:, 0:1]
                vals.append(rs - cs)
                dx_parts.append(_dot(m.astype(BF16), dyp.astype(BF16), TN))
            dxdt_diag.append(jnp.where(lane < P, dx_parts[0], dx_parts[1]))
            dacs_head.append(jnp.where(lane == 0, vals[0], jnp.where(lane == P, vals[1], 0.0)))
        dxdt = dxdt + jnp.concatenate(dxdt_diag, axis=1)
        dacs = dacs + jnp.concatenate(dacs_head, axis=1)
        dacs = jnp.where(row_w == L - 1, dacs + dlast, dacs)
        dcb_b = dcb.astype(BF16)
        dcm = dcm + _dot(dcb_b, bm_b)
        dbm = dbm + _dot(dcb_b, cm_b, TN)
        dacs_h = _dot3(dacs, ex, NT)
        dda = _dot3l((row128 <= lane).astype(BF16), dacs_h)
        ddt = dda * q["a_row"] + _dot3(dxdt * xs, ex, NT)
        dxs = dxs + dxdt * q["dt_l"]
        ddtr = ddt * _sigmoid(dtr + bias)
        d_alog = _colsum(dda * q["dt"]) * q["a_row"]
        d_dh = _dot3(jnp.broadcast_to(d_dl, (8, GROUP_W)), ex, NT)[0:1, :]
        dxs_ref[...] = dxs
        db_ref[...] = dbm
        dc_ref[...] = dcm
        ddt_ref[...] = ddtr
        hv_ref[0:1, :] += d_dh
        hv_ref[1:2, :] += d_alog
        hv_ref[2:3, :] += _colsum(ddtr)

    rev = lambda c: nc - 1 - c
    xs = pl.BlockSpec((L, GROUP_W), lambda g, c: (rev(c), g))
    nb = W // SSD_STATE
    bm = pl.BlockSpec((L, SSD_STATE), lambda g, c: (rev(c), nb + g))
    cm = pl.BlockSpec((L, SSD_STATE), lambda g, c: (rev(c), nb + SSD_GROUPS + g))
    dtr = pl.BlockSpec((L, LANE), lambda g, c: (rev(c), 0))
    st = pl.BlockSpec((None, None, SSD_STATE, GROUP_W), lambda g, c: (rev(c), g, 0, 0))
    vec = pl.BlockSpec((1, LANE), lambda g, c: (0, 0))
    exs = pl.BlockSpec((None, LANE, GROUP_W), lambda g, c: (g, 0, 0))
    dl = pl.BlockSpec((1, GROUP_W), lambda g, c: (0, g))
    grp = pl.BlockSpec((L, SSD_STATE), lambda g, c: (rev(c), g))
    return _call(body, name="ssd_scan_bwd", grid=(SSD_GROUPS, nc),
                 in_specs=[xs, xs, bm, cm, dtr, st, vec, vec, exs, dl],
                 out_specs=[xs, grp, grp, grp, pl.BlockSpec((8, LANE), lambda g, c: (0, 0))],
                 out_shape=[jax.ShapeDtypeStruct((T, W), F32),
                            jax.ShapeDtypeStruct((T, SSD_GROUPS * SSD_STATE), F32),
                            jax.ShapeDtypeStruct((T, SSD_GROUPS * SSD_STATE), F32),
                            jax.ShapeDtypeStruct((T, SSD_GROUPS * LANE), F32),
                            jax.ShapeDtypeStruct((8, LANE), F32)],
                 scratch=[pltpu.VMEM((SSD_STATE, GROUP_W), F32), pltpu.VMEM((LANE, L), F32)],
                 dimension_semantics=("arbitrary", "arbitrary"))(dy, xbc, xbc, xbc, proj_dt, states, bias, alog,
                                                                 ex, d_lanes)


SB_TQ = 512
SB_SUB = 128


def _sb_logits(q, kb, qi, kstart, scale):
    z = _dot(q, kb, NT) * scale
    lb = jnp.minimum(z, 0.0) - jnp.log(1.0 + jnp.exp(-jnp.abs(z)))
    mask = (_iota(z.shape, 1) + kstart) < (_iota(z.shape, 0) + qi * SB_TQ)
    lk = jnp.where(mask, lb - z, 0.0)
    return lb, lk, mask


def _sb_weights(lb, lk, mask, run):
    n = SB_SUB
    strict = (_iota((n, n), 0) > _iota((n, n), 1)).astype(BF16)
    ws = [None] * (lb.shape[1] // n)
    for s in reversed(range(len(ws))):
        sl = slice(s * n, (s + 1) * n)
        c = _dot2(lk[:, sl], strict) + run
        ws[s] = jnp.where(mask[:, sl], jnp.exp(lb[:, sl] + c), 0.0)
        run = run + jnp.sum(lk[:, sl], axis=1, keepdims=True)
    return jnp.concatenate(ws, axis=1), run


def _sb_fwd(qh, kh, proj, v_cb0):
    T, W = qh.shape
    tq = min(SB_TQ, T)
    scale = SB_DIM ** -0.5

    def body(q_ref, k_ref, v_ref, o_ref):
        qi = pl.program_id(1)
        q = q_ref[...]

        def step(it, carry):
            run, acc = carry
            kstart = pl.multiple_of((qi - it) * tq, tq)
            lb, lk, mask = _sb_logits(q, k_ref[pl.ds(kstart, tq), :], qi, kstart, scale)
            w, run = _sb_weights(lb, lk, mask, run)
            acc = acc + _dot2(w, v_ref[pl.ds(kstart, tq), :].astype(BF16))
            return run, acc

        _, acc = lax.fori_loop(0, qi + 1, step, (jnp.zeros((tq, 1), F32), jnp.zeros((tq, SB_DIM), F32)))
        o_ref[...] = acc

    return _call(body, name="sb_attn", grid=(W // SB_DIM, T // tq),
                 in_specs=[pl.BlockSpec((tq, SB_DIM), lambda h, i: (i, h)),
                           pl.BlockSpec((T, SB_DIM), lambda h, i: (0, h)),
                           pl.BlockSpec((T, SB_DIM), lambda h, i: (0, v_cb0 + h))],
                 out_specs=pl.BlockSpec((tq, SB_DIM), lambda h, i: (i, h)),
                 out_shape=jax.ShapeDtypeStruct((T, W), F32),
                 dimension_semantics=("arbitrary", "arbitrary"))(qh, kh, proj)


def _sb_bwd(qh, kh, proj, v_cb0, o, dmerged, do_cb0):
    T, W = qh.shape
    tq = min(SB_TQ, T)
    n = SB_SUB
    scale = SB_DIM ** -0.5

    def body(q_ref, k_ref, v_ref, o_ref, do_ref, dq_ref, dk_ref, dv_ref):
        qi = pl.program_id(1)

        @pl.when(qi == 0)
        def _():
            dk_ref[...] = jnp.zeros_like(dk_ref)
            dv_ref[...] = jnp.zeros_like(dv_ref)

        q = q_ref[...]
        do = do_ref[...]
        do_b = do.astype(BF16)
        etot = jnp.sum(do_b.astype(F32) * o_ref[...], axis=1, keepdims=True)
        incl = (_iota((n, n), 0) >= _iota((n, n), 1)).astype(BF16)

        def step(it, carry):
            run, erun, dq = carry
            kstart = pl.multiple_of((qi - it) * tq, tq)
            kb = k_ref[pl.ds(kstart, tq), :]
            vb = v_ref[pl.ds(kstart, tq), :].astype(BF16)
            lb, lk, mask = _sb_logits(q, kb, qi, kstart, scale)
            w, run = _sb_weights(lb, lk, mask, run)
            e = _dot(do_b, vb, NT) * w
            beta = jnp.exp(lb)
            dzs = [None] * (tq // n)
            for s in reversed(range(tq // n)):
                sl = slice(s * n, (s + 1) * n)
                before = etot - erun - _dot3(e[:, sl], incl)
                dzs[s] = jnp.where(mask[:, sl], e[:, sl] * (1.0 - beta[:, sl]) - before * beta[:, sl], 0.0)
                erun = erun + jnp.sum(e[:, sl], axis=1, keepdims=True)
            dz = (jnp.concatenate(dzs, axis=1) * scale).astype(BF16)
            dq = dq + _dot(dz, kb)
            dk_ref[pl.ds(kstart, tq), :] += _dot(dz, q, TN)
            dv_ref[pl.ds(kstart, tq), :] += _dot(w.astype(BF16), do_b, TN)
            return run, erun, dq

        zero = jnp.zeros((tq, 1), F32)
        _, _, dq = lax.fori_loop(0, qi + 1, step, (zero, zero, jnp.zeros((tq, SB_DIM), F32)))
        dq_ref[...] = dq

    tile = pl.BlockSpec((tq, SB_DIM), lambda h, i: (i, h))
    full = pl.BlockSpec((T, SB_DIM), lambda h, i: (0, h))
    shp = jax.ShapeDtypeStruct((T, W), F32)
    return _call(body, name="sb_attn_bwd", grid=(W // SB_DIM, T // tq),
                 in_specs=[tile, full, pl.BlockSpec((T, SB_DIM), lambda h, i: (0, v_cb0 + h)), tile,
                           pl.BlockSpec((tq, SB_DIM), lambda h, i: (i, do_cb0 + h))],
                 out_specs=[tile, full, full], out_shape=[shp, shp, shp],
                 dimension_semantics=("arbitrary", "arbitrary"))(qh, kh, proj, o, dmerged)


def _mlp_fwd(x, g, w_up, w_down, tag):
    h = _rmsnorm(x, g, name=f"mlp{tag}_norm", dtype=BF16)

    def relu_sq(acc):
        u = jnp.maximum(acc, 0.0)
        return u, u * u

    u, s = _matmul(h, w_up, mode="nn", name=f"mlp{tag}_up", tm=1024, tn=1024, tk=512, epilogue=relu_sq,
                   out_dtypes=(BF16, BF16))
    y = _matmul(s, w_down, mode="nn", name=f"mlp{tag}_down", tm=1024, tn=1024, tk=512, extras=(x,),
                epilogue=lambda acc, r: (acc + r,))
    return y, (h, u, s)


def _mlp_bwd(dy, x, g, w_up, w_down, saved, tag):
    h, u, s = saved
    dw_down = _matmul(s, dy, mode="tn", name=f"mlp{tag}_dwdown", tm=1024, tn=1024, tk=512)
    da = _matmul(dy, w_down, mode="nt", name=f"mlp{tag}_da", tm=1024, tn=1024, tk=512, extras=(u,),
                 epilogue=lambda acc, uv: (acc * (2.0 * uv.astype(F32)),), out_dtypes=(BF16,))
    dw_up = _matmul(h, da, mode="tn", name=f"mlp{tag}_dwup", tm=1024, tn=1024, tk=512)
    dh = _matmul(da, w_up, mode="nt", name=f"mlp{tag}_dh", tm=1024, tn=1024, tk=512)
    dx, dg = _rmsnorm_bwd(dh, x, g, dy, name=f"mlp{tag}_norm_bwd")
    return dx, dw_up, dw_down, dg


def _local_step(x, target, p):
    T, D = x.shape
    W = SSD_HEADS * SSD_HEAD_DIM
    g = {}

    h0 = _rmsnorm(x, p["hyb_norm"], name="hyb_norm", dtype=BF16)
    proj = _matmul(h0, p["w_main"], mode="nn", name="hyb_proj", tm=1024, tn=1024, tk=512)
    proj_dt = _matmul(h0, p["w_dt"], mode="nn", name="hyb_proj_dt", tm=1024, tn=128, tk=512)
    ex, d_lanes, alog, bias = _ssd_consts(p["ssd_d"], p["ssd_a_log"], p["ssd_dt_bias"])
    xbc = _conv_silu_fwd(proj, 4 * W, p["conv_w"].shape[1], p["conv_w"], p["conv_b"])
    ypre, states = _ssd_fwd(xbc, proj_dt, bias, alog, ex, d_lanes)
    y_ssd = _gate_norm_fwd(ypre, proj, p["out_norm"])
    qh, kh = _qk_norm_fwd(proj, p["q_norm"], p["k_norm"], W)
    y_sb = _sb_fwd(qh, kh, proj, 3 * W // SB_DIM)
    x1 = _matmul(y_ssd, p["w_out_a"], mode="nn", name="hyb_out_a", tm=1024, tn=1024, tk=512, extras=(x,),
                 epilogue=lambda acc, r: (acc + r,))
    x1 = _matmul(y_sb, p["w_out_b"], mode="nn", name="hyb_out_b", tm=1024, tn=1024, tk=512, extras=(x1,),
                 epilogue=lambda acc, r: (acc + r,))
    x2, mlp0 = _mlp_fwd(x1, p["mlp_norm"][0:1], p["w_up"][0], p["w_down"][0], 0)
    hp = _rmsnorm(x2, p["pool_norm"], name="pool_norm", dtype=F32)
    dpool = _pool_diff_fwd(hp)
    ypool, x3 = _pool_mm_fwd(dpool, p["pool_w"], p["pool_b"], p["pool_scale"], x2)
    x4, mlp1 = _mlp_fwd(x3, p["mlp_norm"][1:2], p["w_up"][1], p["w_down"][1], 1)

    dy, sq = _loss_grad(x4, target)
    dx3, dup1, ddown1, dgm1 = _mlp_bwd(dy, x3, p["mlp_norm"][1:2], p["w_up"][1], p["w_down"][1], mlp1, 1)
    dpre, g["pool_scale"], g["pool_b"] = _pool_scale_bwd(dx3, ypool, p["pool_scale"])
    g["pool_w"] = _pool_mm_dw(dpool, dpre)
    dhp = _pool_diff_bwd(_pool_mm_dx(dpre, p["pool_w"]))
    dx2, g["pool_norm"] = _rmsnorm_bwd(dhp, x2, p["pool_norm"], dx3, name="pool_norm_bwd")
    dx1, dup0, ddown0, dgm0 = _mlp_bwd(dx2, x1, p["mlp_norm"][0:1], p["w_up"][0], p["w_down"][0], mlp0, 0)
    g["mlp_w_up"] = jnp.stack([dup0, dup1])
    g["mlp_w_down"] = jnp.stack([ddown0, ddown1])
    g["mlp_norm"] = jnp.concatenate([dgm0, dgm1], axis=0)

    g["hyb_w_out"] = jnp.concatenate([
        _matmul(y_ssd, dx1, mode="tn", name="hyb_dwout_a", tm=1024, tn=1024, tk=512),
        _matmul(y_sb, dx1, mode="tn", name="hyb_dwout_b", tm=1024, tn=1024, tk=512)], axis=0)
    dmerged = _matmul(dx1, p["w_out"], mode="nt", name="hyb_dmerged", tm=1024, tn=1024, tk=512)
    dqh, dkh, dv = _sb_bwd(qh, kh, proj, 3 * W // SB_DIM, y_sb, dmerged, W // SB_DIM)
    dq, dk, g["sb_q_norm"], g["sb_k_norm"] = _qk_norm_bwd(dqh, dkh, proj, p["q_norm"], p["k_norm"], W)
    dypre, dz, g["ssd_out_norm"] = _gate_norm_bwd(dmerged, ypre, proj, p["out_norm"])
    dxs, dbm, dcm, ddt4, hv = _ssd_bwd(dypre, xbc, proj_dt, states, bias, alog, ex, d_lanes)
    g["ssd_d"], g["ssd_a_log"], g["ssd_dt_bias"] = (hv[i:i + 1, :SSD_HEADS] for i in range(3))
    ddt = ddt4.reshape(T, SSD_GROUPS, LANE).sum(axis=1).astype(BF16)
    dxbc, dconv_w, g["ssd_conv_b"] = _conv_silu_bwd(jnp.concatenate([dxs, dbm, dcm], axis=1), proj, 4 * W,
                                                    p["conv_w"].shape[1], p["conv_w"], p["conv_b"])
    g["ssd_conv_w"] = dconv_w[:4]
    dproj = jnp.concatenate([dz, dq, dk, dv.astype(BF16), dxbc], axis=1)
    g["w_main"] = _matmul(h0, dproj, mode="tn", name="hyb_dwin", tm=1024, tn=1024, tk=512)
    g["w_dt"] = _matmul(h0, ddt, mode="tn", name="hyb_dwdt", tm=1024, tn=128, tk=512)
    dh0 = _matmul(ddt, p["w_dt"], mode="nt", name="hyb_dh_dt", tm=1024, tn=1024, tk=128)
    dh0 = _matmul(dproj, p["w_main"], mode="nt", name="hyb_dh", tm=1024, tn=1024, tk=512, extras=(dh0,),
                  epilogue=lambda acc, r: (acc + r,))
    grad_x, g["hyb_norm"] = _rmsnorm_bwd(dh0, x, p["hyb_norm"], dx1, name="hyb_norm_bwd")
    return sq, grad_x, g


ANY = pl.BlockSpec(memory_space=pl.ANY)


def _position():
    return lax.axis_index("x"), lax.axis_index("y"), lax.axis_index("c")


def _all_gather(v, name):
    def body(v_ref, out_ref, send_sems, recv_sems, local_sem):
        x, y, c = _position()
        me, sibling = (x, y, c), (x, y, 1 - c)
        chips = [(1 - x, y), (x, 1 - y), (1 - x, 1 - y)]

        def rows(px, py, pc):
            return out_ref.at[4 * px + 2 * py + pc]

        def copy(k, block, to, src=None):
            return pltpu.make_async_remote_copy(
                src_ref=rows(*block) if src is None else src, dst_ref=rows(*block),
                send_sem=send_sems.at[k], recv_sem=recv_sems.at[k], device_id=to, device_id_type=MESH)

        mine = pltpu.make_async_copy(v_ref, rows(*me), local_sem)
        mine.start()
        first = [copy(0, me, sibling, src=v_ref)]
        first += [copy(1 + j, me, (*chip, c), src=v_ref) for j, chip in enumerate(chips)]
        for cp in first:
            cp.start()
        passed = [copy(4 + j, (*chip, c), sibling) for j, chip in enumerate(chips)]
        for j, chip in enumerate(chips):
            copy(1 + j, (*chip, c), me).wait_recv()
            passed[j].start()
        copy(0, sibling, me).wait_recv()
        for j, chip in enumerate(chips):
            copy(4 + j, (*chip, 1 - c), me).wait_recv()
        for cp in first + passed:
            cp.wait_send()
        mine.wait()

    return pl.pallas_call(
        body, name=name, out_shape=jax.ShapeDtypeStruct((N_DEV,) + v.shape, v.dtype), in_specs=[ANY], out_specs=ANY,
        scratch_shapes=[pltpu.SemaphoreType.DMA((7,)), pltpu.SemaphoreType.DMA((7,)), pltpu.SemaphoreType.DMA])(v)


def _exchange_cores(gr):
    _, R, C = gr.shape

    def body(g_ref, r_ref, send_sems, recv_sems):
        x, y, c = _position()
        copies = [pltpu.make_async_remote_copy(
            src_ref=g_ref.at[2 * k + (1 - c)], dst_ref=r_ref.at[k], send_sem=send_sems.at[k],
            recv_sem=recv_sems.at[k], device_id=(x, y, 1 - c), device_id_type=MESH) for k in range(4)]
        for cp in copies:
            cp.start()
        for cp in copies:
            cp.wait_recv()
        for cp in copies:
            cp.wait_send()

    return pl.pallas_call(
        body, name="grad_exchange_cores", out_shape=jax.ShapeDtypeStruct((4, R, C), gr.dtype), in_specs=[ANY],
        out_specs=ANY, scratch_shapes=[pltpu.SemaphoreType.DMA((4,)), pltpu.SemaphoreType.DMA((4,))])(gr)


def _pair_sum(gr, r1, tr):
    _, R, C = gr.shape
    x, y, c = _position()
    pos = jnp.stack([c, 2 * x + y]).astype(jnp.int32)

    def body(pos_ref, g_ref, r_ref, pb_ref, pm_ref):
        s = g_ref[...] + r_ref[...]
        pb_ref[...] = s.astype(BF16)

        @pl.when(pl.program_id(1) == pos_ref[1])
        def _():
            pm_ref[...] = s

    return _call(body, name="grad_pair_sum", grid=(R // tr, 4), prefetch=1,
                 in_specs=[pl.BlockSpec((None, tr, C), lambda r, k, pos: (2 * k + pos[0], r, 0)),
                           pl.BlockSpec((None, tr, C), lambda r, k, pos: (k, r, 0))],
                 out_specs=[pl.BlockSpec((None, tr, C), lambda r, k, pos: (k, r, 0)),
                            pl.BlockSpec((tr, C), lambda r, k, pos: (r, 0))],
                 out_shape=[jax.ShapeDtypeStruct((4, R, C), BF16), jax.ShapeDtypeStruct((R, C), F32)],
                 dimension_semantics=("arbitrary", "arbitrary"))(pos, gr, r1)


def _exchange_chips(pb):
    _, R, C = pb.shape

    def body(p_ref, r_ref, send_sems, recv_sems):
        x, y, c = _position()
        chips = [(1 - x, y), (x, 1 - y), (1 - x, 1 - y)]
        mine = 2 * x + y
        copies = [pltpu.make_async_remote_copy(
            src_ref=p_ref.at[2 * cx + cy], dst_ref=r_ref.at[mine], send_sem=send_sems.at[j],
            recv_sem=recv_sems.at[j], device_id=(cx, cy, c), device_id_type=MESH) for j, (cx, cy) in enumerate(chips)]
        for cp in copies:
            cp.start()
        for j, (cx, cy) in enumerate(chips):
            pltpu.make_async_remote_copy(
                src_ref=p_ref.at[mine], dst_ref=r_ref.at[2 * cx + cy], send_sem=send_sems.at[j],
                recv_sem=recv_sems.at[j], device_id=(cx, cy, c), device_id_type=MESH).wait_recv()
        for cp in copies:
            cp.wait_send()

    return pl.pallas_call(
        body, name="grad_exchange_chips", out_shape=jax.ShapeDtypeStruct((4, R, C), pb.dtype), in_specs=[ANY],
        out_specs=ANY, scratch_shapes=[pltpu.SemaphoreType.DMA((3,)), pltpu.SemaphoreType.DMA((3,))])(pb)


def _adamw(w, grad, m, v):
    m = ADAM_B1 * m + (1.0 - ADAM_B1) * grad
    v = ADAM_B2 * v + (1.0 - ADAM_B2) * (grad * grad)
    m_hat = m / (1.0 - ADAM_B1 ** ADAM_STEP)
    v_hat = v / (1.0 - ADAM_B2 ** ADAM_STEP)
    delta = -ADAM_LR * (m_hat / (jnp.sqrt(v_hat) + ADAM_EPS) + ADAM_WD * w)
    return delta, m, v


def _adamw_sharded(pm, r2, w, m, v, tr):
    R, C = pm.shape
    x, y, _ = _position()
    mine = 2 * x + y
    others = jnp.stack([jnp.where(mine <= j, j + 1, j) for j in range(3)]).astype(jnp.int32)

    def body(oth_ref, pm_ref, a_ref, b_ref, c_ref, w_ref, m_ref, v_ref, g_ref, d_ref, nm_ref, nv_ref):
        grad = ((pm_ref[...] + a_ref[...].astype(F32)) + b_ref[...].astype(F32)) + c_ref[...].astype(F32)
        d, nm, nv = _adamw(w_ref[...], grad, m_ref[...], v_ref[...])
        g_ref[...], d_ref[...], nm_ref[...], nv_ref[...] = grad, d, nm, nv

    tile = pl.BlockSpec((tr, C), lambda r, oth: (r, 0))
    other = [pl.BlockSpec((None, tr, C), functools.partial(lambda r, oth, j: (oth[j], r, 0), j=j)) for j in range(3)]
    shp = jax.ShapeDtypeStruct((R, C), F32)
    return _call(body, name="adamw_sharded", grid=(R // tr,), prefetch=1,
                 in_specs=[tile] + other + [tile, tile, tile], out_specs=[tile] * 4, out_shape=[shp] * 4,
                 dimension_semantics=("arbitrary",))(others, pm, r2, r2, r2, w, m, v)


def _adamw_replicated(parts, w, m, v):
    _, R, C = parts.shape

    def body(p_ref, w_ref, m_ref, v_ref, g_ref, d_ref, nm_ref, nv_ref):
        grad = p_ref[0]
        for j in range(1, N_DEV):
            grad = grad + p_ref[j]
        d, nm, nv = _adamw(w_ref[...], grad, m_ref[...], v_ref[...])
        g_ref[...], d_ref[...], nm_ref[...], nv_ref[...] = grad, d, nm, nv

    tile = pl.BlockSpec((R, C), lambda i: (0, 0))
    shp = jax.ShapeDtypeStruct((R, C), F32)
    return _call(body, name="adamw_replicated", grid=(1,),
                 in_specs=[pl.BlockSpec((N_DEV, R, C), lambda i: (0, 0, 0)), tile, tile, tile],
                 out_specs=[tile] * 4, out_shape=[shp] * 4)(parts, w, m, v)


SHARDED = ("hyb_w_in", "hyb_w_out", "mlp_w_up", "mlp_w_down", "pool_w", "ssd_conv_w", "pool_norm", "pool_b",
           "pool_scale")
GATHER_BF16 = SHARDED[:5]
GATHER_F32 = SHARDED[5:]
REPLICATED = ("hyb_norm", "ssd_conv_b", "ssd_dt_bias", "ssd_a_log", "ssd_d", "ssd_out_norm", "sb_q_norm",
              "sb_k_norm", "mlp_norm")
PACK_COLS = 1024
PACK_ROWS_TILE = 256


def _pack(arrays, cols, row_multiple, dtype):
    flat = jnp.concatenate([a.reshape(-1).astype(dtype) for a in arrays])
    n = flat.shape[0]
    total = -(-n // (cols * row_multiple)) * cols * row_multiple
    return jnp.pad(flat, (0, total - n)).reshape(total // cols, cols)


def _unpack(packed, shapes):
    flat = packed.reshape(packed.shape[:-2] + (-1,))
    out, off = [], 0
    for s in shapes:
        n = math.prod(s)
        out.append(flat[..., off:off + n].reshape(flat.shape[:-1] + tuple(s)))
        off += n
    return out


def _shard_axis(name):
    return {"hyb_w_in": 2, "hyb_w_out": 1, "mlp_w_up": 2, "mlp_w_down": 1, "pool_w": 2, "ssd_conv_w": 2,
            "pool_norm": 1, "pool_b": 1, "pool_scale": 1}[name]


def _whole(blocks, name):
    ax = _shard_axis(name)
    moved = jnp.moveaxis(blocks, 0, ax)
    s = moved.shape
    return moved.reshape(s[:ax] + (s[ax] * s[ax + 1],) + s[ax + 2:])


def _to_blocks(whole, name):
    ax = _shard_axis(name)
    s = whole.shape
    split = whole.reshape(s[:ax] + (N_DEV, s[ax] // N_DEV) + s[ax + 1:])
    return jnp.moveaxis(split, ax, 0).reshape(N_DEV, -1)


def kernel(x, hyb_norm, hyb_w_in, ssd_conv_w, ssd_conv_b, ssd_dt_bias, ssd_a_log, ssd_d, ssd_out_norm, sb_q_norm, sb_k_norm, hyb_w_out, pool_norm, pool_w, pool_b, pool_scale, mlp_norm, mlp_w_up, mlp_w_down, loss_target, m_hyb_norm, m_hyb_w_in, m_ssd_conv_w, m_ssd_conv_b, m_ssd_dt_bias, m_ssd_a_log, m_ssd_d, m_ssd_out_norm, m_sb_q_norm, m_sb_k_norm, m_hyb_w_out, m_pool_norm, m_pool_w, m_pool_b, m_pool_scale, m_mlp_norm, m_mlp_w_up, m_mlp_w_down, v_hyb_norm, v_hyb_w_in, v_ssd_conv_w, v_ssd_conv_b, v_ssd_dt_bias, v_ssd_a_log, v_ssd_d, v_ssd_out_norm, v_sb_q_norm, v_sb_k_norm, v_hyb_w_out, v_pool_norm, v_pool_w, v_pool_b, v_pool_scale, v_mlp_norm, v_mlp_w_up, v_mlp_w_down):
    args = dict(locals())
    names = ("hyb_norm", "hyb_w_in", "ssd_conv_w", "ssd_conv_b", "ssd_dt_bias", "ssd_a_log", "ssd_d", "ssd_out_norm",
             "sb_q_norm", "sb_k_norm", "hyb_w_out", "pool_norm", "pool_w", "pool_b", "pool_scale", "mlp_norm",
             "mlp_w_up", "mlp_w_down")
    wt = {n: args[n] for n in names}
    T, D = x.shape[1], x.shape[2]
    W = SSD_HEADS * SSD_HEAD_DIM

    big = _all_gather(_pack([wt[n] for n in GATHER_BF16], PACK_COLS, 8, BF16), "gather_matrices")
    small = _all_gather(_pack([wt[n] for n in GATHER_F32], LANE, 8, F32), "gather_vectors")
    whole = {n: _whole(b, n) for n, b in zip(GATHER_BF16, _unpack(big, [wt[n].shape for n in GATHER_BF16]))}
    whole.update({n: _whole(b, n) for n, b in zip(GATHER_F32, _unpack(small, [wt[n].shape for n in GATHER_F32]))})
    w_in = whole["hyb_w_in"][0]
    conv_dim = ssd_conv_b.shape[-1]
    c1, c2 = W + conv_dim, W + conv_dim + SSD_HEADS
    w_out = whole["hyb_w_out"][0]
    p = {
        "w_main": jnp.concatenate([w_in[:, :W], w_in[:, c2:], w_in[:, W:c1]], axis=1),
        "w_dt": jnp.pad(w_in[:, c1:c2], ((0, 0), (0, LANE - SSD_HEADS))),
        "w_out": w_out, "w_out_a": w_out[:W], "w_out_b": w_out[W:],
        "w_up": whole["mlp_w_up"], "w_down": whole["mlp_w_down"], "pool_w": whole["pool_w"][0],
        "conv_w": whole["ssd_conv_w"][0], "conv_b": ssd_conv_b, "pool_norm": whole["pool_norm"],
        "pool_b": whole["pool_b"], "pool_scale": whole["pool_scale"], "hyb_norm": hyb_norm, "mlp_norm": mlp_norm,
        "out_norm": ssd_out_norm, "q_norm": sb_q_norm, "k_norm": sb_k_norm, "ssd_d": ssd_d, "ssd_a_log": ssd_a_log,
        "ssd_dt_bias": ssd_dt_bias,
    }

    sq, grad_x, g = _local_step(x[0], loss_target[0], p)
    loss = lax.psum(0.5 * jnp.sum(sq) / D, ("x", "y", "c"))

    gm = g["w_main"]
    g_in = jnp.concatenate([gm[:, :W], gm[:, 4 * W:], g["w_dt"][:, :SSD_HEADS], gm[:, W:4 * W]], axis=1)
    gw = {"hyb_w_in": g_in[None], "hyb_w_out": g["hyb_w_out"][None], "mlp_w_up": g["mlp_w_up"],
          "mlp_w_down": g["mlp_w_down"], "pool_w": g["pool_w"][None], "ssd_conv_w": g["ssd_conv_w"][None],
          "pool_norm": g["pool_norm"], "pool_b": g["pool_b"], "pool_scale": g["pool_scale"]}

    blocks = jnp.concatenate([_to_blocks(gw[n], n) for n in SHARDED], axis=1)
    S = blocks.shape[1]
    unit = PACK_COLS * PACK_ROWS_TILE
    S_pad = -(-S // unit) * unit
    gr = jnp.pad(blocks, ((0, 0), (0, S_pad - S))).reshape(N_DEV, S_pad // PACK_COLS, PACK_COLS)
    r1 = _exchange_cores(gr)
    pb, pm = _pair_sum(gr, r1, PACK_ROWS_TILE)
    r2 = _exchange_chips(pb)
    packed = [_pack([args[pre + n] for n in SHARDED], PACK_COLS, PACK_ROWS_TILE, F32) for pre in ("", "m_", "v_")]
    res_sharded = [_unpack(o, [wt[n].shape for n in SHARDED]) for o in _adamw_sharded(pm, r2, *packed, PACK_ROWS_TILE)]

    gs = {"hyb_norm": g["hyb_norm"], "ssd_conv_b": g["ssd_conv_b"], "ssd_dt_bias": g["ssd_dt_bias"],
          "ssd_a_log": g["ssd_a_log"], "ssd_d": g["ssd_d"], "ssd_out_norm": g["ssd_out_norm"],
          "sb_q_norm": g["sb_q_norm"], "sb_k_norm": g["sb_k_norm"], "mlp_norm": g["mlp_norm"]}
    parts = _all_gather(_pack([gs[n] for n in REPLICATED], LANE, 8, F32), "gather_vector_grads")
    packed = [_pack([args[pre + n] for n in REPLICATED], LANE, 8, F32) for pre in ("", "m_", "v_")]
    res_repl = [_unpack(o, [wt[n].shape for n in REPLICATED]) for o in _adamw_replicated(parts, *packed)]

    outs = []
    for k in range(4):
        by_name = dict(zip(SHARDED, res_sharded[k]))
        by_name.update(zip(REPLICATED, res_repl[k]))
        outs += [by_name[n] for n in names]
    return (loss, grad_x[None], *outs)
```

```python
import functools
import math

import jax
import jax.numpy as jnp
from jax import lax
from jax.experimental import pallas as pl
from jax.experimental.pallas import tpu as pltpu

F32 = jnp.float32
BF16 = jnp.bfloat16
EPS = 1e-6
V7X_VMEM_LIMIT = 56 * 1024 * 1024
MESH = pl.DeviceIdType.MESH
N_DEV = 8

SSD_HEADS = 32
SSD_HEAD_DIM = 64
SSD_STATE = 128
SSD_GROUPS = 4
SSD_CHUNK = 128
GROUP_W = SSD_HEADS * SSD_HEAD_DIM // SSD_GROUPS
HEADS_PER_GROUP = SSD_HEADS // SSD_GROUPS
SB_HEADS = 16
SB_DIM = 128
POOL_WINDOWS = (2, 4, 8, 16)
LANE = 128

ADAM_LR = 0.001
ADAM_B1 = 0.9
ADAM_B2 = 0.999
ADAM_EPS = 1e-08
ADAM_WD = 0.01
ADAM_STEP = 10

NN = (((1,), (0,)), ((), ()))
NT = (((1,), (1,)), ((), ()))
TN = (((0,), (0,)), ((), ()))

def _call(body, *, name, grid, in_specs, out_specs, out_shape, scratch=(), prefetch=0, **params):
    cp = pltpu.CompilerParams(vmem_limit_bytes=V7X_VMEM_LIMIT, **params)
    if prefetch:
        gs = pltpu.PrefetchScalarGridSpec(num_scalar_prefetch=prefetch, grid=grid, in_specs=in_specs,
                                          out_specs=out_specs, scratch_shapes=list(scratch))
        return pl.pallas_call(body, name=name, grid_spec=gs, out_shape=out_shape, compiler_params=cp)
    return pl.pallas_call(body, name=name, grid=grid, in_specs=in_specs, out_specs=out_specs,
                          out_shape=out_shape, scratch_shapes=list(scratch), compiler_params=cp)


def _dot(a, b, dims=NN):
    return lax.dot_general(a, b, dims, preferred_element_type=F32)


def _split3(x):
    hi = x.astype(BF16)
    r = x - hi.astype(F32)
    mid = r.astype(BF16)
    lo = (r - mid.astype(F32)).astype(BF16)
    return hi, mid, lo


def _dot3(x, m, dims=NN):
    hi, mid, lo = _split3(x)
    return _dot(hi, m, dims) + _dot(mid, m, dims) + _dot(lo, m, dims)


def _dot3l(m, x, dims=NN):
    hi, mid, lo = _split3(x)
    return _dot(m, hi, dims) + _dot(m, mid, dims) + _dot(m, lo, dims)


def _dot2(x, m):
    hi = x.astype(BF16)
    lo = (x - hi.astype(F32)).astype(BF16)
    return _dot(hi, m) + _dot(lo, m)


def _sigmoid(x):
    return 1.0 / (1.0 + jnp.exp(-x))


def _softplus(x):
    return jnp.maximum(x, 0.0) + jnp.log(1.0 + jnp.exp(-jnp.abs(x)))


def _iota(shape, dim):
    return lax.broadcasted_iota(jnp.int32, shape, dim)


def _matmul(a, b, *, mode, name, tm, tn, tk, extras=(), epilogue=None, out_dtypes=(F32,), mnk=None, b_spec=None,
            out_spec=None, out_dims=None):
    if mnk is not None:
        M, N, K = mnk
    elif mode == "tn":
        (K, M), N = a.shape, b.shape[1]
    else:
        (M, K), N = a.shape, b.shape[1 if mode == "nn" else 0]
    tm, tn, tk = min(tm, M), min(tn, N), min(tk, K)
    assert M % tm == 0 and N % tn == 0 and K % tk == 0, (name, M, N, K, tm, tn, tk)
    if mode == "nn":
        a_spec = pl.BlockSpec((tm, tk), lambda i, j, k: (i, k))
        b_spec = b_spec or pl.BlockSpec((tk, tn), lambda i, j, k: (k, j))
        dims = NN
    elif mode == "nt":
        a_spec = pl.BlockSpec((tm, tk), lambda i, j, k: (i, k))
        b_spec = b_spec or pl.BlockSpec((tn, tk), lambda i, j, k: (j, k))
        dims = NT
    else:
        a_spec = pl.BlockSpec((tk, tm), lambda i, j, k: (k, i))
        b_spec = b_spec or pl.BlockSpec((tk, tn), lambda i, j, k: (k, j))
        dims = TN
    nk = K // tk
    ex_specs = []
    for e in extras:
        if e.shape[0] == 1:
            ex_specs.append(pl.BlockSpec((1, tn), lambda i, j, k: (0, j)))
        else:
            ex_specs.append(pl.BlockSpec((tm, tn), lambda i, j, k: (i, j)))
    n_ex, n_out = len(extras), len(out_dtypes)

    def body(*refs):
        a_ref, b_ref = refs[0], refs[1]
        ex_refs = refs[2:2 + n_ex]
        o_refs = refs[2 + n_ex:2 + n_ex + n_out]
        acc = refs[-1]
        k = pl.program_id(2)

        @pl.when(k == 0)
        def _():
            acc[...] = jnp.zeros_like(acc)

        acc[...] += _dot(a_ref[...].astype(BF16), b_ref[...].astype(BF16), dims)

        @pl.when(k == nk - 1)
        def _():
            r = acc[...]
            outs = (r,) if epilogue is None else epilogue(r, *[e[...] for e in ex_refs])
            for o_ref, o in zip(o_refs, outs):
                o_ref[...] = o.astype(o_ref.dtype)

    out_shape = [jax.ShapeDtypeStruct(out_dims or (M, N), d) for d in out_dtypes]
    out_specs = [out_spec or pl.BlockSpec((tm, tn), lambda i, j, k: (i, j)) for _ in out_dtypes]
    res = _call(body, name=name, grid=(M // tm, N // tn, nk), in_specs=[a_spec, b_spec] + ex_specs,
                out_specs=out_specs, out_shape=out_shape, scratch=[pltpu.VMEM((tm, tn), F32)],
                dimension_semantics=("parallel", "parallel", "arbitrary"))(a, b, *extras)
    return res if n_out > 1 else res[0]


def _rowwise(fn, *, name, T, tm, tiles, vecs, out_tiles, out_vecs):
    n_t, n_v, n_ot, n_ov = len(tiles), len(vecs), len(out_tiles), len(out_vecs)

    def body(*refs):
        ins = [r[...] for r in refs[:n_t + n_v]]
        outs = fn(*ins)
        ot_refs = refs[n_t + n_v:n_t + n_v + n_ot]
        ov_refs = refs[n_t + n_v + n_ot:]
        for r, o in zip(ot_refs, outs[:n_ot]):
            r[...] = o.astype(r.dtype)
        if n_ov:
            first = pl.program_id(0) == 0

            @pl.when(first)
            def _():
                for r, o in zip(ov_refs, outs[n_ot:]):
                    r[...] = o

            @pl.when(jnp.logical_not(first))
            def _():
                for r, o in zip(ov_refs, outs[n_ot:]):
                    r[...] += o

    in_specs = [pl.BlockSpec((tm, w), functools.partial(lambda i, cb: (i, cb), cb=cb)) for _, w, cb in tiles]
    in_specs += [pl.BlockSpec(v.shape, lambda i: (0, 0)) for v in vecs]
    out_specs = [pl.BlockSpec((tm, w), lambda i: (i, 0)) for w, _ in out_tiles]
    out_specs += [pl.BlockSpec((r, w), lambda i: (0, 0)) for r, w in out_vecs]
    out_shape = [jax.ShapeDtypeStruct((T, w), d) for w, d in out_tiles]
    out_shape += [jax.ShapeDtypeStruct((r, w), F32) for r, w in out_vecs]
    return _call(body, name=name, grid=(T // tm,), in_specs=in_specs, out_specs=out_specs, out_shape=out_shape,
                 dimension_semantics=("arbitrary",))(*[t[0] for t in tiles], *vecs)


def _colsum(x):
    return jnp.sum(x, axis=0, keepdims=True)


def _rms_fwd(x, g):
    r = lax.rsqrt(jnp.mean(x * x, axis=-1, keepdims=True) + EPS)
    return x * r * g


def _rms_bwd(dh, x, g):
    r = lax.rsqrt(jnp.mean(x * x, axis=-1, keepdims=True) + EPS)
    xh = x * r
    dxh = dh * g
    dx = r * (dxh - xh * jnp.mean(dxh * xh, axis=-1, keepdims=True))
    return dx, _colsum(dh * xh)


def _rmsnorm(x, g, *, name, dtype):
    T, D = x.shape
    return _rowwise(lambda xv, gv: (_rms_fwd(xv, gv),), name=name, T=T, tm=256, tiles=[(x, D, 0)], vecs=[g],
                    out_tiles=[(D, dtype)], out_vecs=[])[0]


def _rmsnorm_bwd(dh, x, g, dres, *, name):
    T, D = x.shape

    def fn(dhv, xv, drv, gv):
        dx, dg = _rms_bwd(dhv, xv, gv)
        return drv + dx, dg

    return _rowwise(fn, name=name, T=T, tm=256, tiles=[(dh, D, 0), (x, D, 0), (dres, D, 0)], vecs=[g],
                    out_tiles=[(D, F32)], out_vecs=[(1, D)])


def _group_slices(width, group):
    return [slice(i, i + group) for i in range(0, width, group)]


def _gate_norm_fwd(ypre, proj, gain):
    T, W = ypre.shape

    def fn(y, z, g):
        gated = y * (z * _sigmoid(z))
        return (jnp.concatenate([_rms_fwd(gated[:, s], g[:, s]) for s in _group_slices(W, GROUP_W)], axis=1),)

    return _rowwise(fn, name="ssd_gate_norm", T=T, tm=256, tiles=[(ypre, W, 0), (proj, W, 0)], vecs=[gain],
                    out_tiles=[(W, BF16)], out_vecs=[])[0]


def _gate_norm_bwd(dmerged, ypre, proj, gain):
    T, W = ypre.shape

    def fn(do, y, z, g):
        sg = _sigmoid(z)
        sz = z * sg
        gated = y * sz
        parts = [_rms_bwd(do[:, s], gated[:, s], g[:, s]) for s in _group_slices(W, GROUP_W)]
        dgated = jnp.concatenate([p[0] for p in parts], axis=1)
        dgain = jnp.concatenate([p[1] for p in parts], axis=1)
        return dgated * sz, dgated * y * (sg * (1.0 + z * (1.0 - sg))), dgain

    return _rowwise(fn, name="ssd_gate_norm_bwd", T=T, tm=256, tiles=[(dmerged, W, 0), (ypre, W, 0), (proj, W, 0)],
                    vecs=[gain], out_tiles=[(W, F32), (W, BF16)], out_vecs=[(1, W)])


def _qk_norm_fwd(proj, qg, kg, W):
    T = proj.shape[0]

    def fn(q, k, gq, gk):
        sl = _group_slices(W, SB_DIM)
        return (jnp.concatenate([_rms_fwd(q[:, s], gq) for s in sl], axis=1),
                jnp.concatenate([_rms_fwd(k[:, s], gk) for s in sl], axis=1))

    return _rowwise(fn, name="sb_qk_norm", T=T, tm=256, tiles=[(proj, W, 1), (proj, W, 2)], vecs=[qg, kg],
                    out_tiles=[(W, BF16), (W, BF16)], out_vecs=[])


def _qk_norm_bwd(dqh, dkh, proj, qg, kg, W):
    T = proj.shape[0]

    def fn(dq, dk, q, k, gq, gk):
        sl = _group_slices(W, SB_DIM)
        pq = [_rms_bwd(dq[:, s], q[:, s], gq) for s in sl]
        pk = [_rms_bwd(dk[:, s], k[:, s], gk) for s in sl]
        return (jnp.concatenate([p[0] for p in pq], axis=1), jnp.concatenate([p[0] for p in pk], axis=1),
                sum(p[1] for p in pq), sum(p[1] for p in pk))

    return _rowwise(fn, name="sb_qk_norm_bwd", T=T, tm=256,
                    tiles=[(dqh, W, 0), (dkh, W, 0), (proj, W, 1), (proj, W, 2)], vecs=[qg, kg],
                    out_tiles=[(W, BF16), (W, BF16)], out_vecs=[(1, SB_DIM), (1, SB_DIM)])


def _loss_grad(y, target):
    T, D = y.shape

    def fn(yv, tv):
        err = yv - tv
        return err * (1.0 / D), _colsum(err * err)

    return _rowwise(fn, name="loss_grad", T=T, tm=256, tiles=[(y, D, 0), (target, D, 0)], vecs=[],
                    out_tiles=[(D, F32)], out_vecs=[(1, D)])


def _pool_scale_bwd(dx, ypre, scale):
    T, D = dx.shape

    def fn(d, yp, s):
        dpre = d * s
        return dpre, _colsum(d * yp), _colsum(dpre)

    return _rowwise(fn, name="pool_scale_bwd", T=T, tm=256, tiles=[(dx, D, 0), (ypre, D, 0)], vecs=[scale],
                    out_tiles=[(D, BF16)], out_vecs=[(1, D), (1, D)])


ROWS = 512


def _past(cur, prev, k):
    row = _iota(cur.shape, 0)
    rc = pltpu.roll(cur, k, 0)
    if prev is None:
        return jnp.where(row >= k, rc, 0.0)
    return jnp.where(row >= k, rc, pltpu.roll(prev, k, 0))


def _future(cur, nxt, k):
    n = cur.shape[0]
    row = _iota(cur.shape, 0)
    rc = pltpu.roll(cur, n - k, 0)
    if nxt is None:
        return jnp.where(row < n - k, rc, 0.0)
    return jnp.where(row < n - k, rc, pltpu.roll(nxt, n - k, 0))


def _chunk(ref, ci):
    return ref[ci * ROWS:(ci + 1) * ROWS, :]


def _conv_pre(x_ref, w, b, ci):
    cur = _chunk(x_ref, ci)
    prev = _chunk(x_ref, ci - 1) if ci > 0 else None
    taps = [_past(cur, prev, 3), _past(cur, prev, 2), _past(cur, prev, 1), cur]
    xc = b + sum(w[j:j + 1, :] * taps[j] for j in range(4))
    return xc, taps


CONV_COLS = 256


def _conv_silu_fwd(proj, col0, width, conv_w, conv_b):
    T = proj.shape[0]

    def body(x_ref, w_ref, b_ref, o_ref):
        w, b = w_ref[...], b_ref[...]
        for ci in range(T // ROWS):
            xc, _ = _conv_pre(x_ref, w, b, ci)
            o_ref[ci * ROWS:(ci + 1) * ROWS, :] = xc * _sigmoid(xc)

    cb0 = col0 // CONV_COLS
    return _call(body, name="ssd_conv_silu", grid=(width // CONV_COLS,),
                 in_specs=[pl.BlockSpec((T, CONV_COLS), lambda j: (0, cb0 + j)),
                           pl.BlockSpec((4, CONV_COLS), lambda j: (0, j)),
                           pl.BlockSpec((1, CONV_COLS), lambda j: (0, j))],
                 out_specs=pl.BlockSpec((T, CONV_COLS), lambda j: (0, j)),
                 out_shape=jax.ShapeDtypeStruct((T, width), F32))(proj, conv_w, conv_b)


def _conv_silu_bwd(dxa, proj, col0, width, conv_w, conv_b):
    T = proj.shape[0]
    nchunk = T // ROWS

    def body(d_ref, x_ref, w_ref, b_ref, dx_ref, dw_ref, db_ref, dxc_ref):
        w, b = w_ref[...], b_ref[...]
        dw = [jnp.zeros((1, CONV_COLS), F32) for _ in range(4)]
        db = jnp.zeros((1, CONV_COLS), F32)
        for ci in range(nchunk):
            xc, taps = _conv_pre(x_ref, w, b, ci)
            sg = _sigmoid(xc)
            dxc = _chunk(d_ref, ci) * (sg * (1.0 + xc * (1.0 - sg)))
            dxc_ref[ci * ROWS:(ci + 1) * ROWS, :] = dxc
            db = db + _colsum(dxc)
            dw = [dw[j] + _colsum(dxc * taps[j]) for j in range(4)]
        dw_ref[...] = jnp.concatenate(dw + [jnp.zeros((4, CONV_COLS), F32)], axis=0)
        db_ref[...] = db
        for ci in range(nchunk):
            cur = _chunk(dxc_ref, ci)
            nxt = _chunk(dxc_ref, ci + 1) if ci + 1 < nchunk else None
            dx = (w[3:4, :] * cur + w[2:3, :] * _future(cur, nxt, 1) + w[1:2, :] * _future(cur, nxt, 2)
                  + w[0:1, :] * _future(cur, nxt, 3))
            dx_ref[ci * ROWS:(ci + 1) * ROWS, :] = dx.astype(dx_ref.dtype)

    cb0 = col0 // CONV_COLS
    return _call(body, name="ssd_conv_silu_bwd", grid=(width // CONV_COLS,),
                 in_specs=[pl.BlockSpec((T, CONV_COLS), lambda j: (0, j)),
                           pl.BlockSpec((T, CONV_COLS), lambda j: (0, cb0 + j)),
                           pl.BlockSpec((4, CONV_COLS), lambda j: (0, j)),
                           pl.BlockSpec((1, CONV_COLS), lambda j: (0, j))],
                 out_specs=[pl.BlockSpec((T, CONV_COLS), lambda j: (0, j)),
                            pl.BlockSpec((8, CONV_COLS), lambda j: (0, j)),
                            pl.BlockSpec((1, CONV_COLS), lambda j: (0, j))],
                 out_shape=[jax.ShapeDtypeStruct((T, width), BF16), jax.ShapeDtypeStruct((8, width), F32),
                            jax.ShapeDtypeStruct((1, width), F32)],
                 scratch=[pltpu.VMEM((T, CONV_COLS), F32)])(dxa, proj, conv_w, conv_b)


def _window_count(ci, win, shape):
    t = (_iota(shape, 0) + ci * ROWS + 1).astype(F32)
    return jnp.minimum(t, float(win))


def _pool_diff_fwd(h):
    T, D = h.shape
    per_group = D // len(POOL_WINDOWS) // LANE

    def body(h_ref, o_ref):
        j = pl.program_id(0)
        for gi, win in enumerate(POOL_WINDOWS):
            @pl.when(j // per_group == gi)
            def _(win=win):
                for ci in range(T // ROWS):
                    cur = _chunk(h_ref, ci)
                    prev = _chunk(h_ref, ci - 1) if ci > 0 else None
                    s = cur
                    for k in range(1, win):
                        s = s + _past(cur, prev, k)
                    d = s / _window_count(ci, win, cur.shape) - cur
                    o_ref[ci * ROWS:(ci + 1) * ROWS, :] = d.astype(o_ref.dtype)

    return _call(body, name="pool_diff", grid=(D // LANE,), in_specs=[pl.BlockSpec((T, LANE), lambda j: (0, j))],
                 out_specs=pl.BlockSpec((T, LANE), lambda j: (0, j)),
                 out_shape=jax.ShapeDtypeStruct((T, D), BF16))(h)


def _pool_diff_bwd(dd):
    T, D = dd.shape
    per_group = D // len(POOL_WINDOWS) // LANE
    nchunk = T // ROWS

    def body(d_ref, o_ref):
        j = pl.program_id(0)
        for gi, win in enumerate(POOL_WINDOWS):
            @pl.when(j // per_group == gi)
            def _(win=win):
                for ci in range(nchunk):
                    cur = _chunk(d_ref, ci)
                    q = cur / _window_count(ci, win, cur.shape)
                    qn = None
                    if ci + 1 < nchunk:
                        qn = _chunk(d_ref, ci + 1) / _window_count(ci + 1, win, cur.shape)
                    s = q - cur
                    for k in range(1, win):
                        s = s + _future(q, qn, k)
                    o_ref[ci * ROWS:(ci + 1) * ROWS, :] = s

    return _call(body, name="pool_diff_bwd", grid=(D // LANE,), in_specs=[pl.BlockSpec((T, LANE), lambda j: (0, j))],
                 out_specs=pl.BlockSpec((T, LANE), lambda j: (0, j)),
                 out_shape=jax.ShapeDtypeStruct((T, D), F32))(dd)


def _pool_mm_fwd(d, w, b, scale, x):
    T, D = d.shape
    G = w.shape[1]
    tm = 512

    def body(d_ref, w_ref, b_ref, s_ref, x_ref, yp_ref, o_ref):
        yp = _dot(d_ref[...], w_ref[...]) + b_ref[...]
        yp_ref[...] = yp
        o_ref[...] = x_ref[...] + yp * s_ref[...]

    tile = pl.BlockSpec((tm, G), lambda i, g: (i, g))
    vec = pl.BlockSpec((1, G), lambda i, g: (0, g))
    return _call(body, name="pool_mm", grid=(T // tm, D // G),
                 in_specs=[tile, pl.BlockSpec((None, G, G), lambda i, g: (g, 0, 0)), vec, vec, tile],
                 out_specs=[tile, tile],
                 out_shape=[jax.ShapeDtypeStruct((T, D), F32), jax.ShapeDtypeStruct((T, D), F32)])(d, w, b, scale, x)


def _pool_mm_dx(dpre, w):
    T, D = dpre.shape
    G = w.shape[1]
    tm = 512

    def body(d_ref, w_ref, o_ref):
        o_ref[...] = _dot(d_ref[...], w_ref[...], NT)

    tile = pl.BlockSpec((tm, G), lambda i, g: (i, g))
    return _call(body, name="pool_mm_dx", grid=(T // tm, D // G),
                 in_specs=[tile, pl.BlockSpec((None, G, G), lambda i, g: (g, 0, 0))], out_specs=tile,
                 out_shape=jax.ShapeDtypeStruct((T, D), F32))(dpre, w)


def _pool_mm_dw(d, dpre):
    T, D = d.shape
    G = D // len(POOL_WINDOWS)
    tk = 512

    def body(d_ref, p_ref, o_ref):
        @pl.when(pl.program_id(1) == 0)
        def _():
            o_ref[...] = jnp.zeros_like(o_ref)

        o_ref[...] += _dot(d_ref[...], p_ref[...], TN)

    tile = pl.BlockSpec((tk, G), lambda g, k: (k, g))
    return _call(body, name="pool_mm_dw", grid=(D // G, T // tk), in_specs=[tile, tile],
                 out_specs=pl.BlockSpec((None, G, G), lambda g, k: (g, 0, 0)),
                 out_shape=jax.ShapeDtypeStruct((D // G, G, G), F32))(d, dpre)


def _ssd_consts(ssd_d, a_log, dt_bias):
    head = jnp.arange(LANE)[:, None]
    lane = jnp.arange(GROUP_W)[None, :]
    ex = jnp.stack([(head == g * HEADS_PER_GROUP + lane // SSD_HEAD_DIM) for g in range(SSD_GROUPS)])
    d_lanes = jnp.repeat(ssd_d.reshape(-1), SSD_HEAD_DIM).reshape(1, -1)
    pad = lambda v: jnp.pad(v.reshape(1, -1), ((0, 0), (0, LANE - SSD_HEADS)))
    return ex.astype(BF16), d_lanes, pad(a_log), pad(dt_bias)


def _ssd_chunk(xs, bm, cm, dtr, bias, alog, ex):
    L = SSD_CHUNK
    row, col = _iota((L, L), 0), _iota((L, L), 1)
    causal = col <= row
    ltri = causal.astype(BF16)
    a_row = -jnp.exp(alog)
    dt = _softplus(dtr + bias)
    da = dt * a_row
    dt_l = _dot3(dt, ex)
    da_l = _dot3(da, ex)
    acs_l = _dot3l(ltri, da_l)
    acs_r = _dot3(da, (row <= col).astype(BF16), TN)
    last_l = acs_l[L - 1:L, :]
    e_l = jnp.exp(last_l - acs_l)
    f_l = jnp.exp(acs_l)
    cd_l = jnp.exp(last_l)
    xdt = xs * dt_l
    cb = _dot(cm.astype(BF16), bm.astype(BF16), NT)
    return dict(causal=causal, dt=dt, da=da, dt_l=dt_l, acs_l=acs_l, acs_r=acs_r, e_l=e_l, f_l=f_l, cd_l=cd_l,
                xdt=xdt, cb=cb, a_row=a_row, ltri=ltri)


def _head_decay(q, acsrow_ref, g, r):
    colv = q["acs_l"][:, r * SSD_HEAD_DIM:r * SSD_HEAD_DIM + 1]
    rowv = acsrow_ref[pl.ds(g * HEADS_PER_GROUP + r, 1), :]
    return jnp.exp(jnp.where(q["causal"], colv - rowv, -1e30))


def _ssd_specs(T):
    L = SSD_CHUNK
    xs = pl.BlockSpec((L, GROUP_W), lambda g, c: (c, g))
    nb = SSD_HEADS * SSD_HEAD_DIM // SSD_STATE
    bm = pl.BlockSpec((L, SSD_STATE), lambda g, c: (c, nb + g))
    cm = pl.BlockSpec((L, SSD_STATE), lambda g, c: (c, nb + SSD_GROUPS + g))
    dtr = pl.BlockSpec((L, LANE), lambda g, c: (c, 0))
    vec = pl.BlockSpec((1, LANE), lambda g, c: (0, 0))
    ex = pl.BlockSpec((None, LANE, GROUP_W), lambda g, c: (g, 0, 0))
    dl = pl.BlockSpec((1, GROUP_W), lambda g, c: (0, g))
    return xs, bm, cm, dtr, vec, ex, dl


def _ssd_fwd(xbc, proj_dt, bias, alog, ex, d_lanes):
    T = xbc.shape[0]
    L, nc, W = SSD_CHUNK, T // SSD_CHUNK, SSD_HEADS * SSD_HEAD_DIM

    def body(xs_ref, b_ref, c_ref, dtr_ref, bias_ref, alog_ref, ex_ref, dl_ref, y_ref, st_ref, state, acsrow):
        g, c = pl.program_id(0), pl.program_id(1)

        @pl.when(c == 0)
        def _():
            state[...] = jnp.zeros_like(state)

        xs, bm, cm = xs_ref[...], b_ref[...], c_ref[...]
        q = _ssd_chunk(xs, bm, cm, dtr_ref[...], bias_ref[...], alog_ref[...], ex_ref[...])
        acsrow[...] = q["acs_r"]
        prev = state[...]
        st_ref[...] = prev
        xdt_b = q["xdt"].astype(BF16)
        yoff = q["f_l"] * _dot(cm.astype(BF16), prev.astype(BF16))
        lane = _iota((L, LANE), 1)
        for p in range(HEADS_PER_GROUP // 2):
            sl = slice(p * LANE, (p + 1) * LANE)
            ma = (_head_decay(q, acsrow, g, 2 * p) * q["cb"]).astype(BF16)
            mb = (_head_decay(q, acsrow, g, 2 * p + 1) * q["cb"]).astype(BF16)
            yd = jnp.where(lane < SSD_HEAD_DIM, _dot(ma, xdt_b[:, sl]), _dot(mb, xdt_b[:, sl]))
            y_ref[:, sl] = yd + yoff[:, sl] + dl_ref[:, sl] * xs[:, sl]
        st_new = _dot(bm.astype(BF16), (q["xdt"] * q["e_l"]).astype(BF16), TN)
        state[...] = q["cd_l"] * prev + st_new

    xs, bm, cm, dtr, vec, exs, dl = _ssd_specs(T)
    return _call(body, name="ssd_scan", grid=(SSD_GROUPS, nc), in_specs=[xs, bm, cm, dtr, vec, vec, exs, dl],
                 out_specs=[pl.BlockSpec((L, GROUP_W), lambda g, c: (c, g)),
                            pl.BlockSpec((None, None, SSD_STATE, GROUP_W), lambda g, c: (c, g, 0, 0))],
                 out_shape=[jax.ShapeDtypeStruct((T, W), F32),
                            jax.ShapeDtypeStruct((nc, SSD_GROUPS, SSD_STATE, GROUP_W), F32)],
                 scratch=[pltpu.VMEM((SSD_STATE, GROUP_W), F32), pltpu.VMEM((LANE, L), F32)],
                 dimension_semantics=("arbitrary", "arbitrary"))(xbc, xbc, xbc, proj_dt, bias, alog, ex, d_lanes)


def _ssd_bwd(dy, xbc, proj_dt, states, bias, alog, ex, d_lanes):
    T = xbc.shape[0]
    L, nc, W = SSD_CHUNK, T // SSD_CHUNK, SSD_HEADS * SSD_HEAD_DIM
    P = SSD_HEAD_DIM

    def body(dy_ref, xs_ref, b_ref, c_ref, dtr_ref, st_ref, bias_ref, alog_ref, ex_ref, dl_ref,
             dxs_ref, db_ref, dc_ref, ddt_ref, hv_ref, dstate, acsrow):
        g, c = pl.program_id(0), pl.program_id(1)

        @pl.when(c == 0)
        def _():
            dstate[...] = jnp.zeros_like(dstate)

        @pl.when(jnp.logical_and(g == 0, c == 0))
        def _():
            hv_ref[...] = jnp.zeros_like(hv_ref)

        xs, bm, cm, ex = xs_ref[...], b_ref[...], c_ref[...], ex_ref[...]
        dtr, bias = dtr_ref[...], bias_ref[...]
        q = _ssd_chunk(xs, bm, cm, dtr, bias, alog_ref[...], ex)
        acsrow[...] = q["acs_r"]
        dyv = dy_ref[...]
        prev = st_ref[...]
        dst = dstate[...]
        bm_b, cm_b = bm.astype(BF16), cm.astype(BF16)
        row128 = _iota((L, LANE), 0)
        lane = _iota((L, LANE), 1)
        row_w = _iota((L, GROUP_W), 0)

        dxs = dl_ref[...] * dyv
        d_dl = _colsum(dyv * xs)
        gmat = _dot(cm_b, prev.astype(BF16))
        dg_b = (dyv * q["f_l"]).astype(BF16)
        dacs = dyv * q["f_l"] * gmat
        dcm = _dot(dg_b, prev.astype(BF16), NT)
        dprev = _dot(cm_b, dg_b, TN)
        dcd = _colsum(dst * prev)
        dlast = dcd * q["cd_l"]
        xe = q["xdt"] * q["e_l"]
        dxe = _dot(bm_b, dst.astype(BF16))
        dbm = _dot(xe.astype(BF16), dst.astype(BF16), NT)
        dxdt = dxe * q["e_l"]
        t1 = dxe * xe
        dacs = dacs - t1
        dlast = dlast + _colsum(t1)
        dstate[...] = dprev + q["cd_l"] * dst
        xdt_b = q["xdt"].astype(BF16)
        dcb = jnp.zeros((L, L), F32)
        dacs_head = []
        dxdt_diag = []
        for p in range(HEADS_PER_GROUP // 2):
            sl = slice(p * LANE, (p + 1) * LANE)
            xp = xdt_b[:, sl]
            dyp = dyv[:, sl]
            vals, dx_parts = [], []
            for half in range(2):
                in_half = (lane < P) if half == 0 else (lane >= P)
                decay = _head_decay(q, acsrow, g, 2 * p + half)
                m = decay * q["cb"]
                dyh = jnp.where(in_half, dyp, 0.0).astype(BF16)
                dm = jnp.where(q["causal"], _dot(dyh, xp, NT), 0.0)
                dcb = dcb + dm * decay
                dseg = dm * m
                rs = jnp.sum(dseg, axis=1, keepdims=True)
                cs = jnp.broadcast_to(_colsum(dseg), (L, L)).T[:, 0:1]
                vals.append(rs - cs)
                dx_parts.append(_dot(m.astype(BF16), dyp.astype(BF16), TN))
            dxdt_diag.append(jnp.where(lane < P, dx_parts[0], dx_parts[1]))
            dacs_head.append(jnp.where(lane == 0, vals[0], jnp.where(lane == P, vals[1], 0.0)))
        dxdt = dxdt + jnp.concatenate(dxdt_diag, axis=1)
        dacs = dacs + jnp.concatenate(dacs_head, axis=1)
        dacs = jnp.where(row_w == L - 1, dacs + dlast, dacs)
        dcb_b = dcb.astype(BF16)
        dcm = dcm + _dot(dcb_b, bm_b)
        dbm = dbm + _dot(dcb_b, cm_b, TN)
        dacs_h = _dot3(dacs, ex, NT)
        dda = _dot3l((row128 <= lane).astype(BF16), dacs_h)
        ddt = dda * q["a_row"] + _dot3(dxdt * xs, ex, NT)
        dxs = dxs + dxdt * q["dt_l"]
        ddtr = ddt * _sigmoid(dtr + bias)
        d_alog = _colsum(dda * q["dt"]) * q["a_row"]
        d_dh = _dot3(jnp.broadcast_to(d_dl, (8, GROUP_W)), ex, NT)[0:1, :]
        dxs_ref[...] = dxs
        db_ref[...] = dbm
        dc_ref[...] = dcm
        ddt_ref[...] = ddtr
        hv_ref[0:1, :] += d_dh
        hv_ref[1:2, :] += d_alog
        hv_ref[2:3, :] += _colsum(ddtr)

    rev = lambda c: nc - 1 - c
    xs = pl.BlockSpec((L, GROUP_W), lambda g, c: (rev(c), g))
    nb = W // SSD_STATE
    bm = pl.BlockSpec((L, SSD_STATE), lambda g, c: (rev(c), nb + g))
    cm = pl.BlockSpec((L, SSD_STATE), lambda g, c: (rev(c), nb + SSD_GROUPS + g))
    dtr = pl.BlockSpec((L, LANE), lambda g, c: (rev(c), 0))
    st = pl.BlockSpec((None, None, SSD_STATE, GROUP_W), lambda g, c: (rev(c), g, 0, 0))
    vec = pl.BlockSpec((1, LANE), lambda g, c: (0, 0))
    exs = pl.BlockSpec((None, LANE, GROUP_W), lambda g, c: (g, 0, 0))
    dl = pl.BlockSpec((1, GROUP_W), lambda g, c: (0, g))
    grp = pl.BlockSpec((L, SSD_STATE), lambda g, c: (rev(c), g))
    return _call(body, name="ssd_scan_bwd", grid=(SSD_GROUPS, nc),
                 in_specs=[xs, xs, bm, cm, dtr, st, vec, vec, exs, dl],
                 out_specs=[xs, grp, grp, grp, pl.BlockSpec((8, LANE), lambda g, c: (0, 0))],
                 out_shape=[jax.ShapeDtypeStruct((T, W), F32),
                            jax.ShapeDtypeStruct((T, SSD_GROUPS * SSD_STATE), F32),
                            jax.ShapeDtypeStruct((T, SSD_GROUPS * SSD_STATE), F32),
                            jax.ShapeDtypeStruct((T, SSD_GROUPS * LANE), F32),
                            jax.ShapeDtypeStruct((8, LANE), F32)],
                 scratch=[pltpu.VMEM((SSD_STATE, GROUP_W), F32), pltpu.VMEM((LANE, L), F32)],
                 dimension_semantics=("arbitrary", "arbitrary"))(dy, xbc, xbc, xbc, proj_dt, states, bias, alog,
                                                                 ex, d_lanes)


SB_TQ = 512
SB_SUB = 128


def _sb_logits(q, kb, qi, kstart, scale):
    z = _dot(q, kb, NT) * scale
    lb = jnp.minimum(z, 0.0) - jnp.log(1.0 + jnp.exp(-jnp.abs(z)))
    mask = (_iota(z.shape, 1) + kstart) < (_iota(z.shape, 0) + qi * SB_TQ)
    lk = jnp.where(mask, lb - z, 0.0)
    return lb, lk, mask


def _sb_weights(lb, lk, mask, run):
    n = SB_SUB
    strict = (_iota((n, n), 0) > _iota((n, n), 1)).astype(BF16)
    ws = [None] * (lb.shape[1] // n)
    for s in reversed(range(len(ws))):
        sl = slice(s * n, (s + 1) * n)
        c = _dot2(lk[:, sl], strict) + run
        ws[s] = jnp.where(mask[:, sl], jnp.exp(lb[:, sl] + c), 0.0)
        run = run + jnp.sum(lk[:, sl], axis=1, keepdims=True)
    return jnp.concatenate(ws, axis=1), run


def _sb_fwd(qh, kh, proj, v_cb0):
    T, W = qh.shape
    tq = min(SB_TQ, T)
    scale = SB_DIM ** -0.5

    def body(q_ref, k_ref, v_ref, o_ref):
        qi = pl.program_id(1)
        q = q_ref[...]

        def step(it, carry):
            run, acc = carry
            kstart = pl.multiple_of((qi - it) * tq, tq)
            lb, lk, mask = _sb_logits(q, k_ref[pl.ds(kstart, tq), :], qi, kstart, scale)
            w, run = _sb_weights(lb, lk, mask, run)
            acc = acc + _dot2(w, v_ref[pl.ds(kstart, tq), :].astype(BF16))
            return run, acc

        _, acc = lax.fori_loop(0, qi + 1, step, (jnp.zeros((tq, 1), F32), jnp.zeros((tq, SB_DIM), F32)))
        o_ref[...] = acc

    return _call(body, name="sb_attn", grid=(W // SB_DIM, T // tq),
                 in_specs=[pl.BlockSpec((tq, SB_DIM), lambda h, i: (i, h)),
                           pl.BlockSpec((T, SB_DIM), lambda h, i: (0, h)),
                           pl.BlockSpec((T, SB_DIM), lambda h, i: (0, v_cb0 + h))],
                 out_specs=pl.BlockSpec((tq, SB_DIM), lambda h, i: (i, h)),
                 out_shape=jax.ShapeDtypeStruct((T, W), F32),
                 dimension_semantics=("arbitrary", "arbitrary"))(qh, kh, proj)


def _sb_bwd(qh, kh, proj, v_cb0, o, dmerged, do_cb0):
    T, W = qh.shape
    tq = min(SB_TQ, T)
    n = SB_SUB
    scale = SB_DIM ** -0.5

    def body(q_ref, k_ref, v_ref, o_ref, do_ref, dq_ref, dk_ref, dv_ref):
        qi = pl.program_id(1)

        @pl.when(qi == 0)
        def _():
            dk_ref[...] = jnp.zeros_like(dk_ref)
            dv_ref[...] = jnp.zeros_like(dv_ref)

        q = q_ref[...]
        do = do_ref[...]
        do_b = do.astype(BF16)
        etot = jnp.sum(do_b.astype(F32) * o_ref[...], axis=1, keepdims=True)
        incl = (_iota((n, n), 0) >= _iota((n, n), 1)).astype(BF16)

        def step(it, carry):
            run, erun, dq = carry
            kstart = pl.multiple_of((qi - it) * tq, tq)
            kb = k_ref[pl.ds(kstart, tq), :]
            vb = v_ref[pl.ds(kstart, tq), :].astype(BF16)
            lb, lk, mask = _sb_logits(q, kb, qi, kstart, scale)
            w, run = _sb_weights(lb, lk, mask, run)
            e = _dot(do_b, vb, NT) * w
            beta = jnp.exp(lb)
            dzs = [None] * (tq // n)
            for s in reversed(range(tq // n)):
                sl = slice(s * n, (s + 1) * n)
                before = etot - erun - _dot3(e[:, sl], incl)
                dzs[s] = jnp.where(mask[:, sl], e[:, sl] * (1.0 - beta[:, sl]) - before * beta[:, sl], 0.0)
                erun = erun + jnp.sum(e[:, sl], axis=1, keepdims=True)
            dz = (jnp.concatenate(dzs, axis=1) * scale).astype(BF16)
            dq = dq + _dot(dz, kb)
            dk_ref[pl.ds(kstart, tq), :] += _dot(dz, q, TN)
            dv_ref[pl.ds(kstart, tq), :] += _dot(w.astype(BF16), do_b, TN)
            return run, erun, dq

        zero = jnp.zeros((tq, 1), F32)
        _, _, dq = lax.fori_loop(0, qi + 1, step, (zero, zero, jnp.zeros((tq, SB_DIM), F32)))
        dq_ref[...] = dq

    tile = pl.BlockSpec((tq, SB_DIM), lambda h, i: (i, h))
    full = pl.BlockSpec((T, SB_DIM), lambda h, i: (0, h))
    shp = jax.ShapeDtypeStruct((T, W), F32)
    return _call(body, name="sb_attn_bwd", grid=(W // SB_DIM, T // tq),
                 in_specs=[tile, full, pl.BlockSpec((T, SB_DIM), lambda h, i: (0, v_cb0 + h)), tile,
                           pl.BlockSpec((tq, SB_DIM), lambda h, i: (i, do_cb0 + h))],
                 out_specs=[tile, full, full], out_shape=[shp, shp, shp],
                 dimension_semantics=("arbitrary", "arbitrary"))(qh, kh, proj, o, dmerged)


def _mlp_fwd(x, g, w_up, w_down, layer):
    T, D = x.shape
    fs = w_up.shape[3]
    F = N_DEV * fs
    tk = 512
    per = fs // tk
    h = _rmsnorm(x, g, name=f"mlp{layer}_norm", dtype=BF16)

    def relu_sq(acc):
        u = jnp.maximum(acc, 0.0)
        return u, u * u

    u, s = _matmul(h, w_up, mode="nn", name=f"mlp{layer}_up", tm=1024, tn=fs, tk=tk, epilogue=relu_sq,
                   out_dtypes=(BF16, BF16), mnk=(T, F, D),
                   b_spec=pl.BlockSpec((None, None, tk, fs), lambda i, j, k: (j, layer, k, 0)))
    y = _matmul(s, w_down, mode="nn", name=f"mlp{layer}_down", tm=1024, tn=1024, tk=tk, extras=(x,),
                epilogue=lambda acc, r: (acc + r,), mnk=(T, D, F),
                b_spec=pl.BlockSpec((None, None, tk, 1024), lambda i, j, k: (k // per, layer, k % per, j)))
    return y, (h, u, s)


def _mlp_bwd(dy, x, g, w_up, w_down, saved, layer):
    T, D = x.shape
    fs = w_up.shape[3]
    F = N_DEV * fs
    tk = 512
    per = fs // tk
    h, u, s = saved
    dw_down = _matmul(s, dy, mode="tn", name=f"mlp{layer}_dwdown", tm=1024, tn=1024, tk=tk)
    da = _matmul(dy, w_down, mode="nt", name=f"mlp{layer}_da", tm=1024, tn=fs, tk=tk, extras=(u,),
                 epilogue=lambda acc, uv: (acc * (2.0 * uv.astype(F32)),), out_dtypes=(BF16,), mnk=(T, F, D),
                 b_spec=pl.BlockSpec((None, None, fs, tk), lambda i, j, k: (j, layer, 0, k)))
    dw_up = _matmul(h, da, mode="tn", name=f"mlp{layer}_dwup", tm=1024, tn=fs, tk=tk, out_dims=(N_DEV, D, fs),
                    out_spec=pl.BlockSpec((None, 1024, fs), lambda i, j, k: (j, i, 0)))
    dh = _matmul(da, w_up, mode="nt", name=f"mlp{layer}_dh", tm=1024, tn=1024, tk=tk, mnk=(T, D, F),
                 b_spec=pl.BlockSpec((None, None, 1024, tk), lambda i, j, k: (k // per, layer, j, k % per)))
    dx, dg = _rmsnorm_bwd(dh, x, g, dy, name=f"mlp{layer}_norm_bwd")
    return dx, dw_up, dw_down, dg


def _local_step(x, target, p):
    T, D = x.shape
    W = SSD_HEADS * SSD_HEAD_DIM
    g = {}
    add = lambda acc, r: (acc + r,)

    h0 = _rmsnorm(x, p["hyb_norm"], name="hyb_norm", dtype=BF16)
    proj = _matmul(h0, p["w_main_t"], mode="nt", name="hyb_proj", tm=1024, tn=1024, tk=512)
    proj_dt = _matmul(h0, p["w_dt_t"], mode="nt", name="hyb_proj_dt", tm=1024, tn=128, tk=512)
    ex, d_lanes, alog, bias = _ssd_consts(p["ssd_d"], p["ssd_a_log"], p["ssd_dt_bias"])
    xbc = _conv_silu_fwd(proj, 4 * W, p["conv_w"].shape[1], p["conv_w"], p["conv_b"])
    ypre, states = _ssd_fwd(xbc, proj_dt, bias, alog, ex, d_lanes)
    y_ssd = _gate_norm_fwd(ypre, proj, p["out_norm"])
    qh, kh = _qk_norm_fwd(proj, p["q_norm"], p["k_norm"], W)
    y_sb = _sb_fwd(qh, kh, proj, 3 * W // SB_DIM)
    x1 = _matmul(y_ssd, p["w_out"], mode="nn", name="hyb_out_a", tm=1024, tn=1024, tk=512, extras=(x,), epilogue=add,
                 mnk=(T, D, W))
    x1 = _matmul(y_sb, p["w_out"], mode="nn", name="hyb_out_b", tm=1024, tn=1024, tk=512, extras=(x1,), epilogue=add,
                 mnk=(T, D, W), b_spec=pl.BlockSpec((512, 1024), lambda i, j, k: (k + W // 512, j)))
    x2, mlp0 = _mlp_fwd(x1, p["mlp_norm"][0:1], p["w_up"], p["w_down"], 0)
    hp = _rmsnorm(x2, p["pool_norm"], name="pool_norm", dtype=F32)
    dpool = _pool_diff_fwd(hp)
    ypool, x3 = _pool_mm_fwd(dpool, p["pool_w"], p["pool_b"], p["pool_scale"], x2)
    x4, mlp1 = _mlp_fwd(x3, p["mlp_norm"][1:2], p["w_up"], p["w_down"], 1)

    dy, sq = _loss_grad(x4, target)
    dx3, dup1, ddown1, dgm1 = _mlp_bwd(dy, x3, p["mlp_norm"][1:2], p["w_up"], p["w_down"], mlp1, 1)
    dpre, g["pool_scale"], g["pool_b"] = _pool_scale_bwd(dx3, ypool, p["pool_scale"])
    g["pool_w"] = _pool_mm_dw(dpool, dpre)
    dhp = _pool_diff_bwd(_pool_mm_dx(dpre, p["pool_w"]))
    dx2, g["pool_norm"] = _rmsnorm_bwd(dhp, x2, p["pool_norm"], dx3, name="pool_norm_bwd")
    dx1, dup0, ddown0, dgm0 = _mlp_bwd(dx2, x1, p["mlp_norm"][0:1], p["w_up"], p["w_down"], mlp0, 0)
    g["mlp_w_up"] = [dup0, dup1]
    g["mlp_w_down"] = [ddown0, ddown1]
    g["mlp_norm"] = jnp.concatenate([dgm0, dgm1], axis=0)

    g["hyb_w_out"] = jnp.concatenate([
        _matmul(y_ssd, dx1, mode="tn", name="hyb_dwout_a", tm=1024, tn=1024, tk=512),
        _matmul(y_sb, dx1, mode="tn", name="hyb_dwout_b", tm=1024, tn=1024, tk=512)], axis=0)
    dmerged = _matmul(dx1, p["w_out"], mode="nt", name="hyb_dmerged", tm=1024, tn=1024, tk=512)
    dqh, dkh, dv = _sb_bwd(qh, kh, proj, 3 * W // SB_DIM, y_sb, dmerged, W // SB_DIM)
    dq, dk, g["sb_q_norm"], g["sb_k_norm"] = _qk_norm_bwd(dqh, dkh, proj, p["q_norm"], p["k_norm"], W)
    dypre, dz, g["ssd_out_norm"] = _gate_norm_bwd(dmerged, ypre, proj, p["out_norm"])
    dxs, dbm, dcm, ddt4, hv = _ssd_bwd(dypre, xbc, proj_dt, states, bias, alog, ex, d_lanes)
    g["ssd_d"], g["ssd_a_log"], g["ssd_dt_bias"] = (hv[i:i + 1, :SSD_HEADS] for i in range(3))
    ddt = ddt4.reshape(T, SSD_GROUPS, LANE).sum(axis=1).astype(BF16)
    dxbc, dconv_w, g["ssd_conv_b"] = _conv_silu_bwd(jnp.concatenate([dxs, dbm, dcm], axis=1), proj, 4 * W,
                                                    p["conv_w"].shape[1], p["conv_w"], p["conv_b"])
    g["ssd_conv_w"] = dconv_w[:4]
    dproj = jnp.concatenate([dz, dq, dk, dv.astype(BF16), dxbc], axis=1)
    g["w_main_t"] = _matmul(dproj, h0, mode="tn", name="hyb_dwin", tm=1024, tn=1024, tk=512)
    g["w_dt_t"] = _matmul(ddt, h0, mode="tn", name="hyb_dwdt", tm=128, tn=1024, tk=512)
    dh0 = _matmul(ddt, p["w_dt_t"], mode="nn", name="hyb_dh_dt", tm=1024, tn=1024, tk=128)
    dh0 = _matmul(dproj, p["w_main_t"], mode="nn", name="hyb_dh", tm=1024, tn=1024, tk=512, extras=(dh0,),
                  epilogue=add)
    grad_x, g["hyb_norm"] = _rmsnorm_bwd(dh0, x, p["hyb_norm"], dx1, name="hyb_norm_bwd")
    return sq, grad_x, g


ANY = pl.BlockSpec(memory_space=pl.ANY)


def _position():
    return lax.axis_index("x"), lax.axis_index("y"), lax.axis_index("c")


def _all_gather(vs, name):
    n = len(vs)

    def body(*refs):
        v_refs, out_refs = refs[:n], refs[n:2 * n]
        send_sems, recv_sems, local_sems = refs[2 * n:]
        x, y, c = _position()
        me, sibling = (x, y, c), (x, y, 1 - c)
        chips = [(1 - x, y), (x, 1 - y), (1 - x, 1 - y)]

        def rows(a, px, py, pc):
            return out_refs[a].at[4 * px + 2 * py + pc]

        def copy(a, k, block, to, src=None):
            return pltpu.make_async_remote_copy(
                src_ref=rows(a, *block) if src is None else src, dst_ref=rows(a, *block),
                send_sem=send_sems.at[7 * a + k], recv_sem=recv_sems.at[7 * a + k], device_id=to,
                device_id_type=MESH)

        mine = [pltpu.make_async_copy(v_refs[a], rows(a, *me), local_sems.at[a]) for a in range(n)]
        first = []
        for a in range(n):
            mine[a].start()
            first.append(copy(a, 0, me, sibling, src=v_refs[a]))
            first += [copy(a, 1 + j, me, (*chip, c), src=v_refs[a]) for j, chip in enumerate(chips)]
        for cp in first:
            cp.start()
        passed = []
        for j, chip in enumerate(chips):
            for a in range(n):
                copy(a, 1 + j, (*chip, c), me).wait_recv()
                passed.append(copy(a, 4 + j, (*chip, c), sibling))
                passed[-1].start()
        for a in range(n):
            copy(a, 0, sibling, me).wait_recv()
            for j, chip in enumerate(chips):
                copy(a, 4 + j, (*chip, 1 - c), me).wait_recv()
        for cp in first + passed:
            cp.wait_send()
        for cp in mine:
            cp.wait()

    return pl.pallas_call(
        body, name=name, out_shape=[jax.ShapeDtypeStruct((N_DEV,) + v.shape, v.dtype) for v in vs],
        in_specs=[ANY] * n, out_specs=[ANY] * n,
        scratch_shapes=[pltpu.SemaphoreType.DMA((7 * n,)), pltpu.SemaphoreType.DMA((7 * n,)),
                        pltpu.SemaphoreType.DMA((n,))])(*vs)


def _exchange_cores(gs, name):
    n = len(gs)

    def body(*refs):
        g_refs, r_refs = refs[:n], refs[n:2 * n]
        send_sems, recv_sems = refs[2 * n:]
        x, y, c = _position()
        copies = [pltpu.make_async_remote_copy(
            src_ref=g_refs[a].at[2 * k + (1 - c)], dst_ref=r_refs[a].at[k], send_sem=send_sems.at[4 * a + k],
            recv_sem=recv_sems.at[4 * a + k], device_id=(x, y, 1 - c), device_id_type=MESH)
            for a in range(n) for k in range(4)]
        for cp in copies:
            cp.start()
        for cp in copies:
            cp.wait_recv()
        for cp in copies:
            cp.wait_send()

    return pl.pallas_call(
        body, name=name, out_shape=[jax.ShapeDtypeStruct((4,) + g.shape[1:], g.dtype) for g in gs],
        in_specs=[ANY] * n, out_specs=[ANY] * n,
        scratch_shapes=[pltpu.SemaphoreType.DMA((4 * n,)), pltpu.SemaphoreType.DMA((4 * n,))])(*gs)


TILE_BYTES = 2 * 1024 * 1024


def _col_tile(R, C):
    tc = C
    while R * tc * 4 > TILE_BYTES and tc % (2 * LANE) == 0:
        tc //= 2
    return tc


def _pair_sum(gr, r1, name):
    _, R, C = gr.shape
    tc = _col_tile(R, C)
    x, y, c = _position()
    pos = jnp.stack([c, 2 * x + y]).astype(jnp.int32)

    def body(pos_ref, g_ref, r_ref, pb_ref, pm_ref):
        s = g_ref[...] + r_ref[...]
        pb_ref[...] = s.astype(BF16)

        @pl.when(pl.program_id(1) == pos_ref[1])
        def _():
            pm_ref[...] = s

    return _call(body, name=name, grid=(C // tc, 4), prefetch=1,
                 in_specs=[pl.BlockSpec((None, R, tc), lambda j, k, pos: (2 * k + pos[0], 0, j)),
                           pl.BlockSpec((None, R, tc), lambda j, k, pos: (k, 0, j))],
                 out_specs=[pl.BlockSpec((None, R, tc), lambda j, k, pos: (k, 0, j)),
                            pl.BlockSpec((R, tc), lambda j, k, pos: (0, j))],
                 out_shape=[jax.ShapeDtypeStruct((4, R, C), BF16), jax.ShapeDtypeStruct((R, C), F32)],
                 dimension_semantics=("arbitrary", "arbitrary"))(pos, gr, r1)


def _exchange_chips(pbs, name):
    n = len(pbs)

    def body(*refs):
        p_refs, r_refs = refs[:n], refs[n:2 * n]
        send_sems, recv_sems = refs[2 * n:]
        x, y, c = _position()
        chips = [(1 - x, y), (x, 1 - y), (1 - x, 1 - y)]
        mine = 2 * x + y

        def copy(a, j, src_row, dst_row):
            cx, cy = chips[j]
            return pltpu.make_async_remote_copy(
                src_ref=p_refs[a].at[src_row], dst_ref=r_refs[a].at[dst_row], send_sem=send_sems.at[3 * a + j],
                recv_sem=recv_sems.at[3 * a + j], device_id=(cx, cy, c), device_id_type=MESH)

        copies = [copy(a, j, 2 * cx + cy, mine) for a in range(n) for j, (cx, cy) in enumerate(chips)]
        for cp in copies:
            cp.start()
        for a in range(n):
            for j, (cx, cy) in enumerate(chips):
                copy(a, j, mine, 2 * cx + cy).wait_recv()
        for cp in copies:
            cp.wait_send()

    return pl.pallas_call(
        body, name=name, out_shape=[jax.ShapeDtypeStruct(p.shape, p.dtype) for p in pbs],
        in_specs=[ANY] * n, out_specs=[ANY] * n,
        scratch_shapes=[pltpu.SemaphoreType.DMA((3 * n,)), pltpu.SemaphoreType.DMA((3 * n,))])(*pbs)


def _adamw(w, grad, m, v):
    m = ADAM_B1 * m + (1.0 - ADAM_B1) * grad
    v = ADAM_B2 * v + (1.0 - ADAM_B2) * (grad * grad)
    m_hat = m / (1.0 - ADAM_B1 ** ADAM_STEP)
    v_hat = v / (1.0 - ADAM_B2 ** ADAM_STEP)
    delta = -ADAM_LR * (m_hat / (jnp.sqrt(v_hat) + ADAM_EPS) + ADAM_WD * w)
    return delta, m, v


def _other_chips():
    x, y, _ = _position()
    mine = 2 * x + y
    return jnp.stack([jnp.where(mine <= j, j + 1, j) for j in range(3)]).astype(jnp.int32)


def _adamw_sharded(pm, r2, w, m, v, name):
    R, C = pm.shape
    tc = _col_tile(R, C)
    update = w is not None

    def body(oth_ref, pm_ref, a_ref, b_ref, c_ref, *refs):
        grad = ((pm_ref[...] + a_ref[...].astype(F32)) + b_ref[...].astype(F32)) + c_ref[...].astype(F32)
        if update:
            w_ref, m_ref, v_ref, g_ref, d_ref, nm_ref, nv_ref = refs
            d, nm, nv = _adamw(w_ref[...], grad, m_ref[...], v_ref[...])
            g_ref[...], d_ref[...], nm_ref[...], nv_ref[...] = grad, d, nm, nv
        else:
            refs[0][...] = grad

    tile = pl.BlockSpec((R, tc), lambda j, oth: (0, j))
    other = [pl.BlockSpec((None, R, tc), functools.partial(lambda j, oth, q: (oth[q], 0, j), q=q)) for q in range(3)]
    shp = jax.ShapeDtypeStruct((R, C), F32)
    n_out = 4 if update else 1
    res = _call(body, name=name, grid=(C // tc,), prefetch=1,
                in_specs=[tile] + other + ([tile, tile, tile] if update else []), out_specs=[tile] * n_out,
                out_shape=[shp] * n_out, dimension_semantics=("arbitrary",))(
                    _other_chips(), pm, r2, r2, r2, *((w, m, v) if update else ()))
    return res if update else res[0]


def _adamw_plain(grad, w, m, v, name):
    R, C = w.shape
    tr = 256

    def body(g_ref, w_ref, m_ref, v_ref, d_ref, nm_ref, nv_ref):
        d_ref[...], nm_ref[...], nv_ref[...] = _adamw(w_ref[...], g_ref[...], m_ref[...], v_ref[...])

    tile = pl.BlockSpec((tr, C), lambda i: (i, 0))
    shp = jax.ShapeDtypeStruct((R, C), F32)
    return _call(body, name=name, grid=(R // tr,), in_specs=[tile] * 4, out_specs=[tile] * 3, out_shape=[shp] * 3,
                 dimension_semantics=("arbitrary",))(grad, w, m, v)


def _adamw_replicated(parts, w, m, v):
    _, R, C = parts.shape

    def body(p_ref, w_ref, m_ref, v_ref, g_ref, d_ref, nm_ref, nv_ref):
        grad = p_ref[0]
        for j in range(1, N_DEV):
            grad = grad + p_ref[j]
        d, nm, nv = _adamw(w_ref[...], grad, m_ref[...], v_ref[...])
        g_ref[...], d_ref[...], nm_ref[...], nv_ref[...] = grad, d, nm, nv

    tile = pl.BlockSpec((R, C), lambda i: (0, 0))
    shp = jax.ShapeDtypeStruct((R, C), F32)
    return _call(body, name="adamw_replicated", grid=(1,),
                 in_specs=[pl.BlockSpec((N_DEV, R, C), lambda i: (0, 0, 0)), tile, tile, tile],
                 out_specs=[tile] * 4, out_shape=[shp] * 4)(parts, w, m, v)


SMALL_SHARDED = ("pool_w", "ssd_conv_w", "pool_norm", "pool_b", "pool_scale")
REPLICATED = ("hyb_norm", "ssd_conv_b", "ssd_dt_bias", "ssd_a_log", "ssd_d", "ssd_out_norm", "sb_q_norm",
              "sb_k_norm", "mlp_norm")
PACK_COLS = 1024


def _pack(arrays, cols, row_multiple, dtype):
    flat = jnp.concatenate([a.reshape(-1).astype(dtype) for a in arrays])
    n = flat.shape[0]
    total = -(-n // (cols * row_multiple)) * cols * row_multiple
    return jnp.pad(flat, (0, total - n)).reshape(total // cols, cols)


def _unpack(packed, shapes):
    flat = packed.reshape(packed.shape[:-2] + (-1,))
    out, off = [], 0
    for s in shapes:
        n = math.prod(s)
        out.append(flat[..., off:off + n].reshape(flat.shape[:-1] + tuple(s)))
        off += n
    return out


def _shard_axis(name):
    return {"hyb_w_in": 2, "hyb_w_out": 1, "mlp_w_up": 2, "mlp_w_down": 1, "pool_w": 2, "ssd_conv_w": 2,
            "pool_norm": 1, "pool_b": 1, "pool_scale": 1}[name]


def _whole(blocks, name):
    ax = _shard_axis(name)
    moved = jnp.moveaxis(blocks, 0, ax)
    s = moved.shape
    return moved.reshape(s[:ax] + (s[ax] * s[ax + 1],) + s[ax + 2:])


def _to_blocks(whole, name):
    ax = _shard_axis(name)
    s = whole.shape
    split = whole.reshape(s[:ax] + (N_DEV, s[ax] // N_DEV) + s[ax + 1:])
    return jnp.moveaxis(split, ax, 0).reshape(N_DEV, -1)


def kernel(x, hyb_norm, hyb_w_in, ssd_conv_w, ssd_conv_b, ssd_dt_bias, ssd_a_log, ssd_d, ssd_out_norm, sb_q_norm, sb_k_norm, hyb_w_out, pool_norm, pool_w, pool_b, pool_scale, mlp_norm, mlp_w_up, mlp_w_down, loss_target, m_hyb_norm, m_hyb_w_in, m_ssd_conv_w, m_ssd_conv_b, m_ssd_dt_bias, m_ssd_a_log, m_ssd_d, m_ssd_out_norm, m_sb_q_norm, m_sb_k_norm, m_hyb_w_out, m_pool_norm, m_pool_w, m_pool_b, m_pool_scale, m_mlp_norm, m_mlp_w_up, m_mlp_w_down, v_hyb_norm, v_hyb_w_in, v_ssd_conv_w, v_ssd_conv_b, v_ssd_dt_bias, v_ssd_a_log, v_ssd_d, v_ssd_out_norm, v_sb_q_norm, v_sb_k_norm, v_hyb_w_out, v_pool_norm, v_pool_w, v_pool_b, v_pool_scale, v_mlp_norm, v_mlp_w_up, v_mlp_w_down):
    args = dict(locals())
    names = ("hyb_norm", "hyb_w_in", "ssd_conv_w", "ssd_conv_b", "ssd_dt_bias", "ssd_a_log", "ssd_d", "ssd_out_norm",
             "sb_q_norm", "sb_k_norm", "hyb_w_out", "pool_norm", "pool_w", "pool_b", "pool_scale", "mlp_norm",
             "mlp_w_up", "mlp_w_down")
    wt = {n: args[n] for n in names}
    T, D = x.shape[1], x.shape[2]
    W = SSD_HEADS * SSD_HEAD_DIM

    conv_dim = ssd_conv_b.shape[-1]
    c1, c2 = W + conv_dim, W + conv_dim + SSD_HEADS
    vec_names = SMALL_SHARDED[1:]

    gathered = _all_gather(
        [hyb_w_in[0].T.astype(BF16), hyb_w_out[0].astype(BF16), mlp_w_up.astype(BF16), mlp_w_down.astype(BF16),
         pool_w.astype(BF16), _pack([wt[n] for n in vec_names], LANE, 8, F32)], "gather_weights")
    in_t = gathered[0].reshape(-1, D)
    vec = {n: _whole(b, n) for n, b in zip(vec_names, _unpack(gathered[5], [wt[n].shape for n in vec_names]))}
    p = {
        "w_main_t": jnp.concatenate([in_t[:W], in_t[c2:], in_t[W:c1]], axis=0),
        "w_dt_t": jnp.pad(in_t[c1:c2], ((0, LANE - SSD_HEADS), (0, 0))),
        "w_out": gathered[1].reshape(-1, D), "w_up": gathered[2], "w_down": gathered[3],
        "pool_w": _whole(gathered[4], "pool_w")[0], "conv_w": vec["ssd_conv_w"][0], "conv_b": ssd_conv_b,
        "pool_norm": vec["pool_norm"], "pool_b": vec["pool_b"], "pool_scale": vec["pool_scale"],
        "hyb_norm": hyb_norm, "mlp_norm": mlp_norm, "out_norm": ssd_out_norm, "q_norm": sb_q_norm,
        "k_norm": sb_k_norm, "ssd_d": ssd_d, "ssd_a_log": ssd_a_log, "ssd_dt_bias": ssd_dt_bias,
    }

    sq, grad_x, g = _local_step(x[0], loss_target[0], p)
    loss = lax.psum(0.5 * jnp.sum(sq) / D, ("x", "y", "c"))

    gm = g["w_main_t"]
    g_in_t = jnp.concatenate([gm[:W], gm[4 * W:], g["w_dt_t"][:SSD_HEADS], gm[W:4 * W]], axis=0)
    gw = {"pool_w": g["pool_w"][None], "ssd_conv_w": g["ssd_conv_w"][None], "pool_norm": g["pool_norm"],
          "pool_b": g["pool_b"], "pool_scale": g["pool_scale"]}
    g_small = jnp.concatenate([_to_blocks(gw[n], n) for n in SMALL_SHARDED], axis=1)
    rows = -(-g_small.shape[1] // (8 * PACK_COLS)) * 8
    g_small = jnp.pad(g_small, ((0, 0), (0, rows * PACK_COLS - g_small.shape[1]))).reshape(N_DEV, rows, PACK_COLS)
    arrays = [g_in_t.reshape(N_DEV, -1, D), g["hyb_w_out"].reshape(N_DEV, -1, D), *g["mlp_w_up"],
              *[d.reshape(N_DEV, -1, D) for d in g["mlp_w_down"]], g_small]
    tags = ["w_in", "w_out", "w_up0", "w_up1", "w_down0", "w_down1", "small"]
    r1 = _exchange_cores(arrays, "grad_exchange_cores")
    sums = [_pair_sum(a, r, f"grad_pair_sum_{t}") for a, r, t in zip(arrays, r1, tags)]
    r2 = _exchange_chips([s[0] for s in sums], "grad_exchange_chips")
    pm = [s[1] for s in sums]

    res = {}
    grad_in = _adamw_sharded(pm[0], r2[0], None, None, None, "grad_sum_w_in").T
    res["hyb_w_in"] = [a[None] for a in (grad_in, *_adamw_plain(grad_in, hyb_w_in[0], m_hyb_w_in[0], v_hyb_w_in[0],
                                                                "adamw_w_in"))]
    res["hyb_w_out"] = [a[None] for a in _adamw_sharded(pm[1], r2[1], hyb_w_out[0], m_hyb_w_out[0], v_hyb_w_out[0],
                                                        "adamw_w_out")]
    for i, n in ((2, "mlp_w_up"), (4, "mlp_w_down")):
        layers = [_adamw_sharded(pm[i + l], r2[i + l], args[n][l], args["m_" + n][l], args["v_" + n][l],
                                 f"adamw_{n}{l}") for l in range(2)]
        res[n] = [jnp.stack([layers[0][k], layers[1][k]]) for k in range(4)]
    packed = [_pack([args[pre + n] for n in SMALL_SHARDED], PACK_COLS, 8, F32) for pre in ("", "m_", "v_")]
    shapes = [wt[n].shape for n in SMALL_SHARDED]
    small = [_unpack(o, shapes) for o in _adamw_sharded(pm[6], r2[6], *packed, "adamw_small")]
    for i, n in enumerate(SMALL_SHARDED):
        res[n] = [small[k][i] for k in range(4)]

    parts = _all_gather([_pack([g[n] for n in REPLICATED], LANE, 8, F32)], "gather_vector_grads")[0]
    packed = [_pack([args[pre + n] for n in REPLICATED], LANE, 8, F32) for pre in ("", "m_", "v_")]
    shapes = [wt[n].shape for n in REPLICATED]
    repl = [_unpack(o, shapes) for o in _adamw_replicated(parts, *packed)]
    for i, n in enumerate(REPLICATED):
        res[n] = [repl[k][i] for k in range(4)]

    outs = [res[n][k] for k in range(4) for n in names]
    return (loss, grad_x[None], *outs)
```

```python
import functools
import math
from typing import Callable, NamedTuple, Optional

import jax
import jax.numpy as jnp
from jax import lax
from jax.experimental import pallas as pl
from jax.experimental.pallas import tpu as pltpu

F32 = jnp.float32
BF16 = jnp.bfloat16
EPS = 1e-6
V7X_VMEM_LIMIT = 56 * 1024 * 1024
MESH = pl.DeviceIdType.MESH
ANY = pl.BlockSpec(memory_space=pl.ANY)
N_DEV = 8

SSD_HEADS = 32
SSD_HEAD_DIM = 64
SSD_STATE = 128
SSD_GROUPS = 4
SSD_CHUNK = 128
GROUP_W = SSD_HEADS * SSD_HEAD_DIM // SSD_GROUPS
HEADS_PER_GROUP = SSD_HEADS // SSD_GROUPS
SB_HEADS = 16
SB_DIM = 128
POOL_WINDOWS = (2, 4, 8, 16)
LANE = 128

ADAM_LR = 0.001
ADAM_B1 = 0.9
ADAM_B2 = 0.999
ADAM_EPS = 1e-08
ADAM_WD = 0.01
ADAM_STEP = 10

NN = (((1,), (0,)), ((), ()))
NT = (((1,), (1,)), ((), ()))
TN = (((0,), (0,)), ((), ()))

class _Job(NamedTuple):
    ins: tuple
    outs: tuple
    sems: tuple
    start: Callable
    mid: Optional[Callable]
    finish: Callable


def _call(body, *, name, grid, in_specs, out_specs, out_shape, scratch=(), prefetch=0, side=None, **params):
    if side is not None:
        single = not isinstance(out_shape, (list, tuple))
        out_specs = [out_specs] if single else list(out_specs)
        out_shape = [out_shape] if single else list(out_shape)
        n_in, n_out, n_scr = len(in_specs), len(out_shape), len(scratch)
        k_in, k_out = len(side.ins), len(side.outs)
        inner = body
        steps = math.prod(grid)

        def body(*refs):
            pre, rest = refs[:prefetch], refs[prefetch:]
            ins, s_in = rest[:n_in], rest[n_in:n_in + k_in]
            rest = rest[n_in + k_in:]
            outs, s_out = rest[:n_out], rest[n_out:n_out + k_out]
            rest = rest[n_out + k_out:]
            scr, s_sem = rest[:n_scr], rest[n_scr:]
            step = 0
            for axis, size in enumerate(grid):
                step = step * size + pl.program_id(axis)

            @pl.when(step == 0)
            def _():
                side.start(s_in, s_out, s_sem)

            inner(*pre, *ins, *outs, *scr)
            if side.mid is not None:
                @pl.when(step == (3 * steps) // 4)
                def _():
                    side.mid(s_in, s_out, s_sem)

            @pl.when(step == steps - 1)
            def _():
                side.finish(s_in, s_out, s_sem)

        params = dict(params, dimension_semantics=("arbitrary",) * len(grid))
        res = _call(body, name=name, grid=grid, in_specs=list(in_specs) + [ANY] * k_in,
                    out_specs=out_specs + [ANY] * k_out, out_shape=out_shape + list(side.outs),
                    scratch=list(scratch) + list(side.sems), prefetch=prefetch, **params)
        return lambda *args: (lambda r: ((r[0] if single else r[:n_out]), r[n_out:]))(res(*args, *side.ins))
    cp = pltpu.CompilerParams(vmem_limit_bytes=V7X_VMEM_LIMIT, **params)
    if prefetch:
        gs = pltpu.PrefetchScalarGridSpec(num_scalar_prefetch=prefetch, grid=grid, in_specs=in_specs,
                                          out_specs=out_specs, scratch_shapes=list(scratch))
        return pl.pallas_call(body, name=name, grid_spec=gs, out_shape=out_shape, compiler_params=cp)
    return pl.pallas_call(body, name=name, grid=grid, in_specs=in_specs, out_specs=out_specs,
                          out_shape=out_shape, scratch_shapes=list(scratch), compiler_params=cp)


def _dot(a, b, dims=NN):
    return lax.dot_general(a, b, dims, preferred_element_type=F32)


def _split3(x):
    hi = x.astype(BF16)
    r = x - hi.astype(F32)
    mid = r.astype(BF16)
    lo = (r - mid.astype(F32)).astype(BF16)
    return hi, mid, lo


def _dot3(x, m, dims=NN):
    hi, mid, lo = _split3(x)
    return _dot(hi, m, dims) + _dot(mid, m, dims) + _dot(lo, m, dims)


def _dot3l(m, x, dims=NN):
    hi, mid, lo = _split3(x)
    return _dot(m, hi, dims) + _dot(m, mid, dims) + _dot(m, lo, dims)


def _dot2(x, m):
    hi = x.astype(BF16)
    lo = (x - hi.astype(F32)).astype(BF16)
    return _dot(hi, m) + _dot(lo, m)


def _sigmoid(x):
    return 1.0 / (1.0 + jnp.exp(-x))


def _softplus(x):
    return jnp.maximum(x, 0.0) + jnp.log(1.0 + jnp.exp(-jnp.abs(x)))


def _iota(shape, dim):
    return lax.broadcasted_iota(jnp.int32, shape, dim)


def _matmul(a, b, *, mode, name, tm, tn, tk, extras=(), epilogue=None, out_dtypes=(F32,), mnk=None, b_spec=None,
            out_spec=None, out_dims=None, side=None):
    if mnk is not None:
        M, N, K = mnk
    elif mode == "tn":
        (K, M), N = a.shape, b.shape[1]
    else:
        (M, K), N = a.shape, b.shape[1 if mode == "nn" else 0]
    tm, tn, tk = min(tm, M), min(tn, N), min(tk, K)
    assert M % tm == 0 and N % tn == 0 and K % tk == 0, (name, M, N, K, tm, tn, tk)
    if mode == "nn":
        a_spec = pl.BlockSpec((tm, tk), lambda i, j, k: (i, k))
        b_spec = b_spec or pl.BlockSpec((tk, tn), lambda i, j, k: (k, j))
        dims = NN
    elif mode == "nt":
        a_spec = pl.BlockSpec((tm, tk), lambda i, j, k: (i, k))
        b_spec = b_spec or pl.BlockSpec((tn, tk), lambda i, j, k: (j, k))
        dims = NT
    else:
        a_spec = pl.BlockSpec((tk, tm), lambda i, j, k: (k, i))
        b_spec = b_spec or pl.BlockSpec((tk, tn), lambda i, j, k: (k, j))
        dims = TN
    nk = K // tk
    ex_specs = []
    for e in extras:
        if e.shape[0] == 1:
            ex_specs.append(pl.BlockSpec((1, tn), lambda i, j, k: (0, j)))
        else:
            ex_specs.append(pl.BlockSpec((tm, tn), lambda i, j, k: (i, j)))
    n_ex, n_out = len(extras), len(out_dtypes)

    def body(*refs):
        a_ref, b_ref = refs[0], refs[1]
        ex_refs = refs[2:2 + n_ex]
        o_refs = refs[2 + n_ex:2 + n_ex + n_out]
        acc = refs[-1]
        k = pl.program_id(2)

        @pl.when(k == 0)
        def _():
            acc[...] = jnp.zeros_like(acc)

        acc[...] += _dot(a_ref[...].astype(BF16), b_ref[...].astype(BF16), dims)

        @pl.when(k == nk - 1)
        def _():
            r = acc[...]
            outs = (r,) if epilogue is None else epilogue(r, *[e[...] for e in ex_refs])
            for o_ref, o in zip(o_refs, outs):
                o_ref[...] = o.astype(o_ref.dtype)

    out_shape = [jax.ShapeDtypeStruct(out_dims or (M, N), d) for d in out_dtypes]
    out_specs = [out_spec or pl.BlockSpec((tm, tn), lambda i, j, k: (i, j)) for _ in out_dtypes]
    res = _call(body, name=name, grid=(M // tm, N // tn, nk), in_specs=[a_spec, b_spec] + ex_specs,
                out_specs=out_specs, out_shape=out_shape, scratch=[pltpu.VMEM((tm, tn), F32)],
                dimension_semantics=("parallel", "parallel", "arbitrary"), side=side)(a, b, *extras)
    if side is not None:
        return (res[0] if n_out > 1 else res[0][0]), res[1]
    return res if n_out > 1 else res[0]


def _rowwise(fn, *, name, T, tm, tiles, vecs, out_tiles, out_vecs):
    n_t, n_v, n_ot, n_ov = len(tiles), len(vecs), len(out_tiles), len(out_vecs)

    def body(*refs):
        ins = [r[...] for r in refs[:n_t + n_v]]
        outs = fn(*ins)
        ot_refs = refs[n_t + n_v:n_t + n_v + n_ot]
        ov_refs = refs[n_t + n_v + n_ot:]
        for r, o in zip(ot_refs, outs[:n_ot]):
            r[...] = o.astype(r.dtype)
        if n_ov:
            first = pl.program_id(0) == 0

            @pl.when(first)
            def _():
                for r, o in zip(ov_refs, outs[n_ot:]):
                    r[...] = o

            @pl.when(jnp.logical_not(first))
            def _():
                for r, o in zip(ov_refs, outs[n_ot:]):
                    r[...] += o

    in_specs = [pl.BlockSpec((tm, w), functools.partial(lambda i, cb: (i, cb), cb=cb)) for _, w, cb in tiles]
    in_specs += [pl.BlockSpec(v.shape, lambda i: (0, 0)) for v in vecs]
    out_specs = [pl.BlockSpec((tm, w), lambda i: (i, 0)) for w, _ in out_tiles]
    out_specs += [pl.BlockSpec((r, w), lambda i: (0, 0)) for r, w in out_vecs]
    out_shape = [jax.ShapeDtypeStruct((T, w), d) for w, d in out_tiles]
    out_shape += [jax.ShapeDtypeStruct((r, w), F32) for r, w in out_vecs]
    return _call(body, name=name, grid=(T // tm,), in_specs=in_specs, out_specs=out_specs, out_shape=out_shape,
                 dimension_semantics=("arbitrary",))(*[t[0] for t in tiles], *vecs)


def _colsum(x):
    return jnp.sum(x, axis=0, keepdims=True)


def _rms_fwd(x, g):
    r = lax.rsqrt(jnp.mean(x * x, axis=-1, keepdims=True) + EPS)
    return x * r * g


def _rms_bwd(dh, x, g):
    r = lax.rsqrt(jnp.mean(x * x, axis=-1, keepdims=True) + EPS)
    xh = x * r
    dxh = dh * g
    dx = r * (dxh - xh * jnp.mean(dxh * xh, axis=-1, keepdims=True))
    return dx, _colsum(dh * xh)


def _rmsnorm(x, g, *, name, dtype):
    T, D = x.shape
    return _rowwise(lambda xv, gv: (_rms_fwd(xv, gv),), name=name, T=T, tm=256, tiles=[(x, D, 0)], vecs=[g],
                    out_tiles=[(D, dtype)], out_vecs=[])[0]


def _rmsnorm_bwd(dh, x, g, dres, *, name):
    T, D = x.shape

    def fn(dhv, xv, drv, gv):
        dx, dg = _rms_bwd(dhv, xv, gv)
        return drv + dx, dg

    return _rowwise(fn, name=name, T=T, tm=256, tiles=[(dh, D, 0), (x, D, 0), (dres, D, 0)], vecs=[g],
                    out_tiles=[(D, F32)], out_vecs=[(1, D)])


def _group_slices(width, group):
    return [slice(i, i + group) for i in range(0, width, group)]


def _gate_norm_fwd(ypre, proj, gain):
    T, W = ypre.shape

    def fn(y, z, g):
        gated = y * (z * _sigmoid(z))
        return (jnp.concatenate([_rms_fwd(gated[:, s], g[:, s]) for s in _group_slices(W, GROUP_W)], axis=1),)

    return _rowwise(fn, name="ssd_gate_norm", T=T, tm=256, tiles=[(ypre, W, 0), (proj, W, 0)], vecs=[gain],
                    out_tiles=[(W, BF16)], out_vecs=[])[0]


def _gate_norm_bwd(dmerged, ypre, proj, gain):
    T, W = ypre.shape

    def fn(do, y, z, g):
        sg = _sigmoid(z)
        sz = z * sg
        gated = y * sz
        parts = [_rms_bwd(do[:, s], gated[:, s], g[:, s]) for s in _group_slices(W, GROUP_W)]
        dgated = jnp.concatenate([p[0] for p in parts], axis=1)
        dgain = jnp.concatenate([p[1] for p in parts], axis=1)
        return dgated * sz, dgated * y * (sg * (1.0 + z * (1.0 - sg))), dgain

    return _rowwise(fn, name="ssd_gate_norm_bwd", T=T, tm=256, tiles=[(dmerged, W, 0), (ypre, W, 0), (proj, W, 0)],
                    vecs=[gain], out_tiles=[(W, F32), (W, BF16)], out_vecs=[(1, W)])


def _qk_norm_fwd(proj, qg, kg, W):
    T = proj.shape[0]

    def fn(q, k, gq, gk):
        sl = _group_slices(W, SB_DIM)
        return (jnp.concatenate([_rms_fwd(q[:, s], gq) for s in sl], axis=1),
                jnp.concatenate([_rms_fwd(k[:, s], gk) for s in sl], axis=1))

    return _rowwise(fn, name="sb_qk_norm", T=T, tm=256, tiles=[(proj, W, 1), (proj, W, 2)], vecs=[qg, kg],
                    out_tiles=[(W, BF16), (W, BF16)], out_vecs=[])


def _qk_norm_bwd(dqh, dkh, proj, qg, kg, W):
    T = proj.shape[0]

    def fn(dq, dk, q, k, gq, gk):
        sl = _group_slices(W, SB_DIM)
        pq = [_rms_bwd(dq[:, s], q[:, s], gq) for s in sl]
        pk = [_rms_bwd(dk[:, s], k[:, s], gk) for s in sl]
        return (jnp.concatenate([p[0] for p in pq], axis=1), jnp.concatenate([p[0] for p in pk], axis=1),
                sum(p[1] for p in pq), sum(p[1] for p in pk))

    return _rowwise(fn, name="sb_qk_norm_bwd", T=T, tm=256,
                    tiles=[(dqh, W, 0), (dkh, W, 0), (proj, W, 1), (proj, W, 2)], vecs=[qg, kg],
                    out_tiles=[(W, BF16), (W, BF16)], out_vecs=[(1, SB_DIM), (1, SB_DIM)])


def _loss_grad(y, target):
    T, D = y.shape

    def fn(yv, tv):
        err = yv - tv
        return err * (1.0 / D), _colsum(err * err)

    return _rowwise(fn, name="loss_grad", T=T, tm=256, tiles=[(y, D, 0), (target, D, 0)], vecs=[],
                    out_tiles=[(D, F32)], out_vecs=[(1, D)])


def _pool_scale_bwd(dx, ypre, scale):
    T, D = dx.shape

    def fn(d, yp, s):
        dpre = d * s
        return dpre, _colsum(d * yp), _colsum(dpre)

    return _rowwise(fn, name="pool_scale_bwd", T=T, tm=256, tiles=[(dx, D, 0), (ypre, D, 0)], vecs=[scale],
                    out_tiles=[(D, BF16)], out_vecs=[(1, D), (1, D)])


ROWS = 512


def _past(cur, prev, k):
    row = _iota(cur.shape, 0)
    rc = pltpu.roll(cur, k, 0)
    if prev is None:
        return jnp.where(row >= k, rc, 0.0)
    return jnp.where(row >= k, rc, pltpu.roll(prev, k, 0))


def _future(cur, nxt, k):
    n = cur.shape[0]
    row = _iota(cur.shape, 0)
    rc = pltpu.roll(cur, n - k, 0)
    if nxt is None:
        return jnp.where(row < n - k, rc, 0.0)
    return jnp.where(row < n - k, rc, pltpu.roll(nxt, n - k, 0))


def _chunk(ref, ci):
    return ref[ci * ROWS:(ci + 1) * ROWS, :]


def _conv_pre(x_ref, w, b, ci):
    cur = _chunk(x_ref, ci)
    prev = _chunk(x_ref, ci - 1) if ci > 0 else None
    taps = [_past(cur, prev, 3), _past(cur, prev, 2), _past(cur, prev, 1), cur]
    xc = b + sum(w[j:j + 1, :] * taps[j] for j in range(4))
    return xc, taps


CONV_COLS = 256


def _conv_silu_fwd(proj, col0, width, conv_w, conv_b):
    T = proj.shape[0]

    def body(x_ref, w_ref, b_ref, o_ref):
        w, b = w_ref[...], b_ref[...]
        for ci in range(T // ROWS):
            xc, _ = _conv_pre(x_ref, w, b, ci)
            o_ref[ci * ROWS:(ci + 1) * ROWS, :] = xc * _sigmoid(xc)

    cb0 = col0 // CONV_COLS
    return _call(body, name="ssd_conv_silu", grid=(width // CONV_COLS,),
                 in_specs=[pl.BlockSpec((T, CONV_COLS), lambda j: (0, cb0 + j)),
                           pl.BlockSpec((4, CONV_COLS), lambda j: (0, j)),
                           pl.BlockSpec((1, CONV_COLS), lambda j: (0, j))],
                 out_specs=pl.BlockSpec((T, CONV_COLS), lambda j: (0, j)),
                 out_shape=jax.ShapeDtypeStruct((T, width), F32))(proj, conv_w, conv_b)


def _conv_silu_bwd(dxa, proj, col0, width, conv_w, conv_b):
    T = proj.shape[0]
    nchunk = T // ROWS

    def body(d_ref, x_ref, w_ref, b_ref, dx_ref, dw_ref, db_ref, dxc_ref):
        w, b = w_ref[...], b_ref[...]
        dw = [jnp.zeros((1, CONV_COLS), F32) for _ in range(4)]
        db = jnp.zeros((1, CONV_COLS), F32)
        for ci in range(nchunk):
            xc, taps = _conv_pre(x_ref, w, b, ci)
            sg = _sigmoid(xc)
            dxc = _chunk(d_ref, ci) * (sg * (1.0 + xc * (1.0 - sg)))
            dxc_ref[ci * ROWS:(ci + 1) * ROWS, :] = dxc
            db = db + _colsum(dxc)
            dw = [dw[j] + _colsum(dxc * taps[j]) for j in range(4)]
        dw_ref[...] = jnp.concatenate(dw + [jnp.zeros((4, CONV_COLS), F32)], axis=0)
        db_ref[...] = db
        for ci in range(nchunk):
            cur = _chunk(dxc_ref, ci)
            nxt = _chunk(dxc_ref, ci + 1) if ci + 1 < nchunk else None
            dx = (w[3:4, :] * cur + w[2:3, :] * _future(cur, nxt, 1) + w[1:2, :] * _future(cur, nxt, 2)
                  + w[0:1, :] * _future(cur, nxt, 3))
            dx_ref[ci * ROWS:(ci + 1) * ROWS, :] = dx.astype(dx_ref.dtype)

    cb0 = col0 // CONV_COLS
    return _call(body, name="ssd_conv_silu_bwd", grid=(width // CONV_COLS,),
                 in_specs=[pl.BlockSpec((T, CONV_COLS), lambda j: (0, j)),
                           pl.BlockSpec((T, CONV_COLS), lambda j: (0, cb0 + j)),
                           pl.BlockSpec((4, CONV_COLS), lambda j: (0, j)),
                           pl.BlockSpec((1, CONV_COLS), lambda j: (0, j))],
                 out_specs=[pl.BlockSpec((T, CONV_COLS), lambda j: (0, j)),
                            pl.BlockSpec((8, CONV_COLS), lambda j: (0, j)),
                            pl.BlockSpec((1, CONV_COLS), lambda j: (0, j))],
                 out_shape=[jax.ShapeDtypeStruct((T, width), BF16), jax.ShapeDtypeStruct((8, width), F32),
                            jax.ShapeDtypeStruct((1, width), F32)],
                 scratch=[pltpu.VMEM((T, CONV_COLS), F32)])(dxa, proj, conv_w, conv_b)


def _window_count(ci, win, shape):
    t = (_iota(shape, 0) + ci * ROWS + 1).astype(F32)
    return jnp.minimum(t, float(win))


def _pool_diff_fwd(h):
    T, D = h.shape
    per_group = D // len(POOL_WINDOWS) // LANE

    def body(h_ref, o_ref):
        j = pl.program_id(0)
        for gi, win in enumerate(POOL_WINDOWS):
            @pl.when(j // per_group == gi)
            def _(win=win):
                for ci in range(T // ROWS):
                    cur = _chunk(h_ref, ci)
                    prev = _chunk(h_ref, ci - 1) if ci > 0 else None
                    s = cur
                    for k in range(1, win):
                        s = s + _past(cur, prev, k)
                    d = s / _window_count(ci, win, cur.shape) - cur
                    o_ref[ci * ROWS:(ci + 1) * ROWS, :] = d.astype(o_ref.dtype)

    return _call(body, name="pool_diff", grid=(D // LANE,), in_specs=[pl.BlockSpec((T, LANE), lambda j: (0, j))],
                 out_specs=pl.BlockSpec((T, LANE), lambda j: (0, j)),
                 out_shape=jax.ShapeDtypeStruct((T, D), BF16))(h)


def _pool_diff_bwd(dd):
    T, D = dd.shape
    per_group = D // len(POOL_WINDOWS) // LANE
    nchunk = T // ROWS

    def body(d_ref, o_ref):
        j = pl.program_id(0)
        for gi, win in enumerate(POOL_WINDOWS):
            @pl.when(j // per_group == gi)
            def _(win=win):
                for ci in range(nchunk):
                    cur = _chunk(d_ref, ci)
                    q = cur / _window_count(ci, win, cur.shape)
                    qn = None
                    if ci + 1 < nchunk:
                        qn = _chunk(d_ref, ci + 1) / _window_count(ci + 1, win, cur.shape)
                    s = q - cur
                    for k in range(1, win):
                        s = s + _future(q, qn, k)
                    o_ref[ci * ROWS:(ci + 1) * ROWS, :] = s

    return _call(body, name="pool_diff_bwd", grid=(D // LANE,), in_specs=[pl.BlockSpec((T, LANE), lambda j: (0, j))],
                 out_specs=pl.BlockSpec((T, LANE), lambda j: (0, j)),
                 out_shape=jax.ShapeDtypeStruct((T, D), F32))(dd)


def _pool_mm_fwd(d, w, b, scale, x):
    T, D = d.shape
    G = w.shape[1]
    tm = 512

    def body(d_ref, w_ref, b_ref, s_ref, x_ref, yp_ref, o_ref):
        yp = _dot(d_ref[...], w_ref[...]) + b_ref[...]
        yp_ref[...] = yp
        o_ref[...] = x_ref[...] + yp * s_ref[...]

    tile = pl.BlockSpec((tm, G), lambda i, g: (i, g))
    vec = pl.BlockSpec((1, G), lambda i, g: (0, g))
    return _call(body, name="pool_mm", grid=(T // tm, D // G),
                 in_specs=[tile, pl.BlockSpec((None, G, G), lambda i, g: (g, 0, 0)), vec, vec, tile],
                 out_specs=[tile, tile],
                 out_shape=[jax.ShapeDtypeStruct((T, D), F32), jax.ShapeDtypeStruct((T, D), F32)])(d, w, b, scale, x)


def _pool_mm_dx(dpre, w):
    T, D = dpre.shape
    G = w.shape[1]
    tm = 512

    def body(d_ref, w_ref, o_ref):
        o_ref[...] = _dot(d_ref[...], w_ref[...], NT)

    tile = pl.BlockSpec((tm, G), lambda i, g: (i, g))
    return _call(body, name="pool_mm_dx", grid=(T // tm, D // G),
                 in_specs=[tile, pl.BlockSpec((None, G, G), lambda i, g: (g, 0, 0))], out_specs=tile,
                 out_shape=jax.ShapeDtypeStruct((T, D), F32))(dpre, w)


def _pool_mm_dw(d, dpre):
    T, D = d.shape
    G = D // len(POOL_WINDOWS)
    tk = 512

    def body(d_ref, p_ref, o_ref):
        @pl.when(pl.program_id(1) == 0)
        def _():
            o_ref[...] = jnp.zeros_like(o_ref)

        o_ref[...] += _dot(d_ref[...], p_ref[...], TN)

    tile = pl.BlockSpec((tk, G), lambda g, k: (k, g))
    return _call(body, name="pool_mm_dw", grid=(D // G, T // tk), in_specs=[tile, tile],
                 out_specs=pl.BlockSpec((None, G, G), lambda g, k: (g, 0, 0)),
                 out_shape=jax.ShapeDtypeStruct((D // G, G, G), F32))(d, dpre)


def _ssd_consts(ssd_d, a_log, dt_bias):
    head = jnp.arange(LANE)[:, None]
    lane = jnp.arange(GROUP_W)[None, :]
    ex = jnp.stack([(head == g * HEADS_PER_GROUP + lane // SSD_HEAD_DIM) for g in range(SSD_GROUPS)])
    d_lanes = jnp.repeat(ssd_d.reshape(-1), SSD_HEAD_DIM).reshape(1, -1)
    pad = lambda v: jnp.pad(v.reshape(1, -1), ((0, 0), (0, LANE - SSD_HEADS)))
    return ex.astype(BF16), d_lanes, pad(a_log), pad(dt_bias)


def _ssd_chunk(xs, bm, cm, dtr, bias, alog, ex):
    L = SSD_CHUNK
    row, col = _iota((L, L), 0), _iota((L, L), 1)
    causal = col <= row
    ltri = causal.astype(BF16)
    a_row = -jnp.exp(alog)
    dt = _softplus(dtr + bias)
    da = dt * a_row
    dt_l = _dot3(dt, ex)
    da_l = _dot3(da, ex)
    acs_l = _dot3l(ltri, da_l)
    acs_r = _dot3(da, (row <= col).astype(BF16), TN)
    last_l = acs_l[L - 1:L, :]
    e_l = jnp.exp(last_l - acs_l)
    f_l = jnp.exp(acs_l)
    cd_l = jnp.exp(last_l)
    xdt = xs * dt_l
    cb = _dot(cm.astype(BF16), bm.astype(BF16), NT)
    return dict(causal=causal, dt=dt, da=da, dt_l=dt_l, acs_l=acs_l, acs_r=acs_r, e_l=e_l, f_l=f_l, cd_l=cd_l,
                xdt=xdt, cb=cb, a_row=a_row, ltri=ltri)


def _head_decay(q, acsrow_ref, g, r):
    colv = q["acs_l"][:, r * SSD_HEAD_DIM:r * SSD_HEAD_DIM + 1]
    rowv = acsrow_ref[pl.ds(g * HEADS_PER_GROUP + r, 1), :]
    return jnp.exp(jnp.where(q["causal"], colv - rowv, -1e30))


def _ssd_specs(T):
    L = SSD_CHUNK
    xs = pl.BlockSpec((L, GROUP_W), lambda g, c: (c, g))
    nb = SSD_HEADS * SSD_HEAD_DIM // SSD_STATE
    bm = pl.BlockSpec((L, SSD_STATE), lambda g, c: (c, nb + g))
    cm = pl.BlockSpec((L, SSD_STATE), lambda g, c: (c, nb + SSD_GROUPS + g))
    dtr = pl.BlockSpec((L, LANE), lambda g, c: (c, 0))
    vec = pl.BlockSpec((1, LANE), lambda g, c: (0, 0))
    ex = pl.BlockSpec((None, LANE, GROUP_W), lambda g, c: (g, 0, 0))
    dl = pl.BlockSpec((1, GROUP_W), lambda g, c: (0, g))
    return xs, bm, cm, dtr, vec, ex, dl


def _ssd_fwd(xbc, proj_dt, bias, alog, ex, d_lanes):
    T = xbc.shape[0]
    L, nc, W = SSD_CHUNK, T // SSD_CHUNK, SSD_HEADS * SSD_HEAD_DIM

    def body(xs_ref, b_ref, c_ref, dtr_ref, bias_ref, alog_ref, ex_ref, dl_ref, y_ref, st_ref, state, acsrow):
        g, c = pl.program_id(0), pl.program_id(1)

        @pl.when(c == 0)
        def _():
            state[...] = jnp.zeros_like(state)

        xs, bm, cm = xs_ref[...], b_ref[...], c_ref[...]
        q = _ssd_chunk(xs, bm, cm, dtr_ref[...], bias_ref[...], alog_ref[...], ex_ref[...])
        acsrow[...] = q["acs_r"]
        prev = state[...]
        st_ref[...] = prev
        xdt_b = q["xdt"].astype(BF16)
        yoff = q["f_l"] * _dot(cm.astype(BF16), prev.astype(BF16))
        lane = _iota((L, LANE), 1)
        for p in range(HEADS_PER_GROUP // 2):
            sl = slice(p * LANE, (p + 1) * LANE)
            ma = (_head_decay(q, acsrow, g, 2 * p) * q["cb"]).astype(BF16)
            mb = (_head_decay(q, acsrow, g, 2 * p + 1) * q["cb"]).astype(BF16)
            yd = jnp.where(lane < SSD_HEAD_DIM, _dot(ma, xdt_b[:, sl]), _dot(mb, xdt_b[:, sl]))
            y_ref[:, sl] = yd + yoff[:, sl] + dl_ref[:, sl] * xs[:, sl]
        st_new = _dot(bm.astype(BF16), (q["xdt"] * q["e_l"]).astype(BF16), TN)
        state[...] = q["cd_l"] * prev + st_new

    xs, bm, cm, dtr, vec, exs, dl = _ssd_specs(T)
    return _call(body, name="ssd_scan", grid=(SSD_GROUPS, nc), in_specs=[xs, bm, cm, dtr, vec, vec, exs, dl],
                 out_specs=[pl.BlockSpec((L, GROUP_W), lambda g, c: (c, g)),
                            pl.BlockSpec((None, None, SSD_STATE, GROUP_W), lambda g, c: (c, g, 0, 0))],
                 out_shape=[jax.ShapeDtypeStruct((T, W), F32),
                            jax.ShapeDtypeStruct((nc, SSD_GROUPS, SSD_STATE, GROUP_W), F32)],
                 scratch=[pltpu.VMEM((SSD_STATE, GROUP_W), F32), pltpu.VMEM((LANE, L), F32)],
                 dimension_semantics=("arbitrary", "arbitrary"))(xbc, xbc, xbc, proj_dt, bias, alog, ex, d_lanes)


def _ssd_bwd(dy, xbc, proj_dt, states, bias, alog, ex, d_lanes):
    T = xbc.shape[0]
    L, nc, W = SSD_CHUNK, T // SSD_CHUNK, SSD_HEADS * SSD_HEAD_DIM
    P = SSD_HEAD_DIM

    def body(dy_ref, xs_ref, b_ref, c_ref, dtr_ref, st_ref, bias_ref, alog_ref, ex_ref, dl_ref,
             dxs_ref, db_ref, dc_ref, ddt_ref, hv_ref, dstate, acsrow):
        g, c = pl.program_id(0), pl.program_id(1)

        @pl.when(c == 0)
        def _():
            dstate[...] = jnp.zeros_like(dstate)

        @pl.when(jnp.logical_and(g == 0, c == 0))
        def _():
            hv_ref[...] = jnp.zeros_like(hv_ref)

        xs, bm, cm, ex = xs_ref[...], b_ref[...], c_ref[...], ex_ref[...]
        dtr, bias = dtr_ref[...], bias_ref[...]
        q = _ssd_chunk(xs, bm, cm, dtr, bias, alog_ref[...], ex)
        acsrow[...] = q["acs_r"]
        dyv = dy_ref[...]
        prev = st_ref[...]
        dst = dstate[...]
        bm_b, cm_b = bm.astype(BF16), cm.astype(BF16)
        row128 = _iota((L, LANE), 0)
        lane = _iota((L, LANE), 1)
        row_w = _iota((L, GROUP_W), 0)

        dxs = dl_ref[...] * dyv
        d_dl = _colsum(dyv * xs)
        gmat = _dot(cm_b, prev.astype(BF16))
        dg_b = (dyv * q["f_l"]).astype(BF16)
        dacs = dyv * q["f_l"] * gmat
        dcm = _dot(dg_b, prev.astype(BF16), NT)
        dprev = _dot(cm_b, dg_b, TN)
        dcd = _colsum(dst * prev)
        dlast = dcd * q["cd_l"]
        xe = q["xdt"] * q["e_l"]
        dxe = _dot(bm_b, dst.astype(BF16))
        dbm = _dot(xe.astype(BF16), dst.astype(BF16), NT)
        dxdt = dxe * q["e_l"]
        t1 = dxe * xe
        dacs = dacs - t1
        dlast = dlast + _colsum(t1)
        dstate[...] = dprev + q["cd_l"] * dst
        xdt_b = q["xdt"].astype(BF16)
        dcb = jnp.zeros((L, L), F32)
        dacs_head = []
        dxdt_diag = []
        for p in range(HEADS_PER_GROUP // 2):
            sl = slice(p * LANE, (p + 1) * LANE)
            xp = xdt_b[:, sl]
            dyp = dyv[:, sl]
            vals, dx_parts = [], []
            for half in range(2):
                in_half = (lane < P) if half == 0 else (lane >= P)
                decay = _head_decay(q, acsrow, g, 2 * p + half)
                m = decay * q["cb"]
                dyh = jnp.where(in_half, dyp, 0.0).astype(BF16)
                dm = jnp.where(q["causal"], _dot(dyh, xp, NT), 0.0)
                dcb = dcb + dm * decay
                dseg = dm * m
                rs = jnp.sum(dseg, axis=1, keepdims=True)
                cs = jnp.broadcast_to(_colsum(dseg), (L, L)).T[:, 0:1]
                vals.append(rs - cs)
                dx_parts.append(_dot(m.astype(BF16), dyp.astype(BF16), TN))
            dxdt_diag.append(jnp.where(lane < P, dx_parts[0], dx_parts[1]))
            dacs_head.append(jnp.where(lane == 0, vals[0], jnp.where(lane == P, vals[1], 0.0)))
        dxdt = dxdt + jnp.concatenate(dxdt_diag, axis=1)
        dacs = dacs + jnp.concatenate(dacs_head, axis=1)
        dacs = jnp.where(row_w == L - 1, dacs + dlast, dacs)
        dcb_b = dcb.astype(BF16)
        dcm = dcm + _dot(dcb_b, bm_b)
        dbm = dbm + _dot(dcb_b, cm_b, TN)
        dacs_h = _dot3(dacs, ex, NT)
        dda = _dot3l((row128 <= lane).astype(BF16), dacs_h)
        ddt = dda * q["a_row"] + _dot3(dxdt * xs, ex, NT)
        dxs = dxs + dxdt * q["dt_l"]
        ddtr = ddt * _sigmoid(dtr + bias)
        d_alog = _colsum(dda * q["dt"]) * q["a_row"]
        d_dh = _dot3(jnp.broadcast_to(d_dl, (8, GROUP_W)), ex, NT)[0:1, :]
        dxs_ref[...] = dxs
        db_ref[...] = dbm
        dc_ref[...] = dcm
        ddt_ref[...] = ddtr
        hv_ref[0:1, :] += d_dh
        hv_ref[1:2, :] += d_alog
        hv_ref[2:3, :] += _colsum(ddtr)

    rev = lambda c: nc - 1 - c
    xs = pl.BlockSpec((L, GROUP_W), lambda g, c: (rev(c), g))
    nb = W // SSD_STATE
    bm = pl.BlockSpec((L, SSD_STATE), lambda g, c: (rev(c), nb + g))
    cm = pl.BlockSpec((L, SSD_STATE), lambda g, c: (rev(c), nb + SSD_GROUPS + g))
    dtr = pl.BlockSpec((L, LANE), lambda g, c: (rev(c), 0))
    st = pl.BlockSpec((None, None, SSD_STATE, GROUP_W), lambda g, c: (rev(c), g, 0, 0))
    vec = pl.BlockSpec((1, LANE), lambda g, c: (0, 0))
    exs = pl.BlockSpec((None, LANE, GROUP_W), lambda g, c: (g, 0, 0))
    dl = pl.BlockSpec((1, GROUP_W), lambda g, c: (0, g))
    grp = pl.BlockSpec((L, SSD_STATE), lambda g, c: (rev(c), g))
    return _call(body, name="ssd_scan_bwd", grid=(SSD_GROUPS, nc),
                 in_specs=[xs, xs, bm, cm, dtr, st, vec, vec, exs, dl],
                 out_specs=[xs, grp, grp, grp, pl.BlockSpec((8, LANE), lambda g, c: (0, 0))],
                 out_shape=[jax.ShapeDtypeStruct((T, W), F32),
                            jax.ShapeDtypeStruct((T, SSD_GROUPS * SSD_STATE), F32),
                            jax.ShapeDtypeStruct((T, SSD_GROUPS * SSD_STATE), F32),
                            jax.ShapeDtypeStruct((T, SSD_GROUPS * LANE), F32),
                            jax.ShapeDtypeStruct((8, LANE), F32)],
                 scratch=[pltpu.VMEM((SSD_STATE, GROUP_W), F32), pltpu.VMEM((LANE, L), F32)],
                 dimension_semantics=("arbitrary", "arbitrary"))(dy, xbc, xbc, xbc, proj_dt, states, bias, alog,
                                                                 ex, d_lanes)


SB_TQ = 512
SB_SUB = 128


def _sb_logits(q, kb, qi, kstart, scale):
    z = _dot(q, kb, NT) * scale
    lb = jnp.minimum(z, 0.0) - jnp.log(1.0 + jnp.exp(-jnp.abs(z)))
    mask = (_iota(z.shape, 1) + kstart) < (_iota(z.shape, 0) + qi * SB_TQ)
    lk = jnp.where(mask, lb - z, 0.0)
    return lb, lk, mask


def _sb_weights(lb, lk, mask, run):
    n = SB_SUB
    strict = (_iota((n, n), 0) > _iota((n, n), 1)).astype(BF16)
    ws = [None] * (lb.shape[1] // n)
    for s in reversed(range(len(ws))):
        sl = slice(s * n, (s + 1) * n)
        c = _dot2(lk[:, sl], strict) + run
        ws[s] = jnp.where(mask[:, sl], jnp.exp(lb[:, sl] + c), 0.0)
        run = run + jnp.sum(lk[:, sl], axis=1, keepdims=True)
    return jnp.concatenate(ws, axis=1), run


def _sb_fwd(qh, kh, proj, v_cb0, side=None):
    T, W = qh.shape
    tq = min(SB_TQ, T)
    scale = SB_DIM ** -0.5

    def body(q_ref, k_ref, v_ref, o_ref):
        qi = pl.program_id(1)
        q = q_ref[...]

        def step(it, carry):
            run, acc = carry
            kstart = pl.multiple_of((qi - it) * tq, tq)
            lb, lk, mask = _sb_logits(q, k_ref[pl.ds(kstart, tq), :], qi, kstart, scale)
            w, run = _sb_weights(lb, lk, mask, run)
            acc = acc + _dot2(w, v_ref[pl.ds(kstart, tq), :].astype(BF16))
            return run, acc

        _, acc = lax.fori_loop(0, qi + 1, step, (jnp.zeros((tq, 1), F32), jnp.zeros((tq, SB_DIM), F32)))
        o_ref[...] = acc

    return _call(body, name="sb_attn", grid=(W // SB_DIM, T // tq),
                 in_specs=[pl.BlockSpec((tq, SB_DIM), lambda h, i: (i, h)),
                           pl.BlockSpec((T, SB_DIM), lambda h, i: (0, h)),
                           pl.BlockSpec((T, SB_DIM), lambda h, i: (0, v_cb0 + h))],
                 out_specs=pl.BlockSpec((tq, SB_DIM), lambda h, i: (i, h)),
                 out_shape=jax.ShapeDtypeStruct((T, W), F32), side=side,
                 dimension_semantics=("arbitrary", "arbitrary"))(qh, kh, proj)


def _sb_bwd(qh, kh, proj, v_cb0, o, dmerged, do_cb0, side=None):
    T, W = qh.shape
    tq = min(SB_TQ, T)
    n = SB_SUB
    scale = SB_DIM ** -0.5

    def body(q_ref, k_ref, v_ref, o_ref, do_ref, dq_ref, dk_ref, dv_ref):
        qi = pl.program_id(1)

        @pl.when(qi == 0)
        def _():
            dk_ref[...] = jnp.zeros_like(dk_ref)
            dv_ref[...] = jnp.zeros_like(dv_ref)

        q = q_ref[...]
        do = do_ref[...]
        do_b = do.astype(BF16)
        etot = jnp.sum(do_b.astype(F32) * o_ref[...], axis=1, keepdims=True)
        incl = (_iota((n, n), 0) >= _iota((n, n), 1)).astype(BF16)

        def step(it, carry):
            run, erun, dq = carry
            kstart = pl.multiple_of((qi - it) * tq, tq)
            kb = k_ref[pl.ds(kstart, tq), :]
            vb = v_ref[pl.ds(kstart, tq), :].astype(BF16)
            lb, lk, mask = _sb_logits(q, kb, qi, kstart, scale)
            w, run = _sb_weights(lb, lk, mask, run)
            e = _dot(do_b, vb, NT) * w
            beta = jnp.exp(lb)
            dzs = [None] * (tq // n)
            for s in reversed(range(tq // n)):
                sl = slice(s * n, (s + 1) * n)
                before = etot - erun - _dot3(e[:, sl], incl)
                dzs[s] = jnp.where(mask[:, sl], e[:, sl] * (1.0 - beta[:, sl]) - before * beta[:, sl], 0.0)
                erun = erun + jnp.sum(e[:, sl], axis=1, keepdims=True)
            dz = (jnp.concatenate(dzs, axis=1) * scale).astype(BF16)
            dq = dq + _dot(dz, kb)
            dk_ref[pl.ds(kstart, tq), :] += _dot(dz, q, TN)
            dv_ref[pl.ds(kstart, tq), :] += _dot(w.astype(BF16), do_b, TN)
            return run, erun, dq

        zero = jnp.zeros((tq, 1), F32)
        _, _, dq = lax.fori_loop(0, qi + 1, step, (zero, zero, jnp.zeros((tq, SB_DIM), F32)))
        dq_ref[...] = dq

    tile = pl.BlockSpec((tq, SB_DIM), lambda h, i: (i, h))
    full = pl.BlockSpec((T, SB_DIM), lambda h, i: (0, h))
    shp = jax.ShapeDtypeStruct((T, W), F32)
    return _call(body, name="sb_attn_bwd", grid=(W // SB_DIM, T // tq),
                 in_specs=[tile, full, pl.BlockSpec((T, SB_DIM), lambda h, i: (0, v_cb0 + h)), tile,
                           pl.BlockSpec((tq, SB_DIM), lambda h, i: (i, do_cb0 + h))],
                 out_specs=[tile, full, full], out_shape=[shp, shp, shp], side=side,
                 dimension_semantics=("arbitrary", "arbitrary"))(qh, kh, proj, o, dmerged)


def _mlp_fwd(x, g, w_up, w_down, layer):
    T, D = x.shape
    fs = w_up.shape[3]
    F = N_DEV * fs
    tk = 512
    per = fs // tk
    h = _rmsnorm(x, g, name=f"mlp{layer}_norm", dtype=BF16)

    def relu_sq(acc):
        u = jnp.maximum(acc, 0.0)
        return u, u * u

    u, s = _matmul(h, w_up, mode="nn", name=f"mlp{layer}_up", tm=1024, tn=fs, tk=tk, epilogue=relu_sq,
                   out_dtypes=(BF16, BF16), mnk=(T, F, D),
                   b_spec=pl.BlockSpec((None, None, tk, fs), lambda i, j, k: (j, layer, k, 0)))
    y = _matmul(s, w_down, mode="nn", name=f"mlp{layer}_down", tm=1024, tn=1024, tk=tk, extras=(x,),
                epilogue=lambda acc, r: (acc + r,), mnk=(T, D, F),
                b_spec=pl.BlockSpec((None, None, tk, 1024), lambda i, j, k: (k // per, layer, k % per, j)))
    return y, (h, u, s)


def _mlp_bwd(dy, x, g, w_up, w_down, saved, layer, side=None):
    T, D = x.shape
    fs = w_up.shape[3]
    F = N_DEV * fs
    tk = 512
    per = fs // tk
    h, u, s = saved
    dw_down = _matmul(s, dy, mode="tn", name=f"mlp{layer}_dwdown", tm=1024, tn=1024, tk=tk, side=side)
    if side is not None:
        dw_down, side_res = dw_down
    da = _matmul(dy, w_down, mode="nt", name=f"mlp{layer}_da", tm=1024, tn=fs, tk=tk, extras=(u,),
                 epilogue=lambda acc, uv: (acc * (2.0 * uv.astype(F32)),), out_dtypes=(BF16,), mnk=(T, F, D),
                 b_spec=pl.BlockSpec((None, None, fs, tk), lambda i, j, k: (j, layer, 0, k)))
    dw_up = _matmul(h, da, mode="tn", name=f"mlp{layer}_dwup", tm=1024, tn=fs, tk=tk, out_dims=(N_DEV, D, fs),
                    out_spec=pl.BlockSpec((None, 1024, fs), lambda i, j, k: (j, i, 0)))
    dh = _matmul(da, w_up, mode="nt", name=f"mlp{layer}_dh", tm=1024, tn=1024, tk=tk, mnk=(T, D, F),
                 b_spec=pl.BlockSpec((None, None, 1024, tk), lambda i, j, k: (k // per, layer, j, k % per)))
    dx, dg = _rmsnorm_bwd(dh, x, g, dy, name=f"mlp{layer}_norm_bwd")
    if side is not None:
        return dx, dw_up, dw_down, dg, side_res
    return dx, dw_up, dw_down, dg


def _local_step(x, target, p, late):
    T, D = x.shape
    W = SSD_HEADS * SSD_HEAD_DIM
    g = {}
    add = lambda acc, r: (acc + r,)

    h0 = _rmsnorm(x, p["hyb_norm"], name="hyb_norm", dtype=BF16)
    proj = _matmul(h0, p["w_main_t"], mode="nt", name="hyb_proj", tm=1024, tn=1024, tk=512)
    proj_dt = _matmul(h0, p["w_dt_t"], mode="nt", name="hyb_proj_dt", tm=1024, tn=128, tk=512)
    ex, d_lanes, alog, bias = _ssd_consts(p["ssd_d"], p["ssd_a_log"], p["ssd_dt_bias"])
    xbc = _conv_silu_fwd(proj, 4 * W, p["conv_w"].shape[1], p["conv_w"], p["conv_b"])
    ypre, states = _ssd_fwd(xbc, proj_dt, bias, alog, ex, d_lanes)
    y_ssd = _gate_norm_fwd(ypre, proj, p["out_norm"])
    qh, kh = _qk_norm_fwd(proj, p["q_norm"], p["k_norm"], W)
    y_sb, (w_up, w_down, pool_blocks) = _sb_fwd(qh, kh, proj, 3 * W // SB_DIM, side=_gather_job(late))
    pool_w = _whole(pool_blocks, "pool_w")[0]
    x1 = _matmul(y_ssd, p["w_out"], mode="nn", name="hyb_out_a", tm=1024, tn=1024, tk=512, extras=(x,), epilogue=add,
                 mnk=(T, D, W))
    x1 = _matmul(y_sb, p["w_out"], mode="nn", name="hyb_out_b", tm=1024, tn=1024, tk=512, extras=(x1,), epilogue=add,
                 mnk=(T, D, W), b_spec=pl.BlockSpec((512, 1024), lambda i, j, k: (k + W // 512, j)))
    x2, mlp0 = _mlp_fwd(x1, p["mlp_norm"][0:1], w_up, w_down, 0)
    hp = _rmsnorm(x2, p["pool_norm"], name="pool_norm", dtype=F32)
    dpool = _pool_diff_fwd(hp)
    ypool, x3 = _pool_mm_fwd(dpool, pool_w, p["pool_b"], p["pool_scale"], x2)
    x4, mlp1 = _mlp_fwd(x3, p["mlp_norm"][1:2], w_up, w_down, 1)

    dy, sq = _loss_grad(x4, target)
    dx3, dup1, ddown1, dgm1 = _mlp_bwd(dy, x3, p["mlp_norm"][1:2], w_up, w_down, mlp1, 1)
    dpre, dpool_scale, dpool_b = _pool_scale_bwd(dx3, ypool, p["pool_scale"])
    gw = {"pool_w": _pool_mm_dw(dpool, dpre)[None], "pool_b": dpool_b, "pool_scale": dpool_scale}
    dhp = _pool_diff_bwd(_pool_mm_dx(dpre, pool_w))
    dx2, gw["pool_norm"] = _rmsnorm_bwd(dhp, x2, p["pool_norm"], dx3, name="pool_norm_bwd")
    first = [dup1, ddown1.reshape(N_DEV, -1, D)]
    dx1, dup0, ddown0, dgm0, r1_first = _mlp_bwd(dx2, x1, p["mlp_norm"][0:1], w_up, w_down, mlp0, 0,
                                                side=_cores_job(first))
    g["mlp_norm"] = jnp.concatenate([dgm0, dgm1], axis=0)
    dw_out = jnp.concatenate([
        _matmul(y_ssd, dx1, mode="tn", name="hyb_dwout_a", tm=1024, tn=1024, tk=512),
        _matmul(y_sb, dx1, mode="tn", name="hyb_dwout_b", tm=1024, tn=1024, tk=512)], axis=0)
    second = [dup0, ddown0.reshape(N_DEV, -1, D), dw_out.reshape(N_DEV, -1, D),
              _shard_rows([_to_blocks(gw[n], n) for n in POOL_SHARDED], PACK_COLS)]
    dmerged, r1_second = _matmul(dx1, p["w_out"], mode="nt", name="hyb_dmerged", tm=1024, tn=1024, tk=512,
                                 side=_cores_job(second))
    early = dict(zip(("w_up1", "w_down1", "w_up0", "w_down0", "w_out", "pool"),
                     zip(first + second, list(r1_first) + list(r1_second))))
    sums = {t: _pair_sum(a, r, f"grad_pair_sum_{t}") for t, (a, r) in early.items()}
    (dqh, dkh, dv), r2_early = _sb_bwd(qh, kh, proj, 3 * W // SB_DIM, y_sb, dmerged, W // SB_DIM,
                                       side=_chips_job([s[0] for s in sums.values()]))
    reduced = {t: (s[1], r2) for (t, s), r2 in zip(sums.items(), r2_early)}
    dq, dk, g["sb_q_norm"], g["sb_k_norm"] = _qk_norm_bwd(dqh, dkh, proj, p["q_norm"], p["k_norm"], W)
    dypre, dz, g["ssd_out_norm"] = _gate_norm_bwd(dmerged, ypre, proj, p["out_norm"])
    dxs, dbm, dcm, ddt4, hv = _ssd_bwd(dypre, xbc, proj_dt, states, bias, alog, ex, d_lanes)
    g["ssd_d"], g["ssd_a_log"], g["ssd_dt_bias"] = (hv[i:i + 1, :SSD_HEADS] for i in range(3))
    ddt = ddt4.reshape(T, SSD_GROUPS, LANE).sum(axis=1).astype(BF16)
    dxbc, dconv_w, g["ssd_conv_b"] = _conv_silu_bwd(jnp.concatenate([dxs, dbm, dcm], axis=1), proj, 4 * W,
                                                    p["conv_w"].shape[1], p["conv_w"], p["conv_b"])
    dproj = jnp.concatenate([dz, dq, dk, dv.astype(BF16), dxbc], axis=1)
    gm = _matmul(dproj, h0, mode="tn", name="hyb_dwin", tm=1024, tn=1024, tk=512)
    g_dt = _matmul(ddt, h0, mode="tn", name="hyb_dwdt", tm=128, tn=1024, tk=512)
    g_in_t = jnp.concatenate([gm[:W], gm[4 * W:], g_dt[:SSD_HEADS], gm[W:4 * W]], axis=0)
    last = [g_in_t.reshape(N_DEV, -1, D), _shard_rows([_to_blocks(dconv_w[:4][None], "ssd_conv_w")], LANE)]
    dh0 = _matmul(ddt, p["w_dt_t"], mode="nn", name="hyb_dh_dt", tm=1024, tn=1024, tk=128)
    dh0, r1_last = _matmul(dproj, p["w_main_t"], mode="nn", name="hyb_dh", tm=1024, tn=1024, tk=512, extras=(dh0,),
                           epilogue=add, side=_cores_job(last))
    sums = [_pair_sum(a, r, f"grad_pair_sum_{t}") for a, r, t in zip(last, r1_last, ("w_in", "conv"))]
    r2_last = _run_job(_chips_job([s[0] for s in sums]), "grad_exchange_chips_last")
    reduced.update({t: (s[1], r2) for t, s, r2 in zip(("w_in", "conv"), sums, r2_last)})
    grad_x, g["hyb_norm"] = _rmsnorm_bwd(dh0, x, p["hyb_norm"], dx1, name="hyb_norm_bwd")
    return sq, grad_x, g, reduced


def _position():
    return lax.axis_index("x"), lax.axis_index("y"), lax.axis_index("c")


def _run_job(job, name):
    k_in, k_out = len(job.ins), len(job.outs)

    def body(*refs):
        parts = refs[:k_in], refs[k_in:k_in + k_out], refs[k_in + k_out:]
        job.start(*parts)
        if job.mid is not None:
            job.mid(*parts)
        job.finish(*parts)

    return pl.pallas_call(body, name=name, out_shape=list(job.outs), in_specs=[ANY] * k_in, out_specs=[ANY] * k_out,
                          scratch_shapes=list(job.sems))(*job.ins)


def _gather_job(vs):
    n = len(vs)

    def plan(v_refs, out_refs, sems):
        send_sems, recv_sems, local_sems = sems
        x, y, c = _position()
        me, sibling = (x, y, c), (x, y, 1 - c)
        chips = [(1 - x, y), (x, 1 - y), (1 - x, 1 - y)]

        def rows(a, px, py, pc):
            return out_refs[a].at[4 * px + 2 * py + pc]

        def copy(a, k, block, to, src=None):
            return pltpu.make_async_remote_copy(
                src_ref=rows(a, *block) if src is None else src, dst_ref=rows(a, *block),
                send_sem=send_sems.at[7 * a + k], recv_sem=recv_sems.at[7 * a + k], device_id=to,
                device_id_type=MESH)

        mine = [pltpu.make_async_copy(v_refs[a], rows(a, *me), local_sems.at[a]) for a in range(n)]
        first = [copy(a, 0, me, sibling, src=v_refs[a]) for a in range(n)]
        first += [copy(a, 1 + j, me, (*chip, c), src=v_refs[a]) for a in range(n) for j, chip in enumerate(chips)]
        landed = [copy(a, 1 + j, (*chip, c), me) for j, chip in enumerate(chips) for a in range(n)]
        passed = [copy(a, 4 + j, (*chip, c), sibling) for j, chip in enumerate(chips) for a in range(n)]
        from_sibling = [copy(a, 0, sibling, me) for a in range(n)]
        from_sibling += [copy(a, 4 + j, (*chip, 1 - c), me) for a in range(n) for j, chip in enumerate(chips)]
        return mine, first, landed, passed, from_sibling

    def start(*refs):
        mine, first, _, _, _ = plan(*refs)
        for cp in mine + first:
            cp.start()

    def mid(*refs):
        _, _, landed, passed, _ = plan(*refs)
        for arrived, onward in zip(landed, passed):
            arrived.wait_recv()
            onward.start()

    def finish(*refs):
        mine, first, _, passed, from_sibling = plan(*refs)
        for cp in from_sibling:
            cp.wait_recv()
        for cp in first + passed:
            cp.wait_send()
        for cp in mine:
            cp.wait()

    return _Job(tuple(vs), tuple(jax.ShapeDtypeStruct((N_DEV,) + v.shape, v.dtype) for v in vs),
                (pltpu.SemaphoreType.DMA((7 * n,)), pltpu.SemaphoreType.DMA((7 * n,)), pltpu.SemaphoreType.DMA((n,))),
                start, mid, finish)


def _cores_job(gs):
    n = len(gs)

    def plan(g_refs, r_refs, sems):
        send_sems, recv_sems = sems
        x, y, c = _position()
        return [pltpu.make_async_remote_copy(
            src_ref=g_refs[a].at[2 * k + (1 - c)], dst_ref=r_refs[a].at[k], send_sem=send_sems.at[4 * a + k],
            recv_sem=recv_sems.at[4 * a + k], device_id=(x, y, 1 - c), device_id_type=MESH)
            for a in range(n) for k in range(4)]

    def start(*refs):
        for cp in plan(*refs):
            cp.start()

    def finish(*refs):
        copies = plan(*refs)
        for cp in copies:
            cp.wait_recv()
        for cp in copies:
            cp.wait_send()

    return _Job(tuple(gs), tuple(jax.ShapeDtypeStruct((4,) + g.shape[1:], g.dtype) for g in gs),
                (pltpu.SemaphoreType.DMA((4 * n,)), pltpu.SemaphoreType.DMA((4 * n,))), start, None, finish)


TILE_BYTES = 2 * 1024 * 1024


def _col_tile(R, C):
    tc = C
    while R * tc * 4 > TILE_BYTES and tc % (2 * LANE) == 0:
        tc //= 2
    return tc


def _pair_sum(gr, r1, name):
    _, R, C = gr.shape
    tc = _col_tile(R, C)
    x, y, c = _position()
    pos = jnp.stack([c, 2 * x + y]).astype(jnp.int32)

    def body(pos_ref, g_ref, r_ref, pb_ref, pm_ref):
        s = g_ref[...] + r_ref[...]
        pb_ref[...] = s.astype(BF16)

        @pl.when(pl.program_id(1) == pos_ref[1])
        def _():
            pm_ref[...] = s

    return _call(body, name=name, grid=(C // tc, 4), prefetch=1,
                 in_specs=[pl.BlockSpec((None, R, tc), lambda j, k, pos: (2 * k + pos[0], 0, j)),
                           pl.BlockSpec((None, R, tc), lambda j, k, pos: (k, 0, j))],
                 out_specs=[pl.BlockSpec((None, R, tc), lambda j, k, pos: (k, 0, j)),
                            pl.BlockSpec((R, tc), lambda j, k, pos: (0, j))],
                 out_shape=[jax.ShapeDtypeStruct((4, R, C), BF16), jax.ShapeDtypeStruct((R, C), F32)],
                 dimension_semantics=("arbitrary", "arbitrary"))(pos, gr, r1)


def _chips_job(pbs):
    n = len(pbs)

    def plan(p_refs, r_refs, sems):
        send_sems, recv_sems = sems
        x, y, c = _position()
        chips = [(1 - x, y), (x, 1 - y), (1 - x, 1 - y)]
        mine = 2 * x + y

        def copy(a, j, src_row, dst_row):
            cx, cy = chips[j]
            return pltpu.make_async_remote_copy(
                src_ref=p_refs[a].at[src_row], dst_ref=r_refs[a].at[dst_row], send_sem=send_sems.at[3 * a + j],
                recv_sem=recv_sems.at[3 * a + j], device_id=(cx, cy, c), device_id_type=MESH)

        sends = [copy(a, j, 2 * cx + cy, mine) for a in range(n) for j, (cx, cy) in enumerate(chips)]
        arrivals = [copy(a, j, mine, 2 * cx + cy) for a in range(n) for j, (cx, cy) in enumerate(chips)]
        return sends, arrivals

    def start(*refs):
        for cp in plan(*refs)[0]:
            cp.start()

    def finish(*refs):
        sends, arrivals = plan(*refs)
        for cp in arrivals:
            cp.wait_recv()
        for cp in sends:
            cp.wait_send()

    return _Job(tuple(pbs), tuple(jax.ShapeDtypeStruct(p.shape, p.dtype) for p in pbs),
                (pltpu.SemaphoreType.DMA((3 * n,)), pltpu.SemaphoreType.DMA((3 * n,))), start, None, finish)


def _adamw(w, grad, m, v):
    m = ADAM_B1 * m + (1.0 - ADAM_B1) * grad
    v = ADAM_B2 * v + (1.0 - ADAM_B2) * (grad * grad)
    m_hat = m / (1.0 - ADAM_B1 ** ADAM_STEP)
    v_hat = v / (1.0 - ADAM_B2 ** ADAM_STEP)
    delta = -ADAM_LR * (m_hat / (jnp.sqrt(v_hat) + ADAM_EPS) + ADAM_WD * w)
    return delta, m, v


def _other_chips():
    x, y, _ = _position()
    mine = 2 * x + y
    return jnp.stack([jnp.where(mine <= j, j + 1, j) for j in range(3)]).astype(jnp.int32)


def _adamw_sharded(pm, r2, w, m, v, name):
    R, C = pm.shape
    tc = _col_tile(R, C)
    update = w is not None

    def body(oth_ref, pm_ref, a_ref, b_ref, c_ref, *refs):
        grad = ((pm_ref[...] + a_ref[...].astype(F32)) + b_ref[...].astype(F32)) + c_ref[...].astype(F32)
        if update:
            w_ref, m_ref, v_ref, g_ref, d_ref, nm_ref, nv_ref = refs
            d, nm, nv = _adamw(w_ref[...], grad, m_ref[...], v_ref[...])
            g_ref[...], d_ref[...], nm_ref[...], nv_ref[...] = grad, d, nm, nv
        else:
            refs[0][...] = grad

    tile = pl.BlockSpec((R, tc), lambda j, oth: (0, j))
    other = [pl.BlockSpec((None, R, tc), functools.partial(lambda j, oth, q: (oth[q], 0, j), q=q)) for q in range(3)]
    shp = jax.ShapeDtypeStruct((R, C), F32)
    n_out = 4 if update else 1
    res = _call(body, name=name, grid=(C // tc,), prefetch=1,
                in_specs=[tile] + other + ([tile, tile, tile] if update else []), out_specs=[tile] * n_out,
                out_shape=[shp] * n_out, dimension_semantics=("arbitrary",))(
                    _other_chips(), pm, r2, r2, r2, *((w, m, v) if update else ()))
    return res if update else res[0]


def _adamw_plain(grad, w, m, v, name):
    R, C = w.shape
    tr = 256

    def body(g_ref, w_ref, m_ref, v_ref, d_ref, nm_ref, nv_ref):
        d_ref[...], nm_ref[...], nv_ref[...] = _adamw(w_ref[...], g_ref[...], m_ref[...], v_ref[...])

    tile = pl.BlockSpec((tr, C), lambda i: (i, 0))
    shp = jax.ShapeDtypeStruct((R, C), F32)
    return _call(body, name=name, grid=(R // tr,), in_specs=[tile] * 4, out_specs=[tile] * 3, out_shape=[shp] * 3,
                 dimension_semantics=("arbitrary",))(grad, w, m, v)


def _adamw_replicated(parts, w, m, v):
    _, R, C = parts.shape

    def body(p_ref, w_ref, m_ref, v_ref, g_ref, d_ref, nm_ref, nv_ref):
        grad = p_ref[0]
        for j in range(1, N_DEV):
            grad = grad + p_ref[j]
        d, nm, nv = _adamw(w_ref[...], grad, m_ref[...], v_ref[...])
        g_ref[...], d_ref[...], nm_ref[...], nv_ref[...] = grad, d, nm, nv

    tile = pl.BlockSpec((R, C), lambda i: (0, 0))
    shp = jax.ShapeDtypeStruct((R, C), F32)
    return _call(body, name="adamw_replicated", grid=(1,),
                 in_specs=[pl.BlockSpec((N_DEV, R, C), lambda i: (0, 0, 0)), tile, tile, tile],
                 out_specs=[tile] * 4, out_shape=[shp] * 4)(parts, w, m, v)


POOL_SHARDED = ("pool_w", "pool_norm", "pool_b", "pool_scale")
REPLICATED = ("hyb_norm", "ssd_conv_b", "ssd_dt_bias", "ssd_a_log", "ssd_d", "ssd_out_norm", "sb_q_norm",
              "sb_k_norm", "mlp_norm")
PACK_COLS = 1024


def _pack(arrays, cols, row_multiple, dtype):
    flat = jnp.concatenate([a.reshape(-1).astype(dtype) for a in arrays])
    n = flat.shape[0]
    total = -(-n // (cols * row_multiple)) * cols * row_multiple
    return jnp.pad(flat, (0, total - n)).reshape(total // cols, cols)


def _shard_rows(blocks, cols):
    flat = jnp.concatenate(blocks, axis=1)
    rows = -(-flat.shape[1] // (8 * cols)) * 8
    return jnp.pad(flat, ((0, 0), (0, rows * cols - flat.shape[1]))).reshape(N_DEV, rows, cols)


def _unpack(packed, shapes):
    flat = packed.reshape(packed.shape[:-2] + (-1,))
    out, off = [], 0
    for s in shapes:
        n = math.prod(s)
        out.append(flat[..., off:off + n].reshape(flat.shape[:-1] + tuple(s)))
        off += n
    return out


def _shard_axis(name):
    return {"hyb_w_in": 2, "hyb_w_out": 1, "mlp_w_up": 2, "mlp_w_down": 1, "pool_w": 2, "ssd_conv_w": 2,
            "pool_norm": 1, "pool_b": 1, "pool_scale": 1}[name]


def _whole(blocks, name):
    ax = _shard_axis(name)
    moved = jnp.moveaxis(blocks, 0, ax)
    s = moved.shape
    return moved.reshape(s[:ax] + (s[ax] * s[ax + 1],) + s[ax + 2:])


def _to_blocks(whole, name):
    ax = _shard_axis(name)
    s = whole.shape
    split = whole.reshape(s[:ax] + (N_DEV, s[ax] // N_DEV) + s[ax + 1:])
    return jnp.moveaxis(split, ax, 0).reshape(N_DEV, -1)


def kernel(x, hyb_norm, hyb_w_in, ssd_conv_w, ssd_conv_b, ssd_dt_bias, ssd_a_log, ssd_d, ssd_out_norm, sb_q_norm, sb_k_norm, hyb_w_out, pool_norm, pool_w, pool_b, pool_scale, mlp_norm, mlp_w_up, mlp_w_down, loss_target, m_hyb_norm, m_hyb_w_in, m_ssd_conv_w, m_ssd_conv_b, m_ssd_dt_bias, m_ssd_a_log, m_ssd_d, m_ssd_out_norm, m_sb_q_norm, m_sb_k_norm, m_hyb_w_out, m_pool_norm, m_pool_w, m_pool_b, m_pool_scale, m_mlp_norm, m_mlp_w_up, m_mlp_w_down, v_hyb_norm, v_hyb_w_in, v_ssd_conv_w, v_ssd_conv_b, v_ssd_dt_bias, v_ssd_a_log, v_ssd_d, v_ssd_out_norm, v_sb_q_norm, v_sb_k_norm, v_hyb_w_out, v_pool_norm, v_pool_w, v_pool_b, v_pool_scale, v_mlp_norm, v_mlp_w_up, v_mlp_w_down):
    args = dict(locals())
    names = ("hyb_norm", "hyb_w_in", "ssd_conv_w", "ssd_conv_b", "ssd_dt_bias", "ssd_a_log", "ssd_d", "ssd_out_norm",
             "sb_q_norm", "sb_k_norm", "hyb_w_out", "pool_norm", "pool_w", "pool_b", "pool_scale", "mlp_norm",
             "mlp_w_up", "mlp_w_down")
    wt = {n: args[n] for n in names}
    T, D = x.shape[1], x.shape[2]
    W = SSD_HEADS * SSD_HEAD_DIM

    conv_dim = ssd_conv_b.shape[-1]
    c1, c2 = W + conv_dim, W + conv_dim + SSD_HEADS
    vec_names = ("ssd_conv_w", "pool_norm", "pool_b", "pool_scale")

    gathered = _run_job(_gather_job([hyb_w_in[0].T.astype(BF16), hyb_w_out[0].astype(BF16),
                                     _pack([wt[n] for n in vec_names], LANE, 8, F32)]), "gather_hybrid_weights")
    in_t = gathered[0].reshape(-1, D)
    vec = {n: _whole(b, n) for n, b in zip(vec_names, _unpack(gathered[2], [wt[n].shape for n in vec_names]))}
    p = {
        "w_main_t": jnp.concatenate([in_t[:W], in_t[c2:], in_t[W:c1]], axis=0),
        "w_dt_t": jnp.pad(in_t[c1:c2], ((0, LANE - SSD_HEADS), (0, 0))),
        "w_out": gathered[1].reshape(-1, D), "conv_w": vec["ssd_conv_w"][0], "conv_b": ssd_conv_b,
        "pool_norm": vec["pool_norm"], "pool_b": vec["pool_b"], "pool_scale": vec["pool_scale"],
        "hyb_norm": hyb_norm, "mlp_norm": mlp_norm, "out_norm": ssd_out_norm, "q_norm": sb_q_norm,
        "k_norm": sb_k_norm, "ssd_d": ssd_d, "ssd_a_log": ssd_a_log, "ssd_dt_bias": ssd_dt_bias,
    }

    sq, grad_x, g, reduced = _local_step(x[0], loss_target[0], p, [mlp_w_up.astype(BF16), mlp_w_down.astype(BF16),
                                                                  pool_w.astype(BF16)])
    loss = lax.psum(0.5 * jnp.sum(sq) / D, ("x", "y", "c"))

    res = {}
    grad_in = _adamw_sharded(*reduced["w_in"], None, None, None, "grad_sum_w_in").T
    res["hyb_w_in"] = [a[None] for a in (grad_in, *_adamw_plain(grad_in, hyb_w_in[0], m_hyb_w_in[0], v_hyb_w_in[0],
                                                                "adamw_w_in"))]
    res["hyb_w_out"] = [a[None] for a in _adamw_sharded(*reduced["w_out"], hyb_w_out[0], m_hyb_w_out[0],
                                                        v_hyb_w_out[0], "adamw_w_out")]
    for t, n in (("w_up", "mlp_w_up"), ("w_down", "mlp_w_down")):
        layers = [_adamw_sharded(*reduced[f"{t}{l}"], args[n][l], args["m_" + n][l], args["v_" + n][l],
                                 f"adamw_{n}{l}") for l in range(2)]
        res[n] = [jnp.stack([layers[0][k], layers[1][k]]) for k in range(4)]
    for t, group, cols in (("pool", POOL_SHARDED, PACK_COLS), ("conv", ("ssd_conv_w",), LANE)):
        packed = [_pack([args[pre + n] for n in group], cols, 8, F32) for pre in ("", "m_", "v_")]
        small = [_unpack(o, [wt[n].shape for n in group]) for o in _adamw_sharded(*reduced[t], *packed, f"adamw_{t}")]
        for i, n in enumerate(group):
            res[n] = [small[k][i] for k in range(4)]

    parts = _run_job(_gather_job([_pack([g[n] for n in REPLICATED], LANE, 8, F32)]), "gather_vector_grads")[0]
    packed = [_pack([args[pre + n] for n in REPLICATED], LANE, 8, F32) for pre in ("", "m_", "v_")]
    shapes = [wt[n].shape for n in REPLICATED]
    repl = [_unpack(o, shapes) for o in _adamw_replicated(parts, *packed)]
    for i, n in enumerate(REPLICATED):
        res[n] = [repl[k][i] for k in range(4)]

    outs = [res[n][k] for k in range(4) for n in names]
    return (loss, grad_x[None], *outs)
```

```python
import functools
import math
from typing import Callable, NamedTuple, Optional

import jax
import jax.numpy as jnp
from jax import lax
from jax.experimental import pallas as pl
from jax.experimental.pallas import tpu as pltpu

F32 = jnp.float32
BF16 = jnp.bfloat16
EPS = 1e-6
V7X_VMEM_LIMIT = 56 * 1024 * 1024
MESH = pl.DeviceIdType.MESH
ANY = pl.BlockSpec(memory_space=pl.ANY)
N_DEV = 8

SSD_HEADS = 32
SSD_HEAD_DIM = 64
SSD_STATE = 128
SSD_GROUPS = 4
SSD_CHUNK = 128
GROUP_W = SSD_HEADS * SSD_HEAD_DIM // SSD_GROUPS
HEADS_PER_GROUP = SSD_HEADS // SSD_GROUPS
SB_HEADS = 16
SB_DIM = 128
POOL_WINDOWS = (2, 4, 8, 16)
LANE = 128

ADAM_LR = 0.001
ADAM_B1 = 0.9
ADAM_B2 = 0.999
ADAM_EPS = 1e-08
ADAM_WD = 0.01
ADAM_STEP = 10

NN = (((1,), (0,)), ((), ()))
NT = (((1,), (1,)), ((), ()))
TN = (((0,), (0,)), ((), ()))

class _Job(NamedTuple):
    ins: tuple
    outs: tuple
    sems: tuple
    start: Callable
    mid: Optional[Callable]
    finish: Callable


def _call(body, *, name, grid, in_specs, out_specs, out_shape, scratch=(), prefetch=0, side=None, **params):
    if side is not None:
        single = not isinstance(out_shape, (list, tuple))
        out_specs = [out_specs] if single else list(out_specs)
        out_shape = [out_shape] if single else list(out_shape)
        n_in, n_out, n_scr = len(in_specs), len(out_shape), len(scratch)
        k_in, k_out = len(side.ins), len(side.outs)
        inner = body
        steps = math.prod(grid)

        def body(*refs):
            pre, rest = refs[:prefetch], refs[prefetch:]
            ins, s_in = rest[:n_in], rest[n_in:n_in + k_in]
            rest = rest[n_in + k_in:]
            outs, s_out = rest[:n_out], rest[n_out:n_out + k_out]
            rest = rest[n_out + k_out:]
            scr, s_sem = rest[:n_scr], rest[n_scr:]
            step = 0
            for axis, size in enumerate(grid):
                step = step * size + pl.program_id(axis)

            @pl.when(step == 0)
            def _():
                side.start(s_in, s_out, s_sem)

            inner(*pre, *ins, *outs, *scr)
            if side.mid is not None:
                @pl.when(step == (3 * steps) // 4)
                def _():
                    side.mid(s_in, s_out, s_sem)

            @pl.when(step == steps - 1)
            def _():
                side.finish(s_in, s_out, s_sem)

        params = dict(params, dimension_semantics=("arbitrary",) * len(grid))
        res = _call(body, name=name, grid=grid, in_specs=list(in_specs) + [ANY] * k_in,
                    out_specs=out_specs + [ANY] * k_out, out_shape=out_shape + list(side.outs),
                    scratch=list(scratch) + list(side.sems), prefetch=prefetch, **params)
        return lambda *args: (lambda r: ((r[0] if single else r[:n_out]), r[n_out:]))(res(*args, *side.ins))
    cp = pltpu.CompilerParams(vmem_limit_bytes=V7X_VMEM_LIMIT, **params)
    if prefetch:
        gs = pltpu.PrefetchScalarGridSpec(num_scalar_prefetch=prefetch, grid=grid, in_specs=in_specs,
                                          out_specs=out_specs, scratch_shapes=list(scratch))
        return pl.pallas_call(body, name=name, grid_spec=gs, out_shape=out_shape, compiler_params=cp)
    return pl.pallas_call(body, name=name, grid=grid, in_specs=in_specs, out_specs=out_specs,
                          out_shape=out_shape, scratch_shapes=list(scratch), compiler_params=cp)


def _dot(a, b, dims=NN):
    return lax.dot_general(a, b, dims, preferred_element_type=F32)


def _split3(x):
    hi = x.astype(BF16)
    r = x - hi.astype(F32)
    mid = r.astype(BF16)
    lo = (r - mid.astype(F32)).astype(BF16)
    return hi, mid, lo


def _dot3(x, m, dims=NN):
    hi, mid, lo = _split3(x)
    return _dot(hi, m, dims) + _dot(mid, m, dims) + _dot(lo, m, dims)


def _dot3l(m, x, dims=NN):
    hi, mid, lo = _split3(x)
    return _dot(m, hi, dims) + _dot(m, mid, dims) + _dot(m, lo, dims)


def _dot2(x, m):
    hi = x.astype(BF16)
    lo = (x - hi.astype(F32)).astype(BF16)
    return _dot(hi, m) + _dot(lo, m)


def _sigmoid(x):
    return 1.0 / (1.0 + jnp.exp(-x))


def _softplus(x):
    return jnp.maximum(x, 0.0) + jnp.log(1.0 + jnp.exp(-jnp.abs(x)))


def _iota(shape, dim):
    return lax.broadcasted_iota(jnp.int32, shape, dim)


def _matmul(a, b, *, mode, name, tm, tn, tk, extras=(), epilogue=None, out_dtypes=(F32,), mnk=None, b_spec=None,
            out_spec=None, out_dims=None, side=None):
    if mnk is not None:
        M, N, K = mnk
    elif mode == "tn":
        (K, M), N = a.shape, b.shape[1]
    else:
        (M, K), N = a.shape, b.shape[1 if mode == "nn" else 0]
    tm, tn, tk = min(tm, M), min(tn, N), min(tk, K)
    assert M % tm == 0 and N % tn == 0 and K % tk == 0, (name, M, N, K, tm, tn, tk)
    if mode == "nn":
        a_spec = pl.BlockSpec((tm, tk), lambda i, j, k: (i, k))
        b_spec = b_spec or pl.BlockSpec((tk, tn), lambda i, j, k: (k, j))
        dims = NN
    elif mode == "nt":
        a_spec = pl.BlockSpec((tm, tk), lambda i, j, k: (i, k))
        b_spec = b_spec or pl.BlockSpec((tn, tk), lambda i, j, k: (j, k))
        dims = NT
    else:
        a_spec = pl.BlockSpec((tk, tm), lambda i, j, k: (k, i))
        b_spec = b_spec or pl.BlockSpec((tk, tn), lambda i, j, k: (k, j))
        dims = TN
    nk = K // tk
    ex_specs = []
    for e in extras:
        if e.shape[0] == 1:
            ex_specs.append(pl.BlockSpec((1, tn), lambda i, j, k: (0, j)))
        else:
            ex_specs.append(pl.BlockSpec((tm, tn), lambda i, j, k: (i, j)))
    n_ex, n_out = len(extras), len(out_dtypes)

    def body(*refs):
        a_ref, b_ref = refs[0], refs[1]
        ex_refs = refs[2:2 + n_ex]
        o_refs = refs[2 + n_ex:2 + n_ex + n_out]

        def finish(r):
            outs = (r,) if epilogue is None else epilogue(r, *[e[...] for e in ex_refs])
            for o_ref, o in zip(o_refs, outs):
                o_ref[...] = o.astype(o_ref.dtype)

        part = _dot(a_ref[...].astype(BF16), b_ref[...].astype(BF16), dims)
        if nk == 1:
            finish(part)
            return
        acc = refs[2 + n_ex + n_out]
        k = pl.program_id(2)

        @pl.when(k == 0)
        def _():
            acc[...] = part

        @pl.when(jnp.logical_and(k > 0, k < nk - 1))
        def _():
            acc[...] += part

        @pl.when(k == nk - 1)
        def _():
            finish(acc[...] + part)

    out_shape = [jax.ShapeDtypeStruct(out_dims or (M, N), d) for d in out_dtypes]
    out_specs = [out_spec or pl.BlockSpec((tm, tn), lambda i, j, k: (i, j)) for _ in out_dtypes]
    res = _call(body, name=name, grid=(M // tm, N // tn, nk), in_specs=[a_spec, b_spec] + ex_specs,
                out_specs=out_specs, out_shape=out_shape, scratch=[pltpu.VMEM((tm, tn), F32)] if nk > 1 else [],
                dimension_semantics=("parallel", "parallel", "arbitrary"), side=side)(a, b, *extras)
    if side is not None:
        return (res[0] if n_out > 1 else res[0][0]), res[1]
    return res if n_out > 1 else res[0]


def _rowwise(fn, *, name, T, tm, tiles, vecs, out_tiles, out_vecs):
    n_t, n_v, n_ot, n_ov = len(tiles), len(vecs), len(out_tiles), len(out_vecs)

    def body(*refs):
        ins = [r[...] for r in refs[:n_t + n_v]]
        outs = fn(*ins)
        ot_refs = refs[n_t + n_v:n_t + n_v + n_ot]
        ov_refs = refs[n_t + n_v + n_ot:]
        for r, o in zip(ot_refs, outs[:n_ot]):
            r[...] = o.astype(r.dtype)
        if n_ov:
            first = pl.program_id(0) == 0

            @pl.when(first)
            def _():
                for r, o in zip(ov_refs, outs[n_ot:]):
                    r[...] = o

            @pl.when(jnp.logical_not(first))
            def _():
                for r, o in zip(ov_refs, outs[n_ot:]):
                    r[...] += o

    in_specs = [pl.BlockSpec((tm, w), functools.partial(lambda i, cb: (i, cb), cb=cb)) for _, w, cb in tiles]
    in_specs += [pl.BlockSpec(v.shape, lambda i: (0, 0)) for v in vecs]
    out_specs = [pl.BlockSpec((tm, w), lambda i: (i, 0)) for w, _ in out_tiles]
    out_specs += [pl.BlockSpec((r, w), lambda i: (0, 0)) for r, w in out_vecs]
    out_shape = [jax.ShapeDtypeStruct((T, w), d) for w, d in out_tiles]
    out_shape += [jax.ShapeDtypeStruct((r, w), F32) for r, w in out_vecs]
    return _call(body, name=name, grid=(T // tm,), in_specs=in_specs, out_specs=out_specs, out_shape=out_shape,
                 dimension_semantics=("arbitrary",))(*[t[0] for t in tiles], *vecs)


def _colsum(x):
    return jnp.sum(x, axis=0, keepdims=True)


def _rms_fwd(x, g):
    r = lax.rsqrt(jnp.mean(x * x, axis=-1, keepdims=True) + EPS)
    return x * r * g


def _rms_bwd(dh, x, g):
    r = lax.rsqrt(jnp.mean(x * x, axis=-1, keepdims=True) + EPS)
    xh = x * r
    dxh = dh * g
    dx = r * (dxh - xh * jnp.mean(dxh * xh, axis=-1, keepdims=True))
    return dx, _colsum(dh * xh)


def _rmsnorm(x, g, *, name, dtype):
    T, D = x.shape
    return _rowwise(lambda xv, gv: (_rms_fwd(xv, gv),), name=name, T=T, tm=256, tiles=[(x, D, 0)], vecs=[g],
                    out_tiles=[(D, dtype)], out_vecs=[])[0]


def _rmsnorm_bwd(dh, x, g, dres, *, name):
    T, D = x.shape

    def fn(dhv, xv, drv, gv):
        dx, dg = _rms_bwd(dhv, xv, gv)
        return drv + dx, drv + dx, dg

    return _rowwise(fn, name=name, T=T, tm=256, tiles=[(dh, D, 0), (x, D, 0), (dres, D, 0)], vecs=[g],
                    out_tiles=[(D, F32), (D, BF16)], out_vecs=[(1, D)])


def _group_slices(width, group):
    return [slice(i, i + group) for i in range(0, width, group)]


def _gate_norm_fwd(ypre, proj, gain):
    T, W = ypre.shape

    def fn(y, z, g):
        gated = y * (z * _sigmoid(z))
        return (jnp.concatenate([_rms_fwd(gated[:, s], g[:, s]) for s in _group_slices(W, GROUP_W)], axis=1),)

    return _rowwise(fn, name="ssd_gate_norm", T=T, tm=256, tiles=[(ypre, W, 0), (proj, W, 0)], vecs=[gain],
                    out_tiles=[(W, BF16)], out_vecs=[])[0]


def _gate_norm_bwd(dmerged, ypre, proj, gain):
    T, W = ypre.shape

    def fn(do, y, z, g):
        sg = _sigmoid(z)
        sz = z * sg
        gated = y * sz
        parts = [_rms_bwd(do[:, s], gated[:, s], g[:, s]) for s in _group_slices(W, GROUP_W)]
        dgated = jnp.concatenate([p[0] for p in parts], axis=1)
        dgain = jnp.concatenate([p[1] for p in parts], axis=1)
        return dgated * sz, dgated * y * (sg * (1.0 + z * (1.0 - sg))), dgain

    return _rowwise(fn, name="ssd_gate_norm_bwd", T=T, tm=256, tiles=[(dmerged, W, 0), (ypre, W, 0), (proj, W, 0)],
                    vecs=[gain], out_tiles=[(W, F32), (W, BF16)], out_vecs=[(1, W)])


def _qk_norm_fwd(proj, qg, kg, W):
    T = proj.shape[0]

    def fn(q, k, gq, gk):
        sl = _group_slices(W, SB_DIM)
        return (jnp.concatenate([_rms_fwd(q[:, s], gq) for s in sl], axis=1),
                jnp.concatenate([_rms_fwd(k[:, s], gk) for s in sl], axis=1))

    return _rowwise(fn, name="sb_qk_norm", T=T, tm=256, tiles=[(proj, W, 1), (proj, W, 2)], vecs=[qg, kg],
                    out_tiles=[(W, BF16), (W, BF16)], out_vecs=[])


def _qk_norm_bwd(dqh, dkh, proj, qg, kg, W):
    T = proj.shape[0]

    def fn(dq, dk, q, k, gq, gk):
        sl = _group_slices(W, SB_DIM)
        pq = [_rms_bwd(dq[:, s], q[:, s], gq) for s in sl]
        pk = [_rms_bwd(dk[:, s], k[:, s], gk) for s in sl]
        return (jnp.concatenate([p[0] for p in pq], axis=1), jnp.concatenate([p[0] for p in pk], axis=1),
                sum(p[1] for p in pq), sum(p[1] for p in pk))

    return _rowwise(fn, name="sb_qk_norm_bwd", T=T, tm=256,
                    tiles=[(dqh, W, 0), (dkh, W, 0), (proj, W, 1), (proj, W, 2)], vecs=[qg, kg],
                    out_tiles=[(W, BF16), (W, BF16)], out_vecs=[(1, SB_DIM), (1, SB_DIM)])


def _loss_grad(y, target):
    T, D = y.shape

    def fn(yv, tv):
        err = yv - tv
        return err * (1.0 / D), err * (1.0 / D), _colsum(err * err)

    return _rowwise(fn, name="loss_grad", T=T, tm=256, tiles=[(y, D, 0), (target, D, 0)], vecs=[],
                    out_tiles=[(D, F32), (D, BF16)], out_vecs=[(1, D)])


def _pool_scale_bwd(dx, ypre, scale):
    T, D = dx.shape

    def fn(d, yp, s):
        dpre = d * s
        return dpre, _colsum(d * yp), _colsum(dpre)

    return _rowwise(fn, name="pool_scale_bwd", T=T, tm=256, tiles=[(dx, D, 0), (ypre, D, 0)], vecs=[scale],
                    out_tiles=[(D, BF16)], out_vecs=[(1, D), (1, D)])


ROWS = 512


def _past(cur, prev, k):
    row = _iota(cur.shape, 0)
    rc = pltpu.roll(cur, k, 0)
    if prev is None:
        return jnp.where(row >= k, rc, 0.0)
    return jnp.where(row >= k, rc, pltpu.roll(prev, k, 0))


def _future(cur, nxt, k):
    n = cur.shape[0]
    row = _iota(cur.shape, 0)
    rc = pltpu.roll(cur, n - k, 0)
    if nxt is None:
        return jnp.where(row < n - k, rc, 0.0)
    return jnp.where(row < n - k, rc, pltpu.roll(nxt, n - k, 0))


def _chunk(ref, ci):
    return ref[ci * ROWS:(ci + 1) * ROWS, :]


def _conv_pre(x_ref, w, b, ci):
    cur = _chunk(x_ref, ci)
    prev = _chunk(x_ref, ci - 1) if ci > 0 else None
    taps = [_past(cur, prev, 3), _past(cur, prev, 2), _past(cur, prev, 1), cur]
    xc = b + sum(w[j:j + 1, :] * taps[j] for j in range(4))
    return xc, taps


CONV_COLS = 256


def _conv_silu_fwd(proj, col0, width, conv_w, conv_b):
    T = proj.shape[0]

    def body(x_ref, w_ref, b_ref, o_ref):
        w, b = w_ref[...], b_ref[...]
        for ci in range(T // ROWS):
            xc, _ = _conv_pre(x_ref, w, b, ci)
            o_ref[ci * ROWS:(ci + 1) * ROWS, :] = xc * _sigmoid(xc)

    cb0 = col0 // CONV_COLS
    return _call(body, name="ssd_conv_silu", grid=(width // CONV_COLS,),
                 in_specs=[pl.BlockSpec((T, CONV_COLS), lambda j: (0, cb0 + j)),
                           pl.BlockSpec((4, CONV_COLS), lambda j: (0, j)),
                           pl.BlockSpec((1, CONV_COLS), lambda j: (0, j))],
                 out_specs=pl.BlockSpec((T, CONV_COLS), lambda j: (0, j)),
                 out_shape=jax.ShapeDtypeStruct((T, width), F32))(proj, conv_w, conv_b)


def _conv_silu_bwd(dxa, proj, col0, width, conv_w, conv_b):
    T = proj.shape[0]
    nchunk = T // ROWS

    def body(d_ref, x_ref, w_ref, b_ref, dx_ref, dw_ref, db_ref, dxc_ref):
        w, b = w_ref[...], b_ref[...]
        dw = [jnp.zeros((1, CONV_COLS), F32) for _ in range(4)]
        db = jnp.zeros((1, CONV_COLS), F32)
        for ci in range(nchunk):
            xc, taps = _conv_pre(x_ref, w, b, ci)
            sg = _sigmoid(xc)
            dxc = _chunk(d_ref, ci) * (sg * (1.0 + xc * (1.0 - sg)))
            dxc_ref[ci * ROWS:(ci + 1) * ROWS, :] = dxc
            db = db + _colsum(dxc)
            dw = [dw[j] + _colsum(dxc * taps[j]) for j in range(4)]
        dw_ref[...] = jnp.concatenate(dw + [jnp.zeros((4, CONV_COLS), F32)], axis=0)
        db_ref[...] = db
        for ci in range(nchunk):
            cur = _chunk(dxc_ref, ci)
            nxt = _chunk(dxc_ref, ci + 1) if ci + 1 < nchunk else None
            dx = (w[3:4, :] * cur + w[2:3, :] * _future(cur, nxt, 1) + w[1:2, :] * _future(cur, nxt, 2)
                  + w[0:1, :] * _future(cur, nxt, 3))
            dx_ref[ci * ROWS:(ci + 1) * ROWS, :] = dx.astype(dx_ref.dtype)

    cb0 = col0 // CONV_COLS
    return _call(body, name="ssd_conv_silu_bwd", grid=(width // CONV_COLS,),
                 in_specs=[pl.BlockSpec((T, CONV_COLS), lambda j: (0, j)),
                           pl.BlockSpec((T, CONV_COLS), lambda j: (0, cb0 + j)),
                           pl.BlockSpec((4, CONV_COLS), lambda j: (0, j)),
                           pl.BlockSpec((1, CONV_COLS), lambda j: (0, j))],
                 out_specs=[pl.BlockSpec((T, CONV_COLS), lambda j: (0, j)),
                            pl.BlockSpec((8, CONV_COLS), lambda j: (0, j)),
                            pl.BlockSpec((1, CONV_COLS), lambda j: (0, j))],
                 out_shape=[jax.ShapeDtypeStruct((T, width), BF16), jax.ShapeDtypeStruct((8, width), F32),
                            jax.ShapeDtypeStruct((1, width), F32)],
                 scratch=[pltpu.VMEM((T, CONV_COLS), F32)])(dxa, proj, conv_w, conv_b)


def _window_count(ci, win, shape):
    t = (_iota(shape, 0) + ci * ROWS + 1).astype(F32)
    return jnp.minimum(t, float(win))


def _pool_diff_fwd(h):
    T, D = h.shape
    per_group = D // len(POOL_WINDOWS) // LANE

    def body(h_ref, o_ref):
        j = pl.program_id(0)
        for gi, win in enumerate(POOL_WINDOWS):
            @pl.when(j // per_group == gi)
            def _(win=win):
                for ci in range(T // ROWS):
                    cur = _chunk(h_ref, ci)
                    prev = _chunk(h_ref, ci - 1) if ci > 0 else None
                    s = cur
                    for k in range(1, win):
                        s = s + _past(cur, prev, k)
                    d = s / _window_count(ci, win, cur.shape) - cur
                    o_ref[ci * ROWS:(ci + 1) * ROWS, :] = d.astype(o_ref.dtype)

    return _call(body, name="pool_diff", grid=(D // LANE,), in_specs=[pl.BlockSpec((T, LANE), lambda j: (0, j))],
                 out_specs=pl.BlockSpec((T, LANE), lambda j: (0, j)),
                 out_shape=jax.ShapeDtypeStruct((T, D), BF16))(h)


def _pool_diff_bwd(dd):
    T, D = dd.shape
    per_group = D // len(POOL_WINDOWS) // LANE
    nchunk = T // ROWS

    def body(d_ref, o_ref):
        j = pl.program_id(0)
        for gi, win in enumerate(POOL_WINDOWS):
            @pl.when(j // per_group == gi)
            def _(win=win):
                for ci in range(nchunk):
                    cur = _chunk(d_ref, ci)
                    q = cur / _window_count(ci, win, cur.shape)
                    qn = None
                    if ci + 1 < nchunk:
                        qn = _chunk(d_ref, ci + 1) / _window_count(ci + 1, win, cur.shape)
                    s = q - cur
                    for k in range(1, win):
                        s = s + _future(q, qn, k)
                    o_ref[ci * ROWS:(ci + 1) * ROWS, :] = s

    return _call(body, name="pool_diff_bwd", grid=(D // LANE,), in_specs=[pl.BlockSpec((T, LANE), lambda j: (0, j))],
                 out_specs=pl.BlockSpec((T, LANE), lambda j: (0, j)),
                 out_shape=jax.ShapeDtypeStruct((T, D), F32))(dd)


def _pool_mm_fwd(d, w, b, scale, x):
    T, D = d.shape
    G = w.shape[1]
    tm = 512

    def body(d_ref, w_ref, b_ref, s_ref, x_ref, yp_ref, o_ref):
        yp = _dot(d_ref[...], w_ref[...]) + b_ref[...]
        yp_ref[...] = yp
        o_ref[...] = x_ref[...] + yp * s_ref[...]

    tile = pl.BlockSpec((tm, G), lambda i, g: (i, g))
    vec = pl.BlockSpec((1, G), lambda i, g: (0, g))
    return _call(body, name="pool_mm", grid=(T // tm, D // G),
                 in_specs=[tile, pl.BlockSpec((None, G, G), lambda i, g: (g, 0, 0)), vec, vec, tile],
                 out_specs=[tile, tile],
                 out_shape=[jax.ShapeDtypeStruct((T, D), F32), jax.ShapeDtypeStruct((T, D), F32)])(d, w, b, scale, x)


def _pool_mm_dx(dpre, w):
    T, D = dpre.shape
    G = w.shape[1]
    tm = 512

    def body(d_ref, w_ref, o_ref):
        o_ref[...] = _dot(d_ref[...], w_ref[...], NT)

    tile = pl.BlockSpec((tm, G), lambda i, g: (i, g))
    return _call(body, name="pool_mm_dx", grid=(T // tm, D // G),
                 in_specs=[tile, pl.BlockSpec((None, G, G), lambda i, g: (g, 0, 0))], out_specs=tile,
                 out_shape=jax.ShapeDtypeStruct((T, D), F32))(dpre, w)


def _pool_mm_dw(d, dpre):
    T, D = d.shape
    G = D // len(POOL_WINDOWS)
    tk = 512

    def body(d_ref, p_ref, o_ref):
        @pl.when(pl.program_id(1) == 0)
        def _():
            o_ref[...] = jnp.zeros_like(o_ref)

        o_ref[...] += _dot(d_ref[...], p_ref[...], TN)

    tile = pl.BlockSpec((tk, G), lambda g, k: (k, g))
    return _call(body, name="pool_mm_dw", grid=(D // G, T // tk), in_specs=[tile, tile],
                 out_specs=pl.BlockSpec((None, G, G), lambda g, k: (g, 0, 0)),
                 out_shape=jax.ShapeDtypeStruct((D // G, G, G), F32))(d, dpre)


def _ssd_consts(ssd_d, a_log, dt_bias):
    head = jnp.arange(LANE)[:, None]
    lane = jnp.arange(GROUP_W)[None, :]
    ex = jnp.stack([(head == g * HEADS_PER_GROUP + lane // SSD_HEAD_DIM) for g in range(SSD_GROUPS)])
    d_lanes = jnp.repeat(ssd_d.reshape(-1), SSD_HEAD_DIM).reshape(1, -1)
    pad = lambda v: jnp.pad(v.reshape(1, -1), ((0, 0), (0, LANE - SSD_HEADS)))
    return ex.astype(BF16), d_lanes, pad(a_log), pad(dt_bias)


def _ssd_chunk(xs, bm, cm, dtr, bias, alog, ex):
    L = SSD_CHUNK
    row, col = _iota((L, L), 0), _iota((L, L), 1)
    causal = col <= row
    ltri = causal.astype(BF16)
    a_row = -jnp.exp(alog)
    dt = _softplus(dtr + bias)
    da = dt * a_row
    dt_l = _dot3(dt, ex)
    da_l = _dot3(da, ex)
    acs_l = _dot3l(ltri, da_l)
    acs_r = _dot3(da, (row <= col).astype(BF16), TN)
    last_l = acs_l[L - 1:L, :]
    e_l = jnp.exp(last_l - acs_l)
    f_l = jnp.exp(acs_l)
    cd_l = jnp.exp(last_l)
    xdt = xs * dt_l
    cb = _dot(cm.astype(BF16), bm.astype(BF16), NT)
    return dict(causal=causal, dt=dt, da=da, dt_l=dt_l, acs_l=acs_l, acs_r=acs_r, e_l=e_l, f_l=f_l, cd_l=cd_l,
                xdt=xdt, cb=cb, a_row=a_row, ltri=ltri)


def _head_decay(q, acsrow_ref, g, r):
    colv = q["acs_l"][:, r * SSD_HEAD_DIM:r * SSD_HEAD_DIM + 1]
    rowv = acsrow_ref[pl.ds(g * HEADS_PER_GROUP + r, 1), :]
    return jnp.exp(jnp.where(q["causal"], colv - rowv, -1e30))


def _ssd_specs(T):
    L = SSD_CHUNK
    xs = pl.BlockSpec((L, GROUP_W), lambda g, c: (c, g))
    nb = SSD_HEADS * SSD_HEAD_DIM // SSD_STATE
    bm = pl.BlockSpec((L, SSD_STATE), lambda g, c: (c, nb + g))
    cm = pl.BlockSpec((L, SSD_STATE), lambda g, c: (c, nb + SSD_GROUPS + g))
    dtr = pl.BlockSpec((L, LANE), lambda g, c: (c, 0))
    vec = pl.BlockSpec((1, LANE), lambda g, c: (0, 0))
    ex = pl.BlockSpec((None, LANE, GROUP_W), lambda g, c: (g, 0, 0))
    dl = pl.BlockSpec((1, GROUP_W), lambda g, c: (0, g))
    return xs, bm, cm, dtr, vec, ex, dl


def _ssd_fwd(xbc, proj_dt, bias, alog, ex, d_lanes):
    T = xbc.shape[0]
    L, nc, W = SSD_CHUNK, T // SSD_CHUNK, SSD_HEADS * SSD_HEAD_DIM

    def body(xs_ref, b_ref, c_ref, dtr_ref, bias_ref, alog_ref, ex_ref, dl_ref, y_ref, st_ref, state, acsrow):
        g, c = pl.program_id(0), pl.program_id(1)

        @pl.when(c == 0)
        def _():
            state[...] = jnp.zeros_like(state)

        xs, bm, cm = xs_ref[...], b_ref[...], c_ref[...]
        q = _ssd_chunk(xs, bm, cm, dtr_ref[...], bias_ref[...], alog_ref[...], ex_ref[...])
        acsrow[...] = q["acs_r"]
        prev = state[...]
        st_ref[...] = prev
        xdt_b = q["xdt"].astype(BF16)
        yoff = q["f_l"] * _dot(cm.astype(BF16), prev.astype(BF16))
        lane = _iota((L, LANE), 1)
        for p in range(HEADS_PER_GROUP // 2):
            sl = slice(p * LANE, (p + 1) * LANE)
            ma = (_head_decay(q, acsrow, g, 2 * p) * q["cb"]).astype(BF16)
            mb = (_head_decay(q, acsrow, g, 2 * p + 1) * q["cb"]).astype(BF16)
            yd = jnp.where(lane < SSD_HEAD_DIM, _dot(ma, xdt_b[:, sl]), _dot(mb, xdt_b[:, sl]))
            y_ref[:, sl] = yd + yoff[:, sl] + dl_ref[:, sl] * xs[:, sl]
        st_new = _dot(bm.astype(BF16), (q["xdt"] * q["e_l"]).astype(BF16), TN)
        state[...] = q["cd_l"] * prev + st_new

    xs, bm, cm, dtr, vec, exs, dl = _ssd_specs(T)
    return _call(body, name="ssd_scan", grid=(SSD_GROUPS, nc), in_specs=[xs, bm, cm, dtr, vec, vec, exs, dl],
                 out_specs=[pl.BlockSpec((L, GROUP_W), lambda g, c: (c, g)),
                            pl.BlockSpec((None, None, SSD_STATE, GROUP_W), lambda g, c: (c, g, 0, 0))],
                 out_shape=[jax.ShapeDtypeStruct((T, W), F32),
                            jax.ShapeDtypeStruct((nc, SSD_GROUPS, SSD_STATE, GROUP_W), F32)],
                 scratch=[pltpu.VMEM((SSD_STATE, GROUP_W), F32), pltpu.VMEM((LANE, L), F32)],
                 dimension_semantics=("arbitrary", "arbitrary"))(xbc, xbc, xbc, proj_dt, bias, alog, ex, d_lanes)


def _ssd_bwd(dy, xbc, proj_dt, states, bias, alog, ex, d_lanes):
    T = xbc.shape[0]
    L, nc, W = SSD_CHUNK, T // SSD_CHUNK, SSD_HEADS * SSD_HEAD_DIM
    P = SSD_HEAD_DIM

    def body(dy_ref, xs_ref, b_ref, c_ref, dtr_ref, st_ref, bias_ref, alog_ref, ex_ref, dl_ref,
             dxs_ref, db_ref, dc_ref, ddt_ref, hv_ref, dstate, acsrow):
        g, c = pl.program_id(0), pl.program_id(1)

        @pl.when(c == 0)
        def _():
            dstate[...] = jnp.zeros_like(dstate)

        @pl.when(jnp.logical_and(g == 0, c == 0))
        def _():
            hv_ref[...] = jnp.zeros_like(hv_ref)

        xs, bm, cm, ex = xs_ref[...], b_ref[...], c_ref[...], ex_ref[...]
        dtr, bias = dtr_ref[...], bias_ref[...]
        q = _ssd_chunk(xs, bm, cm, dtr, bias, alog_ref[...], ex)
        acsrow[...] = q["acs_r"]
        dyv = dy_ref[...]
        prev = st_ref[...]
        dst = dstate[...]
        bm_b, cm_b = bm.astype(BF16), cm.astype(BF16)
        row128 = _iota((L, LANE), 0)
        lane = _iota((L, LANE), 1)
        row_w = _iota((L, GROUP_W), 0)

        dxs = dl_ref[...] * dyv
        d_dl = _colsum(dyv * xs)
        gmat = _dot(cm_b, prev.astype(BF16))
        dg_b = (dyv * q["f_l"]).astype(BF16)
        dacs = dyv * q["f_l"] * gmat
        dcm = _dot(dg_b, prev.astype(BF16), NT)
        dprev = _dot(cm_b, dg_b, TN)
        dcd = _colsum(dst * prev)
        dlast = dcd * q["cd_l"]
        xe = q["xdt"] * q["e_l"]
        dxe = _dot(bm_b, dst.astype(BF16))
        dbm = _dot(xe.astype(BF16), dst.astype(BF16), NT)
        dxdt = dxe * q["e_l"]
        t1 = dxe * xe
        dacs = dacs - t1
        dlast = dlast + _colsum(t1)
        dstate[...] = dprev + q["cd_l"] * dst
        xdt_b = q["xdt"].astype(BF16)
        dcb = jnp.zeros((L, L), F32)
        dacs_head = []
        dxdt_diag = []
        for p in range(HEADS_PER_GROUP // 2):
            sl = slice(p * LANE, (p + 1) * LANE)
            xp = xdt_b[:, sl]
            dyp = dyv[:, sl]
            vals, dx_parts = [], []
            for half in range(2):
                in_half = (lane < P) if half == 0 else (lane >= P)
                decay = _head_decay(q, acsrow, g, 2 * p + half)
                m = decay * q["cb"]
                dyh = jnp.where(in_half, dyp, 0.0).astype(BF16)
                dm = jnp.where(q["causal"], _dot(dyh, xp, NT), 0.0)
                dcb = dcb + dm * decay
                dseg = dm * m
                rs = jnp.sum(dseg, axis=1, keepdims=True)
                cs = jnp.broadcast_to(_colsum(dseg), (L, L)).T[:, 0:1]
                vals.append(rs - cs)
                dx_parts.append(_dot(m.astype(BF16), dyp.astype(BF16), TN))
            dxdt_diag.append(jnp.where(lane < P, dx_parts[0], dx_parts[1]))
            dacs_head.append(jnp.where(lane == 0, vals[0], jnp.where(lane == P, vals[1], 0.0)))
        dxdt = dxdt + jnp.concatenate(dxdt_diag, axis=1)
        dacs = dacs + jnp.concatenate(dacs_head, axis=1)
        dacs = jnp.where(row_w == L - 1, dacs + dlast, dacs)
        dcb_b = dcb.astype(BF16)
        dcm = dcm + _dot(dcb_b, bm_b)
        dbm = dbm + _dot(dcb_b, cm_b, TN)
        dacs_h = _dot3(dacs, ex, NT)
        dda = _dot3l((row128 <= lane).astype(BF16), dacs_h)
        ddt = dda * q["a_row"] + _dot3(dxdt * xs, ex, NT)
        dxs = dxs + dxdt * q["dt_l"]
        ddtr = ddt * _sigmoid(dtr + bias)
        d_alog = _colsum(dda * q["dt"]) * q["a_row"]
        d_dh = _dot3(jnp.broadcast_to(d_dl, (8, GROUP_W)), ex, NT)[0:1, :]
        dxs_ref[...] = dxs
        db_ref[...] = dbm
        dc_ref[...] = dcm
        ddt_ref[...] = ddtr
        hv_ref[0:1, :] += d_dh
        hv_ref[1:2, :] += d_alog
        hv_ref[2:3, :] += _colsum(ddtr)

    rev = lambda c: nc - 1 - c
    xs = pl.BlockSpec((L, GROUP_W), lambda g, c: (rev(c), g))
    nb = W // SSD_STATE
    bm = pl.BlockSpec((L, SSD_STATE), lambda g, c: (rev(c), nb + g))
    cm = pl.BlockSpec((L, SSD_STATE), lambda g, c: (rev(c), nb + SSD_GROUPS + g))
    dtr = pl.BlockSpec((L, LANE), lambda g, c: (rev(c), 0))
    st = pl.BlockSpec((None, None, SSD_STATE, GROUP_W), lambda g, c: (rev(c), g, 0, 0))
    vec = pl.BlockSpec((1, LANE), lambda g, c: (0, 0))
    exs = pl.BlockSpec((None, LANE, GROUP_W), lambda g, c: (g, 0, 0))
    dl = pl.BlockSpec((1, GROUP_W), lambda g, c: (0, g))
    grp = pl.BlockSpec((L, SSD_STATE), lambda g, c: (rev(c), g))
    return _call(body, name="ssd_scan_bwd", grid=(SSD_GROUPS, nc),
                 in_specs=[xs, xs, bm, cm, dtr, st, vec, vec, exs, dl],
                 out_specs=[xs, grp, grp, grp, pl.BlockSpec((8, LANE), lambda g, c: (0, 0))],
                 out_shape=[jax.ShapeDtypeStruct((T, W), F32),
                            jax.ShapeDtypeStruct((T, SSD_GROUPS * SSD_STATE), F32),
                            jax.ShapeDtypeStruct((T, SSD_GROUPS * SSD_STATE), F32),
                            jax.ShapeDtypeStruct((T, SSD_GROUPS * LANE), F32),
                            jax.ShapeDtypeStruct((8, LANE), F32)],
                 scratch=[pltpu.VMEM((SSD_STATE, GROUP_W), F32), pltpu.VMEM((LANE, L), F32)],
                 dimension_semantics=("arbitrary", "arbitrary"))(dy, xbc, xbc, xbc, proj_dt, states, bias, alog,
                                                                 ex, d_lanes)


SB_TQ = 512
SB_SUB = 128


def _sb_logits(q, kb, scale, mask):
    z = _dot(q, kb, NT) * scale
    lb = jnp.minimum(z, 0.0) - jnp.log(1.0 + jnp.exp(-jnp.abs(z)))
    lk = lb - z
    return lb, lk if mask is None else jnp.where(mask, lk, 0.0)


def _sb_weights(lb, lk, mask, run):
    n = SB_SUB
    strict = (_iota((n, n), 0) > _iota((n, n), 1)).astype(BF16)
    ws = [None] * (lb.shape[1] // n)
    for s in reversed(range(len(ws))):
        sl = slice(s * n, (s + 1) * n)
        w = jnp.exp(lb[:, sl] + (_dot2(lk[:, sl], strict) + run))
        ws[s] = w if mask is None else jnp.where(mask[:, sl], w, 0.0)
        run = run + jnp.sum(lk[:, sl], axis=1, keepdims=True)
    return jnp.concatenate(ws, axis=1), run


def _sb_diagonal(tq):
    return _iota((tq, tq), 1) < _iota((tq, tq), 0)


def _sb_fwd(qh, kh, proj, v_cb0, side=None):
    T, W = qh.shape
    tq = min(SB_TQ, T)
    scale = SB_DIM ** -0.5

    def body(q_ref, k_ref, v_ref, o_ref, ob_ref):
        qi = pl.program_id(1)
        q = q_ref[...]

        def block(it, carry, mask=None):
            run, acc = carry
            kstart = pl.multiple_of((qi - it) * tq, tq)
            lb, lk = _sb_logits(q, k_ref[pl.ds(kstart, tq), :], scale, mask)
            w, run = _sb_weights(lb, lk, mask, run)
            acc = acc + _dot2(w, v_ref[pl.ds(kstart, tq), :].astype(BF16))
            return run, acc

        first = block(0, (jnp.zeros((tq, 1), F32), jnp.zeros((tq, SB_DIM), F32)), _sb_diagonal(tq))
        _, acc = lax.fori_loop(1, qi + 1, block, first)
        o_ref[...] = acc
        ob_ref[...] = acc.astype(BF16)

    tile = pl.BlockSpec((tq, SB_DIM), lambda h, i: (i, h))
    return _call(body, name="sb_attn", grid=(W // SB_DIM, T // tq),
                 in_specs=[tile, pl.BlockSpec((T, SB_DIM), lambda h, i: (0, h)),
                           pl.BlockSpec((T, SB_DIM), lambda h, i: (0, v_cb0 + h))],
                 out_specs=[tile, tile],
                 out_shape=[jax.ShapeDtypeStruct((T, W), F32), jax.ShapeDtypeStruct((T, W), BF16)], side=side,
                 dimension_semantics=("arbitrary", "arbitrary"))(qh, kh, proj)


def _sb_bwd(qh, kh, proj, v_cb0, o, dmerged, do_cb0, side=None):
    T, W = qh.shape
    tq = min(SB_TQ, T)
    n = SB_SUB
    scale = SB_DIM ** -0.5

    def body(q_ref, k_ref, v_ref, o_ref, do_ref, dq_ref, dk_ref, dv_ref):
        qi = pl.program_id(1)

        @pl.when(qi == 0)
        def _():
            dk_ref[...] = jnp.zeros_like(dk_ref)
            dv_ref[...] = jnp.zeros_like(dv_ref)

        q = q_ref[...]
        do = do_ref[...]
        do_b = do.astype(BF16)
        etot = jnp.sum(do_b.astype(F32) * o_ref[...], axis=1, keepdims=True)
        incl = (_iota((n, n), 0) >= _iota((n, n), 1)).astype(BF16)

        def block(it, carry, mask=None):
            run, erun, dq = carry
            kstart = pl.multiple_of((qi - it) * tq, tq)
            kb = k_ref[pl.ds(kstart, tq), :]
            vb = v_ref[pl.ds(kstart, tq), :].astype(BF16)
            lb, lk = _sb_logits(q, kb, scale, mask)
            w, run = _sb_weights(lb, lk, mask, run)
            e = _dot(do_b, vb, NT) * w
            beta = jnp.exp(lb)
            dzs = [None] * (tq // n)
            for s in reversed(range(tq // n)):
                sl = slice(s * n, (s + 1) * n)
                before = etot - erun - _dot3(e[:, sl], incl)
                dz = e[:, sl] * (1.0 - beta[:, sl]) - before * beta[:, sl]
                dzs[s] = dz if mask is None else jnp.where(mask[:, sl], dz, 0.0)
                erun = erun + jnp.sum(e[:, sl], axis=1, keepdims=True)
            dz = (jnp.concatenate(dzs, axis=1) * scale).astype(BF16)
            dq = dq + _dot(dz, kb)
            dk_ref[pl.ds(kstart, tq), :] += _dot(dz, q, TN)
            dv_ref[pl.ds(kstart, tq), :] += _dot(w.astype(BF16), do_b, TN)
            return run, erun, dq

        zero = jnp.zeros((tq, 1), F32)
        first = block(0, (zero, zero, jnp.zeros((tq, SB_DIM), F32)), _sb_diagonal(tq))
        _, _, dq = lax.fori_loop(1, qi + 1, block, first)
        dq_ref[...] = dq

    tile = pl.BlockSpec((tq, SB_DIM), lambda h, i: (i, h))
    full = pl.BlockSpec((T, SB_DIM), lambda h, i: (0, h))
    shp = jax.ShapeDtypeStruct((T, W), F32)
    return _call(body, name="sb_attn_bwd", grid=(W // SB_DIM, T // tq),
                 in_specs=[tile, full, pl.BlockSpec((T, SB_DIM), lambda h, i: (0, v_cb0 + h)), tile,
                           pl.BlockSpec((tq, SB_DIM), lambda h, i: (i, do_cb0 + h))],
                 out_specs=[tile, full, full], out_shape=[shp, shp, shp], side=side,
                 dimension_semantics=("arbitrary", "arbitrary"))(qh, kh, proj, o, dmerged)


MM_TK = 2048


def _mlp_fwd(x, g, w_up, w_down, layer):
    T, D = x.shape
    fs = w_up.shape[3]
    F = N_DEV * fs
    h = _rmsnorm(x, g, name=f"mlp{layer}_norm", dtype=BF16)

    def relu_sq(acc):
        u = jnp.maximum(acc, 0.0)
        return u, u * u

    u, s = _matmul(h, w_up, mode="nn", name=f"mlp{layer}_up", tm=1024, tn=fs, tk=D, epilogue=relu_sq,
                   out_dtypes=(BF16, BF16), mnk=(T, F, D),
                   b_spec=pl.BlockSpec((None, None, D, fs), lambda i, j, k: (j, layer, 0, 0)))
    y = _matmul(s, w_down, mode="nn", name=f"mlp{layer}_down", tm=1024, tn=1024, tk=fs, extras=(x,),
                epilogue=lambda acc, r: (acc + r,), mnk=(T, D, F),
                b_spec=pl.BlockSpec((None, None, fs, 1024), lambda i, j, k: (k, layer, 0, j)))
    return y, (h, u, s)


def _mlp_bwd(dy, dy_b, x, g, w_up, w_down, saved, layer, side=None):
    T, D = x.shape
    fs = w_up.shape[3]
    F = N_DEV * fs
    h, u, s = saved
    dw_down = _matmul(s, dy_b, mode="tn", name=f"mlp{layer}_dwdown", tm=1024, tn=1024, tk=MM_TK, side=side)
    if side is not None:
        dw_down, side_res = dw_down
    da = _matmul(dy_b, w_down, mode="nt", name=f"mlp{layer}_da", tm=1024, tn=fs, tk=D, extras=(u,),
                 epilogue=lambda acc, uv: (acc * (2.0 * uv.astype(F32)),), out_dtypes=(BF16,), mnk=(T, F, D),
                 b_spec=pl.BlockSpec((None, None, fs, D), lambda i, j, k: (j, layer, 0, 0)))
    dw_up = _matmul(h, da, mode="tn", name=f"mlp{layer}_dwup", tm=1024, tn=fs, tk=MM_TK, out_dims=(N_DEV, D, fs),
                    out_spec=pl.BlockSpec((None, 1024, fs), lambda i, j, k: (j, i, 0)))
    dh = _matmul(da, w_up, mode="nt", name=f"mlp{layer}_dh", tm=1024, tn=1024, tk=fs, mnk=(T, D, F),
                 b_spec=pl.BlockSpec((None, None, 1024, fs), lambda i, j, k: (k, layer, j, 0)))
    dx, dx_b, dg = _rmsnorm_bwd(dh, x, g, dy, name=f"mlp{layer}_norm_bwd")
    if side is not None:
        return dx, dx_b, dw_up, dw_down, dg, side_res
    return dx, dx_b, dw_up, dw_down, dg


def _local_step(x, target, p, late):
    T, D = x.shape
    W = SSD_HEADS * SSD_HEAD_DIM
    g = {}
    add = lambda acc, r: (acc + r,)

    h0 = _rmsnorm(x, p["hyb_norm"], name="hyb_norm", dtype=BF16)
    proj = _matmul(h0, p["w_main_t"], mode="nt", name="hyb_proj", tm=1024, tn=1024, tk=MM_TK)
    proj_dt = _matmul(h0, p["w_dt_t"], mode="nt", name="hyb_proj_dt", tm=1024, tn=128, tk=MM_TK)
    ex, d_lanes, alog, bias = _ssd_consts(p["ssd_d"], p["ssd_a_log"], p["ssd_dt_bias"])
    xbc = _conv_silu_fwd(proj, 4 * W, p["conv_w"].shape[1], p["conv_w"], p["conv_b"])
    ypre, states = _ssd_fwd(xbc, proj_dt, bias, alog, ex, d_lanes)
    y_ssd = _gate_norm_fwd(ypre, proj, p["out_norm"])
    qh, kh = _qk_norm_fwd(proj, p["q_norm"], p["k_norm"], W)
    (y_sb, y_sb_b), (w_up, w_down, pool_blocks) = _sb_fwd(qh, kh, proj, 3 * W // SB_DIM, side=_gather_job(late))
    pool_w = _whole(pool_blocks, "pool_w")[0]
    x1 = _matmul(y_ssd, p["w_out"], mode="nn", name="hyb_out_a", tm=1024, tn=1024, tk=MM_TK, extras=(x,), epilogue=add,
                 mnk=(T, D, W))
    x1 = _matmul(y_sb_b, p["w_out"], mode="nn", name="hyb_out_b", tm=1024, tn=1024, tk=W, extras=(x1,), epilogue=add,
                 mnk=(T, D, W), b_spec=pl.BlockSpec((W, 1024), lambda i, j, k: (1, j)))
    x2, mlp0 = _mlp_fwd(x1, p["mlp_norm"][0:1], w_up, w_down, 0)
    hp = _rmsnorm(x2, p["pool_norm"], name="pool_norm", dtype=F32)
    dpool = _pool_diff_fwd(hp)
    ypool, x3 = _pool_mm_fwd(dpool, pool_w, p["pool_b"], p["pool_scale"], x2)
    x4, mlp1 = _mlp_fwd(x3, p["mlp_norm"][1:2], w_up, w_down, 1)

    dy, dy_b, sq = _loss_grad(x4, target)
    dx3, dx3_b, dup1, ddown1, dgm1 = _mlp_bwd(dy, dy_b, x3, p["mlp_norm"][1:2], w_up, w_down, mlp1, 1)
    dpre, dpool_scale, dpool_b = _pool_scale_bwd(dx3, ypool, p["pool_scale"])
    gw = {"pool_w": _pool_mm_dw(dpool, dpre)[None], "pool_b": dpool_b, "pool_scale": dpool_scale}
    dhp = _pool_diff_bwd(_pool_mm_dx(dpre, pool_w))
    dx2, dx2_b, gw["pool_norm"] = _rmsnorm_bwd(dhp, x2, p["pool_norm"], dx3, name="pool_norm_bwd")
    first = [dup1, ddown1.reshape(N_DEV, -1, D)]
    dx1, dx1_b, dup0, ddown0, dgm0, r1_first = _mlp_bwd(dx2, dx2_b, x1, p["mlp_norm"][0:1], w_up, w_down, mlp0, 0,
                                                       side=_cores_job(first))
    g["mlp_norm"] = jnp.concatenate([dgm0, dgm1], axis=0)
    dw_out = jnp.concatenate([
        _matmul(y_ssd, dx1_b, mode="tn", name="hyb_dwout_a", tm=1024, tn=1024, tk=MM_TK),
        _matmul(y_sb_b, dx1_b, mode="tn", name="hyb_dwout_b", tm=1024, tn=1024, tk=MM_TK)], axis=0)
    second = [dup0, ddown0.reshape(N_DEV, -1, D), dw_out.reshape(N_DEV, -1, D),
              _shard_rows([_to_blocks(gw[n], n) for n in POOL_SHARDED], PACK_COLS)]
    dmerged, r1_second = _matmul(dx1_b, p["w_out"], mode="nt", name="hyb_dmerged", tm=1024, tn=1024, tk=MM_TK,
                                 side=_cores_job(second))
    early = dict(zip(("w_up1", "w_down1", "w_up0", "w_down0", "w_out", "pool"),
                     zip(first + second, list(r1_first) + list(r1_second))))
    sums = {t: _pair_sum(a, r, f"grad_pair_sum_{t}") for t, (a, r) in early.items()}
    (dqh, dkh, dv), r2_early = _sb_bwd(qh, kh, proj, 3 * W // SB_DIM, y_sb, dmerged, W // SB_DIM,
                                       side=_chips_job([s[0] for s in sums.values()]))
    reduced = {t: (s[1], r2) for (t, s), r2 in zip(sums.items(), r2_early)}
    dq, dk, g["sb_q_norm"], g["sb_k_norm"] = _qk_norm_bwd(dqh, dkh, proj, p["q_norm"], p["k_norm"], W)
    dypre, dz, g["ssd_out_norm"] = _gate_norm_bwd(dmerged, ypre, proj, p["out_norm"])
    dxs, dbm, dcm, ddt4, hv = _ssd_bwd(dypre, xbc, proj_dt, states, bias, alog, ex, d_lanes)
    g["ssd_d"], g["ssd_a_log"], g["ssd_dt_bias"] = (hv[i:i + 1, :SSD_HEADS] for i in range(3))
    ddt = ddt4.reshape(T, SSD_GROUPS, LANE).sum(axis=1).astype(BF16)
    dxbc, dconv_w, g["ssd_conv_b"] = _conv_silu_bwd(jnp.concatenate([dxs, dbm, dcm], axis=1), proj, 4 * W,
                                                    p["conv_w"].shape[1], p["conv_w"], p["conv_b"])
    dproj = jnp.concatenate([dz, dq, dk, dv.astype(BF16), dxbc], axis=1)
    gm = _matmul(dproj, h0, mode="tn", name="hyb_dwin", tm=1024, tn=1024, tk=MM_TK)
    g_dt = _matmul(ddt, h0, mode="tn", name="hyb_dwdt", tm=128, tn=1024, tk=MM_TK)
    g_in_t = jnp.concatenate([gm[:W], gm[4 * W:], g_dt[:SSD_HEADS], gm[W:4 * W]], axis=0)
    last = [g_in_t.reshape(N_DEV, -1, D), _shard_rows([_to_blocks(dconv_w[:4][None], "ssd_conv_w")], LANE)]
    dh0 = _matmul(ddt, p["w_dt_t"], mode="nn", name="hyb_dh_dt", tm=1024, tn=1024, tk=128)
    dh0, r1_last = _matmul(dproj, p["w_main_t"], mode="nn", name="hyb_dh", tm=1024, tn=1024, tk=1024, extras=(dh0,),
                           epilogue=add, side=_cores_job(last))
    sums = [_pair_sum(a, r, f"grad_pair_sum_{t}") for a, r, t in zip(last, r1_last, ("w_in", "conv"))]
    r2_last = _run_job(_chips_job([s[0] for s in sums]), "grad_exchange_chips_last")
    reduced.update({t: (s[1], r2) for t, s, r2 in zip(("w_in", "conv"), sums, r2_last)})
    grad_x, _, g["hyb_norm"] = _rmsnorm_bwd(dh0, x, p["hyb_norm"], dx1, name="hyb_norm_bwd")
    return sq, grad_x, g, reduced


def _position():
    return lax.axis_index("x"), lax.axis_index("y"), lax.axis_index("c")


def _run_job(job, name):
    k_in, k_out = len(job.ins), len(job.outs)

    def body(*refs):
        parts = refs[:k_in], refs[k_in:k_in + k_out], refs[k_in + k_out:]
        job.start(*parts)
        if job.mid is not None:
            job.mid(*parts)
        job.finish(*parts)

    return pl.pallas_call(body, name=name, out_shape=list(job.outs), in_specs=[ANY] * k_in, out_specs=[ANY] * k_out,
                          scratch_shapes=list(job.sems))(*job.ins)


def _gather_job(vs):
    n = len(vs)

    def plan(v_refs, out_refs, sems):
        send_sems, recv_sems, local_sems = sems
        x, y, c = _position()
        me, sibling = (x, y, c), (x, y, 1 - c)
        chips = [(1 - x, y), (x, 1 - y), (1 - x, 1 - y)]

        def rows(a, px, py, pc):
            return out_refs[a].at[4 * px + 2 * py + pc]

        def copy(a, k, block, to, src=None):
            return pltpu.make_async_remote_copy(
                src_ref=rows(a, *block) if src is None else src, dst_ref=rows(a, *block),
                send_sem=send_sems.at[7 * a + k], recv_sem=recv_sems.at[7 * a + k], device_id=to,
                device_id_type=MESH)

        mine = [pltpu.make_async_copy(v_refs[a], rows(a, *me), local_sems.at[a]) for a in range(n)]
        first = [copy(a, 0, me, sibling, src=v_refs[a]) for a in range(n)]
        first += [copy(a, 1 + j, me, (*chip, c), src=v_refs[a]) for a in range(n) for j, chip in enumerate(chips)]
        landed = [copy(a, 1 + j, (*chip, c), me) for j, chip in enumerate(chips) for a in range(n)]
        passed = [copy(a, 4 + j, (*chip, c), sibling) for j, chip in enumerate(chips) for a in range(n)]
        from_sibling = [copy(a, 0, sibling, me) for a in range(n)]
        from_sibling += [copy(a, 4 + j, (*chip, 1 - c), me) for a in range(n) for j, chip in enumerate(chips)]
        return mine, first, landed, passed, from_sibling

    def start(*refs):
        mine, first, _, _, _ = plan(*refs)
        for cp in mine + first:
            cp.start()

    def mid(*refs):
        _, _, landed, passed, _ = plan(*refs)
        for arrived, onward in zip(landed, passed):
            arrived.wait_recv()
            onward.start()

    def finish(*refs):
        mine, first, _, passed, from_sibling = plan(*refs)
        for cp in from_sibling:
            cp.wait_recv()
        for cp in first + passed:
            cp.wait_send()
        for cp in mine:
            cp.wait()

    return _Job(tuple(vs), tuple(jax.ShapeDtypeStruct((N_DEV,) + v.shape, v.dtype) for v in vs),
                (pltpu.SemaphoreType.DMA((7 * n,)), pltpu.SemaphoreType.DMA((7 * n,)), pltpu.SemaphoreType.DMA((n,))),
                start, mid, finish)


def _cores_job(gs):
    n = len(gs)

    def plan(g_refs, r_refs, sems):
        send_sems, recv_sems = sems
        x, y, c = _position()
        return [pltpu.make_async_remote_copy(
            src_ref=g_refs[a].at[2 * k + (1 - c)], dst_ref=r_refs[a].at[k], send_sem=send_sems.at[4 * a + k],
            recv_sem=recv_sems.at[4 * a + k], device_id=(x, y, 1 - c), device_id_type=MESH)
            for a in range(n) for k in range(4)]

    def start(*refs):
        for cp in plan(*refs):
            cp.start()

    def finish(*refs):
        copies = plan(*refs)
        for cp in copies:
            cp.wait_recv()
        for cp in copies:
            cp.wait_send()

    return _Job(tuple(gs), tuple(jax.ShapeDtypeStruct((4,) + g.shape[1:], g.dtype) for g in gs),
                (pltpu.SemaphoreType.DMA((4 * n,)), pltpu.SemaphoreType.DMA((4 * n,))), start, None, finish)


TILE_BYTES = 2 * 1024 * 1024


def _col_tile(R, C):
    tc = C
    while R * tc * 4 > TILE_BYTES and tc % (2 * LANE) == 0:
        tc //= 2
    return tc


def _pair_sum(gr, r1, name):
    _, R, C = gr.shape
    tc = _col_tile(R, C)
    x, y, c = _position()
    pos = jnp.stack([c, 2 * x + y]).astype(jnp.int32)

    def body(pos_ref, g_ref, r_ref, pb_ref, pm_ref):
        s = g_ref[...] + r_ref[...]
        pb_ref[...] = s.astype(BF16)

        @pl.when(pl.program_id(1) == pos_ref[1])
        def _():
            pm_ref[...] = s

    return _call(body, name=name, grid=(C // tc, 4), prefetch=1,
                 in_specs=[pl.BlockSpec((None, R, tc), lambda j, k, pos: (2 * k + pos[0], 0, j)),
                           pl.BlockSpec((None, R, tc), lambda j, k, pos: (k, 0, j))],
                 out_specs=[pl.BlockSpec((None, R, tc), lambda j, k, pos: (k, 0, j)),
                            pl.BlockSpec((R, tc), lambda j, k, pos: (0, j))],
                 out_shape=[jax.ShapeDtypeStruct((4, R, C), BF16), jax.ShapeDtypeStruct((R, C), F32)],
                 dimension_semantics=("arbitrary", "arbitrary"))(pos, gr, r1)


def _chips_job(pbs):
    n = len(pbs)

    def plan(p_refs, r_refs, sems):
        send_sems, recv_sems = sems
        x, y, c = _position()
        chips = [(1 - x, y), (x, 1 - y), (1 - x, 1 - y)]
        mine = 2 * x + y

        def copy(a, j, src_row, dst_row):
            cx, cy = chips[j]
            return pltpu.make_async_remote_copy(
                src_ref=p_refs[a].at[src_row], dst_ref=r_refs[a].at[dst_row], send_sem=send_sems.at[3 * a + j],
                recv_sem=recv_sems.at[3 * a + j], device_id=(cx, cy, c), device_id_type=MESH)

        sends = [copy(a, j, 2 * cx + cy, mine) for a in range(n) for j, (cx, cy) in enumerate(chips)]
        arrivals = [copy(a, j, mine, 2 * cx + cy) for a in range(n) for j, (cx, cy) in enumerate(chips)]
        return sends, arrivals

    def start(*refs):
        for cp in plan(*refs)[0]:
            cp.start()

    def finish(*refs):
        sends, arrivals = plan(*refs)
        for cp in arrivals:
            cp.wait_recv()
        for cp in sends:
            cp.wait_send()

    return _Job(tuple(pbs), tuple(jax.ShapeDtypeStruct(p.shape, p.dtype) for p in pbs),
                (pltpu.SemaphoreType.DMA((3 * n,)), pltpu.SemaphoreType.DMA((3 * n,))), start, None, finish)


def _adamw(w, grad, m, v):
    m = ADAM_B1 * m + (1.0 - ADAM_B1) * grad
    v = ADAM_B2 * v + (1.0 - ADAM_B2) * (grad * grad)
    m_hat = m / (1.0 - ADAM_B1 ** ADAM_STEP)
    v_hat = v / (1.0 - ADAM_B2 ** ADAM_STEP)
    delta = -ADAM_LR * (m_hat / (jnp.sqrt(v_hat) + ADAM_EPS) + ADAM_WD * w)
    return delta, m, v


def _other_chips():
    x, y, _ = _position()
    mine = 2 * x + y
    return jnp.stack([jnp.where(mine <= j, j + 1, j) for j in range(3)]).astype(jnp.int32)


def _adamw_sharded(pm, r2, w, m, v, name):
    R, C = pm.shape
    tc = _col_tile(R, C)
    update = w is not None

    def body(oth_ref, pm_ref, a_ref, b_ref, c_ref, *refs):
        grad = ((pm_ref[...] + a_ref[...].astype(F32)) + b_ref[...].astype(F32)) + c_ref[...].astype(F32)
        if update:
            w_ref, m_ref, v_ref, g_ref, d_ref, nm_ref, nv_ref = refs
            d, nm, nv = _adamw(w_ref[...], grad, m_ref[...], v_ref[...])
            g_ref[...], d_ref[...], nm_ref[...], nv_ref[...] = grad, d, nm, nv
        else:
            refs[0][...] = grad

    tile = pl.BlockSpec((R, tc), lambda j, oth: (0, j))
    other = [pl.BlockSpec((None, R, tc), functools.partial(lambda j, oth, q: (oth[q], 0, j), q=q)) for q in range(3)]
    shp = jax.ShapeDtypeStruct((R, C), F32)
    n_out = 4 if update else 1
    res = _call(body, name=name, grid=(C // tc,), prefetch=1,
                in_specs=[tile] + other + ([tile, tile, tile] if update else []), out_specs=[tile] * n_out,
                out_shape=[shp] * n_out, dimension_semantics=("arbitrary",))(
                    _other_chips(), pm, r2, r2, r2, *((w, m, v) if update else ()))
    return res if update else res[0]


def _adamw_plain(grad, w, m, v, name):
    R, C = w.shape
    tr = 256

    def body(g_ref, w_ref, m_ref, v_ref, d_ref, nm_ref, nv_ref):
        d_ref[...], nm_ref[...], nv_ref[...] = _adamw(w_ref[...], g_ref[...], m_ref[...], v_ref[...])

    tile = pl.BlockSpec((tr, C), lambda i: (i, 0))
    shp = jax.ShapeDtypeStruct((R, C), F32)
    return _call(body, name=name, grid=(R // tr,), in_specs=[tile] * 4, out_specs=[tile] * 3, out_shape=[shp] * 3,
                 dimension_semantics=("arbitrary",))(grad, w, m, v)


def _adamw_replicated(parts, w, m, v):
    _, R, C = parts.shape

    def body(p_ref, w_ref, m_ref, v_ref, g_ref, d_ref, nm_ref, nv_ref):
        grad = p_ref[0]
        for j in range(1, N_DEV):
            grad = grad + p_ref[j]
        d, nm, nv = _adamw(w_ref[...], grad, m_ref[...], v_ref[...])
        g_ref[...], d_ref[...], nm_ref[...], nv_ref[...] = grad, d, nm, nv

    tile = pl.BlockSpec((R, C), lambda i: (0, 0))
    shp = jax.ShapeDtypeStruct((R, C), F32)
    return _call(body, name="adamw_replicated", grid=(1,),
                 in_specs=[pl.BlockSpec((N_DEV, R, C), lambda i: (0, 0, 0)), tile, tile, tile],
                 out_specs=[tile] * 4, out_shape=[shp] * 4)(parts, w, m, v)


POOL_SHARDED = ("pool_w", "pool_norm", "pool_b", "pool_scale")
REPLICATED = ("hyb_norm", "ssd_conv_b", "ssd_dt_bias", "ssd_a_log", "ssd_d", "ssd_out_norm", "sb_q_norm",
              "sb_k_norm", "mlp_norm")
PACK_COLS = 1024


def _pack(arrays, cols, row_multiple, dtype):
    flat = jnp.concatenate([a.reshape(-1).astype(dtype) for a in arrays])
    n = flat.shape[0]
    total = -(-n // (cols * row_multiple)) * cols * row_multiple
    return jnp.pad(flat, (0, total - n)).reshape(total // cols, cols)


def _shard_rows(blocks, cols):
    flat = jnp.concatenate(blocks, axis=1)
    rows = -(-flat.shape[1] // (8 * cols)) * 8
    return jnp.pad(flat, ((0, 0), (0, rows * cols - flat.shape[1]))).reshape(N_DEV, rows, cols)


def _unpack(packed, shapes):
    flat = packed.reshape(packed.shape[:-2] + (-1,))
    out, off = [], 0
    for s in shapes:
        n = math.prod(s)
        out.append(flat[..., off:off + n].reshape(flat.shape[:-1] + tuple(s)))
        off += n
    return out


def _shard_axis(name):
    return {"hyb_w_in": 2, "hyb_w_out": 1, "mlp_w_up": 2, "mlp_w_down": 1, "pool_w": 2, "ssd_conv_w": 2,
            "pool_norm": 1, "pool_b": 1, "pool_scale": 1}[name]


def _whole(blocks, name):
    ax = _shard_axis(name)
    moved = jnp.moveaxis(blocks, 0, ax)
    s = moved.shape
    return moved.reshape(s[:ax] + (s[ax] * s[ax + 1],) + s[ax + 2:])


def _to_blocks(whole, name):
    ax = _shard_axis(name)
    s = whole.shape
    split = whole.reshape(s[:ax] + (N_DEV, s[ax] // N_DEV) + s[ax + 1:])
    return jnp.moveaxis(split, ax, 0).reshape(N_DEV, -1)


def kernel(x, hyb_norm, hyb_w_in, ssd_conv_w, ssd_conv_b, ssd_dt_bias, ssd_a_log, ssd_d, ssd_out_norm, sb_q_norm, sb_k_norm, hyb_w_out, pool_norm, pool_w, pool_b, pool_scale, mlp_norm, mlp_w_up, mlp_w_down, loss_target, m_hyb_norm, m_hyb_w_in, m_ssd_conv_w, m_ssd_conv_b, m_ssd_dt_bias, m_ssd_a_log, m_ssd_d, m_ssd_out_norm, m_sb_q_norm, m_sb_k_norm, m_hyb_w_out, m_pool_norm, m_pool_w, m_pool_b, m_pool_scale, m_mlp_norm, m_mlp_w_up, m_mlp_w_down, v_hyb_norm, v_hyb_w_in, v_ssd_conv_w, v_ssd_conv_b, v_ssd_dt_bias, v_ssd_a_log, v_ssd_d, v_ssd_out_norm, v_sb_q_norm, v_sb_k_norm, v_hyb_w_out, v_pool_norm, v_pool_w, v_pool_b, v_pool_scale, v_mlp_norm, v_mlp_w_up, v_mlp_w_down):
    args = dict(locals())
    names = ("hyb_norm", "hyb_w_in", "ssd_conv_w", "ssd_conv_b", "ssd_dt_bias", "ssd_a_log", "ssd_d", "ssd_out_norm",
             "sb_q_norm", "sb_k_norm", "hyb_w_out", "pool_norm", "pool_w", "pool_b", "pool_scale", "mlp_norm",
             "mlp_w_up", "mlp_w_down")
    wt = {n: args[n] for n in names}
    T, D = x.shape[1], x.shape[2]
    W = SSD_HEADS * SSD_HEAD_DIM

    conv_dim = ssd_conv_b.shape[-1]
    c1, c2 = W + conv_dim, W + conv_dim + SSD_HEADS
    vec_names = ("ssd_conv_w", "pool_norm", "pool_b", "pool_scale")

    gathered = _run_job(_gather_job([hyb_w_in[0].T.astype(BF16), hyb_w_out[0].astype(BF16),
                                     _pack([wt[n] for n in vec_names], LANE, 8, F32)]), "gather_hybrid_weights")
    in_t = gathered[0].reshape(-1, D)
    vec = {n: _whole(b, n) for n, b in zip(vec_names, _unpack(gathered[2], [wt[n].shape for n in vec_names]))}
    p = {
        "w_main_t": jnp.concatenate([in_t[:W], in_t[c2:], in_t[W:c1]], axis=0),
        "w_dt_t": jnp.pad(in_t[c1:c2], ((0, LANE - SSD_HEADS), (0, 0))),
        "w_out": gathered[1].reshape(-1, D), "conv_w": vec["ssd_conv_w"][0], "conv_b": ssd_conv_b,
        "pool_norm": vec["pool_norm"], "pool_b": vec["pool_b"], "pool_scale": vec["pool_scale"],
        "hyb_norm": hyb_norm, "mlp_norm": mlp_norm, "out_norm": ssd_out_norm, "q_norm": sb_q_norm,
        "k_norm": sb_k_norm, "ssd_d": ssd_d, "ssd_a_log": ssd_a_log, "ssd_dt_bias": ssd_dt_bias,
    }

    sq, grad_x, g, reduced = _local_step(x[0], loss_target[0], p, [mlp_w_up.astype(BF16), mlp_w_down.astype(BF16),
                                                                  pool_w.astype(BF16)])
    loss = lax.psum(0.5 * jnp.sum(sq) / D, ("x", "y", "c"))

    res = {}
    grad_in = _adamw_sharded(*reduced["w_in"], None, None, None, "grad_sum_w_in").T
    res["hyb_w_in"] = [a[None] for a in (grad_in, *_adamw_plain(grad_in, hyb_w_in[0], m_hyb_w_in[0], v_hyb_w_in[0],
                                                                "adamw_w_in"))]
    res["hyb_w_out"] = [a[None] for a in _adamw_sharded(*reduced["w_out"], hyb_w_out[0], m_hyb_w_out[0],
                                                        v_hyb_w_out[0], "adamw_w_out")]
    for t, n in (("w_up", "mlp_w_up"), ("w_down", "mlp_w_down")):
        layers = [_adamw_sharded(*reduced[f"{t}{l}"], args[n][l], args["m_" + n][l], args["v_" + n][l],
                                 f"adamw_{n}{l}") for l in range(2)]
        res[n] = [jnp.stack([layers[0][k], layers[1][k]]) for k in range(4)]
    for t, group, cols in (("pool", POOL_SHARDED, PACK_COLS), ("conv", ("ssd_conv_w",), LANE)):
        packed = [_pack([args[pre + n] for n in group], cols, 8, F32) for pre in ("", "m_", "v_")]
        small = [_unpack(o, [wt[n].shape for n in group]) for o in _adamw_sharded(*reduced[t], *packed, f"adamw_{t}")]
        for i, n in enumerate(group):
            res[n] = [small[k][i] for k in range(4)]

    parts = _run_job(_gather_job([_pack([g[n] for n in REPLICATED], LANE, 8, F32)]), "gather_vector_grads")[0]
    packed = [_pack([args[pre + n] for n in REPLICATED], LANE, 8, F32) for pre in ("", "m_", "v_")]
    shapes = [wt[n].shape for n in REPLICATED]
    repl = [_unpack(o, shapes) for o in _adamw_replicated(parts, *packed)]
    for i, n in enumerate(REPLICATED):
        res[n] = [repl[k][i] for k in range(4)]

    outs = [res[n][k] for k in range(4) for n in names]
    return (loss, grad_x[None], *outs)
```

```python
import functools
import math
from typing import Callable, NamedTuple, Optional

import jax
import jax.numpy as jnp
from jax import lax
from jax.experimental import pallas as pl
from jax.experimental.pallas import tpu as pltpu

F32 = jnp.float32
BF16 = jnp.bfloat16
EPS = 1e-6
V7X_VMEM_LIMIT = 56 * 1024 * 1024
MESH = pl.DeviceIdType.MESH
ANY = pl.BlockSpec(memory_space=pl.ANY)
N_DEV = 8

SSD_HEADS = 32
SSD_HEAD_DIM = 64
SSD_STATE = 128
SSD_GROUPS = 4
SSD_CHUNK = 128
GROUP_W = SSD_HEADS * SSD_HEAD_DIM // SSD_GROUPS
HEADS_PER_GROUP = SSD_HEADS // SSD_GROUPS
SB_HEADS = 16
SB_DIM = 128
POOL_WINDOWS = (2, 4, 8, 16)
LANE = 128

ADAM_LR = 0.001
ADAM_B1 = 0.9
ADAM_B2 = 0.999
ADAM_EPS = 1e-08
ADAM_WD = 0.01
ADAM_STEP = 10

NN = (((1,), (0,)), ((), ()))
NT = (((1,), (1,)), ((), ()))
TN = (((0,), (0,)), ((), ()))

class _Job(NamedTuple):
    ins: tuple
    outs: tuple
    sems: tuple
    start: Callable
    mid: Optional[Callable]
    finish: Callable


def _call(body, *, name, grid, in_specs, out_specs, out_shape, scratch=(), prefetch=0, side=None, **params):
    if side is not None:
        single = not isinstance(out_shape, (list, tuple))
        out_specs = [out_specs] if single else list(out_specs)
        out_shape = [out_shape] if single else list(out_shape)
        n_in, n_out, n_scr = len(in_specs), len(out_shape), len(scratch)
        k_in, k_out = len(side.ins), len(side.outs)
        inner = body
        steps = math.prod(grid)

        def body(*refs):
            pre, rest = refs[:prefetch], refs[prefetch:]
            ins, s_in = rest[:n_in], rest[n_in:n_in + k_in]
            rest = rest[n_in + k_in:]
            outs, s_out = rest[:n_out], rest[n_out:n_out + k_out]
            rest = rest[n_out + k_out:]
            scr, s_sem = rest[:n_scr], rest[n_scr:]
            step = 0
            for axis, size in enumerate(grid):
                step = step * size + pl.program_id(axis)

            @pl.when(step == 0)
            def _():
                side.start(s_in, s_out, s_sem)

            inner(*pre, *ins, *outs, *scr)
            if side.mid is not None:
                @pl.when(step == (3 * steps) // 4)
                def _():
                    side.mid(s_in, s_out, s_sem)

            @pl.when(step == steps - 1)
            def _():
                side.finish(s_in, s_out, s_sem)

        params = dict(params, dimension_semantics=("arbitrary",) * len(grid))
        res = _call(body, name=name, grid=grid, in_specs=list(in_specs) + [ANY] * k_in,
                    out_specs=out_specs + [ANY] * k_out, out_shape=out_shape + list(side.outs),
                    scratch=list(scratch) + list(side.sems), prefetch=prefetch, **params)
        return lambda *args: (lambda r: ((r[0] if single else r[:n_out]), r[n_out:]))(res(*args, *side.ins))
    cp = pltpu.CompilerParams(vmem_limit_bytes=V7X_VMEM_LIMIT, **params)
    if prefetch:
        gs = pltpu.PrefetchScalarGridSpec(num_scalar_prefetch=prefetch, grid=grid, in_specs=in_specs,
                                          out_specs=out_specs, scratch_shapes=list(scratch))
        return pl.pallas_call(body, name=name, grid_spec=gs, out_shape=out_shape, compiler_params=cp)
    return pl.pallas_call(body, name=name, grid=grid, in_specs=in_specs, out_specs=out_specs,
                          out_shape=out_shape, scratch_shapes=list(scratch), compiler_params=cp)


def _dot(a, b, dims=NN):
    return lax.dot_general(a, b, dims, preferred_element_type=F32)


def _split3(x):
    hi = x.astype(BF16)
    r = x - hi.astype(F32)
    mid = r.astype(BF16)
    lo = (r - mid.astype(F32)).astype(BF16)
    return hi, mid, lo


def _dot3(x, m, dims=NN):
    hi, mid, lo = _split3(x)
    return _dot(hi, m, dims) + _dot(mid, m, dims) + _dot(lo, m, dims)


def _dot3l(m, x, dims=NN):
    hi, mid, lo = _split3(x)
    return _dot(m, hi, dims) + _dot(m, mid, dims) + _dot(m, lo, dims)


def _dot2(x, m):
    hi = x.astype(BF16)
    lo = (x - hi.astype(F32)).astype(BF16)
    return _dot(hi, m) + _dot(lo, m)


def _sigmoid(x):
    return 1.0 / (1.0 + jnp.exp(-x))


def _softplus(x):
    return jnp.maximum(x, 0.0) + jnp.log(1.0 + jnp.exp(-jnp.abs(x)))


def _iota(shape, dim):
    return lax.broadcasted_iota(jnp.int32, shape, dim)


def _matmul(a, b, *, mode, name, tm, tn, tk, extras=(), epilogue=None, out_dtypes=(F32,), mnk=None, b_spec=None,
            out_spec=None, out_dims=None, side=None):
    if mnk is not None:
        M, N, K = mnk
    elif mode == "tn":
        (K, M), N = a.shape, b.shape[1]
    else:
        (M, K), N = a.shape, b.shape[1 if mode == "nn" else 0]
    tm, tn, tk = min(tm, M), min(tn, N), min(tk, K)
    assert M % tm == 0 and N % tn == 0 and K % tk == 0, (name, M, N, K, tm, tn, tk)
    if mode == "nn":
        a_spec = pl.BlockSpec((tm, tk), lambda i, j, k: (i, k))
        b_spec = b_spec or pl.BlockSpec((tk, tn), lambda i, j, k: (k, j))
        dims = NN
    elif mode == "nt":
        a_spec = pl.BlockSpec((tm, tk), lambda i, j, k: (i, k))
        b_spec = b_spec or pl.BlockSpec((tn, tk), lambda i, j, k: (j, k))
        dims = NT
    else:
        a_spec = pl.BlockSpec((tk, tm), lambda i, j, k: (k, i))
        b_spec = b_spec or pl.BlockSpec((tk, tn), lambda i, j, k: (k, j))
        dims = TN
    nk = K // tk
    ex_specs = []
    for e in extras:
        if e.shape[0] == 1:
            ex_specs.append(pl.BlockSpec((1, tn), lambda i, j, k: (0, j)))
        else:
            ex_specs.append(pl.BlockSpec((tm, tn), lambda i, j, k: (i, j)))
    n_ex, n_out = len(extras), len(out_dtypes)

    def body(*refs):
        a_ref, b_ref = refs[0], refs[1]
        ex_refs = refs[2:2 + n_ex]
        o_refs = refs[2 + n_ex:2 + n_ex + n_out]

        def finish(r):
            outs = (r,) if epilogue is None else epilogue(r, *[e[...] for e in ex_refs])
            for o_ref, o in zip(o_refs, outs):
                o_ref[...] = o.astype(o_ref.dtype)

        part = _dot(a_ref[...].astype(BF16), b_ref[...].astype(BF16), dims)
        if nk == 1:
            finish(part)
            return
        acc = refs[2 + n_ex + n_out]
        k = pl.program_id(2)

        @pl.when(k == 0)
        def _():
            acc[...] = part

        @pl.when(jnp.logical_and(k > 0, k < nk - 1))
        def _():
            acc[...] += part

        @pl.when(k == nk - 1)
        def _():
            finish(acc[...] + part)

    out_shape = [jax.ShapeDtypeStruct(out_dims or (M, N), d) for d in out_dtypes]
    out_specs = [out_spec or pl.BlockSpec((tm, tn), lambda i, j, k: (i, j)) for _ in out_dtypes]
    res = _call(body, name=name, grid=(M // tm, N // tn, nk), in_specs=[a_spec, b_spec] + ex_specs,
                out_specs=out_specs, out_shape=out_shape, scratch=[pltpu.VMEM((tm, tn), F32)] if nk > 1 else [],
                dimension_semantics=("parallel", "parallel", "arbitrary"), side=side)(a, b, *extras)
    if side is not None:
        return (res[0] if n_out > 1 else res[0][0]), res[1]
    return res if n_out > 1 else res[0]


def _rowwise(fn, *, name, T, tm, tiles, vecs, out_tiles, out_vecs):
    n_t, n_v, n_ot, n_ov = len(tiles), len(vecs), len(out_tiles), len(out_vecs)

    def body(*refs):
        ins = [r[...] for r in refs[:n_t + n_v]]
        outs = fn(*ins)
        ot_refs = refs[n_t + n_v:n_t + n_v + n_ot]
        ov_refs = refs[n_t + n_v + n_ot:]
        for r, o in zip(ot_refs, outs[:n_ot]):
            r[...] = o.astype(r.dtype)
        if n_ov:
            first = pl.program_id(0) == 0

            @pl.when(first)
            def _():
                for r, o in zip(ov_refs, outs[n_ot:]):
                    r[...] = o

            @pl.when(jnp.logical_not(first))
            def _():
                for r, o in zip(ov_refs, outs[n_ot:]):
                    r[...] += o

    in_specs = [pl.BlockSpec((tm, w), functools.partial(lambda i, cb: (i, cb), cb=cb)) for _, w, cb in tiles]
    in_specs += [pl.BlockSpec(v.shape, lambda i: (0, 0)) for v in vecs]
    out_specs = [pl.BlockSpec((tm, w), lambda i: (i, 0)) for w, _ in out_tiles]
    out_specs += [pl.BlockSpec((r, w), lambda i: (0, 0)) for r, w in out_vecs]
    out_shape = [jax.ShapeDtypeStruct((T, w), d) for w, d in out_tiles]
    out_shape += [jax.ShapeDtypeStruct((r, w), F32) for r, w in out_vecs]
    return _call(body, name=name, grid=(T // tm,), in_specs=in_specs, out_specs=out_specs, out_shape=out_shape,
                 dimension_semantics=("arbitrary",))(*[t[0] for t in tiles], *vecs)


def _colsum(x):
    return jnp.sum(x, axis=0, keepdims=True)


def _rms_fwd(x, g):
    r = lax.rsqrt(jnp.mean(x * x, axis=-1, keepdims=True) + EPS)
    return x * r * g


def _rms_bwd(dh, x, g):
    r = lax.rsqrt(jnp.mean(x * x, axis=-1, keepdims=True) + EPS)
    xh = x * r
    dxh = dh * g
    dx = r * (dxh - xh * jnp.mean(dxh * xh, axis=-1, keepdims=True))
    return dx, _colsum(dh * xh)


def _rmsnorm(x, g, *, name, dtype):
    T, D = x.shape
    return _rowwise(lambda xv, gv: (_rms_fwd(xv, gv),), name=name, T=T, tm=256, tiles=[(x, D, 0)], vecs=[g],
                    out_tiles=[(D, dtype)], out_vecs=[])[0]


def _rmsnorm_bwd(dh, x, g, dres, *, name):
    T, D = x.shape

    def fn(dhv, xv, drv, gv):
        dx, dg = _rms_bwd(dhv, xv, gv)
        return drv + dx, drv + dx, dg

    return _rowwise(fn, name=name, T=T, tm=256, tiles=[(dh, D, 0), (x, D, 0), (dres, D, 0)], vecs=[g],
                    out_tiles=[(D, F32), (D, BF16)], out_vecs=[(1, D)])


def _group_slices(width, group):
    return [slice(i, i + group) for i in range(0, width, group)]


def _gate_norm_fwd(ypre, proj, gain):
    T, W = ypre.shape

    def fn(y, z, g):
        gated = y * (z * _sigmoid(z))
        return (jnp.concatenate([_rms_fwd(gated[:, s], g[:, s]) for s in _group_slices(W, GROUP_W)], axis=1),)

    return _rowwise(fn, name="ssd_gate_norm", T=T, tm=256, tiles=[(ypre, W, 0), (proj, W, 0)], vecs=[gain],
                    out_tiles=[(W, BF16)], out_vecs=[])[0]


def _gate_norm_bwd(dmerged, ypre, proj, gain):
    T, W = ypre.shape

    def fn(do, y, z, g):
        sg = _sigmoid(z)
        sz = z * sg
        gated = y * sz
        parts = [_rms_bwd(do[:, s], gated[:, s], g[:, s]) for s in _group_slices(W, GROUP_W)]
        dgated = jnp.concatenate([p[0] for p in parts], axis=1)
        dgain = jnp.concatenate([p[1] for p in parts], axis=1)
        return dgated * sz, dgated * y * (sg * (1.0 + z * (1.0 - sg))), dgain

    return _rowwise(fn, name="ssd_gate_norm_bwd", T=T, tm=256, tiles=[(dmerged, W, 0), (ypre, W, 0), (proj, W, 0)],
                    vecs=[gain], out_tiles=[(W, F32), (W, BF16)], out_vecs=[(1, W)])


def _qk_norm_fwd(proj, qg, kg, W):
    T = proj.shape[0]

    def fn(q, k, gq, gk):
        sl = _group_slices(W, SB_DIM)
        return (jnp.concatenate([_rms_fwd(q[:, s], gq) for s in sl], axis=1),
                jnp.concatenate([_rms_fwd(k[:, s], gk) for s in sl], axis=1))

    return _rowwise(fn, name="sb_qk_norm", T=T, tm=256, tiles=[(proj, W, 1), (proj, W, 2)], vecs=[qg, kg],
                    out_tiles=[(W, BF16), (W, BF16)], out_vecs=[])


def _qk_norm_bwd(dqh, dkh, proj, qg, kg, W):
    T = proj.shape[0]

    def fn(dq, dk, q, k, gq, gk):
        sl = _group_slices(W, SB_DIM)
        pq = [_rms_bwd(dq[:, s], q[:, s], gq) for s in sl]
        pk = [_rms_bwd(dk[:, s], k[:, s], gk) for s in sl]
        return (jnp.concatenate([p[0] for p in pq], axis=1), jnp.concatenate([p[0] for p in pk], axis=1),
                sum(p[1] for p in pq), sum(p[1] for p in pk))

    return _rowwise(fn, name="sb_qk_norm_bwd", T=T, tm=256,
                    tiles=[(dqh, W, 0), (dkh, W, 0), (proj, W, 1), (proj, W, 2)], vecs=[qg, kg],
                    out_tiles=[(W, BF16), (W, BF16)], out_vecs=[(1, SB_DIM), (1, SB_DIM)])


def _loss_grad(y, target):
    T, D = y.shape

    def fn(yv, tv):
        err = yv - tv
        return err * (1.0 / D), err * (1.0 / D), _colsum(err * err)

    return _rowwise(fn, name="loss_grad", T=T, tm=256, tiles=[(y, D, 0), (target, D, 0)], vecs=[],
                    out_tiles=[(D, F32), (D, BF16)], out_vecs=[(1, D)])


def _pool_scale_bwd(dx, ypre, scale):
    T, D = dx.shape

    def fn(d, yp, s):
        dpre = d * s
        return dpre, _colsum(d * yp), _colsum(dpre)

    return _rowwise(fn, name="pool_scale_bwd", T=T, tm=256, tiles=[(dx, D, 0), (ypre, D, 0)], vecs=[scale],
                    out_tiles=[(D, BF16)], out_vecs=[(1, D), (1, D)])


ROWS = 512


def _past(cur, prev, k):
    row = _iota(cur.shape, 0)
    rc = pltpu.roll(cur, k, 0)
    if prev is None:
        return jnp.where(row >= k, rc, 0.0)
    return jnp.where(row >= k, rc, pltpu.roll(prev, k, 0))


def _future(cur, nxt, k):
    n = cur.shape[0]
    row = _iota(cur.shape, 0)
    rc = pltpu.roll(cur, n - k, 0)
    if nxt is None:
        return jnp.where(row < n - k, rc, 0.0)
    return jnp.where(row < n - k, rc, pltpu.roll(nxt, n - k, 0))


def _chunk(ref, ci):
    return ref[ci * ROWS:(ci + 1) * ROWS, :]


def _conv_pre(x_ref, w, b, ci):
    cur = _chunk(x_ref, ci)
    prev = _chunk(x_ref, ci - 1) if ci > 0 else None
    taps = [_past(cur, prev, 3), _past(cur, prev, 2), _past(cur, prev, 1), cur]
    xc = b + sum(w[j:j + 1, :] * taps[j] for j in range(4))
    return xc, taps


CONV_COLS = 256


def _conv_silu_fwd(proj, col0, width, conv_w, conv_b):
    T = proj.shape[0]

    def body(x_ref, w_ref, b_ref, o_ref):
        w, b = w_ref[...], b_ref[...]
        for ci in range(T // ROWS):
            xc, _ = _conv_pre(x_ref, w, b, ci)
            o_ref[ci * ROWS:(ci + 1) * ROWS, :] = xc * _sigmoid(xc)

    cb0 = col0 // CONV_COLS
    return _call(body, name="ssd_conv_silu", grid=(width // CONV_COLS,),
                 in_specs=[pl.BlockSpec((T, CONV_COLS), lambda j: (0, cb0 + j)),
                           pl.BlockSpec((4, CONV_COLS), lambda j: (0, j)),
                           pl.BlockSpec((1, CONV_COLS), lambda j: (0, j))],
                 out_specs=pl.BlockSpec((T, CONV_COLS), lambda j: (0, j)),
                 out_shape=jax.ShapeDtypeStruct((T, width), F32))(proj, conv_w, conv_b)


def _conv_silu_bwd(dxa, proj, col0, width, conv_w, conv_b):
    T = proj.shape[0]
    nchunk = T // ROWS

    def body(d_ref, x_ref, w_ref, b_ref, dx_ref, dw_ref, db_ref, dxc_ref):
        w, b = w_ref[...], b_ref[...]
        dw = [jnp.zeros((1, CONV_COLS), F32) for _ in range(4)]
        db = jnp.zeros((1, CONV_COLS), F32)
        for ci in range(nchunk):
            xc, taps = _conv_pre(x_ref, w, b, ci)
            sg = _sigmoid(xc)
            dxc = _chunk(d_ref, ci) * (sg * (1.0 + xc * (1.0 - sg)))
            dxc_ref[ci * ROWS:(ci + 1) * ROWS, :] = dxc
            db = db + _colsum(dxc)
            dw = [dw[j] + _colsum(dxc * taps[j]) for j in range(4)]
        dw_ref[...] = jnp.concatenate(dw + [jnp.zeros((4, CONV_COLS), F32)], axis=0)
        db_ref[...] = db
        for ci in range(nchunk):
            cur = _chunk(dxc_ref, ci)
            nxt = _chunk(dxc_ref, ci + 1) if ci + 1 < nchunk else None
            dx = (w[3:4, :] * cur + w[2:3, :] * _future(cur, nxt, 1) + w[1:2, :] * _future(cur, nxt, 2)
                  + w[0:1, :] * _future(cur, nxt, 3))
            dx_ref[ci * ROWS:(ci + 1) * ROWS, :] = dx.astype(dx_ref.dtype)

    cb0 = col0 // CONV_COLS
    return _call(body, name="ssd_conv_silu_bwd", grid=(width // CONV_COLS,),
                 in_specs=[pl.BlockSpec((T, CONV_COLS), lambda j: (0, j)),
                           pl.BlockSpec((T, CONV_COLS), lambda j: (0, cb0 + j)),
                           pl.BlockSpec((4, CONV_COLS), lambda j: (0, j)),
                           pl.BlockSpec((1, CONV_COLS), lambda j: (0, j))],
                 out_specs=[pl.BlockSpec((T, CONV_COLS), lambda j: (0, j)),
                            pl.BlockSpec((8, CONV_COLS), lambda j: (0, j)),
                            pl.BlockSpec((1, CONV_COLS), lambda j: (0, j))],
                 out_shape=[jax.ShapeDtypeStruct((T, width), BF16), jax.ShapeDtypeStruct((8, width), F32),
                            jax.ShapeDtypeStruct((1, width), F32)],
                 scratch=[pltpu.VMEM((T, CONV_COLS), F32)])(dxa, proj, conv_w, conv_b)


def _window_count(ci, win, shape):
    t = (_iota(shape, 0) + ci * ROWS + 1).astype(F32)
    return jnp.minimum(t, float(win))


def _pool_diff_fwd(h):
    T, D = h.shape
    per_group = D // len(POOL_WINDOWS) // LANE

    def body(h_ref, o_ref):
        j = pl.program_id(0)
        for gi, win in enumerate(POOL_WINDOWS):
            @pl.when(j // per_group == gi)
            def _(win=win):
                for ci in range(T // ROWS):
                    cur = _chunk(h_ref, ci)
                    prev = _chunk(h_ref, ci - 1) if ci > 0 else None
                    s = cur
                    for k in range(1, win):
                        s = s + _past(cur, prev, k)
                    d = s / _window_count(ci, win, cur.shape) - cur
                    o_ref[ci * ROWS:(ci + 1) * ROWS, :] = d.astype(o_ref.dtype)

    return _call(body, name="pool_diff", grid=(D // LANE,), in_specs=[pl.BlockSpec((T, LANE), lambda j: (0, j))],
                 out_specs=pl.BlockSpec((T, LANE), lambda j: (0, j)),
                 out_shape=jax.ShapeDtypeStruct((T, D), BF16))(h)


def _pool_diff_bwd(dd):
    T, D = dd.shape
    per_group = D // len(POOL_WINDOWS) // LANE
    nchunk = T // ROWS

    def body(d_ref, o_ref):
        j = pl.program_id(0)
        for gi, win in enumerate(POOL_WINDOWS):
            @pl.when(j // per_group == gi)
            def _(win=win):
                for ci in range(nchunk):
                    cur = _chunk(d_ref, ci)
                    q = cur / _window_count(ci, win, cur.shape)
                    qn = None
                    if ci + 1 < nchunk:
                        qn = _chunk(d_ref, ci + 1) / _window_count(ci + 1, win, cur.shape)
                    s = q - cur
                    for k in range(1, win):
                        s = s + _future(q, qn, k)
                    o_ref[ci * ROWS:(ci + 1) * ROWS, :] = s

    return _call(body, name="pool_diff_bwd", grid=(D // LANE,), in_specs=[pl.BlockSpec((T, LANE), lambda j: (0, j))],
                 out_specs=pl.BlockSpec((T, LANE), lambda j: (0, j)),
                 out_shape=jax.ShapeDtypeStruct((T, D), F32))(dd)


def _pool_mm_fwd(d, w, b, scale, x):
    T, D = d.shape
    G = w.shape[1]
    tm = 512

    def body(d_ref, w_ref, b_ref, s_ref, x_ref, yp_ref, o_ref):
        yp = _dot(d_ref[...], w_ref[...]) + b_ref[...]
        yp_ref[...] = yp
        o_ref[...] = x_ref[...] + yp * s_ref[...]

    tile = pl.BlockSpec((tm, G), lambda i, g: (i, g))
    vec = pl.BlockSpec((1, G), lambda i, g: (0, g))
    return _call(body, name="pool_mm", grid=(T // tm, D // G),
                 in_specs=[tile, pl.BlockSpec((None, G, G), lambda i, g: (g, 0, 0)), vec, vec, tile],
                 out_specs=[tile, tile],
                 out_shape=[jax.ShapeDtypeStruct((T, D), F32), jax.ShapeDtypeStruct((T, D), F32)])(d, w, b, scale, x)


def _pool_mm_dx(dpre, w):
    T, D = dpre.shape
    G = w.shape[1]
    tm = 512

    def body(d_ref, w_ref, o_ref):
        o_ref[...] = _dot(d_ref[...], w_ref[...], NT)

    tile = pl.BlockSpec((tm, G), lambda i, g: (i, g))
    return _call(body, name="pool_mm_dx", grid=(T // tm, D // G),
                 in_specs=[tile, pl.BlockSpec((None, G, G), lambda i, g: (g, 0, 0))], out_specs=tile,
                 out_shape=jax.ShapeDtypeStruct((T, D), F32))(dpre, w)


def _pool_mm_dw(d, dpre):
    T, D = d.shape
    G = D // len(POOL_WINDOWS)
    tk = 512

    def body(d_ref, p_ref, o_ref):
        @pl.when(pl.program_id(1) == 0)
        def _():
            o_ref[...] = jnp.zeros_like(o_ref)

        o_ref[...] += _dot(d_ref[...], p_ref[...], TN)

    tile = pl.BlockSpec((tk, G), lambda g, k: (k, g))
    return _call(body, name="pool_mm_dw", grid=(D // G, T // tk), in_specs=[tile, tile],
                 out_specs=pl.BlockSpec((None, G, G), lambda g, k: (g, 0, 0)),
                 out_shape=jax.ShapeDtypeStruct((D // G, G, G), F32))(d, dpre)


def _ssd_consts(ssd_d, a_log, dt_bias):
    head = jnp.arange(LANE)[:, None]
    lane = jnp.arange(GROUP_W)[None, :]
    ex = jnp.stack([(head == g * HEADS_PER_GROUP + lane // SSD_HEAD_DIM) for g in range(SSD_GROUPS)])
    d_lanes = jnp.repeat(ssd_d.reshape(-1), SSD_HEAD_DIM).reshape(1, -1)
    pad = lambda v: jnp.pad(v.reshape(1, -1), ((0, 0), (0, LANE - SSD_HEADS)))
    return ex.astype(BF16), d_lanes, pad(a_log), pad(dt_bias)


def _ssd_chunk(xs, bm, cm, dtr, bias, alog, ex):
    L = SSD_CHUNK
    row, col = _iota((L, L), 0), _iota((L, L), 1)
    causal = col <= row
    ltri = causal.astype(BF16)
    a_row = -jnp.exp(alog)
    dt = _softplus(dtr + bias)
    da = dt * a_row
    dt_l = _dot3(dt, ex)
    da_l = _dot3(da, ex)
    acs_l = _dot3l(ltri, da_l)
    acs_r = _dot3(da, (row <= col).astype(BF16), TN)
    last_l = acs_l[L - 1:L, :]
    e_l = jnp.exp(last_l - acs_l)
    f_l = jnp.exp(acs_l)
    cd_l = jnp.exp(last_l)
    xdt = xs * dt_l
    cb = _dot(cm.astype(BF16), bm.astype(BF16), NT)
    return dict(causal=causal, dt=dt, da=da, dt_l=dt_l, acs_l=acs_l, acs_r=acs_r, e_l=e_l, f_l=f_l, cd_l=cd_l,
                xdt=xdt, cb=cb, a_row=a_row, ltri=ltri)


def _head_decay(q, acsrow_ref, g, r):
    colv = q["acs_l"][:, r * SSD_HEAD_DIM:r * SSD_HEAD_DIM + 1]
    rowv = acsrow_ref[pl.ds(g * HEADS_PER_GROUP + r, 1), :]
    return jnp.exp(jnp.where(q["causal"], colv - rowv, -1e30))


def _ssd_specs(T):
    L = SSD_CHUNK
    xs = pl.BlockSpec((L, GROUP_W), lambda g, c: (c, g))
    nb = SSD_HEADS * SSD_HEAD_DIM // SSD_STATE
    bm = pl.BlockSpec((L, SSD_STATE), lambda g, c: (c, nb + g))
    cm = pl.BlockSpec((L, SSD_STATE), lambda g, c: (c, nb + SSD_GROUPS + g))
    dtr = pl.BlockSpec((L, LANE), lambda g, c: (c, 0))
    vec = pl.BlockSpec((1, LANE), lambda g, c: (0, 0))
    ex = pl.BlockSpec((None, LANE, GROUP_W), lambda g, c: (g, 0, 0))
    dl = pl.BlockSpec((1, GROUP_W), lambda g, c: (0, g))
    return xs, bm, cm, dtr, vec, ex, dl


def _ssd_fwd(xbc, proj_dt, bias, alog, ex, d_lanes):
    T = xbc.shape[0]
    L, nc, W = SSD_CHUNK, T // SSD_CHUNK, SSD_HEADS * SSD_HEAD_DIM

    def body(xs_ref, b_ref, c_ref, dtr_ref, bias_ref, alog_ref, ex_ref, dl_ref, y_ref, st_ref, state, acsrow):
        g, c = pl.program_id(0), pl.program_id(1)

        @pl.when(c == 0)
        def _():
            state[...] = jnp.zeros_like(state)

        xs, bm, cm = xs_ref[...], b_ref[...], c_ref[...]
        q = _ssd_chunk(xs, bm, cm, dtr_ref[...], bias_ref[...], alog_ref[...], ex_ref[...])
        acsrow[...] = q["acs_r"]
        prev = state[...]
        st_ref[...] = prev
        xdt_b = q["xdt"].astype(BF16)
        yoff = q["f_l"] * _dot(cm.astype(BF16), prev.astype(BF16))
        lane = _iota((L, LANE), 1)
        for p in range(HEADS_PER_GROUP // 2):
            sl = slice(p * LANE, (p + 1) * LANE)
            ma = (_head_decay(q, acsrow, g, 2 * p) * q["cb"]).astype(BF16)
            mb = (_head_decay(q, acsrow, g, 2 * p + 1) * q["cb"]).astype(BF16)
            yd = jnp.where(lane < SSD_HEAD_DIM, _dot(ma, xdt_b[:, sl]), _dot(mb, xdt_b[:, sl]))
            y_ref[:, sl] = yd + yoff[:, sl] + dl_ref[:, sl] * xs[:, sl]
        st_new = _dot(bm.astype(BF16), (q["xdt"] * q["e_l"]).astype(BF16), TN)
        state[...] = q["cd_l"] * prev + st_new

    xs, bm, cm, dtr, vec, exs, dl = _ssd_specs(T)
    return _call(body, name="ssd_scan", grid=(SSD_GROUPS, nc), in_specs=[xs, bm, cm, dtr, vec, vec, exs, dl],
                 out_specs=[pl.BlockSpec((L, GROUP_W), lambda g, c: (c, g)),
                            pl.BlockSpec((None, None, SSD_STATE, GROUP_W), lambda g, c: (c, g, 0, 0))],
                 out_shape=[jax.ShapeDtypeStruct((T, W), F32),
                            jax.ShapeDtypeStruct((nc, SSD_GROUPS, SSD_STATE, GROUP_W), F32)],
                 scratch=[pltpu.VMEM((SSD_STATE, GROUP_W), F32), pltpu.VMEM((LANE, L), F32)],
                 dimension_semantics=("arbitrary", "arbitrary"))(xbc, xbc, xbc, proj_dt, bias, alog, ex, d_lanes)


def _ssd_bwd(dy, xbc, proj_dt, states, bias, alog, ex, d_lanes):
    T = xbc.shape[0]
    L, nc, W = SSD_CHUNK, T // SSD_CHUNK, SSD_HEADS * SSD_HEAD_DIM
    P = SSD_HEAD_DIM

    def body(dy_ref, xs_ref, b_ref, c_ref, dtr_ref, st_ref, bias_ref, alog_ref, ex_ref, dl_ref,
             dxs_ref, db_ref, dc_ref, ddt_ref, hv_ref, dstate, acsrow):
        g, c = pl.program_id(0), pl.program_id(1)

        @pl.when(c == 0)
        def _():
            dstate[...] = jnp.zeros_like(dstate)

        @pl.when(jnp.logical_and(g == 0, c == 0))
        def _():
            hv_ref[...] = jnp.zeros_like(hv_ref)

        xs, bm, cm, ex = xs_ref[...], b_ref[...], c_ref[...], ex_ref[...]
        dtr, bias = dtr_ref[...], bias_ref[...]
        q = _ssd_chunk(xs, bm, cm, dtr, bias, alog_ref[...], ex)
        acsrow[...] = q["acs_r"]
        dyv = dy_ref[...]
        prev = st_ref[...]
        dst = dstate[...]
        bm_b, cm_b = bm.astype(BF16), cm.astype(BF16)
        row128 = _iota((L, LANE), 0)
        lane = _iota((L, LANE), 1)
        row_w = _iota((L, GROUP_W), 0)

        dxs = dl_ref[...] * dyv
        d_dl = _colsum(dyv * xs)
        gmat = _dot(cm_b, prev.astype(BF16))
        dg_b = (dyv * q["f_l"]).astype(BF16)
        dacs = dyv * q["f_l"] * gmat
        dcm = _dot(dg_b, prev.astype(BF16), NT)
        dprev = _dot(cm_b, dg_b, TN)
        dcd = _colsum(dst * prev)
        dlast = dcd * q["cd_l"]
        xe = q["xdt"] * q["e_l"]
        dxe = _dot(bm_b, dst.astype(BF16))
        dbm = _dot(xe.astype(BF16), dst.astype(BF16), NT)
        dxdt = dxe * q["e_l"]
        t1 = dxe * xe
        dacs = dacs - t1
        dlast = dlast + _colsum(t1)
        dstate[...] = dprev + q["cd_l"] * dst
        xdt_b = q["xdt"].astype(BF16)
        dcb = jnp.zeros((L, L), F32)
        dacs_head = []
        dxdt_diag = []
        for p in range(HEADS_PER_GROUP // 2):
            sl = slice(p * LANE, (p + 1) * LANE)
            xp = xdt_b[:, sl]
            dyp = dyv[:, sl]
            vals, dx_parts = [], []
            for half in range(2):
                in_half = (lane < P) if half == 0 else (lane >= P)
                decay = _head_decay(q, acsrow, g, 2 * p + half)
                m = decay * q["cb"]
                dyh = jnp.where(in_half, dyp, 0.0).astype(BF16)
                dm = jnp.where(q["causal"], _dot(dyh, xp, NT), 0.0)
                dcb = dcb + dm * decay
                dseg = dm * m
                rs = jnp.sum(dseg, axis=1, keepdims=True)
                cs = jnp.broadcast_to(_colsum(dseg), (L, L)).T[:, 0:1]
                vals.append(rs - cs)
                dx_parts.append(_dot(m.astype(BF16), dyp.astype(BF16), TN))
            dxdt_diag.append(jnp.where(lane < P, dx_parts[0], dx_parts[1]))
            dacs_head.append(jnp.where(lane == 0, vals[0], jnp.where(lane == P, vals[1], 0.0)))
        dxdt = dxdt + jnp.concatenate(dxdt_diag, axis=1)
        dacs = dacs + jnp.concatenate(dacs_head, axis=1)
        dacs = jnp.where(row_w == L - 1, dacs + dlast, dacs)
        dcb_b = dcb.astype(BF16)
        dcm = dcm + _dot(dcb_b, bm_b)
        dbm = dbm + _dot(dcb_b, cm_b, TN)
        dacs_h = _dot3(dacs, ex, NT)
        dda = _dot3l((row128 <= lane).astype(BF16), dacs_h)
        ddt = dda * q["a_row"] + _dot3(dxdt * xs, ex, NT)
        dxs = dxs + dxdt * q["dt_l"]
        ddtr = ddt * _sigmoid(dtr + bias)
        d_alog = _colsum(dda * q["dt"]) * q["a_row"]
        d_dh = _dot3(jnp.broadcast_to(d_dl, (8, GROUP_W)), ex, NT)[0:1, :]
        dxs_ref[...] = dxs
        db_ref[...] = dbm
        dc_ref[...] = dcm
        ddt_ref[...] = ddtr
        hv_ref[0:1, :] += d_dh
        hv_ref[1:2, :] += d_alog
        hv_ref[2:3, :] += _colsum(ddtr)

    rev = lambda c: nc - 1 - c
    xs = pl.BlockSpec((L, GROUP_W), lambda g, c: (rev(c), g))
    nb = W // SSD_STATE
    bm = pl.BlockSpec((L, SSD_STATE), lambda g, c: (rev(c), nb + g))
    cm = pl.BlockSpec((L, SSD_STATE), lambda g, c: (rev(c), nb + SSD_GROUPS + g))
    dtr = pl.BlockSpec((L, LANE), lambda g, c: (rev(c), 0))
    st = pl.BlockSpec((None, None, SSD_STATE, GROUP_W), lambda g, c: (rev(c), g, 0, 0))
    vec = pl.BlockSpec((1, LANE), lambda g, c: (0, 0))
    exs = pl.BlockSpec((None, LANE, GROUP_W), lambda g, c: (g, 0, 0))
    dl = pl.BlockSpec((1, GROUP_W), lambda g, c: (0, g))
    grp = pl.BlockSpec((L, SSD_STATE), lambda g, c: (rev(c), g))
    return _call(body, name="ssd_scan_bwd", grid=(SSD_GROUPS, nc),
                 in_specs=[xs, xs, bm, cm, dtr, st, vec, vec, exs, dl],
                 out_specs=[xs, grp, grp, grp, pl.BlockSpec((8, LANE), lambda g, c: (0, 0))],
                 out_shape=[jax.ShapeDtypeStruct((T, W), F32),
                            jax.ShapeDtypeStruct((T, SSD_GROUPS * SSD_STATE), F32),
                            jax.ShapeDtypeStruct((T, SSD_GROUPS * SSD_STATE), F32),
                            jax.ShapeDtypeStruct((T, SSD_GROUPS * LANE), F32),
                            jax.ShapeDtypeStruct((8, LANE), F32)],
                 scratch=[pltpu.VMEM((SSD_STATE, GROUP_W), F32), pltpu.VMEM((LANE, L), F32)],
                 dimension_semantics=("arbitrary", "arbitrary"))(dy, xbc, xbc, xbc, proj_dt, states, bias, alog,
                                                                 ex, d_lanes)


SB_TQ = 512
SB_SUB = 128


def _sb_logits(q, kb, scale, mask):
    z = _dot(q, kb, NT) * scale
    lb = jnp.minimum(z, 0.0) - jnp.log(1.0 + jnp.exp(-jnp.abs(z)))
    lk = lb - z
    return lb, lk if mask is None else jnp.where(mask, lk, 0.0)


def _sb_weights(lb, lk, mask, run):
    n = SB_SUB
    strict = (_iota((n, n), 0) > _iota((n, n), 1)).astype(BF16)
    ws = [None] * (lb.shape[1] // n)
    for s in reversed(range(len(ws))):
        sl = slice(s * n, (s + 1) * n)
        w = jnp.exp(lb[:, sl] + (_dot2(lk[:, sl], strict) + run))
        ws[s] = w if mask is None else jnp.where(mask[:, sl], w, 0.0)
        run = run + jnp.sum(lk[:, sl], axis=1, keepdims=True)
    return jnp.concatenate(ws, axis=1), run


def _sb_diagonal(tq):
    return _iota((tq, tq), 1) < _iota((tq, tq), 0)


SB_LOG_CUT = -110.0


def _sb_more(it, qi, run):
    return jnp.logical_and(it <= qi, jnp.max(run) > SB_LOG_CUT)


def _sb_fwd(qh, kh, proj, v_cb0, side=None):
    T, W = qh.shape
    tq = min(SB_TQ, T)
    scale = SB_DIM ** -0.5

    def body(q_ref, k_ref, v_ref, o_ref, ob_ref):
        qi = pl.program_id(1)
        q = q_ref[...]

        def block(it, carry, mask=None):
            run, acc = carry
            kstart = pl.multiple_of((qi - it) * tq, tq)
            lb, lk = _sb_logits(q, k_ref[pl.ds(kstart, tq), :], scale, mask)
            w, run = _sb_weights(lb, lk, mask, run)
            acc = acc + _dot2(w, v_ref[pl.ds(kstart, tq), :].astype(BF16))
            return run, acc

        first = block(0, (jnp.zeros((tq, 1), F32), jnp.zeros((tq, SB_DIM), F32)), _sb_diagonal(tq))
        _, _, acc = lax.while_loop(lambda c: _sb_more(c[0], qi, c[1]), lambda c: (c[0] + 1, *block(c[0], c[1:])),
                                   (jnp.int32(1), *first))
        o_ref[...] = acc
        ob_ref[...] = acc.astype(BF16)

    tile = pl.BlockSpec((tq, SB_DIM), lambda h, i: (i, h))
    return _call(body, name="sb_attn", grid=(W // SB_DIM, T // tq),
                 in_specs=[tile, pl.BlockSpec((T, SB_DIM), lambda h, i: (0, h)),
                           pl.BlockSpec((T, SB_DIM), lambda h, i: (0, v_cb0 + h))],
                 out_specs=[tile, tile],
                 out_shape=[jax.ShapeDtypeStruct((T, W), F32), jax.ShapeDtypeStruct((T, W), BF16)], side=side,
                 dimension_semantics=("arbitrary", "arbitrary"))(qh, kh, proj)


def _sb_bwd(qh, kh, proj, v_cb0, o, dmerged, do_cb0, side=None):
    T, W = qh.shape
    tq = min(SB_TQ, T)
    n = SB_SUB
    scale = SB_DIM ** -0.5

    def body(q_ref, k_ref, v_ref, o_ref, do_ref, dq_ref, dk_ref, dv_ref):
        qi = pl.program_id(1)

        @pl.when(qi == 0)
        def _():
            dk_ref[...] = jnp.zeros_like(dk_ref)
            dv_ref[...] = jnp.zeros_like(dv_ref)

        q = q_ref[...]
        do = do_ref[...]
        do_b = do.astype(BF16)
        etot = jnp.sum(do_b.astype(F32) * o_ref[...], axis=1, keepdims=True)
        incl = (_iota((n, n), 0) >= _iota((n, n), 1)).astype(BF16)

        def block(it, carry, mask=None):
            run, erun, dq = carry
            kstart = pl.multiple_of((qi - it) * tq, tq)
            kb = k_ref[pl.ds(kstart, tq), :]
            vb = v_ref[pl.ds(kstart, tq), :].astype(BF16)
            lb, lk = _sb_logits(q, kb, scale, mask)
            w, run = _sb_weights(lb, lk, mask, run)
            e = _dot(do_b, vb, NT) * w
            beta = jnp.exp(lb)
            dzs = [None] * (tq // n)
            for s in reversed(range(tq // n)):
                sl = slice(s * n, (s + 1) * n)
                before = etot - erun - _dot3(e[:, sl], incl)
                dz = e[:, sl] * (1.0 - beta[:, sl]) - before * beta[:, sl]
                dzs[s] = dz if mask is None else jnp.where(mask[:, sl], dz, 0.0)
                erun = erun + jnp.sum(e[:, sl], axis=1, keepdims=True)
            dz = (jnp.concatenate(dzs, axis=1) * scale).astype(BF16)
            dq = dq + _dot(dz, kb)
            dk_ref[pl.ds(kstart, tq), :] += _dot(dz, q, TN)
            dv_ref[pl.ds(kstart, tq), :] += _dot(w.astype(BF16), do_b, TN)
            return run, erun, dq

        zero = jnp.zeros((tq, 1), F32)
        first = block(0, (zero, zero, jnp.zeros((tq, SB_DIM), F32)), _sb_diagonal(tq))
        out = lax.while_loop(lambda c: _sb_more(c[0], qi, c[1]), lambda c: (c[0] + 1, *block(c[0], c[1:])),
                             (jnp.int32(1), *first))
        dq_ref[...] = out[3]

    tile = pl.BlockSpec((tq, SB_DIM), lambda h, i: (i, h))
    full = pl.BlockSpec((T, SB_DIM), lambda h, i: (0, h))
    shp = jax.ShapeDtypeStruct((T, W), F32)
    return _call(body, name="sb_attn_bwd", grid=(W // SB_DIM, T // tq),
                 in_specs=[tile, full, pl.BlockSpec((T, SB_DIM), lambda h, i: (0, v_cb0 + h)), tile,
                           pl.BlockSpec((tq, SB_DIM), lambda h, i: (i, do_cb0 + h))],
                 out_specs=[tile, full, full], out_shape=[shp, shp, shp], side=side,
                 dimension_semantics=("arbitrary", "arbitrary"))(qh, kh, proj, o, dmerged)


MM_TK = 2048


def _mlp_fwd(x, g, w_up, w_down, layer, sides=(None, None)):
    T, D = x.shape
    fs = w_up.shape[2]
    F = N_DEV * fs
    h = _rmsnorm(x, g, name=f"mlp{layer}_norm", dtype=BF16)

    def relu_sq(acc):
        u = jnp.maximum(acc, 0.0)
        return u, u * u

    up = _matmul(h, w_up, mode="nn", name=f"mlp{layer}_up", tm=1024, tn=fs, tk=D, epilogue=relu_sq,
                 out_dtypes=(BF16, BF16), mnk=(T, F, D), side=sides[0],
                 b_spec=pl.BlockSpec((None, D, fs), lambda i, j, k: (j, 0, 0)))
    (u, s), res_up = up if sides[0] is not None else (up, None)
    y = _matmul(s, w_down, mode="nn", name=f"mlp{layer}_down", tm=1024, tn=1024, tk=fs, extras=(x,),
                epilogue=lambda acc, r: (acc + r,), mnk=(T, D, F), side=sides[1],
                b_spec=pl.BlockSpec((None, fs, 1024), lambda i, j, k: (k, 0, j)))
    y, res_down = y if sides[1] is not None else (y, None)
    return y, (h, u, s), (res_up, res_down)


def _mlp_bwd(dy, dy_b, x, g, w_up, w_down, saved, layer, pending=None):
    T, D = x.shape
    fs = w_up.shape[2]
    F = N_DEV * fs
    h, u, s = saved
    dw_down = _matmul(s, dy_b, mode="tn", name=f"mlp{layer}_dwdown", tm=1024, tn=1024, tk=MM_TK, side=pending)
    dw_down, pending_res = dw_down if pending is not None else (dw_down, None)
    dw_down = dw_down.reshape(N_DEV, -1, D)
    da, (r1_down,) = _matmul(dy_b, w_down, mode="nt", name=f"mlp{layer}_da", tm=1024, tn=fs, tk=D, extras=(u,),
                             epilogue=lambda acc, uv: (acc * (2.0 * uv.astype(F32)),), out_dtypes=(BF16,),
                             mnk=(T, F, D), side=_cores_job([dw_down]),
                             b_spec=pl.BlockSpec((None, fs, D), lambda i, j, k: (j, 0, 0)))
    pb_down, pm_down = _pair_sum(dw_down, r1_down, f"grad_pair_sum_w_down{layer}")
    dw_up = _matmul(h, da, mode="tn", name=f"mlp{layer}_dwup", tm=1024, tn=fs, tk=MM_TK, out_dims=(N_DEV, D, fs),
                    out_spec=pl.BlockSpec((None, 1024, fs), lambda i, j, k: (j, i, 0)))
    dh, (r2_down, r1_up) = _matmul(da, w_up, mode="nt", name=f"mlp{layer}_dh", tm=1024, tn=1024, tk=fs, mnk=(T, D, F),
                                   side=_join(_chips_job([pb_down]), _cores_job([dw_up])),
                                   b_spec=pl.BlockSpec((None, 1024, fs), lambda i, j, k: (k, j, 0)))
    up_sums = _pair_sum(dw_up, r1_up, f"grad_pair_sum_w_up{layer}")
    dx, dx_b, dg = _rmsnorm_bwd(dh, x, g, dy, name=f"mlp{layer}_norm_bwd")
    return dx, dx_b, dg, (pm_down, r2_down), up_sums, pending_res


def _local_step(x, target, p, late):
    T, D = x.shape
    W = SSD_HEADS * SSD_HEAD_DIM
    g = {}
    add = lambda acc, r: (acc + r,)

    h0 = _rmsnorm(x, p["hyb_norm"], name="hyb_norm", dtype=BF16)
    proj = _matmul(h0, p["w_main_t"], mode="nt", name="hyb_proj", tm=1024, tn=1024, tk=MM_TK)
    proj_dt = _matmul(h0, p["w_dt_t"], mode="nt", name="hyb_proj_dt", tm=1024, tn=128, tk=MM_TK)
    ex, d_lanes, alog, bias = _ssd_consts(p["ssd_d"], p["ssd_a_log"], p["ssd_dt_bias"])
    xbc = _conv_silu_fwd(proj, 4 * W, p["conv_w"].shape[1], p["conv_w"], p["conv_b"])
    ypre, states = _ssd_fwd(xbc, proj_dt, bias, alog, ex, d_lanes)
    y_ssd = _gate_norm_fwd(ypre, proj, p["out_norm"])
    qh, kh = _qk_norm_fwd(proj, p["q_norm"], p["k_norm"], W)
    (y_sb, y_sb_b), (w_up0, w_down0, pool_blocks) = _sb_fwd(qh, kh, proj, 3 * W // SB_DIM, side=_gather_job(late[0]))
    pool_w = _whole(pool_blocks, "pool_w")[0]
    x1 = _matmul(y_ssd, p["w_out"], mode="nn", name="hyb_out_a", tm=1024, tn=1024, tk=MM_TK, extras=(x,), epilogue=add,
                 mnk=(T, D, W))
    x1 = _matmul(y_sb_b, p["w_out"], mode="nn", name="hyb_out_b", tm=1024, tn=1024, tk=W, extras=(x1,), epilogue=add,
                 mnk=(T, D, W), b_spec=pl.BlockSpec((W, 1024), lambda i, j, k: (1, j)))
    x2, mlp0, ((w_up1,), (w_down1,)) = _mlp_fwd(x1, p["mlp_norm"][0:1], w_up0, w_down0, 0,
                                                sides=(_gather_job(late[1][:1]), _gather_job(late[1][1:])))
    hp = _rmsnorm(x2, p["pool_norm"], name="pool_norm", dtype=F32)
    dpool = _pool_diff_fwd(hp)
    ypool, x3 = _pool_mm_fwd(dpool, pool_w, p["pool_b"], p["pool_scale"], x2)
    x4, mlp1, _ = _mlp_fwd(x3, p["mlp_norm"][1:2], w_up1, w_down1, 1)

    reduced = {}
    dy, dy_b, sq = _loss_grad(x4, target)
    dx3, dx3_b, dgm1, reduced["w_down1"], up1_sums, _ = _mlp_bwd(dy, dy_b, x3, p["mlp_norm"][1:2], w_up1, w_down1,
                                                                 mlp1, 1)
    dpre, dpool_scale, dpool_b = _pool_scale_bwd(dx3, ypool, p["pool_scale"])
    gw = {"pool_w": _pool_mm_dw(dpool, dpre)[None], "pool_b": dpool_b, "pool_scale": dpool_scale}
    dhp = _pool_diff_bwd(_pool_mm_dx(dpre, pool_w))
    dx2, dx2_b, gw["pool_norm"] = _rmsnorm_bwd(dhp, x2, p["pool_norm"], dx3, name="pool_norm_bwd")
    dx1, dx1_b, dgm0, reduced["w_down0"], up0_sums, (r2_up1,) = _mlp_bwd(
        dx2, dx2_b, x1, p["mlp_norm"][0:1], w_up0, w_down0, mlp0, 0, pending=_chips_job([up1_sums[0]]))
    reduced["w_up1"] = (up1_sums[1], r2_up1)
    g["mlp_norm"] = jnp.concatenate([dgm0, dgm1], axis=0)
    dw_out = jnp.concatenate([
        _matmul(y_ssd, dx1_b, mode="tn", name="hyb_dwout_a", tm=1024, tn=1024, tk=MM_TK),
        _matmul(y_sb_b, dx1_b, mode="tn", name="hyb_dwout_b", tm=1024, tn=1024, tk=MM_TK)], axis=0)
    dw_out = dw_out.reshape(N_DEV, -1, D)
    dpool_small = _shard_rows([_to_blocks(gw[n], n) for n in POOL_SHARDED], PACK_COLS)
    dmerged, (r2_up0, r1_out, r1_pool) = _matmul(
        dx1_b, p["w_out"], mode="nt", name="hyb_dmerged", tm=1024, tn=1024, tk=MM_TK,
        side=_join(_chips_job([up0_sums[0]]), _cores_job([dw_out, dpool_small])))
    reduced["w_up0"] = (up0_sums[1], r2_up0)
    out_sums = _pair_sum(dw_out, r1_out, "grad_pair_sum_w_out")
    pool_sums = _pair_sum(dpool_small, r1_pool, "grad_pair_sum_pool")
    (dqh, dkh, dv), (r2_out, r2_pool) = _sb_bwd(qh, kh, proj, 3 * W // SB_DIM, y_sb, dmerged, W // SB_DIM,
                                                side=_chips_job([out_sums[0], pool_sums[0]]))
    reduced["w_out"], reduced["pool"] = (out_sums[1], r2_out), (pool_sums[1], r2_pool)
    dq, dk, g["sb_q_norm"], g["sb_k_norm"] = _qk_norm_bwd(dqh, dkh, proj, p["q_norm"], p["k_norm"], W)
    dypre, dz, g["ssd_out_norm"] = _gate_norm_bwd(dmerged, ypre, proj, p["out_norm"])
    dxs, dbm, dcm, ddt4, hv = _ssd_bwd(dypre, xbc, proj_dt, states, bias, alog, ex, d_lanes)
    g["ssd_d"], g["ssd_a_log"], g["ssd_dt_bias"] = (hv[i:i + 1, :SSD_HEADS] for i in range(3))
    ddt = ddt4.reshape(T, SSD_GROUPS, LANE).sum(axis=1).astype(BF16)
    dxbc, dconv_w, g["ssd_conv_b"] = _conv_silu_bwd(jnp.concatenate([dxs, dbm, dcm], axis=1), proj, 4 * W,
                                                    p["conv_w"].shape[1], p["conv_w"], p["conv_b"])
    dproj = jnp.concatenate([dz, dq, dk, dv.astype(BF16), dxbc], axis=1)
    gm = _matmul(dproj, h0, mode="tn", name="hyb_dwin", tm=1024, tn=1024, tk=MM_TK)
    g_dt = _matmul(ddt, h0, mode="tn", name="hyb_dwdt", tm=128, tn=1024, tk=MM_TK)
    g_in_t = jnp.concatenate([gm[:W], gm[4 * W:], g_dt[:SSD_HEADS], gm[W:4 * W]], axis=0)
    last = [g_in_t.reshape(N_DEV, -1, D), _shard_rows([_to_blocks(dconv_w[:4][None], "ssd_conv_w")], LANE)]
    dh0, r1_last = _matmul(ddt, p["w_dt_t"], mode="nn", name="hyb_dh_dt", tm=1024, tn=1024, tk=128,
                           side=_cores_job(last))
    sums = [_pair_sum(a, r, f"grad_pair_sum_{t}") for a, r, t in zip(last, r1_last, ("w_in", "conv"))]
    dh0, r2_last = _matmul(dproj, p["w_main_t"], mode="nn", name="hyb_dh", tm=1024, tn=1024, tk=1024, extras=(dh0,),
                           epilogue=add, side=_chips_job([s[0] for s in sums]))
    reduced.update({t: (s[1], r2) for t, s, r2 in zip(("w_in", "conv"), sums, r2_last)})
    grad_x, _, g["hyb_norm"] = _rmsnorm_bwd(dh0, x, p["hyb_norm"], dx1, name="hyb_norm_bwd")
    return sq, grad_x, g, reduced


def _position():
    return lax.axis_index("x"), lax.axis_index("y"), lax.axis_index("c")


def _run_job(job, name):
    k_in, k_out = len(job.ins), len(job.outs)

    def body(*refs):
        parts = refs[:k_in], refs[k_in:k_in + k_out], refs[k_in + k_out:]
        job.start(*parts)
        if job.mid is not None:
            job.mid(*parts)
        job.finish(*parts)

    return pl.pallas_call(body, name=name, out_shape=list(job.outs), in_specs=[ANY] * k_in, out_specs=[ANY] * k_out,
                          scratch_shapes=list(job.sems))(*job.ins)


def _join(*jobs):
    def parts(ins, outs, sems):
        i = o = s = 0
        for j in jobs:
            yield j, (ins[i:i + len(j.ins)], outs[o:o + len(j.outs)], sems[s:s + len(j.sems)])
            i, o, s = i + len(j.ins), o + len(j.outs), s + len(j.sems)

    def start(*refs):
        for j, p in parts(*refs):
            j.start(*p)

    def mid(*refs):
        for j, p in parts(*refs):
            if j.mid is not None:
                j.mid(*p)

    def finish(*refs):
        for j, p in parts(*refs):
            j.finish(*p)

    return _Job(sum((j.ins for j in jobs), ()), sum((j.outs for j in jobs), ()), sum((j.sems for j in jobs), ()),
                start, mid if any(j.mid is not None for j in jobs) else None, finish)


def _gather_job(vs):
    n = len(vs)

    def plan(v_refs, out_refs, sems):
        send_sems, recv_sems, local_sems = sems
        x, y, c = _position()
        me, sibling = (x, y, c), (x, y, 1 - c)
        chips = [(1 - x, y), (x, 1 - y), (1 - x, 1 - y)]

        def rows(a, px, py, pc):
            return out_refs[a].at[4 * px + 2 * py + pc]

        def copy(a, k, block, to, src=None):
            return pltpu.make_async_remote_copy(
                src_ref=rows(a, *block) if src is None else src, dst_ref=rows(a, *block),
                send_sem=send_sems.at[7 * a + k], recv_sem=recv_sems.at[7 * a + k], device_id=to,
                device_id_type=MESH)

        mine = [pltpu.make_async_copy(v_refs[a], rows(a, *me), local_sems.at[a]) for a in range(n)]
        first = [copy(a, 0, me, sibling, src=v_refs[a]) for a in range(n)]
        first += [copy(a, 1 + j, me, (*chip, c), src=v_refs[a]) for a in range(n) for j, chip in enumerate(chips)]
        landed = [copy(a, 1 + j, (*chip, c), me) for j, chip in enumerate(chips) for a in range(n)]
        passed = [copy(a, 4 + j, (*chip, c), sibling) for j, chip in enumerate(chips) for a in range(n)]
        from_sibling = [copy(a, 0, sibling, me) for a in range(n)]
        from_sibling += [copy(a, 4 + j, (*chip, 1 - c), me) for a in range(n) for j, chip in enumerate(chips)]
        return mine, first, landed, passed, from_sibling

    def start(*refs):
        mine, first, _, _, _ = plan(*refs)
        for cp in mine + first:
            cp.start()

    def mid(*refs):
        _, _, landed, passed, _ = plan(*refs)
        for arrived, onward in zip(landed, passed):
            arrived.wait_recv()
            onward.start()

    def finish(*refs):
        mine, first, _, passed, from_sibling = plan(*refs)
        for cp in from_sibling:
            cp.wait_recv()
        for cp in first + passed:
            cp.wait_send()
        for cp in mine:
            cp.wait()

    return _Job(tuple(vs), tuple(jax.ShapeDtypeStruct((N_DEV,) + v.shape, v.dtype) for v in vs),
                (pltpu.SemaphoreType.DMA((7 * n,)), pltpu.SemaphoreType.DMA((7 * n,)), pltpu.SemaphoreType.DMA((n,))),
                start, mid, finish)


def _cores_job(gs):
    n = len(gs)

    def plan(g_refs, r_refs, sems):
        send_sems, recv_sems = sems
        x, y, c = _position()
        return [pltpu.make_async_remote_copy(
            src_ref=g_refs[a].at[2 * k + (1 - c)], dst_ref=r_refs[a].at[k], send_sem=send_sems.at[4 * a + k],
            recv_sem=recv_sems.at[4 * a + k], device_id=(x, y, 1 - c), device_id_type=MESH)
            for a in range(n) for k in range(4)]

    def start(*refs):
        for cp in plan(*refs):
            cp.start()

    def finish(*refs):
        copies = plan(*refs)
        for cp in copies:
            cp.wait_recv()
        for cp in copies:
            cp.wait_send()

    return _Job(tuple(gs), tuple(jax.ShapeDtypeStruct((4,) + g.shape[1:], g.dtype) for g in gs),
                (pltpu.SemaphoreType.DMA((4 * n,)), pltpu.SemaphoreType.DMA((4 * n,))), start, None, finish)


TILE_BYTES = 2 * 1024 * 1024


def _col_tile(R, C):
    tc = C
    while R * tc * 4 > TILE_BYTES and tc % (2 * LANE) == 0:
        tc //= 2
    return tc


def _pair_sum(gr, r1, name):
    _, R, C = gr.shape
    tc = _col_tile(R, C)
    x, y, c = _position()
    pos = jnp.stack([c, 2 * x + y]).astype(jnp.int32)

    def body(pos_ref, g_ref, r_ref, pb_ref, pm_ref):
        s = g_ref[...] + r_ref[...]
        pb_ref[...] = s.astype(BF16)

        @pl.when(pl.program_id(1) == pos_ref[1])
        def _():
            pm_ref[...] = s

    return _call(body, name=name, grid=(C // tc, 4), prefetch=1,
                 in_specs=[pl.BlockSpec((None, R, tc), lambda j, k, pos: (2 * k + pos[0], 0, j)),
                           pl.BlockSpec((None, R, tc), lambda j, k, pos: (k, 0, j))],
                 out_specs=[pl.BlockSpec((None, R, tc), lambda j, k, pos: (k, 0, j)),
                            pl.BlockSpec((R, tc), lambda j, k, pos: (0, j))],
                 out_shape=[jax.ShapeDtypeStruct((4, R, C), BF16), jax.ShapeDtypeStruct((R, C), F32)],
                 dimension_semantics=("arbitrary", "arbitrary"))(pos, gr, r1)


def _chips_job(pbs):
    n = len(pbs)

    def plan(p_refs, r_refs, sems):
        send_sems, recv_sems = sems
        x, y, c = _position()
        chips = [(1 - x, y), (x, 1 - y), (1 - x, 1 - y)]
        mine = 2 * x + y

        def copy(a, j, src_row, dst_row):
            cx, cy = chips[j]
            return pltpu.make_async_remote_copy(
                src_ref=p_refs[a].at[src_row], dst_ref=r_refs[a].at[dst_row], send_sem=send_sems.at[3 * a + j],
                recv_sem=recv_sems.at[3 * a + j], device_id=(cx, cy, c), device_id_type=MESH)

        sends = [copy(a, j, 2 * cx + cy, mine) for a in range(n) for j, (cx, cy) in enumerate(chips)]
        arrivals = [copy(a, j, mine, 2 * cx + cy) for a in range(n) for j, (cx, cy) in enumerate(chips)]
        return sends, arrivals

    def start(*refs):
        for cp in plan(*refs)[0]:
            cp.start()

    def finish(*refs):
        sends, arrivals = plan(*refs)
        for cp in arrivals:
            cp.wait_recv()
        for cp in sends:
            cp.wait_send()

    return _Job(tuple(pbs), tuple(jax.ShapeDtypeStruct(p.shape, p.dtype) for p in pbs),
                (pltpu.SemaphoreType.DMA((3 * n,)), pltpu.SemaphoreType.DMA((3 * n,))), start, None, finish)


def _adamw(w, grad, m, v):
    m = ADAM_B1 * m + (1.0 - ADAM_B1) * grad
    v = ADAM_B2 * v + (1.0 - ADAM_B2) * (grad * grad)
    m_hat = m / (1.0 - ADAM_B1 ** ADAM_STEP)
    v_hat = v / (1.0 - ADAM_B2 ** ADAM_STEP)
    delta = -ADAM_LR * (m_hat / (jnp.sqrt(v_hat) + ADAM_EPS) + ADAM_WD * w)
    return delta, m, v


def _other_chips():
    x, y, _ = _position()
    mine = 2 * x + y
    return jnp.stack([jnp.where(mine <= j, j + 1, j) for j in range(3)]).astype(jnp.int32)


def _adamw_sharded(pm, r2, w, m, v, name):
    R, C = pm.shape
    tc = _col_tile(R, C)
    update = w is not None

    def body(oth_ref, pm_ref, a_ref, b_ref, c_ref, *refs):
        grad = ((pm_ref[...] + a_ref[...].astype(F32)) + b_ref[...].astype(F32)) + c_ref[...].astype(F32)
        if update:
            w_ref, m_ref, v_ref, g_ref, d_ref, nm_ref, nv_ref = refs
            d, nm, nv = _adamw(w_ref[...], grad, m_ref[...], v_ref[...])
            g_ref[...], d_ref[...], nm_ref[...], nv_ref[...] = grad, d, nm, nv
        else:
            refs[0][...] = grad

    tile = pl.BlockSpec((R, tc), lambda j, oth: (0, j))
    other = [pl.BlockSpec((None, R, tc), functools.partial(lambda j, oth, q: (oth[q], 0, j), q=q)) for q in range(3)]
    shp = jax.ShapeDtypeStruct((R, C), F32)
    n_out = 4 if update else 1
    res = _call(body, name=name, grid=(C // tc,), prefetch=1,
                in_specs=[tile] + other + ([tile, tile, tile] if update else []), out_specs=[tile] * n_out,
                out_shape=[shp] * n_out, dimension_semantics=("arbitrary",))(
                    _other_chips(), pm, r2, r2, r2, *((w, m, v) if update else ()))
    return res if update else res[0]


def _adamw_plain(grad, w, m, v, name):
    R, C = w.shape
    tr = 256

    def body(g_ref, w_ref, m_ref, v_ref, d_ref, nm_ref, nv_ref):
        d_ref[...], nm_ref[...], nv_ref[...] = _adamw(w_ref[...], g_ref[...], m_ref[...], v_ref[...])

    tile = pl.BlockSpec((tr, C), lambda i: (i, 0))
    shp = jax.ShapeDtypeStruct((R, C), F32)
    return _call(body, name=name, grid=(R // tr,), in_specs=[tile] * 4, out_specs=[tile] * 3, out_shape=[shp] * 3,
                 dimension_semantics=("arbitrary",))(grad, w, m, v)


def _adamw_replicated(parts, w, m, v):
    _, R, C = parts.shape

    def body(p_ref, w_ref, m_ref, v_ref, g_ref, d_ref, nm_ref, nv_ref):
        grad = p_ref[0]
        for j in range(1, N_DEV):
            grad = grad + p_ref[j]
        d, nm, nv = _adamw(w_ref[...], grad, m_ref[...], v_ref[...])
        g_ref[...], d_ref[...], nm_ref[...], nv_ref[...] = grad, d, nm, nv

    tile = pl.BlockSpec((R, C), lambda i: (0, 0))
    shp = jax.ShapeDtypeStruct((R, C), F32)
    return _call(body, name="adamw_replicated", grid=(1,),
                 in_specs=[pl.BlockSpec((N_DEV, R, C), lambda i: (0, 0, 0)), tile, tile, tile],
                 out_specs=[tile] * 4, out_shape=[shp] * 4)(parts, w, m, v)


POOL_SHARDED = ("pool_w", "pool_norm", "pool_b", "pool_scale")
REPLICATED = ("hyb_norm", "ssd_conv_b", "ssd_dt_bias", "ssd_a_log", "ssd_d", "ssd_out_norm", "sb_q_norm",
              "sb_k_norm", "mlp_norm")
PACK_COLS = 1024


def _pack(arrays, cols, row_multiple, dtype):
    flat = jnp.concatenate([a.reshape(-1).astype(dtype) for a in arrays])
    n = flat.shape[0]
    total = -(-n // (cols * row_multiple)) * cols * row_multiple
    return jnp.pad(flat, (0, total - n)).reshape(total // cols, cols)


def _shard_rows(blocks, cols):
    flat = jnp.concatenate(blocks, axis=1)
    rows = -(-flat.shape[1] // (8 * cols)) * 8
    return jnp.pad(flat, ((0, 0), (0, rows * cols - flat.shape[1]))).reshape(N_DEV, rows, cols)


def _unpack(packed, shapes):
    flat = packed.reshape(packed.shape[:-2] + (-1,))
    out, off = [], 0
    for s in shapes:
        n = math.prod(s)
        out.append(flat[..., off:off + n].reshape(flat.shape[:-1] + tuple(s)))
        off += n
    return out


def _shard_axis(name):
    return {"hyb_w_in": 2, "hyb_w_out": 1, "mlp_w_up": 2, "mlp_w_down": 1, "pool_w": 2, "ssd_conv_w": 2,
            "pool_norm": 1, "pool_b": 1, "pool_scale": 1}[name]


def _whole(blocks, name):
    ax = _shard_axis(name)
    moved = jnp.moveaxis(blocks, 0, ax)
    s = moved.shape
    return moved.reshape(s[:ax] + (s[ax] * s[ax + 1],) + s[ax + 2:])


def _to_blocks(whole, name):
    ax = _shard_axis(name)
    s = whole.shape
    split = whole.reshape(s[:ax] + (N_DEV, s[ax] // N_DEV) + s[ax + 1:])
    return jnp.moveaxis(split, ax, 0).reshape(N_DEV, -1)


def kernel(x, hyb_norm, hyb_w_in, ssd_conv_w, ssd_conv_b, ssd_dt_bias, ssd_a_log, ssd_d, ssd_out_norm, sb_q_norm, sb_k_norm, hyb_w_out, pool_norm, pool_w, pool_b, pool_scale, mlp_norm, mlp_w_up, mlp_w_down, loss_target, m_hyb_norm, m_hyb_w_in, m_ssd_conv_w, m_ssd_conv_b, m_ssd_dt_bias, m_ssd_a_log, m_ssd_d, m_ssd_out_norm, m_sb_q_norm, m_sb_k_norm, m_hyb_w_out, m_pool_norm, m_pool_w, m_pool_b, m_pool_scale, m_mlp_norm, m_mlp_w_up, m_mlp_w_down, v_hyb_norm, v_hyb_w_in, v_ssd_conv_w, v_ssd_conv_b, v_ssd_dt_bias, v_ssd_a_log, v_ssd_d, v_ssd_out_norm, v_sb_q_norm, v_sb_k_norm, v_hyb_w_out, v_pool_norm, v_pool_w, v_pool_b, v_pool_scale, v_mlp_norm, v_mlp_w_up, v_mlp_w_down):
    args = dict(locals())
    names = ("hyb_norm", "hyb_w_in", "ssd_conv_w", "ssd_conv_b", "ssd_dt_bias", "ssd_a_log", "ssd_d", "ssd_out_norm",
             "sb_q_norm", "sb_k_norm", "hyb_w_out", "pool_norm", "pool_w", "pool_b", "pool_scale", "mlp_norm",
             "mlp_w_up", "mlp_w_down")
    wt = {n: args[n] for n in names}
    T, D = x.shape[1], x.shape[2]
    W = SSD_HEADS * SSD_HEAD_DIM

    conv_dim = ssd_conv_b.shape[-1]
    c1, c2 = W + conv_dim, W + conv_dim + SSD_HEADS
    vec_names = ("ssd_conv_w", "pool_norm", "pool_b", "pool_scale")

    gathered = _run_job(_gather_job([hyb_w_in[0].T.astype(BF16), hyb_w_out[0].astype(BF16),
                                     _pack([wt[n] for n in vec_names], LANE, 8, F32)]), "gather_hybrid_weights")
    in_t = gathered[0].reshape(-1, D)
    vec = {n: _whole(b, n) for n, b in zip(vec_names, _unpack(gathered[2], [wt[n].shape for n in vec_names]))}
    p = {
        "w_main_t": jnp.concatenate([in_t[:W], in_t[c2:], in_t[W:c1]], axis=0),
        "w_dt_t": jnp.pad(in_t[c1:c2], ((0, LANE - SSD_HEADS), (0, 0))),
        "w_out": gathered[1].reshape(-1, D), "conv_w": vec["ssd_conv_w"][0], "conv_b": ssd_conv_b,
        "pool_norm": vec["pool_norm"], "pool_b": vec["pool_b"], "pool_scale": vec["pool_scale"],
        "hyb_norm": hyb_norm, "mlp_norm": mlp_norm, "out_norm": ssd_out_norm, "q_norm": sb_q_norm,
        "k_norm": sb_k_norm, "ssd_d": ssd_d, "ssd_a_log": ssd_a_log, "ssd_dt_bias": ssd_dt_bias,
    }

    up, down = mlp_w_up.astype(BF16), mlp_w_down.astype(BF16)
    sq, grad_x, g, reduced = _local_step(x[0], loss_target[0], p,
                                         ([up[0], down[0], pool_w.astype(BF16)], [up[1], down[1]]))
    loss = lax.psum(0.5 * jnp.sum(sq) / D, ("x", "y", "c"))

    res = {}
    grad_in = _adamw_sharded(*reduced["w_in"], None, None, None, "grad_sum_w_in").T
    res["hyb_w_in"] = [a[None] for a in (grad_in, *_adamw_plain(grad_in, hyb_w_in[0], m_hyb_w_in[0], v_hyb_w_in[0],
                                                                "adamw_w_in"))]
    res["hyb_w_out"] = [a[None] for a in _adamw_sharded(*reduced["w_out"], hyb_w_out[0], m_hyb_w_out[0],
                                                        v_hyb_w_out[0], "adamw_w_out")]
    for t, n in (("w_up", "mlp_w_up"), ("w_down", "mlp_w_down")):
        layers = [_adamw_sharded(*reduced[f"{t}{l}"], args[n][l], args["m_" + n][l], args["v_" + n][l],
                                 f"adamw_{n}{l}") for l in range(2)]
        res[n] = [jnp.stack([layers[0][k], layers[1][k]]) for k in range(4)]
    for t, group, cols in (("pool", POOL_SHARDED, PACK_COLS), ("conv", ("ssd_conv_w",), LANE)):
        packed = [_pack([args[pre + n] for n in group], cols, 8, F32) for pre in ("", "m_", "v_")]
        small = [_unpack(o, [wt[n].shape for n in group]) for o in _adamw_sharded(*reduced[t], *packed, f"adamw_{t}")]
        for i, n in enumerate(group):
            res[n] = [small[k][i] for k in range(4)]

    parts = _run_job(_gather_job([_pack([g[n] for n in REPLICATED], LANE, 8, F32)]), "gather_vector_grads")[0]
    packed = [_pack([args[pre + n] for n in REPLICATED], LANE, 8, F32) for pre in ("", "m_", "v_")]
    shapes = [wt[n].shape for n in REPLICATED]
    repl = [_unpack(o, shapes) for o in _adamw_replicated(parts, *packed)]
    for i, n in enumerate(REPLICATED):
        res[n] = [repl[k][i] for k in range(4)]

    outs = [res[n][k] for k in range(4) for n in names]
    return (loss, grad_x[None], *outs)
```

```python
import functools
import math
from typing import Callable, NamedTuple, Optional

import jax
import jax.numpy as jnp
from jax import lax
from jax.experimental import pallas as pl
from jax.experimental.pallas import tpu as pltpu

F32 = jnp.float32
BF16 = jnp.bfloat16
EPS = 1e-6
V7X_VMEM_LIMIT = 56 * 1024 * 1024
MESH = pl.DeviceIdType.MESH
ANY = pl.BlockSpec(memory_space=pl.ANY)
N_DEV = 8

SSD_HEADS = 32
SSD_HEAD_DIM = 64
SSD_STATE = 128
SSD_GROUPS = 4
SSD_CHUNK = 128
GROUP_W = SSD_HEADS * SSD_HEAD_DIM // SSD_GROUPS
HEADS_PER_GROUP = SSD_HEADS // SSD_GROUPS
SB_HEADS = 16
SB_DIM = 128
POOL_WINDOWS = (2, 4, 8, 16)
LANE = 128

ADAM_LR = 0.001
ADAM_B1 = 0.9
ADAM_B2 = 0.999
ADAM_EPS = 1e-08
ADAM_WD = 0.01
ADAM_STEP = 10

NN = (((1,), (0,)), ((), ()))
NT = (((1,), (1,)), ((), ()))
TN = (((0,), (0,)), ((), ()))

class _Job(NamedTuple):
    ins: tuple
    outs: tuple
    sems: tuple
    start: Callable
    mid: Optional[Callable]
    finish: Callable


def _call(body, *, name, grid, in_specs, out_specs, out_shape, scratch=(), prefetch=0, side=None, **params):
    if side is not None:
        single = not isinstance(out_shape, (list, tuple))
        out_specs = [out_specs] if single else list(out_specs)
        out_shape = [out_shape] if single else list(out_shape)
        n_in, n_out, n_scr = len(in_specs), len(out_shape), len(scratch)
        k_in, k_out = len(side.ins), len(side.outs)
        inner = body
        steps = math.prod(grid)

        def body(*refs):
            pre, rest = refs[:prefetch], refs[prefetch:]
            ins, s_in = rest[:n_in], rest[n_in:n_in + k_in]
            rest = rest[n_in + k_in:]
            outs, s_out = rest[:n_out], rest[n_out:n_out + k_out]
            rest = rest[n_out + k_out:]
            scr, s_sem = rest[:n_scr], rest[n_scr:]
            step = 0
            for axis, size in enumerate(grid):
                step = step * size + pl.program_id(axis)

            @pl.when(step == 0)
            def _():
                side.start(s_in, s_out, s_sem)

            inner(*pre, *ins, *outs, *scr)
            if side.mid is not None:
                @pl.when(step == (3 * steps) // 4)
                def _():
                    side.mid(s_in, s_out, s_sem)

            @pl.when(step == steps - 1)
            def _():
                side.finish(s_in, s_out, s_sem)

        params = dict(params, dimension_semantics=("arbitrary",) * len(grid))
        res = _call(body, name=name, grid=grid, in_specs=list(in_specs) + [ANY] * k_in,
                    out_specs=out_specs + [ANY] * k_out, out_shape=out_shape + list(side.outs),
                    scratch=list(scratch) + list(side.sems), prefetch=prefetch, **params)
        return lambda *args: (lambda r: ((r[0] if single else r[:n_out]), r[n_out:]))(res(*args, *side.ins))
    cp = pltpu.CompilerParams(vmem_limit_bytes=V7X_VMEM_LIMIT, **params)
    if prefetch:
        gs = pltpu.PrefetchScalarGridSpec(num_scalar_prefetch=prefetch, grid=grid, in_specs=in_specs,
                                          out_specs=out_specs, scratch_shapes=list(scratch))
        return pl.pallas_call(body, name=name, grid_spec=gs, out_shape=out_shape, compiler_params=cp)
    return pl.pallas_call(body, name=name, grid=grid, in_specs=in_specs, out_specs=out_specs,
                          out_shape=out_shape, scratch_shapes=list(scratch), compiler_params=cp)


def _dot(a, b, dims=NN):
    return lax.dot_general(a, b, dims, preferred_element_type=F32)


def _split3(x):
    hi = x.astype(BF16)
    r = x - hi.astype(F32)
    mid = r.astype(BF16)
    lo = (r - mid.astype(F32)).astype(BF16)
    return hi, mid, lo


def _dot3(x, m, dims=NN):
    hi, mid, lo = _split3(x)
    return _dot(hi, m, dims) + _dot(mid, m, dims) + _dot(lo, m, dims)


def _dot3l(m, x, dims=NN):
    hi, mid, lo = _split3(x)
    return _dot(m, hi, dims) + _dot(m, mid, dims) + _dot(m, lo, dims)


def _dot2(x, m):
    hi = x.astype(BF16)
    lo = (x - hi.astype(F32)).astype(BF16)
    return _dot(hi, m) + _dot(lo, m)


def _sigmoid(x):
    return 1.0 / (1.0 + jnp.exp(-x))


def _softplus(x):
    return jnp.maximum(x, 0.0) + jnp.log(1.0 + jnp.exp(-jnp.abs(x)))


def _iota(shape, dim):
    return lax.broadcasted_iota(jnp.int32, shape, dim)


def _matmul(a, b, *, mode, name, tm, tn, tk, extras=(), epilogue=None, out_dtypes=(F32,), mnk=None, b_spec=None,
            out_spec=None, out_dims=None, side=None):
    if mnk is not None:
        M, N, K = mnk
    elif mode == "tn":
        (K, M), N = a.shape, b.shape[1]
    else:
        (M, K), N = a.shape, b.shape[1 if mode == "nn" else 0]
    tm, tn, tk = min(tm, M), min(tn, N), min(tk, K)
    assert M % tm == 0 and N % tn == 0 and K % tk == 0, (name, M, N, K, tm, tn, tk)
    if mode == "nn":
        a_spec = pl.BlockSpec((tm, tk), lambda i, j, k: (i, k))
        b_spec = b_spec or pl.BlockSpec((tk, tn), lambda i, j, k: (k, j))
        dims = NN
    elif mode == "nt":
        a_spec = pl.BlockSpec((tm, tk), lambda i, j, k: (i, k))
        b_spec = b_spec or pl.BlockSpec((tn, tk), lambda i, j, k: (j, k))
        dims = NT
    else:
        a_spec = pl.BlockSpec((tk, tm), lambda i, j, k: (k, i))
        b_spec = b_spec or pl.BlockSpec((tk, tn), lambda i, j, k: (k, j))
        dims = TN
    nk = K // tk
    ex_specs = []
    for e in extras:
        if e.shape[0] == 1:
            ex_specs.append(pl.BlockSpec((1, tn), lambda i, j, k: (0, j)))
        else:
            ex_specs.append(pl.BlockSpec((tm, tn), lambda i, j, k: (i, j)))
    n_ex, n_out = len(extras), len(out_dtypes)

    def body(*refs):
        a_ref, b_ref = refs[0], refs[1]
        ex_refs = refs[2:2 + n_ex]
        o_refs = refs[2 + n_ex:2 + n_ex + n_out]

        def finish(r):
            outs = (r,) if epilogue is None else epilogue(r, *[e[...] for e in ex_refs])
            for o_ref, o in zip(o_refs, outs):
                o_ref[...] = o.astype(o_ref.dtype)

        b = b_ref[...]
        if b.ndim == 3:
            b = b.reshape(-1, b.shape[-1]) if mode == "nn" else jnp.concatenate([b[0], b[1]], axis=1)
        part = _dot(a_ref[...].astype(BF16), b.astype(BF16), dims)
        if nk == 1:
            finish(part)
            return
        acc = refs[2 + n_ex + n_out]
        k = pl.program_id(2)

        @pl.when(k == 0)
        def _():
            acc[...] = part

        @pl.when(jnp.logical_and(k > 0, k < nk - 1))
        def _():
            acc[...] += part

        @pl.when(k == nk - 1)
        def _():
            finish(acc[...] + part)

    out_shape = [jax.ShapeDtypeStruct(out_dims or (M, N), d) for d in out_dtypes]
    out_specs = [out_spec or pl.BlockSpec((tm, tn), lambda i, j, k: (i, j)) for _ in out_dtypes]
    res = _call(body, name=name, grid=(M // tm, N // tn, nk), in_specs=[a_spec, b_spec] + ex_specs,
                out_specs=out_specs, out_shape=out_shape, scratch=[pltpu.VMEM((tm, tn), F32)] if nk > 1 else [],
                dimension_semantics=("parallel", "parallel", "arbitrary"), side=side)(a, b, *extras)
    if side is not None:
        return (res[0] if n_out > 1 else res[0][0]), res[1]
    return res if n_out > 1 else res[0]


def _rowwise(fn, *, name, T, tm, tiles, vecs, out_tiles, out_vecs):
    n_t, n_v, n_ot, n_ov = len(tiles), len(vecs), len(out_tiles), len(out_vecs)

    def body(*refs):
        ins = [r[...] for r in refs[:n_t + n_v]]
        outs = fn(*ins)
        ot_refs = refs[n_t + n_v:n_t + n_v + n_ot]
        ov_refs = refs[n_t + n_v + n_ot:]
        for r, o in zip(ot_refs, outs[:n_ot]):
            r[...] = o.astype(r.dtype)
        if n_ov:
            first = pl.program_id(0) == 0

            @pl.when(first)
            def _():
                for r, o in zip(ov_refs, outs[n_ot:]):
                    r[...] = o

            @pl.when(jnp.logical_not(first))
            def _():
                for r, o in zip(ov_refs, outs[n_ot:]):
                    r[...] += o

    in_specs = [pl.BlockSpec((tm, w), functools.partial(lambda i, cb: (i, cb), cb=cb)) for _, w, cb in tiles]
    in_specs += [pl.BlockSpec(v.shape, lambda i: (0, 0)) for v in vecs]
    out_specs = [pl.BlockSpec((tm, w), lambda i: (i, 0)) for w, _ in out_tiles]
    out_specs += [pl.BlockSpec((r, w), lambda i: (0, 0)) for r, w in out_vecs]
    out_shape = [jax.ShapeDtypeStruct((T, w), d) for w, d in out_tiles]
    out_shape += [jax.ShapeDtypeStruct((r, w), F32) for r, w in out_vecs]
    return _call(body, name=name, grid=(T // tm,), in_specs=in_specs, out_specs=out_specs, out_shape=out_shape,
                 dimension_semantics=("arbitrary",))(*[t[0] for t in tiles], *vecs)


def _colsum(x):
    return jnp.sum(x, axis=0, keepdims=True)


def _rms_fwd(x, g):
    r = lax.rsqrt(jnp.mean(x * x, axis=-1, keepdims=True) + EPS)
    return x * r * g


def _rms_bwd(dh, x, g):
    r = lax.rsqrt(jnp.mean(x * x, axis=-1, keepdims=True) + EPS)
    xh = x * r
    dxh = dh * g
    dx = r * (dxh - xh * jnp.mean(dxh * xh, axis=-1, keepdims=True))
    return dx, _colsum(dh * xh)


def _rmsnorm(x, g, *, name, dtype):
    T, D = x.shape
    return _rowwise(lambda xv, gv: (_rms_fwd(xv, gv),), name=name, T=T, tm=256, tiles=[(x, D, 0)], vecs=[g],
                    out_tiles=[(D, dtype)], out_vecs=[])[0]


def _rmsnorm_bwd(dh, x, g, dres, *, name):
    T, D = x.shape

    def fn(dhv, xv, drv, gv):
        dx, dg = _rms_bwd(dhv, xv, gv)
        return drv + dx, drv + dx, dg

    return _rowwise(fn, name=name, T=T, tm=256, tiles=[(dh, D, 0), (x, D, 0), (dres, D, 0)], vecs=[g],
                    out_tiles=[(D, F32), (D, BF16)], out_vecs=[(1, D)])


def _group_slices(width, group):
    return [slice(i, i + group) for i in range(0, width, group)]


def _gate_norm_fwd(ypre, proj, gain):
    T, W = ypre.shape

    def fn(y, z, g):
        gated = y * (z * _sigmoid(z))
        return (jnp.concatenate([_rms_fwd(gated[:, s], g[:, s]) for s in _group_slices(W, GROUP_W)], axis=1),)

    return _rowwise(fn, name="ssd_gate_norm", T=T, tm=256, tiles=[(ypre, W, 0), (proj, W, 0)], vecs=[gain],
                    out_tiles=[(W, BF16)], out_vecs=[])[0]


def _gate_norm_bwd(dmerged, ypre, proj, gain):
    T, W = ypre.shape

    def fn(do, y, z, g):
        sg = _sigmoid(z)
        sz = z * sg
        gated = y * sz
        parts = [_rms_bwd(do[:, s], gated[:, s], g[:, s]) for s in _group_slices(W, GROUP_W)]
        dgated = jnp.concatenate([p[0] for p in parts], axis=1)
        dgain = jnp.concatenate([p[1] for p in parts], axis=1)
        return dgated * sz, dgated * y * (sg * (1.0 + z * (1.0 - sg))), dgain

    return _rowwise(fn, name="ssd_gate_norm_bwd", T=T, tm=256, tiles=[(dmerged, W, 0), (ypre, W, 0), (proj, W, 0)],
                    vecs=[gain], out_tiles=[(W, F32), (W, BF16)], out_vecs=[(1, W)])


def _qk_norm_fwd(proj, qg, kg, W):
    T = proj.shape[0]

    def fn(q, k, gq, gk):
        sl = _group_slices(W, SB_DIM)
        return (jnp.concatenate([_rms_fwd(q[:, s], gq) for s in sl], axis=1),
                jnp.concatenate([_rms_fwd(k[:, s], gk) for s in sl], axis=1))

    return _rowwise(fn, name="sb_qk_norm", T=T, tm=256, tiles=[(proj, W, 1), (proj, W, 2)], vecs=[qg, kg],
                    out_tiles=[(W, BF16), (W, BF16)], out_vecs=[])


def _qk_norm_bwd(dqh, dkh, proj, qg, kg, W):
    T = proj.shape[0]

    def fn(dq, dk, q, k, gq, gk):
        sl = _group_slices(W, SB_DIM)
        pq = [_rms_bwd(dq[:, s], q[:, s], gq) for s in sl]
        pk = [_rms_bwd(dk[:, s], k[:, s], gk) for s in sl]
        return (jnp.concatenate([p[0] for p in pq], axis=1), jnp.concatenate([p[0] for p in pk], axis=1),
                sum(p[1] for p in pq), sum(p[1] for p in pk))

    return _rowwise(fn, name="sb_qk_norm_bwd", T=T, tm=256,
                    tiles=[(dqh, W, 0), (dkh, W, 0), (proj, W, 1), (proj, W, 2)], vecs=[qg, kg],
                    out_tiles=[(W, BF16), (W, BF16)], out_vecs=[(1, SB_DIM), (1, SB_DIM)])


def _loss_grad(y, target):
    T, D = y.shape

    def fn(yv, tv):
        err = yv - tv
        return err * (1.0 / D), err * (1.0 / D), _colsum(err * err)

    return _rowwise(fn, name="loss_grad", T=T, tm=256, tiles=[(y, D, 0), (target, D, 0)], vecs=[],
                    out_tiles=[(D, F32), (D, BF16)], out_vecs=[(1, D)])


def _pool_scale_bwd(dx, ypre, scale):
    T, D = dx.shape

    def fn(d, yp, s):
        dpre = d * s
        return dpre, _colsum(d * yp), _colsum(dpre)

    return _rowwise(fn, name="pool_scale_bwd", T=T, tm=256, tiles=[(dx, D, 0), (ypre, D, 0)], vecs=[scale],
                    out_tiles=[(D, BF16)], out_vecs=[(1, D), (1, D)])


ROWS = 512


def _past(cur, prev, k):
    row = _iota(cur.shape, 0)
    rc = pltpu.roll(cur, k, 0)
    if prev is None:
        return jnp.where(row >= k, rc, 0.0)
    return jnp.where(row >= k, rc, pltpu.roll(prev, k, 0))


def _future(cur, nxt, k):
    n = cur.shape[0]
    row = _iota(cur.shape, 0)
    rc = pltpu.roll(cur, n - k, 0)
    if nxt is None:
        return jnp.where(row < n - k, rc, 0.0)
    return jnp.where(row < n - k, rc, pltpu.roll(nxt, n - k, 0))


def _chunk(ref, ci):
    return ref[ci * ROWS:(ci + 1) * ROWS, :]


def _conv_pre(x_ref, w, b, ci):
    cur = _chunk(x_ref, ci)
    prev = _chunk(x_ref, ci - 1) if ci > 0 else None
    taps = [_past(cur, prev, 3), _past(cur, prev, 2), _past(cur, prev, 1), cur]
    xc = b + sum(w[j:j + 1, :] * taps[j] for j in range(4))
    return xc, taps


CONV_COLS = 256


def _conv_silu_fwd(proj, col0, width, conv_w, conv_b):
    T = proj.shape[0]

    def body(x_ref, w_ref, b_ref, o_ref):
        w, b = w_ref[...], b_ref[...]
        for ci in range(T // ROWS):
            xc, _ = _conv_pre(x_ref, w, b, ci)
            o_ref[ci * ROWS:(ci + 1) * ROWS, :] = xc * _sigmoid(xc)

    cb0 = col0 // CONV_COLS
    return _call(body, name="ssd_conv_silu", grid=(width // CONV_COLS,),
                 in_specs=[pl.BlockSpec((T, CONV_COLS), lambda j: (0, cb0 + j)),
                           pl.BlockSpec((4, CONV_COLS), lambda j: (0, j)),
                           pl.BlockSpec((1, CONV_COLS), lambda j: (0, j))],
                 out_specs=pl.BlockSpec((T, CONV_COLS), lambda j: (0, j)),
                 out_shape=jax.ShapeDtypeStruct((T, width), F32))(proj, conv_w, conv_b)


def _conv_silu_bwd(dxa, proj, col0, width, conv_w, conv_b):
    T = proj.shape[0]
    nchunk = T // ROWS

    def body(d_ref, x_ref, w_ref, b_ref, dx_ref, dw_ref, db_ref, dxc_ref):
        w, b = w_ref[...], b_ref[...]
        dw = [jnp.zeros((1, CONV_COLS), F32) for _ in range(4)]
        db = jnp.zeros((1, CONV_COLS), F32)
        for ci in range(nchunk):
            xc, taps = _conv_pre(x_ref, w, b, ci)
            sg = _sigmoid(xc)
            dxc = _chunk(d_ref, ci) * (sg * (1.0 + xc * (1.0 - sg)))
            dxc_ref[ci * ROWS:(ci + 1) * ROWS, :] = dxc
            db = db + _colsum(dxc)
            dw = [dw[j] + _colsum(dxc * taps[j]) for j in range(4)]
        dw_ref[...] = jnp.concatenate(dw + [jnp.zeros((4, CONV_COLS), F32)], axis=0)
        db_ref[...] = db
        for ci in range(nchunk):
            cur = _chunk(dxc_ref, ci)
            nxt = _chunk(dxc_ref, ci + 1) if ci + 1 < nchunk else None
            dx = (w[3:4, :] * cur + w[2:3, :] * _future(cur, nxt, 1) + w[1:2, :] * _future(cur, nxt, 2)
                  + w[0:1, :] * _future(cur, nxt, 3))
            dx_ref[ci * ROWS:(ci + 1) * ROWS, :] = dx.astype(dx_ref.dtype)

    cb0 = col0 // CONV_COLS
    return _call(body, name="ssd_conv_silu_bwd", grid=(width // CONV_COLS,),
                 in_specs=[pl.BlockSpec((T, CONV_COLS), lambda j: (0, j)),
                           pl.BlockSpec((T, CONV_COLS), lambda j: (0, cb0 + j)),
                           pl.BlockSpec((4, CONV_COLS), lambda j: (0, j)),
                           pl.BlockSpec((1, CONV_COLS), lambda j: (0, j))],
                 out_specs=[pl.BlockSpec((T, CONV_COLS), lambda j: (0, j)),
                            pl.BlockSpec((8, CONV_COLS), lambda j: (0, j)),
                            pl.BlockSpec((1, CONV_COLS), lambda j: (0, j))],
                 out_shape=[jax.ShapeDtypeStruct((T, width), BF16), jax.ShapeDtypeStruct((8, width), F32),
                            jax.ShapeDtypeStruct((1, width), F32)],
                 scratch=[pltpu.VMEM((T, CONV_COLS), F32)])(dxa, proj, conv_w, conv_b)


def _window_count(ci, win, shape):
    t = (_iota(shape, 0) + ci * ROWS + 1).astype(F32)
    return jnp.minimum(t, float(win))


def _pool_diff_fwd(h):
    T, D = h.shape
    per_group = D // len(POOL_WINDOWS) // LANE

    def body(h_ref, o_ref):
        j = pl.program_id(0)
        for gi, win in enumerate(POOL_WINDOWS):
            @pl.when(j // per_group == gi)
            def _(win=win):
                for ci in range(T // ROWS):
                    cur = _chunk(h_ref, ci)
                    prev = _chunk(h_ref, ci - 1) if ci > 0 else None
                    s = cur
                    for k in range(1, win):
                        s = s + _past(cur, prev, k)
                    d = s / _window_count(ci, win, cur.shape) - cur
                    o_ref[ci * ROWS:(ci + 1) * ROWS, :] = d.astype(o_ref.dtype)

    return _call(body, name="pool_diff", grid=(D // LANE,), in_specs=[pl.BlockSpec((T, LANE), lambda j: (0, j))],
                 out_specs=pl.BlockSpec((T, LANE), lambda j: (0, j)),
                 out_shape=jax.ShapeDtypeStruct((T, D), BF16))(h)


def _pool_diff_bwd(dd):
    T, D = dd.shape
    per_group = D // len(POOL_WINDOWS) // LANE
    nchunk = T // ROWS

    def body(d_ref, o_ref):
        j = pl.program_id(0)
        for gi, win in enumerate(POOL_WINDOWS):
            @pl.when(j // per_group == gi)
            def _(win=win):
                for ci in range(nchunk):
                    cur = _chunk(d_ref, ci)
                    q = cur / _window_count(ci, win, cur.shape)
                    qn = None
                    if ci + 1 < nchunk:
                        qn = _chunk(d_ref, ci + 1) / _window_count(ci + 1, win, cur.shape)
                    s = q - cur
                    for k in range(1, win):
                        s = s + _future(q, qn, k)
                    o_ref[ci * ROWS:(ci + 1) * ROWS, :] = s

    return _call(body, name="pool_diff_bwd", grid=(D // LANE,), in_specs=[pl.BlockSpec((T, LANE), lambda j: (0, j))],
                 out_specs=pl.BlockSpec((T, LANE), lambda j: (0, j)),
                 out_shape=jax.ShapeDtypeStruct((T, D), F32))(dd)


def _pool_mm_fwd(d, w, b, scale, x):
    T, D = d.shape
    G = w.shape[1]
    tm = 512

    def body(d_ref, w_ref, b_ref, s_ref, x_ref, yp_ref, o_ref):
        yp = _dot(d_ref[...], w_ref[...]) + b_ref[...]
        yp_ref[...] = yp
        o_ref[...] = x_ref[...] + yp * s_ref[...]

    tile = pl.BlockSpec((tm, G), lambda i, g: (i, g))
    vec = pl.BlockSpec((1, G), lambda i, g: (0, g))
    return _call(body, name="pool_mm", grid=(T // tm, D // G),
                 in_specs=[tile, pl.BlockSpec((None, G, G), lambda i, g: (g, 0, 0)), vec, vec, tile],
                 out_specs=[tile, tile],
                 out_shape=[jax.ShapeDtypeStruct((T, D), F32), jax.ShapeDtypeStruct((T, D), F32)])(d, w, b, scale, x)


def _pool_mm_dx(dpre, w):
    T, D = dpre.shape
    G = w.shape[1]
    tm = 512

    def body(d_ref, w_ref, o_ref):
        o_ref[...] = _dot(d_ref[...], w_ref[...], NT)

    tile = pl.BlockSpec((tm, G), lambda i, g: (i, g))
    return _call(body, name="pool_mm_dx", grid=(T // tm, D // G),
                 in_specs=[tile, pl.BlockSpec((None, G, G), lambda i, g: (g, 0, 0))], out_specs=tile,
                 out_shape=jax.ShapeDtypeStruct((T, D), F32))(dpre, w)


def _pool_mm_dw(d, dpre):
    T, D = d.shape
    G = D // len(POOL_WINDOWS)
    tk = 512

    def body(d_ref, p_ref, o_ref):
        @pl.when(pl.program_id(1) == 0)
        def _():
            o_ref[...] = jnp.zeros_like(o_ref)

        o_ref[...] += _dot(d_ref[...], p_ref[...], TN)

    tile = pl.BlockSpec((tk, G), lambda g, k: (k, g))
    return _call(body, name="pool_mm_dw", grid=(D // G, T // tk), in_specs=[tile, tile],
                 out_specs=pl.BlockSpec((None, G, G), lambda g, k: (g, 0, 0)),
                 out_shape=jax.ShapeDtypeStruct((D // G, G, G), F32))(d, dpre)


def _ssd_consts(ssd_d, a_log, dt_bias):
    head = jnp.arange(LANE)[:, None]
    lane = jnp.arange(GROUP_W)[None, :]
    ex = jnp.stack([(head == g * HEADS_PER_GROUP + lane // SSD_HEAD_DIM) for g in range(SSD_GROUPS)])
    d_lanes = jnp.repeat(ssd_d.reshape(-1), SSD_HEAD_DIM).reshape(1, -1)
    pad = lambda v: jnp.pad(v.reshape(1, -1), ((0, 0), (0, LANE - SSD_HEADS)))
    return ex.astype(BF16), d_lanes, pad(a_log), pad(dt_bias)


def _ssd_chunk(xs, bm, cm, dtr, bias, alog, ex):
    L = SSD_CHUNK
    row, col = _iota((L, L), 0), _iota((L, L), 1)
    causal = col <= row
    ltri = causal.astype(BF16)
    a_row = -jnp.exp(alog)
    dt = _softplus(dtr + bias)
    da = dt * a_row
    dt_l = _dot3(dt, ex)
    da_l = _dot3(da, ex)
    acs_l = _dot3l(ltri, da_l)
    acs_r = _dot3(da, (row <= col).astype(BF16), TN)
    last_l = acs_l[L - 1:L, :]
    e_l = jnp.exp(last_l - acs_l)
    f_l = jnp.exp(acs_l)
    cd_l = jnp.exp(last_l)
    xdt = xs * dt_l
    cb = _dot(cm.astype(BF16), bm.astype(BF16), NT)
    return dict(causal=causal, dt=dt, da=da, dt_l=dt_l, acs_l=acs_l, acs_r=acs_r, e_l=e_l, f_l=f_l, cd_l=cd_l,
                xdt=xdt, cb=cb, a_row=a_row, ltri=ltri)


def _head_decay(q, acsrow_ref, g, r):
    colv = q["acs_l"][:, r * SSD_HEAD_DIM:r * SSD_HEAD_DIM + 1]
    rowv = acsrow_ref[pl.ds(g * HEADS_PER_GROUP + r, 1), :]
    return jnp.exp(jnp.where(q["causal"], colv - rowv, -1e30))


def _ssd_specs(T):
    L = SSD_CHUNK
    xs = pl.BlockSpec((L, GROUP_W), lambda g, c: (c, g))
    nb = SSD_HEADS * SSD_HEAD_DIM // SSD_STATE
    bm = pl.BlockSpec((L, SSD_STATE), lambda g, c: (c, nb + g))
    cm = pl.BlockSpec((L, SSD_STATE), lambda g, c: (c, nb + SSD_GROUPS + g))
    dtr = pl.BlockSpec((L, LANE), lambda g, c: (c, 0))
    vec = pl.BlockSpec((1, LANE), lambda g, c: (0, 0))
    ex = pl.BlockSpec((None, LANE, GROUP_W), lambda g, c: (g, 0, 0))
    dl = pl.BlockSpec((1, GROUP_W), lambda g, c: (0, g))
    return xs, bm, cm, dtr, vec, ex, dl


def _ssd_fwd(xbc, proj_dt, bias, alog, ex, d_lanes):
    T = xbc.shape[0]
    L, nc, W = SSD_CHUNK, T // SSD_CHUNK, SSD_HEADS * SSD_HEAD_DIM

    def body(xs_ref, b_ref, c_ref, dtr_ref, bias_ref, alog_ref, ex_ref, dl_ref, y_ref, st_ref, state, acsrow):
        g, c = pl.program_id(0), pl.program_id(1)

        @pl.when(c == 0)
        def _():
            state[...] = jnp.zeros_like(state)

        xs, bm, cm = xs_ref[...], b_ref[...], c_ref[...]
        q = _ssd_chunk(xs, bm, cm, dtr_ref[...], bias_ref[...], alog_ref[...], ex_ref[...])
        acsrow[...] = q["acs_r"]
        prev = state[...]
        st_ref[...] = prev
        xdt_b = q["xdt"].astype(BF16)
        yoff = q["f_l"] * _dot(cm.astype(BF16), prev.astype(BF16))
        lane = _iota((L, LANE), 1)
        for p in range(HEADS_PER_GROUP // 2):
            sl = slice(p * LANE, (p + 1) * LANE)
            ma = (_head_decay(q, acsrow, g, 2 * p) * q["cb"]).astype(BF16)
            mb = (_head_decay(q, acsrow, g, 2 * p + 1) * q["cb"]).astype(BF16)
            yd = jnp.where(lane < SSD_HEAD_DIM, _dot(ma, xdt_b[:, sl]), _dot(mb, xdt_b[:, sl]))
            y_ref[:, sl] = yd + yoff[:, sl] + dl_ref[:, sl] * xs[:, sl]
        st_new = _dot(bm.astype(BF16), (q["xdt"] * q["e_l"]).astype(BF16), TN)
        state[...] = q["cd_l"] * prev + st_new

    xs, bm, cm, dtr, vec, exs, dl = _ssd_specs(T)
    return _call(body, name="ssd_scan", grid=(SSD_GROUPS, nc), in_specs=[xs, bm, cm, dtr, vec, vec, exs, dl],
                 out_specs=[pl.BlockSpec((L, GROUP_W), lambda g, c: (c, g)),
                            pl.BlockSpec((None, None, SSD_STATE, GROUP_W), lambda g, c: (c, g, 0, 0))],
                 out_shape=[jax.ShapeDtypeStruct((T, W), F32),
                            jax.ShapeDtypeStruct((nc, SSD_GROUPS, SSD_STATE, GROUP_W), F32)],
                 scratch=[pltpu.VMEM((SSD_STATE, GROUP_W), F32), pltpu.VMEM((LANE, L), F32)],
                 dimension_semantics=("arbitrary", "arbitrary"))(xbc, xbc, xbc, proj_dt, bias, alog, ex, d_lanes)


def _ssd_bwd(dy, xbc, proj_dt, states, bias, alog, ex, d_lanes):
    T = xbc.shape[0]
    L, nc, W = SSD_CHUNK, T // SSD_CHUNK, SSD_HEADS * SSD_HEAD_DIM
    P = SSD_HEAD_DIM

    def body(dy_ref, xs_ref, b_ref, c_ref, dtr_ref, st_ref, bias_ref, alog_ref, ex_ref, dl_ref,
             dxs_ref, db_ref, dc_ref, ddt_ref, hv_ref, dstate, acsrow):
        g, c = pl.program_id(0), pl.program_id(1)

        @pl.when(c == 0)
        def _():
            dstate[...] = jnp.zeros_like(dstate)

        @pl.when(jnp.logical_and(g == 0, c == 0))
        def _():
            hv_ref[...] = jnp.zeros_like(hv_ref)

        xs, bm, cm, ex = xs_ref[...], b_ref[...], c_ref[...], ex_ref[...]
        dtr, bias = dtr_ref[...], bias_ref[...]
        q = _ssd_chunk(xs, bm, cm, dtr, bias, alog_ref[...], ex)
        acsrow[...] = q["acs_r"]
        dyv = dy_ref[...]
        prev = st_ref[...]
        dst = dstate[...]
        bm_b, cm_b = bm.astype(BF16), cm.astype(BF16)
        row128 = _iota((L, LANE), 0)
        lane = _iota((L, LANE), 1)
        row_w = _iota((L, GROUP_W), 0)

        dxs = dl_ref[...] * dyv
        d_dl = _colsum(dyv * xs)
        gmat = _dot(cm_b, prev.astype(BF16))
        dg_b = (dyv * q["f_l"]).astype(BF16)
        dacs = dyv * q["f_l"] * gmat
        dcm = _dot(dg_b, prev.astype(BF16), NT)
        dprev = _dot(cm_b, dg_b, TN)
        dcd = _colsum(dst * prev)
        dlast = dcd * q["cd_l"]
        xe = q["xdt"] * q["e_l"]
        dxe = _dot(bm_b, dst.astype(BF16))
        dbm = _dot(xe.astype(BF16), dst.astype(BF16), NT)
        dxdt = dxe * q["e_l"]
        t1 = dxe * xe
        dacs = dacs - t1
        dlast = dlast + _colsum(t1)
        dstate[...] = dprev + q["cd_l"] * dst
        xdt_b = q["xdt"].astype(BF16)
        dcb = jnp.zeros((L, L), F32)
        dacs_head = []
        dxdt_diag = []
        for p in range(HEADS_PER_GROUP // 2):
            sl = slice(p * LANE, (p + 1) * LANE)
            xp = xdt_b[:, sl]
            dyp = dyv[:, sl]
            vals, dx_parts = [], []
            for half in range(2):
                in_half = (lane < P) if half == 0 else (lane >= P)
                decay = _head_decay(q, acsrow, g, 2 * p + half)
                m = decay * q["cb"]
                dyh = jnp.where(in_half, dyp, 0.0).astype(BF16)
                dm = jnp.where(q["causal"], _dot(dyh, xp, NT), 0.0)
                dcb = dcb + dm * decay
                dseg = dm * m
                rs = jnp.sum(dseg, axis=1, keepdims=True)
                cs = jnp.broadcast_to(_colsum(dseg), (L, L)).T[:, 0:1]
                vals.append(rs - cs)
                dx_parts.append(_dot(m.astype(BF16), dyp.astype(BF16), TN))
            dxdt_diag.append(jnp.where(lane < P, dx_parts[0], dx_parts[1]))
            dacs_head.append(jnp.where(lane == 0, vals[0], jnp.where(lane == P, vals[1], 0.0)))
        dxdt = dxdt + jnp.concatenate(dxdt_diag, axis=1)
        dacs = dacs + jnp.concatenate(dacs_head, axis=1)
        dacs = jnp.where(row_w == L - 1, dacs + dlast, dacs)
        dcb_b = dcb.astype(BF16)
        dcm = dcm + _dot(dcb_b, bm_b)
        dbm = dbm + _dot(dcb_b, cm_b, TN)
        dacs_h = _dot3(dacs, ex, NT)
        dda = _dot3l((row128 <= lane).astype(BF16), dacs_h)
        ddt = dda * q["a_row"] + _dot3(dxdt * xs, ex, NT)
        dxs = dxs + dxdt * q["dt_l"]
        ddtr = ddt * _sigmoid(dtr + bias)
        d_alog = _colsum(dda * q["dt"]) * q["a_row"]
        d_dh = _dot3(jnp.broadcast_to(d_dl, (8, GROUP_W)), ex, NT)[0:1, :]
        dxs_ref[...] = dxs
        db_ref[...] = dbm
        dc_ref[...] = dcm
        ddt_ref[...] = ddtr
        hv_ref[0:1, :] += d_dh
        hv_ref[1:2, :] += d_alog
        hv_ref[2:3, :] += _colsum(ddtr)

    rev = lambda c: nc - 1 - c
    xs = pl.BlockSpec((L, GROUP_W), lambda g, c: (rev(c), g))
    nb = W // SSD_STATE
    bm = pl.BlockSpec((L, SSD_STATE), lambda g, c: (rev(c), nb + g))
    cm = pl.BlockSpec((L, SSD_STATE), lambda g, c: (rev(c), nb + SSD_GROUPS + g))
    dtr = pl.BlockSpec((L, LANE), lambda g, c: (rev(c), 0))
    st = pl.BlockSpec((None, None, SSD_STATE, GROUP_W), lambda g, c: (rev(c), g, 0, 0))
    vec = pl.BlockSpec((1, LANE), lambda g, c: (0, 0))
    exs = pl.BlockSpec((None, LANE, GROUP_W), lambda g, c: (g, 0, 0))
    dl = pl.BlockSpec((1, GROUP_W), lambda g, c: (0, g))
    grp = pl.BlockSpec((L, SSD_STATE), lambda g, c: (rev(c), g))
    return _call(body, name="ssd_scan_bwd", grid=(SSD_GROUPS, nc),
                 in_specs=[xs, xs, bm, cm, dtr, st, vec, vec, exs, dl],
                 out_specs=[xs, grp, grp, grp, pl.BlockSpec((8, LANE), lambda g, c: (0, 0))],
                 out_shape=[jax.ShapeDtypeStruct((T, W), F32),
                            jax.ShapeDtypeStruct((T, SSD_GROUPS * SSD_STATE), F32),
                            jax.ShapeDtypeStruct((T, SSD_GROUPS * SSD_STATE), F32),
                            jax.ShapeDtypeStruct((T, SSD_GROUPS * LANE), F32),
                            jax.ShapeDtypeStruct((8, LANE), F32)],
                 scratch=[pltpu.VMEM((SSD_STATE, GROUP_W), F32), pltpu.VMEM((LANE, L), F32)],
                 dimension_semantics=("arbitrary", "arbitrary"))(dy, xbc, xbc, xbc, proj_dt, states, bias, alog,
                                                                 ex, d_lanes)


SB_TQ = 256
SB_SUB = 128


def _sb_logits(q, kb, scale, mask):
    z = _dot(q, kb, NT) * scale
    lb = jnp.minimum(z, 0.0) - jnp.log(1.0 + jnp.exp(-jnp.abs(z)))
    lk = lb - z
    return lb, lk if mask is None else jnp.where(mask, lk, 0.0)


def _sb_weights(lb, lk, mask, run):
    n = SB_SUB
    strict = (_iota((n, n), 0) > _iota((n, n), 1)).astype(BF16)
    ws = [None] * (lb.shape[1] // n)
    for s in reversed(range(len(ws))):
        sl = slice(s * n, (s + 1) * n)
        w = jnp.exp(lb[:, sl] + (_dot2(lk[:, sl], strict) + run))
        ws[s] = w if mask is None else jnp.where(mask[:, sl], w, 0.0)
        run = run + jnp.sum(lk[:, sl], axis=1, keepdims=True)
    return jnp.concatenate(ws, axis=1), run


def _sb_diagonal(tq):
    return _iota((tq, tq), 1) < _iota((tq, tq), 0)


SB_LOG_CUT = -110.0


def _sb_more(it, qi, run):
    return jnp.logical_and(it <= qi, jnp.max(run) > SB_LOG_CUT)


def _sb_fwd(qh, kh, proj, v_cb0, side=None):
    T, W = qh.shape
    tq = min(SB_TQ, T)
    scale = SB_DIM ** -0.5

    def body(q_ref, k_ref, v_ref, o_ref, ob_ref):
        qi = pl.program_id(1)
        q = q_ref[...]

        def block(it, carry, mask=None):
            run, acc = carry
            kstart = pl.multiple_of((qi - it) * tq, tq)
            lb, lk = _sb_logits(q, k_ref[pl.ds(kstart, tq), :], scale, mask)
            w, run = _sb_weights(lb, lk, mask, run)
            acc = acc + _dot2(w, v_ref[pl.ds(kstart, tq), :].astype(BF16))
            return run, acc

        first = block(0, (jnp.zeros((tq, 1), F32), jnp.zeros((tq, SB_DIM), F32)), _sb_diagonal(tq))
        _, _, acc = lax.while_loop(lambda c: _sb_more(c[0], qi, c[1]), lambda c: (c[0] + 1, *block(c[0], c[1:])),
                                   (jnp.int32(1), *first))
        o_ref[...] = acc
        ob_ref[...] = acc.astype(BF16)

    tile = pl.BlockSpec((tq, SB_DIM), lambda h, i: (i, h))
    return _call(body, name="sb_attn", grid=(W // SB_DIM, T // tq),
                 in_specs=[tile, pl.BlockSpec((T, SB_DIM), lambda h, i: (0, h)),
                           pl.BlockSpec((T, SB_DIM), lambda h, i: (0, v_cb0 + h))],
                 out_specs=[tile, tile],
                 out_shape=[jax.ShapeDtypeStruct((T, W), F32), jax.ShapeDtypeStruct((T, W), BF16)], side=side,
                 dimension_semantics=("arbitrary", "arbitrary"))(qh, kh, proj)


def _sb_bwd(qh, kh, proj, v_cb0, o, dmerged, do_cb0, side=None):
    T, W = qh.shape
    tq = min(SB_TQ, T)
    n = SB_SUB
    scale = SB_DIM ** -0.5

    def body(q_ref, k_ref, v_ref, o_ref, do_ref, dq_ref, dk_ref, dv_ref):
        qi = pl.program_id(1)

        @pl.when(qi == 0)
        def _():
            dk_ref[...] = jnp.zeros_like(dk_ref)
            dv_ref[...] = jnp.zeros_like(dv_ref)

        q = q_ref[...]
        do = do_ref[...]
        do_b = do.astype(BF16)
        etot = jnp.sum(do_b.astype(F32) * o_ref[...], axis=1, keepdims=True)
        incl = (_iota((n, n), 0) >= _iota((n, n), 1)).astype(BF16)

        def block(it, carry, mask=None):
            run, erun, dq = carry
            kstart = pl.multiple_of((qi - it) * tq, tq)
            kb = k_ref[pl.ds(kstart, tq), :]
            vb = v_ref[pl.ds(kstart, tq), :].astype(BF16)
            lb, lk = _sb_logits(q, kb, scale, mask)
            w, run = _sb_weights(lb, lk, mask, run)
            e = _dot(do_b, vb, NT) * w
            beta = jnp.exp(lb)
            dzs = [None] * (tq // n)
            for s in reversed(range(tq // n)):
                sl = slice(s * n, (s + 1) * n)
                before = etot - erun - _dot3(e[:, sl], incl)
                dz = e[:, sl] * (1.0 - beta[:, sl]) - before * beta[:, sl]
                dzs[s] = dz if mask is None else jnp.where(mask[:, sl], dz, 0.0)
                erun = erun + jnp.sum(e[:, sl], axis=1, keepdims=True)
            dz = (jnp.concatenate(dzs, axis=1) * scale).astype(BF16)
            dq = dq + _dot(dz, kb)
            dk_ref[pl.ds(kstart, tq), :] += _dot(dz, q, TN)
            dv_ref[pl.ds(kstart, tq), :] += _dot(w.astype(BF16), do_b, TN)
            return run, erun, dq

        zero = jnp.zeros((tq, 1), F32)
        first = block(0, (zero, zero, jnp.zeros((tq, SB_DIM), F32)), _sb_diagonal(tq))
        out = lax.while_loop(lambda c: _sb_more(c[0], qi, c[1]), lambda c: (c[0] + 1, *block(c[0], c[1:])),
                             (jnp.int32(1), *first))
        dq_ref[...] = out[3]

    tile = pl.BlockSpec((tq, SB_DIM), lambda h, i: (i, h))
    full = pl.BlockSpec((T, SB_DIM), lambda h, i: (0, h))
    shp = jax.ShapeDtypeStruct((T, W), F32)
    return _call(body, name="sb_attn_bwd", grid=(W // SB_DIM, T // tq),
                 in_specs=[tile, full, pl.BlockSpec((T, SB_DIM), lambda h, i: (0, v_cb0 + h)), tile,
                           pl.BlockSpec((tq, SB_DIM), lambda h, i: (i, do_cb0 + h))],
                 out_specs=[tile, full, full], out_shape=[shp, shp, shp], side=side,
                 dimension_semantics=("arbitrary", "arbitrary"))(qh, kh, proj, o, dmerged)


MM_TK = 2048
MM_TK_TOKENS = 4096


def _mlp_fwd(x, g, w_up, w_down, layer, sides=(None, None)):
    T, D = x.shape
    fs = w_up.shape[2]
    F = N_DEV * fs
    h = _rmsnorm(x, g, name=f"mlp{layer}_norm", dtype=BF16)

    def relu_sq(acc):
        u = jnp.maximum(acc, 0.0)
        return u, u * u

    up = _matmul(h, w_up, mode="nn", name=f"mlp{layer}_up", tm=1024, tn=fs, tk=D, epilogue=relu_sq,
                 out_dtypes=(BF16, BF16), mnk=(T, F, D), side=sides[0],
                 b_spec=pl.BlockSpec((None, D, fs), lambda i, j, k: (j, 0, 0)))
    (u, s), res_up = up if sides[0] is not None else (up, None)
    y = _matmul(s, w_down, mode="nn", name=f"mlp{layer}_down", tm=1024, tn=1024, tk=2 * fs, extras=(x,),
                epilogue=lambda acc, r: (acc + r,), mnk=(T, D, F), side=sides[1],
                b_spec=pl.BlockSpec((2, fs, 1024), lambda i, j, k: (k, 0, j)))
    y, res_down = y if sides[1] is not None else (y, None)
    return y, (h, u, s), (res_up, res_down)


def _mlp_bwd(dy, dy_b, x, g, w_up, w_down, saved, layer, pending=None):
    T, D = x.shape
    fs = w_up.shape[2]
    F = N_DEV * fs
    h, u, s = saved
    dw_down = _matmul(s, dy_b, mode="tn", name=f"mlp{layer}_dwdown", tm=1024, tn=1024, tk=MM_TK_TOKENS, side=pending)
    dw_down, pending_res = dw_down if pending is not None else (dw_down, None)
    dw_down = dw_down.reshape(N_DEV, -1, D)
    da, (r1_down,) = _matmul(dy_b, w_down, mode="nt", name=f"mlp{layer}_da", tm=1024, tn=fs, tk=D, extras=(u,),
                             epilogue=lambda acc, uv: (acc * (2.0 * uv.astype(F32)),), out_dtypes=(BF16,),
                             mnk=(T, F, D), side=_cores_job([dw_down]),
                             b_spec=pl.BlockSpec((None, fs, D), lambda i, j, k: (j, 0, 0)))
    pb_down, pm_down = _pair_sum(dw_down, r1_down, f"grad_pair_sum_w_down{layer}")
    dw_up = _matmul(h, da, mode="tn", name=f"mlp{layer}_dwup", tm=1024, tn=fs, tk=MM_TK_TOKENS, out_dims=(N_DEV, D, fs),
                    out_spec=pl.BlockSpec((None, 1024, fs), lambda i, j, k: (j, i, 0)))
    dh, (r2_down, r1_up) = _matmul(da, w_up, mode="nt", name=f"mlp{layer}_dh", tm=1024, tn=1024, tk=2 * fs,
                                   mnk=(T, D, F), side=_join(_chips_job([pb_down]), _cores_job([dw_up])),
                                   b_spec=pl.BlockSpec((2, 1024, fs), lambda i, j, k: (k, j, 0)))
    up_sums = _pair_sum(dw_up, r1_up, f"grad_pair_sum_w_up{layer}")
    dx, dx_b, dg = _rmsnorm_bwd(dh, x, g, dy, name=f"mlp{layer}_norm_bwd")
    return dx, dx_b, dg, (pm_down, r2_down), up_sums, pending_res


def _local_step(x, target, p, late):
    T, D = x.shape
    W = SSD_HEADS * SSD_HEAD_DIM
    g = {}
    add = lambda acc, r: (acc + r,)

    h0 = _rmsnorm(x, p["hyb_norm"], name="hyb_norm", dtype=BF16)
    proj, (w_out,) = _matmul(h0, p["w_main_t"], mode="nt", name="hyb_proj", tm=1024, tn=1024, tk=MM_TK,
                             side=_gather_job(late[2]))
    p = dict(p, w_out=w_out.reshape(-1, D))
    proj_dt = _matmul(h0, p["w_dt_t"], mode="nt", name="hyb_proj_dt", tm=1024, tn=128, tk=MM_TK)
    ex, d_lanes, alog, bias = _ssd_consts(p["ssd_d"], p["ssd_a_log"], p["ssd_dt_bias"])
    xbc = _conv_silu_fwd(proj, 4 * W, p["conv_w"].shape[1], p["conv_w"], p["conv_b"])
    ypre, states = _ssd_fwd(xbc, proj_dt, bias, alog, ex, d_lanes)
    y_ssd = _gate_norm_fwd(ypre, proj, p["out_norm"])
    qh, kh = _qk_norm_fwd(proj, p["q_norm"], p["k_norm"], W)
    (y_sb, y_sb_b), (w_up0, w_down0, pool_blocks) = _sb_fwd(qh, kh, proj, 3 * W // SB_DIM, side=_gather_job(late[0]))
    pool_w = _whole(pool_blocks, "pool_w")[0]
    x1 = _matmul(y_ssd, p["w_out"], mode="nn", name="hyb_out_a", tm=1024, tn=1024, tk=MM_TK, extras=(x,), epilogue=add,
                 mnk=(T, D, W))
    x1 = _matmul(y_sb_b, p["w_out"], mode="nn", name="hyb_out_b", tm=1024, tn=1024, tk=W, extras=(x1,), epilogue=add,
                 mnk=(T, D, W), b_spec=pl.BlockSpec((W, 1024), lambda i, j, k: (1, j)))
    x2, mlp0, ((w_up1,), (w_down1,)) = _mlp_fwd(x1, p["mlp_norm"][0:1], w_up0, w_down0, 0,
                                                sides=(_gather_job(late[1][:1]), _gather_job(late[1][1:])))
    hp = _rmsnorm(x2, p["pool_norm"], name="pool_norm", dtype=F32)
    dpool = _pool_diff_fwd(hp)
    ypool, x3 = _pool_mm_fwd(dpool, pool_w, p["pool_b"], p["pool_scale"], x2)
    x4, mlp1, _ = _mlp_fwd(x3, p["mlp_norm"][1:2], w_up1, w_down1, 1)

    reduced = {}
    dy, dy_b, sq = _loss_grad(x4, target)
    dx3, dx3_b, dgm1, reduced["w_down1"], up1_sums, _ = _mlp_bwd(dy, dy_b, x3, p["mlp_norm"][1:2], w_up1, w_down1,
                                                                 mlp1, 1)
    dpre, dpool_scale, dpool_b = _pool_scale_bwd(dx3, ypool, p["pool_scale"])
    gw = {"pool_w": _pool_mm_dw(dpool, dpre)[None], "pool_b": dpool_b, "pool_scale": dpool_scale}
    dhp = _pool_diff_bwd(_pool_mm_dx(dpre, pool_w))
    dx2, dx2_b, gw["pool_norm"] = _rmsnorm_bwd(dhp, x2, p["pool_norm"], dx3, name="pool_norm_bwd")
    dx1, dx1_b, dgm0, reduced["w_down0"], up0_sums, (r2_up1,) = _mlp_bwd(
        dx2, dx2_b, x1, p["mlp_norm"][0:1], w_up0, w_down0, mlp0, 0, pending=_chips_job([up1_sums[0]]))
    reduced["w_up1"] = (up1_sums[1], r2_up1)
    g["mlp_norm"] = jnp.concatenate([dgm0, dgm1], axis=0)
    dw_out = jnp.concatenate([
        _matmul(y_ssd, dx1_b, mode="tn", name="hyb_dwout_a", tm=1024, tn=1024, tk=MM_TK_TOKENS),
        _matmul(y_sb_b, dx1_b, mode="tn", name="hyb_dwout_b", tm=1024, tn=1024, tk=MM_TK_TOKENS)], axis=0)
    dw_out = dw_out.reshape(N_DEV, -1, D)
    dpool_small = _shard_rows([_to_blocks(gw[n], n) for n in POOL_SHARDED], PACK_COLS)
    dmerged, (r2_up0, r1_out, r1_pool) = _matmul(
        dx1_b, p["w_out"], mode="nt", name="hyb_dmerged", tm=1024, tn=1024, tk=MM_TK,
        side=_join(_chips_job([up0_sums[0]]), _cores_job([dw_out, dpool_small])))
    reduced["w_up0"] = (up0_sums[1], r2_up0)
    out_sums = _pair_sum(dw_out, r1_out, "grad_pair_sum_w_out")
    pool_sums = _pair_sum(dpool_small, r1_pool, "grad_pair_sum_pool")
    (dqh, dkh, dv), (r2_out, r2_pool) = _sb_bwd(qh, kh, proj, 3 * W // SB_DIM, y_sb, dmerged, W // SB_DIM,
                                                side=_chips_job([out_sums[0], pool_sums[0]]))
    reduced["w_out"], reduced["pool"] = (out_sums[1], r2_out), (pool_sums[1], r2_pool)
    dq, dk, g["sb_q_norm"], g["sb_k_norm"] = _qk_norm_bwd(dqh, dkh, proj, p["q_norm"], p["k_norm"], W)
    dypre, dz, g["ssd_out_norm"] = _gate_norm_bwd(dmerged, ypre, proj, p["out_norm"])
    dxs, dbm, dcm, ddt4, hv = _ssd_bwd(dypre, xbc, proj_dt, states, bias, alog, ex, d_lanes)
    g["ssd_d"], g["ssd_a_log"], g["ssd_dt_bias"] = (hv[i:i + 1, :SSD_HEADS] for i in range(3))
    ddt = ddt4.reshape(T, SSD_GROUPS, LANE).sum(axis=1).astype(BF16)
    dxbc, dconv_w, g["ssd_conv_b"] = _conv_silu_bwd(jnp.concatenate([dxs, dbm, dcm], axis=1), proj, 4 * W,
                                                    p["conv_w"].shape[1], p["conv_w"], p["conv_b"])
    dproj = jnp.concatenate([dz, dq, dk, dv.astype(BF16), dxbc], axis=1)
    gm = _matmul(dproj, h0, mode="tn", name="hyb_dwin", tm=1024, tn=1024, tk=MM_TK_TOKENS)
    g_dt = _matmul(ddt, h0, mode="tn", name="hyb_dwdt", tm=128, tn=1024, tk=MM_TK)
    g_in_t = jnp.concatenate([gm[:W], gm[4 * W:], g_dt[:SSD_HEADS], gm[W:4 * W]], axis=0)
    last = [g_in_t.reshape(N_DEV, -1, D), _shard_rows([_to_blocks(dconv_w[:4][None], "ssd_conv_w")], LANE)]
    dh0, r1_last = _matmul(ddt, p["w_dt_t"], mode="nn", name="hyb_dh_dt", tm=1024, tn=1024, tk=128,
                           side=_cores_job(last))
    sums = [_pair_sum(a, r, f"grad_pair_sum_{t}") for a, r, t in zip(last, r1_last, ("w_in", "conv"))]
    dh0, r2_last = _matmul(dproj, p["w_main_t"], mode="nn", name="hyb_dh", tm=1024, tn=1024, tk=1024, extras=(dh0,),
                           epilogue=add, side=_chips_job([s[0] for s in sums]))
    reduced.update({t: (s[1], r2) for t, s, r2 in zip(("w_in", "conv"), sums, r2_last)})
    grad_x, _, g["hyb_norm"] = _rmsnorm_bwd(dh0, x, p["hyb_norm"], dx1, name="hyb_norm_bwd")
    return sq, grad_x, g, reduced


def _position():
    return lax.axis_index("x"), lax.axis_index("y"), lax.axis_index("c")


def _run_job(job, name):
    k_in, k_out = len(job.ins), len(job.outs)

    def body(*refs):
        parts = refs[:k_in], refs[k_in:k_in + k_out], refs[k_in + k_out:]
        job.start(*parts)
        if job.mid is not None:
            job.mid(*parts)
        job.finish(*parts)

    return pl.pallas_call(body, name=name, out_shape=list(job.outs), in_specs=[ANY] * k_in, out_specs=[ANY] * k_out,
                          scratch_shapes=list(job.sems))(*job.ins)


def _join(*jobs):
    def parts(ins, outs, sems):
        i = o = s = 0
        for j in jobs:
            yield j, (ins[i:i + len(j.ins)], outs[o:o + len(j.outs)], sems[s:s + len(j.sems)])
            i, o, s = i + len(j.ins), o + len(j.outs), s + len(j.sems)

    def start(*refs):
        for j, p in parts(*refs):
            j.start(*p)

    def mid(*refs):
        for j, p in parts(*refs):
            if j.mid is not None:
                j.mid(*p)

    def finish(*refs):
        for j, p in parts(*refs):
            j.finish(*p)

    return _Job(sum((j.ins for j in jobs), ()), sum((j.outs for j in jobs), ()), sum((j.sems for j in jobs), ()),
                start, mid if any(j.mid is not None for j in jobs) else None, finish)


def _gather_job(vs):
    n = len(vs)

    def plan(v_refs, out_refs, sems):
        send_sems, recv_sems, local_sems = sems
        x, y, c = _position()
        me, sibling = (x, y, c), (x, y, 1 - c)
        chips = [(1 - x, y), (x, 1 - y), (1 - x, 1 - y)]

        def rows(a, px, py, pc):
            return out_refs[a].at[4 * px + 2 * py + pc]

        def copy(a, k, block, to, src=None):
            return pltpu.make_async_remote_copy(
                src_ref=rows(a, *block) if src is None else src, dst_ref=rows(a, *block),
                send_sem=send_sems.at[7 * a + k], recv_sem=recv_sems.at[7 * a + k], device_id=to,
                device_id_type=MESH)

        mine = [pltpu.make_async_copy(v_refs[a], rows(a, *me), local_sems.at[a]) for a in range(n)]
        first = [copy(a, 0, me, sibling, src=v_refs[a]) for a in range(n)]
        first += [copy(a, 1 + j, me, (*chip, c), src=v_refs[a]) for a in range(n) for j, chip in enumerate(chips)]
        landed = [copy(a, 1 + j, (*chip, c), me) for j, chip in enumerate(chips) for a in range(n)]
        passed = [copy(a, 4 + j, (*chip, c), sibling) for j, chip in enumerate(chips) for a in range(n)]
        from_sibling = [copy(a, 0, sibling, me) for a in range(n)]
        from_sibling += [copy(a, 4 + j, (*chip, 1 - c), me) for a in range(n) for j, chip in enumerate(chips)]
        return mine, first, landed, passed, from_sibling

    def start(*refs):
        mine, first, _, _, _ = plan(*refs)
        for cp in mine + first:
            cp.start()

    def mid(*refs):
        _, _, landed, passed, _ = plan(*refs)
        for arrived, onward in zip(landed, passed):
            arrived.wait_recv()
            onward.start()

    def finish(*refs):
        mine, first, _, passed, from_sibling = plan(*refs)
        for cp in from_sibling:
            cp.wait_recv()
        for cp in first + passed:
            cp.wait_send()
        for cp in mine:
            cp.wait()

    return _Job(tuple(vs), tuple(jax.ShapeDtypeStruct((N_DEV,) + v.shape, v.dtype) for v in vs),
                (pltpu.SemaphoreType.DMA((7 * n,)), pltpu.SemaphoreType.DMA((7 * n,)), pltpu.SemaphoreType.DMA((n,))),
                start, mid, finish)


def _cores_job(gs):
    n = len(gs)

    def plan(g_refs, r_refs, sems):
        send_sems, recv_sems = sems
        x, y, c = _position()
        return [pltpu.make_async_remote_copy(
            src_ref=g_refs[a].at[2 * k + (1 - c)], dst_ref=r_refs[a].at[k], send_sem=send_sems.at[4 * a + k],
            recv_sem=recv_sems.at[4 * a + k], device_id=(x, y, 1 - c), device_id_type=MESH)
            for a in range(n) for k in range(4)]

    def start(*refs):
        for cp in plan(*refs):
            cp.start()

    def finish(*refs):
        copies = plan(*refs)
        for cp in copies:
            cp.wait_recv()
        for cp in copies:
            cp.wait_send()

    return _Job(tuple(gs), tuple(jax.ShapeDtypeStruct((4,) + g.shape[1:], g.dtype) for g in gs),
                (pltpu.SemaphoreType.DMA((4 * n,)), pltpu.SemaphoreType.DMA((4 * n,))), start, None, finish)


TILE_BYTES = 2 * 1024 * 1024


def _col_tile(R, C):
    tc = C
    while R * tc * 4 > TILE_BYTES and tc % (2 * LANE) == 0:
        tc //= 2
    return tc


def _pair_sum(gr, r1, name):
    _, R, C = gr.shape
    tc = _col_tile(R, C)
    x, y, c = _position()
    pos = jnp.stack([c, 2 * x + y]).astype(jnp.int32)

    def body(pos_ref, g_ref, r_ref, pb_ref, pm_ref):
        s = g_ref[...] + r_ref[...]
        pb_ref[...] = s.astype(BF16)

        @pl.when(pl.program_id(1) == pos_ref[1])
        def _():
            pm_ref[...] = s

    return _call(body, name=name, grid=(C // tc, 4), prefetch=1,
                 in_specs=[pl.BlockSpec((None, R, tc), lambda j, k, pos: (2 * k + pos[0], 0, j)),
                           pl.BlockSpec((None, R, tc), lambda j, k, pos: (k, 0, j))],
                 out_specs=[pl.BlockSpec((None, R, tc), lambda j, k, pos: (k, 0, j)),
                            pl.BlockSpec((R, tc), lambda j, k, pos: (0, j))],
                 out_shape=[jax.ShapeDtypeStruct((4, R, C), BF16), jax.ShapeDtypeStruct((R, C), F32)],
                 dimension_semantics=("arbitrary", "arbitrary"))(pos, gr, r1)


def _chips_job(pbs):
    n = len(pbs)

    def plan(p_refs, r_refs, sems):
        send_sems, recv_sems = sems
        x, y, c = _position()
        chips = [(1 - x, y), (x, 1 - y), (1 - x, 1 - y)]
        mine = 2 * x + y

        def copy(a, j, src_row, dst_row):
            cx, cy = chips[j]
            return pltpu.make_async_remote_copy(
                src_ref=p_refs[a].at[src_row], dst_ref=r_refs[a].at[dst_row], send_sem=send_sems.at[3 * a + j],
                recv_sem=recv_sems.at[3 * a + j], device_id=(cx, cy, c), device_id_type=MESH)

        sends = [copy(a, j, 2 * cx + cy, mine) for a in range(n) for j, (cx, cy) in enumerate(chips)]
        arrivals = [copy(a, j, mine, 2 * cx + cy) for a in range(n) for j, (cx, cy) in enumerate(chips)]
        return sends, arrivals

    def start(*refs):
        for cp in plan(*refs)[0]:
            cp.start()

    def finish(*refs):
        sends, arrivals = plan(*refs)
        for cp in arrivals:
            cp.wait_recv()
        for cp in sends:
            cp.wait_send()

    return _Job(tuple(pbs), tuple(jax.ShapeDtypeStruct(p.shape, p.dtype) for p in pbs),
                (pltpu.SemaphoreType.DMA((3 * n,)), pltpu.SemaphoreType.DMA((3 * n,))), start, None, finish)


def _adamw(w, grad, m, v):
    m = ADAM_B1 * m + (1.0 - ADAM_B1) * grad
    v = ADAM_B2 * v + (1.0 - ADAM_B2) * (grad * grad)
    m_hat = m / (1.0 - ADAM_B1 ** ADAM_STEP)
    v_hat = v / (1.0 - ADAM_B2 ** ADAM_STEP)
    delta = -ADAM_LR * (m_hat / (jnp.sqrt(v_hat) + ADAM_EPS) + ADAM_WD * w)
    return delta, m, v


def _other_chips():
    x, y, _ = _position()
    mine = 2 * x + y
    return jnp.stack([jnp.where(mine <= j, j + 1, j) for j in range(3)]).astype(jnp.int32)


def _adamw_sharded(pm, r2, w, m, v, name):
    R, C = pm.shape
    tc = _col_tile(R, C)
    update = w is not None

    def body(oth_ref, pm_ref, a_ref, b_ref, c_ref, *refs):
        grad = ((pm_ref[...] + a_ref[...].astype(F32)) + b_ref[...].astype(F32)) + c_ref[...].astype(F32)
        if update:
            w_ref, m_ref, v_ref, g_ref, d_ref, nm_ref, nv_ref = refs
            d, nm, nv = _adamw(w_ref[...], grad, m_ref[...], v_ref[...])
            g_ref[...], d_ref[...], nm_ref[...], nv_ref[...] = grad, d, nm, nv
        else:
            refs[0][...] = grad

    tile = pl.BlockSpec((R, tc), lambda j, oth: (0, j))
    other = [pl.BlockSpec((None, R, tc), functools.partial(lambda j, oth, q: (oth[q], 0, j), q=q)) for q in range(3)]
    shp = jax.ShapeDtypeStruct((R, C), F32)
    n_out = 4 if update else 1
    res = _call(body, name=name, grid=(C // tc,), prefetch=1,
                in_specs=[tile] + other + ([tile, tile, tile] if update else []), out_specs=[tile] * n_out,
                out_shape=[shp] * n_out, dimension_semantics=("arbitrary",))(
                    _other_chips(), pm, r2, r2, r2, *((w, m, v) if update else ()))
    return res if update else res[0]


def _adamw_plain(grad, w, m, v, name):
    R, C = w.shape
    tr = 256

    def body(g_ref, w_ref, m_ref, v_ref, d_ref, nm_ref, nv_ref):
        d_ref[...], nm_ref[...], nv_ref[...] = _adamw(w_ref[...], g_ref[...], m_ref[...], v_ref[...])

    tile = pl.BlockSpec((tr, C), lambda i: (i, 0))
    shp = jax.ShapeDtypeStruct((R, C), F32)
    return _call(body, name=name, grid=(R // tr,), in_specs=[tile] * 4, out_specs=[tile] * 3, out_shape=[shp] * 3,
                 dimension_semantics=("arbitrary",))(grad, w, m, v)


def _adamw_replicated(parts, w, m, v):
    _, R, C = parts.shape

    def body(p_ref, w_ref, m_ref, v_ref, g_ref, d_ref, nm_ref, nv_ref):
        grad = p_ref[0]
        for j in range(1, N_DEV):
            grad = grad + p_ref[j]
        d, nm, nv = _adamw(w_ref[...], grad, m_ref[...], v_ref[...])
        g_ref[...], d_ref[...], nm_ref[...], nv_ref[...] = grad, d, nm, nv

    tile = pl.BlockSpec((R, C), lambda i: (0, 0))
    shp = jax.ShapeDtypeStruct((R, C), F32)
    return _call(body, name="adamw_replicated", grid=(1,),
                 in_specs=[pl.BlockSpec((N_DEV, R, C), lambda i: (0, 0, 0)), tile, tile, tile],
                 out_specs=[tile] * 4, out_shape=[shp] * 4)(parts, w, m, v)


POOL_SHARDED = ("pool_w", "pool_norm", "pool_b", "pool_scale")
REPLICATED = ("hyb_norm", "ssd_conv_b", "ssd_dt_bias", "ssd_a_log", "ssd_d", "ssd_out_norm", "sb_q_norm",
              "sb_k_norm", "mlp_norm")
PACK_COLS = 1024


def _pack(arrays, cols, row_multiple, dtype):
    flat = jnp.concatenate([a.reshape(-1).astype(dtype) for a in arrays])
    n = flat.shape[0]
    total = -(-n // (cols * row_multiple)) * cols * row_multiple
    return jnp.pad(flat, (0, total - n)).reshape(total // cols, cols)


def _shard_rows(blocks, cols):
    flat = jnp.concatenate(blocks, axis=1)
    rows = -(-flat.shape[1] // (8 * cols)) * 8
    return jnp.pad(flat, ((0, 0), (0, rows * cols - flat.shape[1]))).reshape(N_DEV, rows, cols)


def _unpack(packed, shapes):
    flat = packed.reshape(packed.shape[:-2] + (-1,))
    out, off = [], 0
    for s in shapes:
        n = math.prod(s)
        out.append(flat[..., off:off + n].reshape(flat.shape[:-1] + tuple(s)))
        off += n
    return out


def _shard_axis(name):
    return {"hyb_w_in": 2, "hyb_w_out": 1, "mlp_w_up": 2, "mlp_w_down": 1, "pool_w": 2, "ssd_conv_w": 2,
            "pool_norm": 1, "pool_b": 1, "pool_scale": 1}[name]


def _whole(blocks, name):
    ax = _shard_axis(name)
    moved = jnp.moveaxis(blocks, 0, ax)
    s = moved.shape
    return moved.reshape(s[:ax] + (s[ax] * s[ax + 1],) + s[ax + 2:])


def _to_blocks(whole, name):
    ax = _shard_axis(name)
    s = whole.shape
    split = whole.reshape(s[:ax] + (N_DEV, s[ax] // N_DEV) + s[ax + 1:])
    return jnp.moveaxis(split, ax, 0).reshape(N_DEV, -1)


def kernel(x, hyb_norm, hyb_w_in, ssd_conv_w, ssd_conv_b, ssd_dt_bias, ssd_a_log, ssd_d, ssd_out_norm, sb_q_norm, sb_k_norm, hyb_w_out, pool_norm, pool_w, pool_b, pool_scale, mlp_norm, mlp_w_up, mlp_w_down, loss_target, m_hyb_norm, m_hyb_w_in, m_ssd_conv_w, m_ssd_conv_b, m_ssd_dt_bias, m_ssd_a_log, m_ssd_d, m_ssd_out_norm, m_sb_q_norm, m_sb_k_norm, m_hyb_w_out, m_pool_norm, m_pool_w, m_pool_b, m_pool_scale, m_mlp_norm, m_mlp_w_up, m_mlp_w_down, v_hyb_norm, v_hyb_w_in, v_ssd_conv_w, v_ssd_conv_b, v_ssd_dt_bias, v_ssd_a_log, v_ssd_d, v_ssd_out_norm, v_sb_q_norm, v_sb_k_norm, v_hyb_w_out, v_pool_norm, v_pool_w, v_pool_b, v_pool_scale, v_mlp_norm, v_mlp_w_up, v_mlp_w_down):
    args = dict(locals())
    names = ("hyb_norm", "hyb_w_in", "ssd_conv_w", "ssd_conv_b", "ssd_dt_bias", "ssd_a_log", "ssd_d", "ssd_out_norm",
             "sb_q_norm", "sb_k_norm", "hyb_w_out", "pool_norm", "pool_w", "pool_b", "pool_scale", "mlp_norm",
             "mlp_w_up", "mlp_w_down")
    wt = {n: args[n] for n in names}
    T, D = x.shape[1], x.shape[2]
    W = SSD_HEADS * SSD_HEAD_DIM

    conv_dim = ssd_conv_b.shape[-1]
    c1, c2 = W + conv_dim, W + conv_dim + SSD_HEADS
    vec_names = ("ssd_conv_w", "pool_norm", "pool_b", "pool_scale")

    gathered = _run_job(_gather_job([hyb_w_in[0].T.astype(BF16), _pack([wt[n] for n in vec_names], LANE, 8, F32)]),
                        "gather_hybrid_weights")
    in_t = gathered[0].reshape(-1, D)
    vec = {n: _whole(b, n) for n, b in zip(vec_names, _unpack(gathered[1], [wt[n].shape for n in vec_names]))}
    p = {
        "w_main_t": jnp.concatenate([in_t[:W], in_t[c2:], in_t[W:c1]], axis=0),
        "w_dt_t": jnp.pad(in_t[c1:c2], ((0, LANE - SSD_HEADS), (0, 0))),
        "conv_w": vec["ssd_conv_w"][0], "conv_b": ssd_conv_b,
        "pool_norm": vec["pool_norm"], "pool_b": vec["pool_b"], "pool_scale": vec["pool_scale"],
        "hyb_norm": hyb_norm, "mlp_norm": mlp_norm, "out_norm": ssd_out_norm, "q_norm": sb_q_norm,
        "k_norm": sb_k_norm, "ssd_d": ssd_d, "ssd_a_log": ssd_a_log, "ssd_dt_bias": ssd_dt_bias,
    }

    up, down = mlp_w_up.astype(BF16), mlp_w_down.astype(BF16)
    sq, grad_x, g, reduced = _local_step(x[0], loss_target[0], p,
                                         ([up[0], down[0], pool_w.astype(BF16)], [up[1], down[1]],
                                          [hyb_w_out[0].astype(BF16)]))
    loss = lax.psum(0.5 * jnp.sum(sq) / D, ("x", "y", "c"))

    res = {}
    grad_in = _adamw_sharded(*reduced["w_in"], None, None, None, "grad_sum_w_in").T
    res["hyb_w_in"] = [a[None] for a in (grad_in, *_adamw_plain(grad_in, hyb_w_in[0], m_hyb_w_in[0], v_hyb_w_in[0],
                                                                "adamw_w_in"))]
    res["hyb_w_out"] = [a[None] for a in _adamw_sharded(*reduced["w_out"], hyb_w_out[0], m_hyb_w_out[0],
                                                        v_hyb_w_out[0], "adamw_w_out")]
    for t, n in (("w_up", "mlp_w_up"), ("w_down", "mlp_w_down")):
        layers = [_adamw_sharded(*reduced[f"{t}{l}"], args[n][l], args["m_" + n][l], args["v_" + n][l],
                                 f"adamw_{n}{l}") for l in range(2)]
        res[n] = [jnp.stack([layers[0][k], layers[1][k]]) for k in range(4)]
    for t, group, cols in (("pool", POOL_SHARDED, PACK_COLS), ("conv", ("ssd_conv_w",), LANE)):
        packed = [_pack([args[pre + n] for n in group], cols, 8, F32) for pre in ("", "m_", "v_")]
        small = [_unpack(o, [wt[n].shape for n in group]) for o in _adamw_sharded(*reduced[t], *packed, f"adamw_{t}")]
        for i, n in enumerate(group):
            res[n] = [small[k][i] for k in range(4)]

    parts = _run_job(_gather_job([_pack([g[n] for n in REPLICATED], LANE, 8, F32)]), "gather_vector_grads")[0]
    packed = [_pack([args[pre + n] for n in REPLICATED], LANE, 8, F32) for pre in ("", "m_", "v_")]
    shapes = [wt[n].shape for n in REPLICATED]
    repl = [_unpack(o, shapes) for o in _adamw_replicated(parts, *packed)]
    for i, n in enumerate(REPLICATED):
        res[n] = [repl[k][i] for k in range(4)]

    outs = [res[n][k] for k in range(4) for n in names]
    return (loss, grad_x[None], *outs)
```

```python
import functools
import math
from typing import Callable, NamedTuple, Optional

import jax
import jax.numpy as jnp
from jax import lax
from jax.experimental import pallas as pl
from jax.experimental.pallas import tpu as pltpu

F32 = jnp.float32
BF16 = jnp.bfloat16
EPS = 1e-6
V7X_VMEM_LIMIT = 56 * 1024 * 1024
MESH = pl.DeviceIdType.MESH
ANY = pl.BlockSpec(memory_space=pl.ANY)
N_DEV = 8

SSD_HEADS = 32
SSD_HEAD_DIM = 64
SSD_STATE = 128
SSD_GROUPS = 4
SSD_CHUNK = 128
GROUP_W = SSD_HEADS * SSD_HEAD_DIM // SSD_GROUPS
HEADS_PER_GROUP = SSD_HEADS // SSD_GROUPS
SB_HEADS = 16
SB_DIM = 128
POOL_WINDOWS = (2, 4, 8, 16)
LANE = 128

ADAM_LR = 0.001
ADAM_B1 = 0.9
ADAM_B2 = 0.999
ADAM_EPS = 1e-08
ADAM_WD = 0.01
ADAM_STEP = 10

NN = (((1,), (0,)), ((), ()))
NT = (((1,), (1,)), ((), ()))
TN = (((0,), (0,)), ((), ()))

class _Job(NamedTuple):
    ins: tuple
    outs: tuple
    sems: tuple
    start: Callable
    mid: Optional[Callable]
    finish: Callable


def _call(body, *, name, grid, in_specs, out_specs, out_shape, scratch=(), prefetch=0, side=None, **params):
    if side is not None:
        single = not isinstance(out_shape, (list, tuple))
        out_specs = [out_specs] if single else list(out_specs)
        out_shape = [out_shape] if single else list(out_shape)
        n_in, n_out, n_scr = len(in_specs), len(out_shape), len(scratch)
        k_in, k_out = len(side.ins), len(side.outs)
        inner = body
        steps = math.prod(grid)

        def body(*refs):
            pre, rest = refs[:prefetch], refs[prefetch:]
            ins, s_in = rest[:n_in], rest[n_in:n_in + k_in]
            rest = rest[n_in + k_in:]
            outs, s_out = rest[:n_out], rest[n_out:n_out + k_out]
            rest = rest[n_out + k_out:]
            scr, s_sem = rest[:n_scr], rest[n_scr:]
            step = 0
            for axis, size in enumerate(grid):
                step = step * size + pl.program_id(axis)

            @pl.when(step == 0)
            def _():
                side.start(s_in, s_out, s_sem)

            inner(*pre, *ins, *outs, *scr)
            if side.mid is not None:
                @pl.when(step == (3 * steps) // 4)
                def _():
                    side.mid(s_in, s_out, s_sem)

            @pl.when(step == steps - 1)
            def _():
                side.finish(s_in, s_out, s_sem)

        params = dict(params, dimension_semantics=("arbitrary",) * len(grid))
        res = _call(body, name=name, grid=grid, in_specs=list(in_specs) + [ANY] * k_in,
                    out_specs=out_specs + [ANY] * k_out, out_shape=out_shape + list(side.outs),
                    scratch=list(scratch) + list(side.sems), prefetch=prefetch, **params)
        return lambda *args: (lambda r: ((r[0] if single else r[:n_out]), r[n_out:]))(res(*args, *side.ins))
    cp = pltpu.CompilerParams(vmem_limit_bytes=V7X_VMEM_LIMIT, **params)
    if prefetch:
        gs = pltpu.PrefetchScalarGridSpec(num_scalar_prefetch=prefetch, grid=grid, in_specs=in_specs,
                                          out_specs=out_specs, scratch_shapes=list(scratch))
        return pl.pallas_call(body, name=name, grid_spec=gs, out_shape=out_shape, compiler_params=cp)
    return pl.pallas_call(body, name=name, grid=grid, in_specs=in_specs, out_specs=out_specs,
                          out_shape=out_shape, scratch_shapes=list(scratch), compiler_params=cp)


def _dot(a, b, dims=NN):
    return lax.dot_general(a, b, dims, preferred_element_type=F32)


def _split3(x):
    hi = x.astype(BF16)
    r = x - hi.astype(F32)
    mid = r.astype(BF16)
    lo = (r - mid.astype(F32)).astype(BF16)
    return hi, mid, lo


def _dot3(x, m, dims=NN):
    hi, mid, lo = _split3(x)
    return _dot(hi, m, dims) + _dot(mid, m, dims) + _dot(lo, m, dims)


def _dot3l(m, x, dims=NN):
    hi, mid, lo = _split3(x)
    return _dot(m, hi, dims) + _dot(m, mid, dims) + _dot(m, lo, dims)


def _dot2(x, m):
    hi = x.astype(BF16)
    lo = (x - hi.astype(F32)).astype(BF16)
    return _dot(hi, m) + _dot(lo, m)


def _sigmoid(x):
    return 1.0 / (1.0 + jnp.exp(-x))


def _softplus(x):
    return jnp.maximum(x, 0.0) + jnp.log(1.0 + jnp.exp(-jnp.abs(x)))


def _iota(shape, dim):
    return lax.broadcasted_iota(jnp.int32, shape, dim)


def _matmul(a, b, *, mode, name, tm, tn, tk, extras=(), epilogue=None, out_dtypes=(F32,), mnk=None, b_spec=None,
            out_spec=None, out_dims=None, side=None):
    if mnk is not None:
        M, N, K = mnk
    elif mode == "tn":
        (K, M), N = a.shape, b.shape[1]
    else:
        (M, K), N = a.shape, b.shape[1 if mode == "nn" else 0]
    tm, tn, tk = min(tm, M), min(tn, N), min(tk, K)
    assert M % tm == 0 and N % tn == 0 and K % tk == 0, (name, M, N, K, tm, tn, tk)
    if mode == "nn":
        a_spec = pl.BlockSpec((tm, tk), lambda i, j, k: (i, k))
        b_spec = b_spec or pl.BlockSpec((tk, tn), lambda i, j, k: (k, j))
        dims = NN
    elif mode == "nt":
        a_spec = pl.BlockSpec((tm, tk), lambda i, j, k: (i, k))
        b_spec = b_spec or pl.BlockSpec((tn, tk), lambda i, j, k: (j, k))
        dims = NT
    else:
        a_spec = pl.BlockSpec((tk, tm), lambda i, j, k: (k, i))
        b_spec = b_spec or pl.BlockSpec((tk, tn), lambda i, j, k: (k, j))
        dims = TN
    nk = K // tk
    ex_specs = []
    for e in extras:
        if e.shape[0] == 1:
            ex_specs.append(pl.BlockSpec((1, tn), lambda i, j, k: (0, j)))
        else:
            ex_specs.append(pl.BlockSpec((tm, tn), lambda i, j, k: (i, j)))
    n_ex, n_out = len(extras), len(out_dtypes)

    def body(*refs):
        a_ref, b_ref = refs[0], refs[1]
        ex_refs = refs[2:2 + n_ex]
        o_refs = refs[2 + n_ex:2 + n_ex + n_out]

        def finish(r):
            outs = (r,) if epilogue is None else epilogue(r, *[e[...] for e in ex_refs])
            for o_ref, o in zip(o_refs, outs):
                o_ref[...] = o.astype(o_ref.dtype)

        b = b_ref[...]
        if b.ndim == 3:
            b = b.reshape(-1, b.shape[-1]) if mode == "nn" else jnp.concatenate([b[0], b[1]], axis=1)
        part = _dot(a_ref[...].astype(BF16), b.astype(BF16), dims)
        if nk == 1:
            finish(part)
            return
        acc = refs[2 + n_ex + n_out]
        k = pl.program_id(2)

        @pl.when(k == 0)
        def _():
            acc[...] = part

        @pl.when(jnp.logical_and(k > 0, k < nk - 1))
        def _():
            acc[...] += part

        @pl.when(k == nk - 1)
        def _():
            finish(acc[...] + part)

    out_shape = [jax.ShapeDtypeStruct(out_dims or (M, N), d) for d in out_dtypes]
    out_specs = [out_spec or pl.BlockSpec((tm, tn), lambda i, j, k: (i, j)) for _ in out_dtypes]
    res = _call(body, name=name, grid=(M // tm, N // tn, nk), in_specs=[a_spec, b_spec] + ex_specs,
                out_specs=out_specs, out_shape=out_shape, scratch=[pltpu.VMEM((tm, tn), F32)] if nk > 1 else [],
                dimension_semantics=("parallel", "parallel", "arbitrary"), side=side)(a, b, *extras)
    if side is not None:
        return (res[0] if n_out > 1 else res[0][0]), res[1]
    return res if n_out > 1 else res[0]


def _rowwise(fn, *, name, T, tm, tiles, vecs, out_tiles, out_vecs):
    n_t, n_v, n_ot, n_ov = len(tiles), len(vecs), len(out_tiles), len(out_vecs)

    def body(*refs):
        ins = [r[...] for r in refs[:n_t + n_v]]
        outs = fn(*ins)
        ot_refs = refs[n_t + n_v:n_t + n_v + n_ot]
        ov_refs = refs[n_t + n_v + n_ot:]
        for r, o in zip(ot_refs, outs[:n_ot]):
            r[...] = o.astype(r.dtype)
        if n_ov:
            first = pl.program_id(0) == 0

            @pl.when(first)
            def _():
                for r, o in zip(ov_refs, outs[n_ot:]):
                    r[...] = o

            @pl.when(jnp.logical_not(first))
            def _():
                for r, o in zip(ov_refs, outs[n_ot:]):
                    r[...] += o

    in_specs = [pl.BlockSpec((tm, w), functools.partial(lambda i, cb: (i, cb), cb=cb)) for _, w, cb in tiles]
    in_specs += [pl.BlockSpec(v.shape, lambda i: (0, 0)) for v in vecs]
    out_specs = [pl.BlockSpec((tm, w), lambda i: (i, 0)) for w, _ in out_tiles]
    out_specs += [pl.BlockSpec((r, w), lambda i: (0, 0)) for r, w in out_vecs]
    out_shape = [jax.ShapeDtypeStruct((T, w), d) for w, d in out_tiles]
    out_shape += [jax.ShapeDtypeStruct((r, w), F32) for r, w in out_vecs]
    return _call(body, name=name, grid=(T // tm,), in_specs=in_specs, out_specs=out_specs, out_shape=out_shape,
                 dimension_semantics=("arbitrary",))(*[t[0] for t in tiles], *vecs)


def _colsum(x):
    return jnp.sum(x, axis=0, keepdims=True)


def _rms_fwd(x, g):
    r = lax.rsqrt(jnp.mean(x * x, axis=-1, keepdims=True) + EPS)
    return x * r * g


def _rms_bwd(dh, x, g):
    r = lax.rsqrt(jnp.mean(x * x, axis=-1, keepdims=True) + EPS)
    xh = x * r
    dxh = dh * g
    dx = r * (dxh - xh * jnp.mean(dxh * xh, axis=-1, keepdims=True))
    return dx, _colsum(dh * xh)


def _rmsnorm(x, g, *, name, dtype):
    T, D = x.shape
    return _rowwise(lambda xv, gv: (_rms_fwd(xv, gv),), name=name, T=T, tm=256, tiles=[(x, D, 0)], vecs=[g],
                    out_tiles=[(D, dtype)], out_vecs=[])[0]


def _rmsnorm_bwd(dh, x, g, dres, *, name):
    T, D = x.shape

    def fn(dhv, xv, drv, gv):
        dx, dg = _rms_bwd(dhv, xv, gv)
        return drv + dx, drv + dx, dg

    return _rowwise(fn, name=name, T=T, tm=256, tiles=[(dh, D, 0), (x, D, 0), (dres, D, 0)], vecs=[g],
                    out_tiles=[(D, F32), (D, BF16)], out_vecs=[(1, D)])


def _group_slices(width, group):
    return [slice(i, i + group) for i in range(0, width, group)]


def _gate_norm_fwd(ypre, proj, gain):
    T, W = ypre.shape

    def fn(y, z, g):
        gated = y * (z * _sigmoid(z))
        return (jnp.concatenate([_rms_fwd(gated[:, s], g[:, s]) for s in _group_slices(W, GROUP_W)], axis=1),)

    return _rowwise(fn, name="ssd_gate_norm", T=T, tm=256, tiles=[(ypre, W, 0), (proj, W, 0)], vecs=[gain],
                    out_tiles=[(W, BF16)], out_vecs=[])[0]


def _gate_norm_bwd(dmerged, ypre, proj, gain):
    T, W = ypre.shape

    def fn(do, y, z, g):
        sg = _sigmoid(z)
        sz = z * sg
        gated = y * sz
        parts = [_rms_bwd(do[:, s], gated[:, s], g[:, s]) for s in _group_slices(W, GROUP_W)]
        dgated = jnp.concatenate([p[0] for p in parts], axis=1)
        dgain = jnp.concatenate([p[1] for p in parts], axis=1)
        return dgated * sz, dgated * y * (sg * (1.0 + z * (1.0 - sg))), dgain

    return _rowwise(fn, name="ssd_gate_norm_bwd", T=T, tm=256, tiles=[(dmerged, W, 0), (ypre, W, 0), (proj, W, 0)],
                    vecs=[gain], out_tiles=[(W, F32), (W, BF16)], out_vecs=[(1, W)])


def _qk_norm_fwd(proj, qg, kg, W):
    T = proj.shape[0]

    def fn(q, k, gq, gk):
        sl = _group_slices(W, SB_DIM)
        return (jnp.concatenate([_rms_fwd(q[:, s], gq) for s in sl], axis=1),
                jnp.concatenate([_rms_fwd(k[:, s], gk) for s in sl], axis=1))

    return _rowwise(fn, name="sb_qk_norm", T=T, tm=256, tiles=[(proj, W, 1), (proj, W, 2)], vecs=[qg, kg],
                    out_tiles=[(W, BF16), (W, BF16)], out_vecs=[])


def _qk_norm_bwd(dqh, dkh, proj, qg, kg, W):
    T = proj.shape[0]

    def fn(dq, dk, q, k, gq, gk):
        sl = _group_slices(W, SB_DIM)
        pq = [_rms_bwd(dq[:, s], q[:, s], gq) for s in sl]
        pk = [_rms_bwd(dk[:, s], k[:, s], gk) for s in sl]
        return (jnp.concatenate([p[0] for p in pq], axis=1), jnp.concatenate([p[0] for p in pk], axis=1),
                sum(p[1] for p in pq), sum(p[1] for p in pk))

    return _rowwise(fn, name="sb_qk_norm_bwd", T=T, tm=256,
                    tiles=[(dqh, W, 0), (dkh, W, 0), (proj, W, 1), (proj, W, 2)], vecs=[qg, kg],
                    out_tiles=[(W, BF16), (W, BF16)], out_vecs=[(1, SB_DIM), (1, SB_DIM)])


def _loss_grad(y, target):
    T, D = y.shape

    def fn(yv, tv):
        err = yv - tv
        return err * (1.0 / D), err * (1.0 / D), _colsum(err * err)

    return _rowwise(fn, name="loss_grad", T=T, tm=256, tiles=[(y, D, 0), (target, D, 0)], vecs=[],
                    out_tiles=[(D, F32), (D, BF16)], out_vecs=[(1, D)])


def _pool_scale_bwd(dx, ypre, scale):
    T, D = dx.shape

    def fn(d, yp, s):
        dpre = d * s
        return dpre, _colsum(d * yp), _colsum(dpre)

    return _rowwise(fn, name="pool_scale_bwd", T=T, tm=256, tiles=[(dx, D, 0), (ypre, D, 0)], vecs=[scale],
                    out_tiles=[(D, BF16)], out_vecs=[(1, D), (1, D)])


ROWS = 512


def _past(cur, prev, k):
    row = _iota(cur.shape, 0)
    rc = pltpu.roll(cur, k, 0)
    if prev is None:
        return jnp.where(row >= k, rc, 0.0)
    return jnp.where(row >= k, rc, pltpu.roll(prev, k, 0))


def _future(cur, nxt, k):
    n = cur.shape[0]
    row = _iota(cur.shape, 0)
    rc = pltpu.roll(cur, n - k, 0)
    if nxt is None:
        return jnp.where(row < n - k, rc, 0.0)
    return jnp.where(row < n - k, rc, pltpu.roll(nxt, n - k, 0))


def _chunk(ref, ci):
    return ref[ci * ROWS:(ci + 1) * ROWS, :]


def _conv_pre(x_ref, w, b, ci):
    cur = _chunk(x_ref, ci)
    prev = _chunk(x_ref, ci - 1) if ci > 0 else None
    taps = [_past(cur, prev, 3), _past(cur, prev, 2), _past(cur, prev, 1), cur]
    xc = b + sum(w[j:j + 1, :] * taps[j] for j in range(4))
    return xc, taps


CONV_COLS = 256


def _conv_silu_fwd(proj, col0, width, conv_w, conv_b):
    T = proj.shape[0]

    def body(x_ref, w_ref, b_ref, o_ref):
        w, b = w_ref[...], b_ref[...]
        for ci in range(T // ROWS):
            xc, _ = _conv_pre(x_ref, w, b, ci)
            o_ref[ci * ROWS:(ci + 1) * ROWS, :] = xc * _sigmoid(xc)

    cb0 = col0 // CONV_COLS
    return _call(body, name="ssd_conv_silu", grid=(width // CONV_COLS,),
                 in_specs=[pl.BlockSpec((T, CONV_COLS), lambda j: (0, cb0 + j)),
                           pl.BlockSpec((4, CONV_COLS), lambda j: (0, j)),
                           pl.BlockSpec((1, CONV_COLS), lambda j: (0, j))],
                 out_specs=pl.BlockSpec((T, CONV_COLS), lambda j: (0, j)),
                 out_shape=jax.ShapeDtypeStruct((T, width), F32))(proj, conv_w, conv_b)


def _conv_silu_bwd(dxa, proj, col0, width, conv_w, conv_b):
    T = proj.shape[0]
    nchunk = T // ROWS

    def body(d_ref, x_ref, w_ref, b_ref, dx_ref, dw_ref, db_ref, dxc_ref):
        w, b = w_ref[...], b_ref[...]
        dw = [jnp.zeros((1, CONV_COLS), F32) for _ in range(4)]
        db = jnp.zeros((1, CONV_COLS), F32)
        for ci in range(nchunk):
            xc, taps = _conv_pre(x_ref, w, b, ci)
            sg = _sigmoid(xc)
            dxc = _chunk(d_ref, ci) * (sg * (1.0 + xc * (1.0 - sg)))
            dxc_ref[ci * ROWS:(ci + 1) * ROWS, :] = dxc
            db = db + _colsum(dxc)
            dw = [dw[j] + _colsum(dxc * taps[j]) for j in range(4)]
        dw_ref[...] = jnp.concatenate(dw + [jnp.zeros((4, CONV_COLS), F32)], axis=0)
        db_ref[...] = db
        for ci in range(nchunk):
            cur = _chunk(dxc_ref, ci)
            nxt = _chunk(dxc_ref, ci + 1) if ci + 1 < nchunk else None
            dx = (w[3:4, :] * cur + w[2:3, :] * _future(cur, nxt, 1) + w[1:2, :] * _future(cur, nxt, 2)
                  + w[0:1, :] * _future(cur, nxt, 3))
            dx_ref[ci * ROWS:(ci + 1) * ROWS, :] = dx.astype(dx_ref.dtype)

    cb0 = col0 // CONV_COLS
    return _call(body, name="ssd_conv_silu_bwd", grid=(width // CONV_COLS,),
                 in_specs=[pl.BlockSpec((T, CONV_COLS), lambda j: (0, j)),
                           pl.BlockSpec((T, CONV_COLS), lambda j: (0, cb0 + j)),
                           pl.BlockSpec((4, CONV_COLS), lambda j: (0, j)),
                           pl.BlockSpec((1, CONV_COLS), lambda j: (0, j))],
                 out_specs=[pl.BlockSpec((T, CONV_COLS), lambda j: (0, j)),
                            pl.BlockSpec((8, CONV_COLS), lambda j: (0, j)),
                            pl.BlockSpec((1, CONV_COLS), lambda j: (0, j))],
                 out_shape=[jax.ShapeDtypeStruct((T, width), BF16), jax.ShapeDtypeStruct((8, width), F32),
                            jax.ShapeDtypeStruct((1, width), F32)],
                 scratch=[pltpu.VMEM((T, CONV_COLS), F32)])(dxa, proj, conv_w, conv_b)


def _window_count(ci, win, shape):
    t = (_iota(shape, 0) + ci * ROWS + 1).astype(F32)
    return jnp.minimum(t, float(win))


def _pool_diff_fwd(h):
    T, D = h.shape
    per_group = D // len(POOL_WINDOWS) // LANE

    def body(h_ref, o_ref):
        j = pl.program_id(0)
        for gi, win in enumerate(POOL_WINDOWS):
            @pl.when(j // per_group == gi)
            def _(win=win):
                for ci in range(T // ROWS):
                    cur = _chunk(h_ref, ci)
                    prev = _chunk(h_ref, ci - 1) if ci > 0 else None
                    s = cur
                    for k in range(1, win):
                        s = s + _past(cur, prev, k)
                    d = s / _window_count(ci, win, cur.shape) - cur
                    o_ref[ci * ROWS:(ci + 1) * ROWS, :] = d.astype(o_ref.dtype)

    return _call(body, name="pool_diff", grid=(D // LANE,), in_specs=[pl.BlockSpec((T, LANE), lambda j: (0, j))],
                 out_specs=pl.BlockSpec((T, LANE), lambda j: (0, j)),
                 out_shape=jax.ShapeDtypeStruct((T, D), BF16))(h)


def _pool_diff_bwd(dd):
    T, D = dd.shape
    per_group = D // len(POOL_WINDOWS) // LANE
    nchunk = T // ROWS

    def body(d_ref, o_ref):
        j = pl.program_id(0)
        for gi, win in enumerate(POOL_WINDOWS):
            @pl.when(j // per_group == gi)
            def _(win=win):
                for ci in range(nchunk):
                    cur = _chunk(d_ref, ci)
                    q = cur / _window_count(ci, win, cur.shape)
                    qn = None
                    if ci + 1 < nchunk:
                        qn = _chunk(d_ref, ci + 1) / _window_count(ci + 1, win, cur.shape)
                    s = q - cur
                    for k in range(1, win):
                        s = s + _future(q, qn, k)
                    o_ref[ci * ROWS:(ci + 1) * ROWS, :] = s

    return _call(body, name="pool_diff_bwd", grid=(D // LANE,), in_specs=[pl.BlockSpec((T, LANE), lambda j: (0, j))],
                 out_specs=pl.BlockSpec((T, LANE), lambda j: (0, j)),
                 out_shape=jax.ShapeDtypeStruct((T, D), F32))(dd)


def _pool_mm_fwd(d, w, b, scale, x):
    T, D = d.shape
    G = w.shape[1]
    tm = 512

    def body(d_ref, w_ref, b_ref, s_ref, x_ref, yp_ref, o_ref):
        yp = _dot(d_ref[...], w_ref[...]) + b_ref[...]
        yp_ref[...] = yp
        o_ref[...] = x_ref[...] + yp * s_ref[...]

    tile = pl.BlockSpec((tm, G), lambda i, g: (i, g))
    vec = pl.BlockSpec((1, G), lambda i, g: (0, g))
    return _call(body, name="pool_mm", grid=(T // tm, D // G),
                 in_specs=[tile, pl.BlockSpec((None, G, G), lambda i, g: (g, 0, 0)), vec, vec, tile],
                 out_specs=[tile, tile],
                 out_shape=[jax.ShapeDtypeStruct((T, D), F32), jax.ShapeDtypeStruct((T, D), F32)])(d, w, b, scale, x)


def _pool_mm_dx(dpre, w):
    T, D = dpre.shape
    G = w.shape[1]
    tm = 512

    def body(d_ref, w_ref, o_ref):
        o_ref[...] = _dot(d_ref[...], w_ref[...], NT)

    tile = pl.BlockSpec((tm, G), lambda i, g: (i, g))
    return _call(body, name="pool_mm_dx", grid=(T // tm, D // G),
                 in_specs=[tile, pl.BlockSpec((None, G, G), lambda i, g: (g, 0, 0))], out_specs=tile,
                 out_shape=jax.ShapeDtypeStruct((T, D), F32))(dpre, w)


def _pool_mm_dw(d, dpre):
    T, D = d.shape
    G = D // len(POOL_WINDOWS)
    tk = 512

    def body(d_ref, p_ref, o_ref):
        @pl.when(pl.program_id(1) == 0)
        def _():
            o_ref[...] = jnp.zeros_like(o_ref)

        o_ref[...] += _dot(d_ref[...], p_ref[...], TN)

    tile = pl.BlockSpec((tk, G), lambda g, k: (k, g))
    return _call(body, name="pool_mm_dw", grid=(D // G, T // tk), in_specs=[tile, tile],
                 out_specs=pl.BlockSpec((None, G, G), lambda g, k: (g, 0, 0)),
                 out_shape=jax.ShapeDtypeStruct((D // G, G, G), F32))(d, dpre)


def _ssd_consts(ssd_d, a_log, dt_bias):
    head = jnp.arange(LANE)[:, None]
    lane = jnp.arange(GROUP_W)[None, :]
    ex = jnp.stack([(head == g * HEADS_PER_GROUP + lane // SSD_HEAD_DIM) for g in range(SSD_GROUPS)])
    d_lanes = jnp.repeat(ssd_d.reshape(-1), SSD_HEAD_DIM).reshape(1, -1)
    pad = lambda v: jnp.pad(v.reshape(1, -1), ((0, 0), (0, LANE - SSD_HEADS)))
    return ex.astype(BF16), d_lanes, pad(a_log), pad(dt_bias)


def _ssd_chunk(xs, bm, cm, dtr, bias, alog, ex):
    L = SSD_CHUNK
    row, col = _iota((L, L), 0), _iota((L, L), 1)
    causal = col <= row
    ltri = causal.astype(BF16)
    a_row = -jnp.exp(alog)
    dt = _softplus(dtr + bias)
    da = dt * a_row
    dt_l = _dot3(dt, ex)
    da_l = _dot3(da, ex)
    acs_l = _dot3l(ltri, da_l)
    acs_r = _dot3(da, (row <= col).astype(BF16), TN)
    last_l = acs_l[L - 1:L, :]
    e_l = jnp.exp(last_l - acs_l)
    f_l = jnp.exp(acs_l)
    cd_l = jnp.exp(last_l)
    xdt = xs * dt_l
    cb = _dot(cm.astype(BF16), bm.astype(BF16), NT)
    return dict(causal=causal, dt=dt, da=da, dt_l=dt_l, acs_l=acs_l, acs_r=acs_r, e_l=e_l, f_l=f_l, cd_l=cd_l,
                xdt=xdt, cb=cb, a_row=a_row, ltri=ltri)


def _head_decay(q, acsrow_ref, g, r):
    colv = q["acs_l"][:, r * SSD_HEAD_DIM:r * SSD_HEAD_DIM + 1]
    rowv = acsrow_ref[pl.ds(g * HEADS_PER_GROUP + r, 1), :]
    return jnp.exp(jnp.where(q["causal"], colv - rowv, -1e30))


def _ssd_specs(T):
    L = SSD_CHUNK
    xs = pl.BlockSpec((L, GROUP_W), lambda g, c: (c, g))
    nb = SSD_HEADS * SSD_HEAD_DIM // SSD_STATE
    bm = pl.BlockSpec((L, SSD_STATE), lambda g, c: (c, nb + g))
    cm = pl.BlockSpec((L, SSD_STATE), lambda g, c: (c, nb + SSD_GROUPS + g))
    dtr = pl.BlockSpec((L, LANE), lambda g, c: (c, 0))
    vec = pl.BlockSpec((1, LANE), lambda g, c: (0, 0))
    ex = pl.BlockSpec((None, LANE, GROUP_W), lambda g, c: (g, 0, 0))
    dl = pl.BlockSpec((1, GROUP_W), lambda g, c: (0, g))
    return xs, bm, cm, dtr, vec, ex, dl


def _ssd_fwd(xbc, proj_dt, bias, alog, ex, d_lanes, side=None):
    T = xbc.shape[0]
    L, nc, W = SSD_CHUNK, T // SSD_CHUNK, SSD_HEADS * SSD_HEAD_DIM

    def body(xs_ref, b_ref, c_ref, dtr_ref, bias_ref, alog_ref, ex_ref, dl_ref, y_ref, st_ref, state, acsrow):
        g, c = pl.program_id(0), pl.program_id(1)

        @pl.when(c == 0)
        def _():
            state[...] = jnp.zeros_like(state)

        xs, bm, cm = xs_ref[...], b_ref[...], c_ref[...]
        q = _ssd_chunk(xs, bm, cm, dtr_ref[...], bias_ref[...], alog_ref[...], ex_ref[...])
        acsrow[...] = q["acs_r"]
        prev = state[...]
        st_ref[...] = prev
        xdt_b = q["xdt"].astype(BF16)
        yoff = q["f_l"] * _dot(cm.astype(BF16), prev.astype(BF16))
        lane = _iota((L, LANE), 1)
        for p in range(HEADS_PER_GROUP // 2):
            sl = slice(p * LANE, (p + 1) * LANE)
            ma = (_head_decay(q, acsrow, g, 2 * p) * q["cb"]).astype(BF16)
            mb = (_head_decay(q, acsrow, g, 2 * p + 1) * q["cb"]).astype(BF16)
            yd = jnp.where(lane < SSD_HEAD_DIM, _dot(ma, xdt_b[:, sl]), _dot(mb, xdt_b[:, sl]))
            y_ref[:, sl] = yd + yoff[:, sl] + dl_ref[:, sl] * xs[:, sl]
        st_new = _dot(bm.astype(BF16), (q["xdt"] * q["e_l"]).astype(BF16), TN)
        state[...] = q["cd_l"] * prev + st_new

    xs, bm, cm, dtr, vec, exs, dl = _ssd_specs(T)
    return _call(body, name="ssd_scan", grid=(SSD_GROUPS, nc), in_specs=[xs, bm, cm, dtr, vec, vec, exs, dl],
                 out_specs=[pl.BlockSpec((L, GROUP_W), lambda g, c: (c, g)),
                            pl.BlockSpec((None, None, SSD_STATE, GROUP_W), lambda g, c: (c, g, 0, 0))],
                 out_shape=[jax.ShapeDtypeStruct((T, W), F32),
                            jax.ShapeDtypeStruct((nc, SSD_GROUPS, SSD_STATE, GROUP_W), F32)],
                 scratch=[pltpu.VMEM((SSD_STATE, GROUP_W), F32), pltpu.VMEM((LANE, L), F32)], side=side,
                 dimension_semantics=("arbitrary", "arbitrary"))(xbc, xbc, xbc, proj_dt, bias, alog, ex, d_lanes)


def _ssd_bwd(dy, xbc, proj_dt, states, bias, alog, ex, d_lanes):
    T = xbc.shape[0]
    L, nc, W = SSD_CHUNK, T // SSD_CHUNK, SSD_HEADS * SSD_HEAD_DIM
    P = SSD_HEAD_DIM

    def body(dy_ref, xs_ref, b_ref, c_ref, dtr_ref, st_ref, bias_ref, alog_ref, ex_ref, dl_ref,
             dxs_ref, db_ref, dc_ref, ddt_ref, hv_ref, dstate, acsrow):
        g, c = pl.program_id(0), pl.program_id(1)

        @pl.when(c == 0)
        def _():
            dstate[...] = jnp.zeros_like(dstate)

        @pl.when(jnp.logical_and(g == 0, c == 0))
        def _():
            hv_ref[...] = jnp.zeros_like(hv_ref)

        xs, bm, cm, ex = xs_ref[...], b_ref[...], c_ref[...], ex_ref[...]
        dtr, bias = dtr_ref[...], bias_ref[...]
        q = _ssd_chunk(xs, bm, cm, dtr, bias, alog_ref[...], ex)
        acsrow[...] = q["acs_r"]
        dyv = dy_ref[...]
        prev = st_ref[...]
        dst = dstate[...]
        bm_b, cm_b = bm.astype(BF16), cm.astype(BF16)
        row128 = _iota((L, LANE), 0)
        lane = _iota((L, LANE), 1)
        row_w = _iota((L, GROUP_W), 0)

        dxs = dl_ref[...] * dyv
        d_dl = _colsum(dyv * xs)
        gmat = _dot(cm_b, prev.astype(BF16))
        dg_b = (dyv * q["f_l"]).astype(BF16)
        dacs = dyv * q["f_l"] * gmat
        dcm = _dot(dg_b, prev.astype(BF16), NT)
        dprev = _dot(cm_b, dg_b, TN)
        dcd = _colsum(dst * prev)
        dlast = dcd * q["cd_l"]
        xe = q["xdt"] * q["e_l"]
        dxe = _dot(bm_b, dst.astype(BF16))
        dbm = _dot(xe.astype(BF16), dst.astype(BF16), NT)
        dxdt = dxe * q["e_l"]
        t1 = dxe * xe
        dacs = dacs - t1
        dlast = dlast + _colsum(t1)
        dstate[...] = dprev + q["cd_l"] * dst
        xdt_b = q["xdt"].astype(BF16)
        dcb = jnp.zeros((L, L), F32)
        dacs_head = []
        dxdt_diag = []
        for p in range(HEADS_PER_GROUP // 2):
            sl = slice(p * LANE, (p + 1) * LANE)
            xp = xdt_b[:, sl]
            dyp = dyv[:, sl]
            vals, dx_parts = [], []
            for half in range(2):
                in_half = (lane < P) if half == 0 else (lane >= P)
                decay = _head_decay(q, acsrow, g, 2 * p + half)
                m = decay * q["cb"]
                dyh = jnp.where(in_half, dyp, 0.0).astype(BF16)
                dm = jnp.where(q["causal"], _dot(dyh, xp, NT), 0.0)
                dcb = dcb + dm * decay
                dseg = dm * m
                rs = jnp.sum(dseg, axis=1, keepdims=True)
                cs = jnp.broadcast_to(_colsum(dseg), (L, L)).T[:, 0:1]
                vals.append(rs - cs)
                dx_parts.append(_dot(m.astype(BF16), dyp.astype(BF16), TN))
            dxdt_diag.append(jnp.where(lane < P, dx_parts[0], dx_parts[1]))
            dacs_head.append(jnp.where(lane == 0, vals[0], jnp.where(lane == P, vals[1], 0.0)))
        dxdt = dxdt + jnp.concatenate(dxdt_diag, axis=1)
        dacs = dacs + jnp.concatenate(dacs_head, axis=1)
        dacs = jnp.where(row_w == L - 1, dacs + dlast, dacs)
        dcb_b = dcb.astype(BF16)
        dcm = dcm + _dot(dcb_b, bm_b)
        dbm = dbm + _dot(dcb_b, cm_b, TN)
        dacs_h = _dot3(dacs, ex, NT)
        dda = _dot3l((row128 <= lane).astype(BF16), dacs_h)
        ddt = dda * q["a_row"] + _dot3(dxdt * xs, ex, NT)
        dxs = dxs + dxdt * q["dt_l"]
        ddtr = ddt * _sigmoid(dtr + bias)
        d_alog = _colsum(dda * q["dt"]) * q["a_row"]
        d_dh = _dot3(jnp.broadcast_to(d_dl, (8, GROUP_W)), ex, NT)[0:1, :]
        dxs_ref[...] = dxs
        db_ref[...] = dbm
        dc_ref[...] = dcm
        ddt_ref[...] = ddtr
        hv_ref[0:1, :] += d_dh
        hv_ref[1:2, :] += d_alog
        hv_ref[2:3, :] += _colsum(ddtr)

    rev = lambda c: nc - 1 - c
    xs = pl.BlockSpec((L, GROUP_W), lambda g, c: (rev(c), g))
    nb = W // SSD_STATE
    bm = pl.BlockSpec((L, SSD_STATE), lambda g, c: (rev(c), nb + g))
    cm = pl.BlockSpec((L, SSD_STATE), lambda g, c: (rev(c), nb + SSD_GROUPS + g))
    dtr = pl.BlockSpec((L, LANE), lambda g, c: (rev(c), 0))
    st = pl.BlockSpec((None, None, SSD_STATE, GROUP_W), lambda g, c: (rev(c), g, 0, 0))
    vec = pl.BlockSpec((1, LANE), lambda g, c: (0, 0))
    exs = pl.BlockSpec((None, LANE, GROUP_W), lambda g, c: (g, 0, 0))
    dl = pl.BlockSpec((1, GROUP_W), lambda g, c: (0, g))
    grp = pl.BlockSpec((L, SSD_STATE), lambda g, c: (rev(c), g))
    return _call(body, name="ssd_scan_bwd", grid=(SSD_GROUPS, nc),
                 in_specs=[xs, xs, bm, cm, dtr, st, vec, vec, exs, dl],
                 out_specs=[xs, grp, grp, grp, pl.BlockSpec((8, LANE), lambda g, c: (0, 0))],
                 out_shape=[jax.ShapeDtypeStruct((T, W), F32),
                            jax.ShapeDtypeStruct((T, SSD_GROUPS * SSD_STATE), F32),
                            jax.ShapeDtypeStruct((T, SSD_GROUPS * SSD_STATE), F32),
                            jax.ShapeDtypeStruct((T, SSD_GROUPS * LANE), F32),
                            jax.ShapeDtypeStruct((8, LANE), F32)],
                 scratch=[pltpu.VMEM((SSD_STATE, GROUP_W), F32), pltpu.VMEM((LANE, L), F32)],
                 dimension_semantics=("arbitrary", "arbitrary"))(dy, xbc, xbc, xbc, proj_dt, states, bias, alog,
                                                                 ex, d_lanes)


SB_TQ = 256
SB_SUB = 128


def _sb_logits(q, kb, scale, mask):
    z = _dot(q, kb, NT) * scale
    lb = jnp.minimum(z, 0.0) - jnp.log(1.0 + jnp.exp(-jnp.abs(z)))
    lk = lb - z
    return lb, lk if mask is None else jnp.where(mask, lk, 0.0)


def _sb_weights(lb, lk, mask, run):
    n = SB_SUB
    strict = (_iota((n, n), 0) > _iota((n, n), 1)).astype(BF16)
    ws = [None] * (lb.shape[1] // n)
    for s in reversed(range(len(ws))):
        sl = slice(s * n, (s + 1) * n)
        w = jnp.exp(lb[:, sl] + (_dot2(lk[:, sl], strict) + run))
        ws[s] = w if mask is None else jnp.where(mask[:, sl], w, 0.0)
        run = run + jnp.sum(lk[:, sl], axis=1, keepdims=True)
    return jnp.concatenate(ws, axis=1), run


def _sb_diagonal(tq):
    return _iota((tq, tq), 1) < _iota((tq, tq), 0)


SB_LOG_CUT = -110.0


def _sb_more(it, qi, run):
    return jnp.logical_and(it <= qi, jnp.max(run) > SB_LOG_CUT)


SB_PAIR = 2
SB_PW = SB_PAIR * SB_DIM


def _sb_head_slices():
    return [slice(j * SB_DIM, (j + 1) * SB_DIM) for j in range(SB_PAIR)]


def _sb_fwd(qh, kh, proj, v_cb0, side=None):
    T, W = qh.shape
    tq = min(SB_TQ, T)
    scale = SB_DIM ** -0.5
    heads = _sb_head_slices()

    def body(q_ref, k_ref, v_ref, o_ref, ob_ref):
        qi = pl.program_id(1)
        qs = [q_ref[:, hs] for hs in heads]

        def block(it, carry, mask=None):
            kstart = pl.multiple_of((qi - it) * tq, tq)
            out = []
            for j, hs in enumerate(heads):
                run, acc = carry[2 * j], carry[2 * j + 1]
                lb, lk = _sb_logits(qs[j], k_ref[pl.ds(kstart, tq), hs], scale, mask)
                w, run = _sb_weights(lb, lk, mask, run)
                acc = acc + _dot2(w, v_ref[pl.ds(kstart, tq), hs].astype(BF16))
                out += [run, acc]
            return tuple(out)

        first = block(0, (jnp.zeros((tq, 1), F32), jnp.zeros((tq, SB_DIM), F32)) * SB_PAIR, _sb_diagonal(tq))
        more = lambda c: _sb_more(c[0], qi, functools.reduce(jnp.maximum, c[1::2]))
        out = lax.while_loop(more, lambda c: (c[0] + 1, *block(c[0], c[1:])), (jnp.int32(1), *first))
        for j, hs in enumerate(heads):
            o_ref[:, hs] = out[2 + 2 * j]
            ob_ref[:, hs] = out[2 + 2 * j].astype(BF16)

    tile = pl.BlockSpec((tq, SB_PW), lambda h, i: (i, h))
    return _call(body, name="sb_attn", grid=(W // SB_PW, T // tq),
                 in_specs=[tile, pl.BlockSpec((T, SB_PW), lambda h, i: (0, h)),
                           pl.BlockSpec((T, SB_PW), lambda h, i: (0, v_cb0 // SB_PAIR + h))],
                 out_specs=[tile, tile],
                 out_shape=[jax.ShapeDtypeStruct((T, W), F32), jax.ShapeDtypeStruct((T, W), BF16)], side=side,
                 dimension_semantics=("arbitrary", "arbitrary"))(qh, kh, proj)


def _sb_bwd(qh, kh, proj, v_cb0, o, dmerged, do_cb0, side=None):
    T, W = qh.shape
    tq = min(SB_TQ, T)
    n = SB_SUB
    scale = SB_DIM ** -0.5
    heads = _sb_head_slices()

    def body(q_ref, k_ref, v_ref, o_ref, do_ref, dq_ref, dk_ref, dv_ref):
        qi = pl.program_id(1)

        @pl.when(qi == 0)
        def _():
            dk_ref[...] = jnp.zeros_like(dk_ref)
            dv_ref[...] = jnp.zeros_like(dv_ref)

        qs = [q_ref[:, hs] for hs in heads]
        dos = [do_ref[:, hs].astype(BF16) for hs in heads]
        etots = [jnp.sum(d.astype(F32) * o_ref[:, hs], axis=1, keepdims=True) for d, hs in zip(dos, heads)]
        incl = (_iota((n, n), 0) >= _iota((n, n), 1)).astype(BF16)

        def block(it, carry, mask=None):
            kstart = pl.multiple_of((qi - it) * tq, tq)
            out = []
            for j, hs in enumerate(heads):
                run, erun, dq = carry[3 * j:3 * j + 3]
                kb = k_ref[pl.ds(kstart, tq), hs]
                vb = v_ref[pl.ds(kstart, tq), hs].astype(BF16)
                lb, lk = _sb_logits(qs[j], kb, scale, mask)
                w, run = _sb_weights(lb, lk, mask, run)
                e = _dot(dos[j], vb, NT) * w
                beta = jnp.exp(lb)
                dzs = [None] * (tq // n)
                for s in reversed(range(tq // n)):
                    sl = slice(s * n, (s + 1) * n)
                    before = etots[j] - erun - _dot3(e[:, sl], incl)
                    dz = e[:, sl] * (1.0 - beta[:, sl]) - before * beta[:, sl]
                    dzs[s] = dz if mask is None else jnp.where(mask[:, sl], dz, 0.0)
                    erun = erun + jnp.sum(e[:, sl], axis=1, keepdims=True)
                dz = (jnp.concatenate(dzs, axis=1) * scale).astype(BF16)
                dq = dq + _dot(dz, kb)
                dk_ref[pl.ds(kstart, tq), hs] += _dot(dz, qs[j], TN)
                dv_ref[pl.ds(kstart, tq), hs] += _dot(w.astype(BF16), dos[j], TN)
                out += [run, erun, dq]
            return tuple(out)

        zero = jnp.zeros((tq, 1), F32)
        first = block(0, (zero, zero, jnp.zeros((tq, SB_DIM), F32)) * SB_PAIR, _sb_diagonal(tq))
        more = lambda c: _sb_more(c[0], qi, functools.reduce(jnp.maximum, c[1::3]))
        out = lax.while_loop(more, lambda c: (c[0] + 1, *block(c[0], c[1:])), (jnp.int32(1), *first))
        for j, hs in enumerate(heads):
            dq_ref[:, hs] = out[3 + 3 * j]

    tile = pl.BlockSpec((tq, SB_PW), lambda h, i: (i, h))
    full = pl.BlockSpec((T, SB_PW), lambda h, i: (0, h))
    shp = jax.ShapeDtypeStruct((T, W), F32)
    return _call(body, name="sb_attn_bwd", grid=(W // SB_PW, T // tq),
                 in_specs=[tile, full, pl.BlockSpec((T, SB_PW), lambda h, i: (0, v_cb0 // SB_PAIR + h)), tile,
                           pl.BlockSpec((tq, SB_PW), lambda h, i: (i, do_cb0 // SB_PAIR + h))],
                 out_specs=[tile, full, full], out_shape=[shp, shp, shp], side=side,
                 dimension_semantics=("arbitrary", "arbitrary"))(qh, kh, proj, o, dmerged)


MM_TK = 2048
MM_TK_TOKENS = 4096


def _mlp_fwd(x, g, w_up, w_down, layer, sides=(None, None)):
    T, D = x.shape
    fs = w_up.shape[2]
    F = N_DEV * fs
    h = _rmsnorm(x, g, name=f"mlp{layer}_norm", dtype=BF16)

    def relu_sq(acc):
        u = jnp.maximum(acc, 0.0)
        return u, u * u

    up = _matmul(h, w_up, mode="nn", name=f"mlp{layer}_up", tm=1024, tn=fs, tk=D, epilogue=relu_sq,
                 out_dtypes=(BF16, BF16), mnk=(T, F, D), side=sides[0],
                 b_spec=pl.BlockSpec((None, D, fs), lambda i, j, k: (j, 0, 0)))
    (u, s), res_up = up if sides[0] is not None else (up, None)
    y = _matmul(s, w_down, mode="nn", name=f"mlp{layer}_down", tm=1024, tn=1024, tk=2 * fs, extras=(x,),
                epilogue=lambda acc, r: (acc + r,), mnk=(T, D, F), side=sides[1],
                b_spec=pl.BlockSpec((2, fs, 1024), lambda i, j, k: (k, 0, j)))
    y, res_down = y if sides[1] is not None else (y, None)
    return y, (h, u, s), (res_up, res_down)


def _mlp_bwd(dy, dy_b, x, g, w_up, w_down, saved, layer, pending=None):
    T, D = x.shape
    fs = w_up.shape[2]
    F = N_DEV * fs
    h, u, s = saved
    dw_down = _matmul(s, dy_b, mode="tn", name=f"mlp{layer}_dwdown", tm=1024, tn=1024, tk=MM_TK_TOKENS, side=pending)
    dw_down, pending_res = dw_down if pending is not None else (dw_down, None)
    dw_down = dw_down.reshape(N_DEV, -1, D)
    da, (r1_down,) = _matmul(dy_b, w_down, mode="nt", name=f"mlp{layer}_da", tm=1024, tn=fs, tk=D, extras=(u,),
                             epilogue=lambda acc, uv: (acc * (2.0 * uv.astype(F32)),), out_dtypes=(BF16,),
                             mnk=(T, F, D), side=_cores_job([dw_down]),
                             b_spec=pl.BlockSpec((None, fs, D), lambda i, j, k: (j, 0, 0)))
    pb_down, pm_down = _pair_sum(dw_down, r1_down, f"grad_pair_sum_w_down{layer}")
    dw_up = _matmul(h, da, mode="tn", name=f"mlp{layer}_dwup", tm=1024, tn=fs, tk=MM_TK_TOKENS, out_dims=(N_DEV, D, fs),
                    out_spec=pl.BlockSpec((None, 1024, fs), lambda i, j, k: (j, i, 0)))
    dh, (r2_down, r1_up) = _matmul(da, w_up, mode="nt", name=f"mlp{layer}_dh", tm=1024, tn=1024, tk=2 * fs,
                                   mnk=(T, D, F), side=_join(_chips_job([pb_down]), _cores_job([dw_up])),
                                   b_spec=pl.BlockSpec((2, 1024, fs), lambda i, j, k: (k, j, 0)))
    up_sums = _pair_sum(dw_up, r1_up, f"grad_pair_sum_w_up{layer}")
    dx, dx_b, dg = _rmsnorm_bwd(dh, x, g, dy, name=f"mlp{layer}_norm_bwd")
    return dx, dx_b, dg, (pm_down, r2_down), up_sums, pending_res


def _local_step(x, target, p, late):
    T, D = x.shape
    W = SSD_HEADS * SSD_HEAD_DIM
    g = {}
    add = lambda acc, r: (acc + r,)

    h0 = _rmsnorm(x, p["hyb_norm"], name="hyb_norm", dtype=BF16)
    proj, (w_out,) = _matmul(h0, p["w_main_t"], mode="nt", name="hyb_proj", tm=1024, tn=1024, tk=MM_TK,
                             side=_gather_job(late[2]))
    p = dict(p, w_out=w_out.reshape(-1, D))
    proj_dt = _matmul(h0, p["w_dt_t"], mode="nt", name="hyb_proj_dt", tm=1024, tn=128, tk=MM_TK)
    ex, d_lanes, alog, bias = _ssd_consts(p["ssd_d"], p["ssd_a_log"], p["ssd_dt_bias"])
    xbc = _conv_silu_fwd(proj, 4 * W, p["conv_w"].shape[1], p["conv_w"], p["conv_b"])
    (ypre, states), (w_up0,) = _ssd_fwd(xbc, proj_dt, bias, alog, ex, d_lanes, side=_gather_job(late[0][:1]))
    y_ssd = _gate_norm_fwd(ypre, proj, p["out_norm"])
    qh, kh = _qk_norm_fwd(proj, p["q_norm"], p["k_norm"], W)
    (y_sb, y_sb_b), (w_down0, pool_blocks) = _sb_fwd(qh, kh, proj, 3 * W // SB_DIM, side=_gather_job(late[0][1:]))
    pool_w = _whole(pool_blocks, "pool_w")[0]
    x1 = _matmul(y_ssd, p["w_out"], mode="nn", name="hyb_out_a", tm=1024, tn=1024, tk=MM_TK, extras=(x,), epilogue=add,
                 mnk=(T, D, W))
    x1 = _matmul(y_sb_b, p["w_out"], mode="nn", name="hyb_out_b", tm=1024, tn=1024, tk=W, extras=(x1,), epilogue=add,
                 mnk=(T, D, W), b_spec=pl.BlockSpec((W, 1024), lambda i, j, k: (1, j)))
    x2, mlp0, ((w_up1,), (w_down1,)) = _mlp_fwd(x1, p["mlp_norm"][0:1], w_up0, w_down0, 0,
                                                sides=(_gather_job(late[1][:1]), _gather_job(late[1][1:])))
    hp = _rmsnorm(x2, p["pool_norm"], name="pool_norm", dtype=F32)
    dpool = _pool_diff_fwd(hp)
    ypool, x3 = _pool_mm_fwd(dpool, pool_w, p["pool_b"], p["pool_scale"], x2)
    x4, mlp1, _ = _mlp_fwd(x3, p["mlp_norm"][1:2], w_up1, w_down1, 1)

    reduced = {}
    dy, dy_b, sq = _loss_grad(x4, target)
    dx3, dx3_b, dgm1, reduced["w_down1"], up1_sums, _ = _mlp_bwd(dy, dy_b, x3, p["mlp_norm"][1:2], w_up1, w_down1,
                                                                 mlp1, 1)
    dpre, dpool_scale, dpool_b = _pool_scale_bwd(dx3, ypool, p["pool_scale"])
    gw = {"pool_w": _pool_mm_dw(dpool, dpre)[None], "pool_b": dpool_b, "pool_scale": dpool_scale}
    dhp = _pool_diff_bwd(_pool_mm_dx(dpre, pool_w))
    dx2, dx2_b, gw["pool_norm"] = _rmsnorm_bwd(dhp, x2, p["pool_norm"], dx3, name="pool_norm_bwd")
    dx1, dx1_b, dgm0, reduced["w_down0"], up0_sums, (r2_up1,) = _mlp_bwd(
        dx2, dx2_b, x1, p["mlp_norm"][0:1], w_up0, w_down0, mlp0, 0, pending=_chips_job([up1_sums[0]]))
    reduced["w_up1"] = (up1_sums[1], r2_up1)
    g["mlp_norm"] = jnp.concatenate([dgm0, dgm1], axis=0)
    dw_out = jnp.concatenate([
        _matmul(y_ssd, dx1_b, mode="tn", name="hyb_dwout_a", tm=1024, tn=1024, tk=MM_TK_TOKENS),
        _matmul(y_sb_b, dx1_b, mode="tn", name="hyb_dwout_b", tm=1024, tn=1024, tk=MM_TK_TOKENS)], axis=0)
    dw_out = dw_out.reshape(N_DEV, -1, D)
    dpool_small = _shard_rows([_to_blocks(gw[n], n) for n in POOL_SHARDED], PACK_COLS)
    dmerged, (r2_up0, r1_out, r1_pool) = _matmul(
        dx1_b, p["w_out"], mode="nt", name="hyb_dmerged", tm=1024, tn=1024, tk=MM_TK,
        side=_join(_chips_job([up0_sums[0]]), _cores_job([dw_out, dpool_small])))
    reduced["w_up0"] = (up0_sums[1], r2_up0)
    out_sums = _pair_sum(dw_out, r1_out, "grad_pair_sum_w_out")
    pool_sums = _pair_sum(dpool_small, r1_pool, "grad_pair_sum_pool")
    (dqh, dkh, dv), (r2_out, r2_pool) = _sb_bwd(qh, kh, proj, 3 * W // SB_DIM, y_sb, dmerged, W // SB_DIM,
                                                side=_chips_job([out_sums[0], pool_sums[0]]))
    reduced["w_out"], reduced["pool"] = (out_sums[1], r2_out), (pool_sums[1], r2_pool)
    dq, dk, g["sb_q_norm"], g["sb_k_norm"] = _qk_norm_bwd(dqh, dkh, proj, p["q_norm"], p["k_norm"], W)
    dypre, dz, g["ssd_out_norm"] = _gate_norm_bwd(dmerged, ypre, proj, p["out_norm"])
    dxs, dbm, dcm, ddt4, hv = _ssd_bwd(dypre, xbc, proj_dt, states, bias, alog, ex, d_lanes)
    g["ssd_d"], g["ssd_a_log"], g["ssd_dt_bias"] = (hv[i:i + 1, :SSD_HEADS] for i in range(3))
    ddt = ddt4.reshape(T, SSD_GROUPS, LANE).sum(axis=1).astype(BF16)
    dxbc, dconv_w, g["ssd_conv_b"] = _conv_silu_bwd(jnp.concatenate([dxs, dbm, dcm], axis=1), proj, 4 * W,
                                                    p["conv_w"].shape[1], p["conv_w"], p["conv_b"])
    dproj = jnp.concatenate([dz, dq, dk, dv.astype(BF16), dxbc], axis=1)
    gm = _matmul(dproj, h0, mode="tn", name="hyb_dwin", tm=1024, tn=1024, tk=MM_TK_TOKENS)
    g_dt = _matmul(ddt, h0, mode="tn", name="hyb_dwdt", tm=128, tn=1024, tk=MM_TK)
    g_in_t = jnp.concatenate([gm[:W], gm[4 * W:], g_dt[:SSD_HEADS], gm[W:4 * W]], axis=0)
    last = [g_in_t.reshape(N_DEV, -1, D), _shard_rows([_to_blocks(dconv_w[:4][None], "ssd_conv_w")], LANE)]
    dh0, r1_last = _matmul(ddt, p["w_dt_t"], mode="nn", name="hyb_dh_dt", tm=1024, tn=1024, tk=128,
                           side=_cores_job(last))
    sums = [_pair_sum(a, r, f"grad_pair_sum_{t}") for a, r, t in zip(last, r1_last, ("w_in", "conv"))]
    dh0, r2_last = _matmul(dproj, p["w_main_t"], mode="nn", name="hyb_dh", tm=1024, tn=1024, tk=1024, extras=(dh0,),
                           epilogue=add, side=_chips_job([s[0] for s in sums]))
    reduced.update({t: (s[1], r2) for t, s, r2 in zip(("w_in", "conv"), sums, r2_last)})
    grad_x, _, g["hyb_norm"] = _rmsnorm_bwd(dh0, x, p["hyb_norm"], dx1, name="hyb_norm_bwd")
    return sq, grad_x, g, reduced


def _position():
    return lax.axis_index("x"), lax.axis_index("y"), lax.axis_index("c")


def _run_job(job, name):
    k_in, k_out = len(job.ins), len(job.outs)

    def body(*refs):
        parts = refs[:k_in], refs[k_in:k_in + k_out], refs[k_in + k_out:]
        job.start(*parts)
        if job.mid is not None:
            job.mid(*parts)
        job.finish(*parts)

    return pl.pallas_call(body, name=name, out_shape=list(job.outs), in_specs=[ANY] * k_in, out_specs=[ANY] * k_out,
                          scratch_shapes=list(job.sems))(*job.ins)


def _join(*jobs):
    def parts(ins, outs, sems):
        i = o = s = 0
        for j in jobs:
            yield j, (ins[i:i + len(j.ins)], outs[o:o + len(j.outs)], sems[s:s + len(j.sems)])
            i, o, s = i + len(j.ins), o + len(j.outs), s + len(j.sems)

    def start(*refs):
        for j, p in parts(*refs):
            j.start(*p)

    def mid(*refs):
        for j, p in parts(*refs):
            if j.mid is not None:
                j.mid(*p)

    def finish(*refs):
        for j, p in parts(*refs):
            j.finish(*p)

    return _Job(sum((j.ins for j in jobs), ()), sum((j.outs for j in jobs), ()), sum((j.sems for j in jobs), ()),
                start, mid if any(j.mid is not None for j in jobs) else None, finish)


def _gather_job(vs):
    n = len(vs)

    def plan(v_refs, out_refs, sems):
        send_sems, recv_sems, local_sems = sems
        x, y, c = _position()
        me, sibling = (x, y, c), (x, y, 1 - c)
        chips = [(1 - x, y), (x, 1 - y), (1 - x, 1 - y)]

        def rows(a, px, py, pc):
            return out_refs[a].at[4 * px + 2 * py + pc]

        def copy(a, k, block, to, src=None):
            return pltpu.make_async_remote_copy(
                src_ref=rows(a, *block) if src is None else src, dst_ref=rows(a, *block),
                send_sem=send_sems.at[7 * a + k], recv_sem=recv_sems.at[7 * a + k], device_id=to,
                device_id_type=MESH)

        mine = [pltpu.make_async_copy(v_refs[a], rows(a, *me), local_sems.at[a]) for a in range(n)]
        first = [copy(a, 0, me, sibling, src=v_refs[a]) for a in range(n)]
        first += [copy(a, 1 + j, me, (*chip, c), src=v_refs[a]) for a in range(n) for j, chip in enumerate(chips)]
        landed = [copy(a, 1 + j, (*chip, c), me) for j, chip in enumerate(chips) for a in range(n)]
        passed = [copy(a, 4 + j, (*chip, c), sibling) for j, chip in enumerate(chips) for a in range(n)]
        from_sibling = [copy(a, 0, sibling, me) for a in range(n)]
        from_sibling += [copy(a, 4 + j, (*chip, 1 - c), me) for a in range(n) for j, chip in enumerate(chips)]
        return mine, first, landed, passed, from_sibling

    def start(*refs):
        mine, first, _, _, _ = plan(*refs)
        for cp in mine + first:
            cp.start()

    def mid(*refs):
        _, _, landed, passed, _ = plan(*refs)
        for arrived, onward in zip(landed, passed):
            arrived.wait_recv()
            onward.start()

    def finish(*refs):
        mine, first, _, passed, from_sibling = plan(*refs)
        for cp in from_sibling:
            cp.wait_recv()
        for cp in first + passed:
            cp.wait_send()
        for cp in mine:
            cp.wait()

    return _Job(tuple(vs), tuple(jax.ShapeDtypeStruct((N_DEV,) + v.shape, v.dtype) for v in vs),
                (pltpu.SemaphoreType.DMA((7 * n,)), pltpu.SemaphoreType.DMA((7 * n,)), pltpu.SemaphoreType.DMA((n,))),
                start, mid, finish)


def _cores_job(gs):
    n = len(gs)

    def plan(g_refs, r_refs, sems):
        send_sems, recv_sems = sems
        x, y, c = _position()
        return [pltpu.make_async_remote_copy(
            src_ref=g_refs[a].at[2 * k + (1 - c)], dst_ref=r_refs[a].at[k], send_sem=send_sems.at[4 * a + k],
            recv_sem=recv_sems.at[4 * a + k], device_id=(x, y, 1 - c), device_id_type=MESH)
            for a in range(n) for k in range(4)]

    def start(*refs):
        for cp in plan(*refs):
            cp.start()

    def finish(*refs):
        copies = plan(*refs)
        for cp in copies:
            cp.wait_recv()
        for cp in copies:
            cp.wait_send()

    return _Job(tuple(gs), tuple(jax.ShapeDtypeStruct((4,) + g.shape[1:], g.dtype) for g in gs),
                (pltpu.SemaphoreType.DMA((4 * n,)), pltpu.SemaphoreType.DMA((4 * n,))), start, None, finish)


TILE_BYTES = 2 * 1024 * 1024


def _col_tile(R, C):
    tc = C
    while R * tc * 4 > TILE_BYTES and tc % (2 * LANE) == 0:
        tc //= 2
    return tc


def _pair_sum(gr, r1, name):
    _, R, C = gr.shape
    tc = _col_tile(R, C)
    x, y, c = _position()
    pos = jnp.stack([c, 2 * x + y]).astype(jnp.int32)

    def body(pos_ref, g_ref, r_ref, pb_ref, pm_ref):
        s = g_ref[...] + r_ref[...]
        pb_ref[...] = s.astype(BF16)

        @pl.when(pl.program_id(1) == pos_ref[1])
        def _():
            pm_ref[...] = s

    return _call(body, name=name, grid=(C // tc, 4), prefetch=1,
                 in_specs=[pl.BlockSpec((None, R, tc), lambda j, k, pos: (2 * k + pos[0], 0, j)),
                           pl.BlockSpec((None, R, tc), lambda j, k, pos: (k, 0, j))],
                 out_specs=[pl.BlockSpec((None, R, tc), lambda j, k, pos: (k, 0, j)),
                            pl.BlockSpec((R, tc), lambda j, k, pos: (0, j))],
                 out_shape=[jax.ShapeDtypeStruct((4, R, C), BF16), jax.ShapeDtypeStruct((R, C), F32)],
                 dimension_semantics=("arbitrary", "arbitrary"))(pos, gr, r1)


def _chips_job(pbs):
    n = len(pbs)

    def plan(p_refs, r_refs, sems):
        send_sems, recv_sems = sems
        x, y, c = _position()
        chips = [(1 - x, y), (x, 1 - y), (1 - x, 1 - y)]
        mine = 2 * x + y

        def copy(a, j, src_row, dst_row):
            cx, cy = chips[j]
            return pltpu.make_async_remote_copy(
                src_ref=p_refs[a].at[src_row], dst_ref=r_refs[a].at[dst_row], send_sem=send_sems.at[3 * a + j],
                recv_sem=recv_sems.at[3 * a + j], device_id=(cx, cy, c), device_id_type=MESH)

        sends = [copy(a, j, 2 * cx + cy, mine) for a in range(n) for j, (cx, cy) in enumerate(chips)]
        arrivals = [copy(a, j, mine, 2 * cx + cy) for a in range(n) for j, (cx, cy) in enumerate(chips)]
        return sends, arrivals

    def start(*refs):
        for cp in plan(*refs)[0]:
            cp.start()

    def finish(*refs):
        sends, arrivals = plan(*refs)
        for cp in arrivals:
            cp.wait_recv()
        for cp in sends:
            cp.wait_send()

    return _Job(tuple(pbs), tuple(jax.ShapeDtypeStruct(p.shape, p.dtype) for p in pbs),
                (pltpu.SemaphoreType.DMA((3 * n,)), pltpu.SemaphoreType.DMA((3 * n,))), start, None, finish)


def _adamw(w, grad, m, v):
    m = ADAM_B1 * m + (1.0 - ADAM_B1) * grad
    v = ADAM_B2 * v + (1.0 - ADAM_B2) * (grad * grad)
    m_hat = m / (1.0 - ADAM_B1 ** ADAM_STEP)
    v_hat = v / (1.0 - ADAM_B2 ** ADAM_STEP)
    delta = -ADAM_LR * (m_hat / (jnp.sqrt(v_hat) + ADAM_EPS) + ADAM_WD * w)
    return delta, m, v


def _other_chips():
    x, y, _ = _position()
    mine = 2 * x + y
    return jnp.stack([jnp.where(mine <= j, j + 1, j) for j in range(3)]).astype(jnp.int32)


def _adamw_sharded(pm, r2, w, m, v, name):
    R, C = pm.shape
    tc = _col_tile(R, C)
    update = w is not None

    def body(oth_ref, pm_ref, a_ref, b_ref, c_ref, *refs):
        grad = ((pm_ref[...] + a_ref[...].astype(F32)) + b_ref[...].astype(F32)) + c_ref[...].astype(F32)
        if update:
            w_ref, m_ref, v_ref, g_ref, d_ref, nm_ref, nv_ref = refs
            d, nm, nv = _adamw(w_ref[...], grad, m_ref[...], v_ref[...])
            g_ref[...], d_ref[...], nm_ref[...], nv_ref[...] = grad, d, nm, nv
        else:
            refs[0][...] = grad

    tile = pl.BlockSpec((R, tc), lambda j, oth: (0, j))
    other = [pl.BlockSpec((None, R, tc), functools.partial(lambda j, oth, q: (oth[q], 0, j), q=q)) for q in range(3)]
    shp = jax.ShapeDtypeStruct((R, C), F32)
    n_out = 4 if update else 1
    res = _call(body, name=name, grid=(C // tc,), prefetch=1,
                in_specs=[tile] + other + ([tile, tile, tile] if update else []), out_specs=[tile] * n_out,
                out_shape=[shp] * n_out, dimension_semantics=("arbitrary",))(
                    _other_chips(), pm, r2, r2, r2, *((w, m, v) if update else ()))
    return res if update else res[0]


def _adamw_plain(grad, w, m, v, name):
    R, C = w.shape
    tr = 256

    def body(g_ref, w_ref, m_ref, v_ref, d_ref, nm_ref, nv_ref):
        d_ref[...], nm_ref[...], nv_ref[...] = _adamw(w_ref[...], g_ref[...], m_ref[...], v_ref[...])

    tile = pl.BlockSpec((tr, C), lambda i: (i, 0))
    shp = jax.ShapeDtypeStruct((R, C), F32)
    return _call(body, name=name, grid=(R // tr,), in_specs=[tile] * 4, out_specs=[tile] * 3, out_shape=[shp] * 3,
                 dimension_semantics=("arbitrary",))(grad, w, m, v)


def _adamw_replicated(parts, w, m, v):
    _, R, C = parts.shape

    def body(p_ref, w_ref, m_ref, v_ref, g_ref, d_ref, nm_ref, nv_ref):
        grad = p_ref[0]
        for j in range(1, N_DEV):
            grad = grad + p_ref[j]
        d, nm, nv = _adamw(w_ref[...], grad, m_ref[...], v_ref[...])
        g_ref[...], d_ref[...], nm_ref[...], nv_ref[...] = grad, d, nm, nv

    tile = pl.BlockSpec((R, C), lambda i: (0, 0))
    shp = jax.ShapeDtypeStruct((R, C), F32)
    return _call(body, name="adamw_replicated", grid=(1,),
                 in_specs=[pl.BlockSpec((N_DEV, R, C), lambda i: (0, 0, 0)), tile, tile, tile],
                 out_specs=[tile] * 4, out_shape=[shp] * 4)(parts, w, m, v)


POOL_SHARDED = ("pool_w", "pool_norm", "pool_b", "pool_scale")
REPLICATED = ("hyb_norm", "ssd_conv_b", "ssd_dt_bias", "ssd_a_log", "ssd_d", "ssd_out_norm", "sb_q_norm",
              "sb_k_norm", "mlp_norm")
PACK_COLS = 1024


def _pack(arrays, cols, row_multiple, dtype):
    flat = jnp.concatenate([a.reshape(-1).astype(dtype) for a in arrays])
    n = flat.shape[0]
    total = -(-n // (cols * row_multiple)) * cols * row_multiple
    return jnp.pad(flat, (0, total - n)).reshape(total // cols, cols)


def _shard_rows(blocks, cols):
    flat = jnp.concatenate(blocks, axis=1)
    rows = -(-flat.shape[1] // (8 * cols)) * 8
    return jnp.pad(flat, ((0, 0), (0, rows * cols - flat.shape[1]))).reshape(N_DEV, rows, cols)


def _unpack(packed, shapes):
    flat = packed.reshape(packed.shape[:-2] + (-1,))
    out, off = [], 0
    for s in shapes:
        n = math.prod(s)
        out.append(flat[..., off:off + n].reshape(flat.shape[:-1] + tuple(s)))
        off += n
    return out


def _shard_axis(name):
    return {"hyb_w_in": 2, "hyb_w_out": 1, "mlp_w_up": 2, "mlp_w_down": 1, "pool_w": 2, "ssd_conv_w": 2,
            "pool_norm": 1, "pool_b": 1, "pool_scale": 1}[name]


def _whole(blocks, name):
    ax = _shard_axis(name)
    moved = jnp.moveaxis(blocks, 0, ax)
    s = moved.shape
    return moved.reshape(s[:ax] + (s[ax] * s[ax + 1],) + s[ax + 2:])


def _to_blocks(whole, name):
    ax = _shard_axis(name)
    s = whole.shape
    split = whole.reshape(s[:ax] + (N_DEV, s[ax] // N_DEV) + s[ax + 1:])
    return jnp.moveaxis(split, ax, 0).reshape(N_DEV, -1)


def kernel(x, hyb_norm, hyb_w_in, ssd_conv_w, ssd_conv_b, ssd_dt_bias, ssd_a_log, ssd_d, ssd_out_norm, sb_q_norm, sb_k_norm, hyb_w_out, pool_norm, pool_w, pool_b, pool_scale, mlp_norm, mlp_w_up, mlp_w_down, loss_target, m_hyb_norm, m_hyb_w_in, m_ssd_conv_w, m_ssd_conv_b, m_ssd_dt_bias, m_ssd_a_log, m_ssd_d, m_ssd_out_norm, m_sb_q_norm, m_sb_k_norm, m_hyb_w_out, m_pool_norm, m_pool_w, m_pool_b, m_pool_scale, m_mlp_norm, m_mlp_w_up, m_mlp_w_down, v_hyb_norm, v_hyb_w_in, v_ssd_conv_w, v_ssd_conv_b, v_ssd_dt_bias, v_ssd_a_log, v_ssd_d, v_ssd_out_norm, v_sb_q_norm, v_sb_k_norm, v_hyb_w_out, v_pool_norm, v_pool_w, v_pool_b, v_pool_scale, v_mlp_norm, v_mlp_w_up, v_mlp_w_down):
    args = dict(locals())
    names = ("hyb_norm", "hyb_w_in", "ssd_conv_w", "ssd_conv_b", "ssd_dt_bias", "ssd_a_log", "ssd_d", "ssd_out_norm",
             "sb_q_norm", "sb_k_norm", "hyb_w_out", "pool_norm", "pool_w", "pool_b", "pool_scale", "mlp_norm",
             "mlp_w_up", "mlp_w_down")
    wt = {n: args[n] for n in names}
    T, D = x.shape[1], x.shape[2]
    W = SSD_HEADS * SSD_HEAD_DIM

    conv_dim = ssd_conv_b.shape[-1]
    c1, c2 = W + conv_dim, W + conv_dim + SSD_HEADS
    vec_names = ("ssd_conv_w", "pool_norm", "pool_b", "pool_scale")

    gathered = _run_job(_gather_job([hyb_w_in[0].T.astype(BF16), _pack([wt[n] for n in vec_names], LANE, 8, F32)]),
                        "gather_hybrid_weights")
    in_t = gathered[0].reshape(-1, D)
    vec = {n: _whole(b, n) for n, b in zip(vec_names, _unpack(gathered[1], [wt[n].shape for n in vec_names]))}
    p = {
        "w_main_t": jnp.concatenate([in_t[:W], in_t[c2:], in_t[W:c1]], axis=0),
        "w_dt_t": jnp.pad(in_t[c1:c2], ((0, LANE - SSD_HEADS), (0, 0))),
        "conv_w": vec["ssd_conv_w"][0], "conv_b": ssd_conv_b,
        "pool_norm": vec["pool_norm"], "pool_b": vec["pool_b"], "pool_scale": vec["pool_scale"],
        "hyb_norm": hyb_norm, "mlp_norm": mlp_norm, "out_norm": ssd_out_norm, "q_norm": sb_q_norm,
        "k_norm": sb_k_norm, "ssd_d": ssd_d, "ssd_a_log": ssd_a_log, "ssd_dt_bias": ssd_dt_bias,
    }

    up, down = mlp_w_up.astype(BF16), mlp_w_down.astype(BF16)
    sq, grad_x, g, reduced = _local_step(x[0], loss_target[0], p,
                                         ([up[0], down[0], pool_w.astype(BF16)], [up[1], down[1]],
                                          [hyb_w_out[0].astype(BF16)]))
    loss = lax.psum(0.5 * jnp.sum(sq) / D, ("x", "y", "c"))

    res = {}
    grad_in = _adamw_sharded(*reduced["w_in"], None, None, None, "grad_sum_w_in").T
    res["hyb_w_in"] = [a[None] for a in (grad_in, *_adamw_plain(grad_in, hyb_w_in[0], m_hyb_w_in[0], v_hyb_w_in[0],
                                                                "adamw_w_in"))]
    res["hyb_w_out"] = [a[None] for a in _adamw_sharded(*reduced["w_out"], hyb_w_out[0], m_hyb_w_out[0],
                                                        v_hyb_w_out[0], "adamw_w_out")]
    for t, n in (("w_up", "mlp_w_up"), ("w_down", "mlp_w_down")):
        layers = [_adamw_sharded(*reduced[f"{t}{l}"], args[n][l], args["m_" + n][l], args["v_" + n][l],
                                 f"adamw_{n}{l}") for l in range(2)]
        res[n] = [jnp.stack([layers[0][k], layers[1][k]]) for k in range(4)]
    for t, group, cols in (("pool", POOL_SHARDED, PACK_COLS), ("conv", ("ssd_conv_w",), LANE)):
        packed = [_pack([args[pre + n] for n in group], cols, 8, F32) for pre in ("", "m_", "v_")]
        small = [_unpack(o, [wt[n].shape for n in group]) for o in _adamw_sharded(*reduced[t], *packed, f"adamw_{t}")]
        for i, n in enumerate(group):
            res[n] = [small[k][i] for k in range(4)]

    parts = _run_job(_gather_job([_pack([g[n] for n in REPLICATED], LANE, 8, F32)]), "gather_vector_grads")[0]
    packed = [_pack([args[pre + n] for n in REPLICATED], LANE, 8, F32) for pre in ("", "m_", "v_")]
    shapes = [wt[n].shape for n in REPLICATED]
    repl = [_unpack(o, shapes) for o in _adamw_replicated(parts, *packed)]
    for i, n in enumerate(REPLICATED):
        res[n] = [repl[k][i] for k in range(4)]

    outs = [res[n][k] for k in range(4) for n in names]
    return (loss, grad_x[None], *outs)
```

```python
import functools
import math
from typing import Callable, NamedTuple, Optional

import jax
import jax.numpy as jnp
from jax import lax
from jax.experimental import pallas as pl
from jax.experimental.pallas import tpu as pltpu

F32 = jnp.float32
BF16 = jnp.bfloat16
EPS = 1e-6
V7X_VMEM_LIMIT = 56 * 1024 * 1024
MESH = pl.DeviceIdType.MESH
ANY = pl.BlockSpec(memory_space=pl.ANY)
N_DEV = 8

SSD_HEADS = 32
SSD_HEAD_DIM = 64
SSD_STATE = 128
SSD_GROUPS = 4
SSD_CHUNK = 128
GROUP_W = SSD_HEADS * SSD_HEAD_DIM // SSD_GROUPS
HEADS_PER_GROUP = SSD_HEADS // SSD_GROUPS
SB_HEADS = 16
SB_DIM = 128
POOL_WINDOWS = (2, 4, 8, 16)
LANE = 128

ADAM_LR = 0.001
ADAM_B1 = 0.9
ADAM_B2 = 0.999
ADAM_EPS = 1e-08
ADAM_WD = 0.01
ADAM_STEP = 10

NN = (((1,), (0,)), ((), ()))
NT = (((1,), (1,)), ((), ()))
TN = (((0,), (0,)), ((), ()))

class _Job(NamedTuple):
    ins: tuple
    outs: tuple
    sems: tuple
    start: Callable
    mid: Optional[Callable]
    finish: Callable


def _call(body, *, name, grid, in_specs, out_specs, out_shape, scratch=(), prefetch=0, side=None, **params):
    if side is not None:
        single = not isinstance(out_shape, (list, tuple))
        out_specs = [out_specs] if single else list(out_specs)
        out_shape = [out_shape] if single else list(out_shape)
        n_in, n_out, n_scr = len(in_specs), len(out_shape), len(scratch)
        k_in, k_out = len(side.ins), len(side.outs)
        inner = body
        steps = math.prod(grid)

        def body(*refs):
            pre, rest = refs[:prefetch], refs[prefetch:]
            ins, s_in = rest[:n_in], rest[n_in:n_in + k_in]
            rest = rest[n_in + k_in:]
            outs, s_out = rest[:n_out], rest[n_out:n_out + k_out]
            rest = rest[n_out + k_out:]
            scr, s_sem = rest[:n_scr], rest[n_scr:]
            step = 0
            for axis, size in enumerate(grid):
                step = step * size + pl.program_id(axis)

            @pl.when(step == 0)
            def _():
                side.start(s_in, s_out, s_sem)

            inner(*pre, *ins, *outs, *scr)
            if side.mid is not None:
                @pl.when(step == (3 * steps) // 4)
                def _():
                    side.mid(s_in, s_out, s_sem)

            @pl.when(step == steps - 1)
            def _():
                side.finish(s_in, s_out, s_sem)

        params = dict(params, dimension_semantics=("arbitrary",) * len(grid))
        res = _call(body, name=name, grid=grid, in_specs=list(in_specs) + [ANY] * k_in,
                    out_specs=out_specs + [ANY] * k_out, out_shape=out_shape + list(side.outs),
                    scratch=list(scratch) + list(side.sems), prefetch=prefetch, **params)
        return lambda *args: (lambda r: ((r[0] if single else r[:n_out]), r[n_out:]))(res(*args, *side.ins))
    cp = pltpu.CompilerParams(vmem_limit_bytes=V7X_VMEM_LIMIT, **params)
    if prefetch:
        gs = pltpu.PrefetchScalarGridSpec(num_scalar_prefetch=prefetch, grid=grid, in_specs=in_specs,
                                          out_specs=out_specs, scratch_shapes=list(scratch))
        return pl.pallas_call(body, name=name, grid_spec=gs, out_shape=out_shape, compiler_params=cp)
    return pl.pallas_call(body, name=name, grid=grid, in_specs=in_specs, out_specs=out_specs,
                          out_shape=out_shape, scratch_shapes=list(scratch), compiler_params=cp)


def _dot(a, b, dims=NN):
    return lax.dot_general(a, b, dims, preferred_element_type=F32)


def _split3(x):
    hi = x.astype(BF16)
    r = x - hi.astype(F32)
    mid = r.astype(BF16)
    lo = (r - mid.astype(F32)).astype(BF16)
    return hi, mid, lo


def _dot3(x, m, dims=NN):
    hi, mid, lo = _split3(x)
    return _dot(hi, m, dims) + _dot(mid, m, dims) + _dot(lo, m, dims)


def _dot3l(m, x, dims=NN):
    hi, mid, lo = _split3(x)
    return _dot(m, hi, dims) + _dot(m, mid, dims) + _dot(m, lo, dims)


def _dot2(x, m):
    hi = x.astype(BF16)
    lo = (x - hi.astype(F32)).astype(BF16)
    return _dot(hi, m) + _dot(lo, m)


def _sigmoid(x):
    return 1.0 / (1.0 + jnp.exp(-x))


def _softplus(x):
    return jnp.maximum(x, 0.0) + jnp.log(1.0 + jnp.exp(-jnp.abs(x)))


def _iota(shape, dim):
    return lax.broadcasted_iota(jnp.int32, shape, dim)


def _matmul(a, b, *, mode, name, tm, tn, tk, extras=(), epilogue=None, out_dtypes=(F32,), mnk=None, b_spec=None,
            out_spec=None, out_dims=None, side=None):
    if mnk is not None:
        M, N, K = mnk
    elif mode == "tn":
        (K, M), N = a.shape, b.shape[1]
    else:
        (M, K), N = a.shape, b.shape[1 if mode == "nn" else 0]
    tm, tn, tk = min(tm, M), min(tn, N), min(tk, K)
    assert M % tm == 0 and N % tn == 0 and K % tk == 0, (name, M, N, K, tm, tn, tk)
    if mode == "nn":
        a_spec = pl.BlockSpec((tm, tk), lambda i, j, k: (i, k))
        b_spec = b_spec or pl.BlockSpec((tk, tn), lambda i, j, k: (k, j))
        dims = NN
    elif mode == "nt":
        a_spec = pl.BlockSpec((tm, tk), lambda i, j, k: (i, k))
        b_spec = b_spec or pl.BlockSpec((tn, tk), lambda i, j, k: (j, k))
        dims = NT
    else:
        a_spec = pl.BlockSpec((tk, tm), lambda i, j, k: (k, i))
        b_spec = b_spec or pl.BlockSpec((tk, tn), lambda i, j, k: (k, j))
        dims = TN
    nk = K // tk
    ex_specs = []
    for e in extras:
        if e.shape[0] == 1:
            ex_specs.append(pl.BlockSpec((1, tn), lambda i, j, k: (0, j)))
        else:
            ex_specs.append(pl.BlockSpec((tm, tn), lambda i, j, k: (i, j)))
    n_ex, n_out = len(extras), len(out_dtypes)

    def body(*refs):
        a_ref, b_ref = refs[0], refs[1]
        ex_refs = refs[2:2 + n_ex]
        o_refs = refs[2 + n_ex:2 + n_ex + n_out]

        def finish(r):
            outs = (r,) if epilogue is None else epilogue(r, *[e[...] for e in ex_refs])
            for o_ref, o in zip(o_refs, outs):
                o_ref[...] = o.astype(o_ref.dtype)

        b = b_ref[...]
        if b.ndim == 3:
            b = b.reshape(-1, b.shape[-1]) if mode == "nn" else jnp.concatenate([b[0], b[1]], axis=1)
        part = _dot(a_ref[...].astype(BF16), b.astype(BF16), dims)
        if nk == 1:
            finish(part)
            return
        acc = refs[2 + n_ex + n_out]
        k = pl.program_id(2)

        @pl.when(k == 0)
        def _():
            acc[...] = part

        @pl.when(jnp.logical_and(k > 0, k < nk - 1))
        def _():
            acc[...] += part

        @pl.when(k == nk - 1)
        def _():
            finish(acc[...] + part)

    out_shape = [jax.ShapeDtypeStruct(out_dims or (M, N), d) for d in out_dtypes]
    out_specs = [out_spec or pl.BlockSpec((tm, tn), lambda i, j, k: (i, j)) for _ in out_dtypes]
    res = _call(body, name=name, grid=(M // tm, N // tn, nk), in_specs=[a_spec, b_spec] + ex_specs,
                out_specs=out_specs, out_shape=out_shape, scratch=[pltpu.VMEM((tm, tn), F32)] if nk > 1 else [],
                dimension_semantics=("parallel", "parallel", "arbitrary"), side=side)(a, b, *extras)
    if side is not None:
        return (res[0] if n_out > 1 else res[0][0]), res[1]
    return res if n_out > 1 else res[0]


def _rowwise(fn, *, name, T, tm, tiles, vecs, out_tiles, out_vecs):
    n_t, n_v, n_ot, n_ov = len(tiles), len(vecs), len(out_tiles), len(out_vecs)

    def body(*refs):
        ins = [r[...] for r in refs[:n_t + n_v]]
        outs = fn(*ins)
        ot_refs = refs[n_t + n_v:n_t + n_v + n_ot]
        ov_refs = refs[n_t + n_v + n_ot:]
        for r, o in zip(ot_refs, outs[:n_ot]):
            r[...] = o.astype(r.dtype)
        if n_ov:
            first = pl.program_id(0) == 0

            @pl.when(first)
            def _():
                for r, o in zip(ov_refs, outs[n_ot:]):
                    r[...] = o

            @pl.when(jnp.logical_not(first))
            def _():
                for r, o in zip(ov_refs, outs[n_ot:]):
                    r[...] += o

    in_specs = [pl.BlockSpec((tm, w), functools.partial(lambda i, cb: (i, cb), cb=cb)) for _, w, cb in tiles]
    in_specs += [pl.BlockSpec(v.shape, lambda i: (0, 0)) for v in vecs]
    out_specs = [pl.BlockSpec((tm, w), lambda i: (i, 0)) for w, _ in out_tiles]
    out_specs += [pl.BlockSpec((r, w), lambda i: (0, 0)) for r, w in out_vecs]
    out_shape = [jax.ShapeDtypeStruct((T, w), d) for w, d in out_tiles]
    out_shape += [jax.ShapeDtypeStruct((r, w), F32) for r, w in out_vecs]
    return _call(body, name=name, grid=(T // tm,), in_specs=in_specs, out_specs=out_specs, out_shape=out_shape,
                 dimension_semantics=("arbitrary",))(*[t[0] for t in tiles], *vecs)


def _colsum(x):
    return jnp.sum(x, axis=0, keepdims=True)


def _rms_fwd(x, g):
    r = lax.rsqrt(jnp.mean(x * x, axis=-1, keepdims=True) + EPS)
    return x * r * g


def _rms_bwd(dh, x, g):
    r = lax.rsqrt(jnp.mean(x * x, axis=-1, keepdims=True) + EPS)
    xh = x * r
    dxh = dh * g
    dx = r * (dxh - xh * jnp.mean(dxh * xh, axis=-1, keepdims=True))
    return dx, _colsum(dh * xh)


def _rmsnorm(x, g, *, name, dtype):
    T, D = x.shape
    return _rowwise(lambda xv, gv: (_rms_fwd(xv, gv),), name=name, T=T, tm=256, tiles=[(x, D, 0)], vecs=[g],
                    out_tiles=[(D, dtype)], out_vecs=[])[0]


def _rmsnorm_bwd(dh, x, g, dres, *, name):
    T, D = x.shape

    def fn(dhv, xv, drv, gv):
        dx, dg = _rms_bwd(dhv, xv, gv)
        return drv + dx, drv + dx, dg

    return _rowwise(fn, name=name, T=T, tm=256, tiles=[(dh, D, 0), (x, D, 0), (dres, D, 0)], vecs=[g],
                    out_tiles=[(D, F32), (D, BF16)], out_vecs=[(1, D)])


def _group_slices(width, group):
    return [slice(i, i + group) for i in range(0, width, group)]


def _gate_norm_fwd(ypre, proj, gain):
    T, W = ypre.shape

    def fn(y, z, g):
        gated = y * (z * _sigmoid(z))
        return (jnp.concatenate([_rms_fwd(gated[:, s], g[:, s]) for s in _group_slices(W, GROUP_W)], axis=1),)

    return _rowwise(fn, name="ssd_gate_norm", T=T, tm=256, tiles=[(ypre, W, 0), (proj, W, 0)], vecs=[gain],
                    out_tiles=[(W, BF16)], out_vecs=[])[0]


def _gate_norm_bwd(dmerged, ypre, proj, gain):
    T, W = ypre.shape

    def fn(do, y, z, g):
        sg = _sigmoid(z)
        sz = z * sg
        gated = y * sz
        parts = [_rms_bwd(do[:, s], gated[:, s], g[:, s]) for s in _group_slices(W, GROUP_W)]
        dgated = jnp.concatenate([p[0] for p in parts], axis=1)
        dgain = jnp.concatenate([p[1] for p in parts], axis=1)
        return dgated * sz, dgated * y * (sg * (1.0 + z * (1.0 - sg))), dgain

    return _rowwise(fn, name="ssd_gate_norm_bwd", T=T, tm=256, tiles=[(dmerged, W, 0), (ypre, W, 0), (proj, W, 0)],
                    vecs=[gain], out_tiles=[(W, F32), (W, BF16)], out_vecs=[(1, W)])


def _qk_norm_fwd(proj, qg, kg, W):
    T = proj.shape[0]

    def fn(q, k, gq, gk):
        sl = _group_slices(W, SB_DIM)
        return (jnp.concatenate([_rms_fwd(q[:, s], gq) for s in sl], axis=1),
                jnp.concatenate([_rms_fwd(k[:, s], gk) for s in sl], axis=1))

    return _rowwise(fn, name="sb_qk_norm", T=T, tm=256, tiles=[(proj, W, 1), (proj, W, 2)], vecs=[qg, kg],
                    out_tiles=[(W, BF16), (W, BF16)], out_vecs=[])


def _qk_norm_bwd(dqh, dkh, proj, qg, kg, W):
    T = proj.shape[0]

    def fn(dq, dk, q, k, gq, gk):
        sl = _group_slices(W, SB_DIM)
        pq = [_rms_bwd(dq[:, s], q[:, s], gq) for s in sl]
        pk = [_rms_bwd(dk[:, s], k[:, s], gk) for s in sl]
        return (jnp.concatenate([p[0] for p in pq], axis=1), jnp.concatenate([p[0] for p in pk], axis=1),
                sum(p[1] for p in pq), sum(p[1] for p in pk))

    return _rowwise(fn, name="sb_qk_norm_bwd", T=T, tm=256,
                    tiles=[(dqh, W, 0), (dkh, W, 0), (proj, W, 1), (proj, W, 2)], vecs=[qg, kg],
                    out_tiles=[(W, BF16), (W, BF16)], out_vecs=[(1, SB_DIM), (1, SB_DIM)])


def _loss_grad(y, target):
    T, D = y.shape

    def fn(yv, tv):
        err = yv - tv
        return err * (1.0 / D), err * (1.0 / D), _colsum(err * err)

    return _rowwise(fn, name="loss_grad", T=T, tm=256, tiles=[(y, D, 0), (target, D, 0)], vecs=[],
                    out_tiles=[(D, F32), (D, BF16)], out_vecs=[(1, D)])


def _pool_scale_bwd(dx, ypre, scale):
    T, D = dx.shape

    def fn(d, yp, s):
        dpre = d * s
        return dpre, _colsum(d * yp), _colsum(dpre)

    return _rowwise(fn, name="pool_scale_bwd", T=T, tm=256, tiles=[(dx, D, 0), (ypre, D, 0)], vecs=[scale],
                    out_tiles=[(D, BF16)], out_vecs=[(1, D), (1, D)])


ROWS = 512


def _past(cur, prev, k):
    row = _iota(cur.shape, 0)
    rc = pltpu.roll(cur, k, 0)
    if prev is None:
        return jnp.where(row >= k, rc, 0.0)
    return jnp.where(row >= k, rc, pltpu.roll(prev, k, 0))


def _future(cur, nxt, k):
    n = cur.shape[0]
    row = _iota(cur.shape, 0)
    rc = pltpu.roll(cur, n - k, 0)
    if nxt is None:
        return jnp.where(row < n - k, rc, 0.0)
    return jnp.where(row < n - k, rc, pltpu.roll(nxt, n - k, 0))


def _chunk(ref, ci):
    return ref[ci * ROWS:(ci + 1) * ROWS, :]


def _conv_pre(x_ref, w, b, ci):
    cur = _chunk(x_ref, ci)
    prev = _chunk(x_ref, ci - 1) if ci > 0 else None
    taps = [_past(cur, prev, 3), _past(cur, prev, 2), _past(cur, prev, 1), cur]
    xc = b + sum(w[j:j + 1, :] * taps[j] for j in range(4))
    return xc, taps


CONV_COLS = 256


def _conv_silu_fwd(proj, col0, width, conv_w, conv_b):
    T = proj.shape[0]

    def body(x_ref, w_ref, b_ref, o_ref):
        w, b = w_ref[...], b_ref[...]
        for ci in range(T // ROWS):
            xc, _ = _conv_pre(x_ref, w, b, ci)
            o_ref[ci * ROWS:(ci + 1) * ROWS, :] = xc * _sigmoid(xc)

    cb0 = col0 // CONV_COLS
    return _call(body, name="ssd_conv_silu", grid=(width // CONV_COLS,),
                 in_specs=[pl.BlockSpec((T, CONV_COLS), lambda j: (0, cb0 + j)),
                           pl.BlockSpec((4, CONV_COLS), lambda j: (0, j)),
                           pl.BlockSpec((1, CONV_COLS), lambda j: (0, j))],
                 out_specs=pl.BlockSpec((T, CONV_COLS), lambda j: (0, j)),
                 out_shape=jax.ShapeDtypeStruct((T, width), F32))(proj, conv_w, conv_b)


def _conv_silu_bwd(dxa, proj, col0, width, conv_w, conv_b):
    T = proj.shape[0]
    nchunk = T // ROWS

    def body(d_ref, x_ref, w_ref, b_ref, dx_ref, dw_ref, db_ref, dxc_ref):
        w, b = w_ref[...], b_ref[...]
        dw = [jnp.zeros((1, CONV_COLS), F32) for _ in range(4)]
        db = jnp.zeros((1, CONV_COLS), F32)
        for ci in range(nchunk):
            xc, taps = _conv_pre(x_ref, w, b, ci)
            sg = _sigmoid(xc)
            dxc = _chunk(d_ref, ci) * (sg * (1.0 + xc * (1.0 - sg)))
            dxc_ref[ci * ROWS:(ci + 1) * ROWS, :] = dxc
            db = db + _colsum(dxc)
            dw = [dw[j] + _colsum(dxc * taps[j]) for j in range(4)]
        dw_ref[...] = jnp.concatenate(dw + [jnp.zeros((4, CONV_COLS), F32)], axis=0)
        db_ref[...] = db
        for ci in range(nchunk):
            cur = _chunk(dxc_ref, ci)
            nxt = _chunk(dxc_ref, ci + 1) if ci + 1 < nchunk else None
            dx = (w[3:4, :] * cur + w[2:3, :] * _future(cur, nxt, 1) + w[1:2, :] * _future(cur, nxt, 2)
                  + w[0:1, :] * _future(cur, nxt, 3))
            dx_ref[ci * ROWS:(ci + 1) * ROWS, :] = dx.astype(dx_ref.dtype)

    cb0 = col0 // CONV_COLS
    return _call(body, name="ssd_conv_silu_bwd", grid=(width // CONV_COLS,),
                 in_specs=[pl.BlockSpec((T, CONV_COLS), lambda j: (0, j)),
                           pl.BlockSpec((T, CONV_COLS), lambda j: (0, cb0 + j)),
                           pl.BlockSpec((4, CONV_COLS), lambda j: (0, j)),
                           pl.BlockSpec((1, CONV_COLS), lambda j: (0, j))],
                 out_specs=[pl.BlockSpec((T, CONV_COLS), lambda j: (0, j)),
                            pl.BlockSpec((8, CONV_COLS), lambda j: (0, j)),
                            pl.BlockSpec((1, CONV_COLS), lambda j: (0, j))],
                 out_shape=[jax.ShapeDtypeStruct((T, width), BF16), jax.ShapeDtypeStruct((8, width), F32),
                            jax.ShapeDtypeStruct((1, width), F32)],
                 scratch=[pltpu.VMEM((T, CONV_COLS), F32)])(dxa, proj, conv_w, conv_b)


def _window_count(ci, win, shape):
    t = (_iota(shape, 0) + ci * ROWS + 1).astype(F32)
    return jnp.minimum(t, float(win))


def _pool_diff_fwd(h):
    T, D = h.shape
    per_group = D // len(POOL_WINDOWS) // LANE

    def body(h_ref, o_ref):
        j = pl.program_id(0)
        for gi, win in enumerate(POOL_WINDOWS):
            @pl.when(j // per_group == gi)
            def _(win=win):
                for ci in range(T // ROWS):
                    cur = _chunk(h_ref, ci)
                    prev = _chunk(h_ref, ci - 1) if ci > 0 else None
                    s = cur
                    for k in range(1, win):
                        s = s + _past(cur, prev, k)
                    d = s / _window_count(ci, win, cur.shape) - cur
                    o_ref[ci * ROWS:(ci + 1) * ROWS, :] = d.astype(o_ref.dtype)

    return _call(body, name="pool_diff", grid=(D // LANE,), in_specs=[pl.BlockSpec((T, LANE), lambda j: (0, j))],
                 out_specs=pl.BlockSpec((T, LANE), lambda j: (0, j)),
                 out_shape=jax.ShapeDtypeStruct((T, D), BF16))(h)


def _pool_diff_bwd(dd):
    T, D = dd.shape
    per_group = D // len(POOL_WINDOWS) // LANE
    nchunk = T // ROWS

    def body(d_ref, o_ref):
        j = pl.program_id(0)
        for gi, win in enumerate(POOL_WINDOWS):
            @pl.when(j // per_group == gi)
            def _(win=win):
                for ci in range(nchunk):
                    cur = _chunk(d_ref, ci)
                    q = cur / _window_count(ci, win, cur.shape)
                    qn = None
                    if ci + 1 < nchunk:
                        qn = _chunk(d_ref, ci + 1) / _window_count(ci + 1, win, cur.shape)
                    s = q - cur
                    for k in range(1, win):
                        s = s + _future(q, qn, k)
                    o_ref[ci * ROWS:(ci + 1) * ROWS, :] = s

    return _call(body, name="pool_diff_bwd", grid=(D // LANE,), in_specs=[pl.BlockSpec((T, LANE), lambda j: (0, j))],
                 out_specs=pl.BlockSpec((T, LANE), lambda j: (0, j)),
                 out_shape=jax.ShapeDtypeStruct((T, D), F32))(dd)


def _pool_mm_fwd(d, w, b, scale, x):
    T, D = d.shape
    G = w.shape[1]
    tm = 512

    def body(d_ref, w_ref, b_ref, s_ref, x_ref, yp_ref, o_ref):
        yp = _dot(d_ref[...], w_ref[...]) + b_ref[...]
        yp_ref[...] = yp
        o_ref[...] = x_ref[...] + yp * s_ref[...]

    tile = pl.BlockSpec((tm, G), lambda i, g: (i, g))
    vec = pl.BlockSpec((1, G), lambda i, g: (0, g))
    return _call(body, name="pool_mm", grid=(T // tm, D // G),
                 in_specs=[tile, pl.BlockSpec((None, G, G), lambda i, g: (g, 0, 0)), vec, vec, tile],
                 out_specs=[tile, tile],
                 out_shape=[jax.ShapeDtypeStruct((T, D), F32), jax.ShapeDtypeStruct((T, D), F32)])(d, w, b, scale, x)


def _pool_mm_dx(dpre, w):
    T, D = dpre.shape
    G = w.shape[1]
    tm = 512

    def body(d_ref, w_ref, o_ref):
        o_ref[...] = _dot(d_ref[...], w_ref[...], NT)

    tile = pl.BlockSpec((tm, G), lambda i, g: (i, g))
    return _call(body, name="pool_mm_dx", grid=(T // tm, D // G),
                 in_specs=[tile, pl.BlockSpec((None, G, G), lambda i, g: (g, 0, 0))], out_specs=tile,
                 out_shape=jax.ShapeDtypeStruct((T, D), F32))(dpre, w)


def _pool_mm_dw(d, dpre):
    T, D = d.shape
    G = D // len(POOL_WINDOWS)
    tk = 512

    def body(d_ref, p_ref, o_ref):
        @pl.when(pl.program_id(1) == 0)
        def _():
            o_ref[...] = jnp.zeros_like(o_ref)

        o_ref[...] += _dot(d_ref[...], p_ref[...], TN)

    tile = pl.BlockSpec((tk, G), lambda g, k: (k, g))
    return _call(body, name="pool_mm_dw", grid=(D // G, T // tk), in_specs=[tile, tile],
                 out_specs=pl.BlockSpec((None, G, G), lambda g, k: (g, 0, 0)),
                 out_shape=jax.ShapeDtypeStruct((D // G, G, G), F32))(d, dpre)


def _ssd_consts(ssd_d, a_log, dt_bias):
    head = jnp.arange(LANE)[:, None]
    lane = jnp.arange(GROUP_W)[None, :]
    ex = jnp.stack([(head == g * HEADS_PER_GROUP + lane // SSD_HEAD_DIM) for g in range(SSD_GROUPS)])
    d_lanes = jnp.repeat(ssd_d.reshape(-1), SSD_HEAD_DIM).reshape(1, -1)
    pad = lambda v: jnp.pad(v.reshape(1, -1), ((0, 0), (0, LANE - SSD_HEADS)))
    return ex.astype(BF16), d_lanes, pad(a_log), pad(dt_bias)


def _ssd_chunk(xs, bm, cm, dtr, bias, alog, ex):
    L = SSD_CHUNK
    row, col = _iota((L, L), 0), _iota((L, L), 1)
    causal = col <= row
    ltri = causal.astype(BF16)
    a_row = -jnp.exp(alog)
    dt = _softplus(dtr + bias)
    da = dt * a_row
    dt_l = _dot3(dt, ex)
    da_l = _dot3(da, ex)
    acs_l = _dot3l(ltri, da_l)
    acs_r = _dot3(da, (row <= col).astype(BF16), TN)
    last_l = acs_l[L - 1:L, :]
    e_l = jnp.exp(last_l - acs_l)
    f_l = jnp.exp(acs_l)
    cd_l = jnp.exp(last_l)
    xdt = xs * dt_l
    cb = _dot(cm.astype(BF16), bm.astype(BF16), NT)
    return dict(causal=causal, dt=dt, da=da, dt_l=dt_l, acs_l=acs_l, acs_r=acs_r, e_l=e_l, f_l=f_l, cd_l=cd_l,
                xdt=xdt, cb=cb, a_row=a_row, ltri=ltri)


def _head_decay(q, acsrow_ref, g, r):
    colv = q["acs_l"][:, r * SSD_HEAD_DIM:r * SSD_HEAD_DIM + 1]
    rowv = acsrow_ref[pl.ds(g * HEADS_PER_GROUP + r, 1), :]
    return jnp.exp(jnp.where(q["causal"], colv - rowv, -1e30))


def _ssd_specs(T):
    L = SSD_CHUNK
    xs = pl.BlockSpec((L, GROUP_W), lambda g, c: (c, g))
    nb = SSD_HEADS * SSD_HEAD_DIM // SSD_STATE
    bm = pl.BlockSpec((L, SSD_STATE), lambda g, c: (c, nb + g))
    cm = pl.BlockSpec((L, SSD_STATE), lambda g, c: (c, nb + SSD_GROUPS + g))
    dtr = pl.BlockSpec((L, LANE), lambda g, c: (c, 0))
    vec = pl.BlockSpec((1, LANE), lambda g, c: (0, 0))
    ex = pl.BlockSpec((None, LANE, GROUP_W), lambda g, c: (g, 0, 0))
    dl = pl.BlockSpec((1, GROUP_W), lambda g, c: (0, g))
    return xs, bm, cm, dtr, vec, ex, dl


def _ssd_fwd(xbc, proj_dt, bias, alog, ex, d_lanes, side=None):
    T = xbc.shape[0]
    L, nc, W = SSD_CHUNK, T // SSD_CHUNK, SSD_HEADS * SSD_HEAD_DIM

    def body(xs_ref, b_ref, c_ref, dtr_ref, bias_ref, alog_ref, ex_ref, dl_ref, y_ref, st_ref, state, acsrow):
        g, c = pl.program_id(0), pl.program_id(1)

        @pl.when(c == 0)
        def _():
            state[...] = jnp.zeros_like(state)

        xs, bm, cm = xs_ref[...], b_ref[...], c_ref[...]
        q = _ssd_chunk(xs, bm, cm, dtr_ref[...], bias_ref[...], alog_ref[...], ex_ref[...])
        acsrow[...] = q["acs_r"]
        prev = state[...]
        st_ref[...] = prev
        xdt_b = q["xdt"].astype(BF16)
        yoff = q["f_l"] * _dot(cm.astype(BF16), prev.astype(BF16))
        lane = _iota((L, LANE), 1)
        for p in range(HEADS_PER_GROUP // 2):
            sl = slice(p * LANE, (p + 1) * LANE)
            ma = (_head_decay(q, acsrow, g, 2 * p) * q["cb"]).astype(BF16)
            mb = (_head_decay(q, acsrow, g, 2 * p + 1) * q["cb"]).astype(BF16)
            yd = jnp.where(lane < SSD_HEAD_DIM, _dot(ma, xdt_b[:, sl]), _dot(mb, xdt_b[:, sl]))
            y_ref[:, sl] = yd + yoff[:, sl] + dl_ref[:, sl] * xs[:, sl]
        st_new = _dot(bm.astype(BF16), (q["xdt"] * q["e_l"]).astype(BF16), TN)
        state[...] = q["cd_l"] * prev + st_new

    xs, bm, cm, dtr, vec, exs, dl = _ssd_specs(T)
    return _call(body, name="ssd_scan", grid=(SSD_GROUPS, nc), in_specs=[xs, bm, cm, dtr, vec, vec, exs, dl],
                 out_specs=[pl.BlockSpec((L, GROUP_W), lambda g, c: (c, g)),
                            pl.BlockSpec((None, None, SSD_STATE, GROUP_W), lambda g, c: (c, g, 0, 0))],
                 out_shape=[jax.ShapeDtypeStruct((T, W), F32),
                            jax.ShapeDtypeStruct((nc, SSD_GROUPS, SSD_STATE, GROUP_W), F32)],
                 scratch=[pltpu.VMEM((SSD_STATE, GROUP_W), F32), pltpu.VMEM((LANE, L), F32)], side=side,
                 dimension_semantics=("arbitrary", "arbitrary"))(xbc, xbc, xbc, proj_dt, bias, alog, ex, d_lanes)


SSD_PER_STEP = 2


def _ssd_bwd(dy, xbc, proj_dt, states, bias, alog, ex, d_lanes):
    T = xbc.shape[0]
    L, nc, W = SSD_CHUNK, T // SSD_CHUNK, SSD_HEADS * SSD_HEAD_DIM
    P = SSD_HEAD_DIM

    def body(dy_ref, xs_ref, b_ref, c_ref, dtr_ref, st_ref, bias_ref, alog_ref, ex_ref, dl_ref,
             dxs_ref, db_ref, dc_ref, ddt_ref, hv_ref, dstate, acsrow):
        g2, c = pl.program_id(0), pl.program_id(1)

        @pl.when(c == 0)
        def _():
            dstate[...] = jnp.zeros_like(dstate)

        @pl.when(jnp.logical_and(g2 == 0, c == 0))
        def _():
            hv_ref[...] = jnp.zeros_like(hv_ref)

        dtr, bias = dtr_ref[...], bias_ref[...]
        row128 = _iota((L, LANE), 0)
        lane = _iota((L, LANE), 1)
        row_w = _iota((L, GROUP_W), 0)

        def one_group(gg):
            g = SSD_PER_STEP * g2 + gg
            gs = slice(gg * GROUP_W, (gg + 1) * GROUP_W)
            ns = slice(gg * SSD_STATE, (gg + 1) * SSD_STATE)
            acs = acsrow.at[gg]
            xs, bm, cm, ex = xs_ref[:, gs], b_ref[:, ns], c_ref[:, ns], ex_ref[gg]
            q = _ssd_chunk(xs, bm, cm, dtr, bias, alog_ref[...], ex)
            acs[...] = q["acs_r"]
            dyv = dy_ref[:, gs]
            prev = st_ref[gg]
            dst = dstate[gg]
            bm_b, cm_b = bm.astype(BF16), cm.astype(BF16)

            dxs = dl_ref[:, gs] * dyv
            d_dl = _colsum(dyv * xs)
            gmat = _dot(cm_b, prev.astype(BF16))
            dg_b = (dyv * q["f_l"]).astype(BF16)
            dacs = dyv * q["f_l"] * gmat
            dcm = _dot(dg_b, prev.astype(BF16), NT)
            dprev = _dot(cm_b, dg_b, TN)
            dcd = _colsum(dst * prev)
            dlast = dcd * q["cd_l"]
            xe = q["xdt"] * q["e_l"]
            dxe = _dot(bm_b, dst.astype(BF16))
            dbm = _dot(xe.astype(BF16), dst.astype(BF16), NT)
            dxdt = dxe * q["e_l"]
            t1 = dxe * xe
            dacs = dacs - t1
            dlast = dlast + _colsum(t1)
            dstate[gg] = dprev + q["cd_l"] * dst
            xdt_b = q["xdt"].astype(BF16)
            dcb = jnp.zeros((L, L), F32)
            dacs_head = []
            dxdt_diag = []
            for p in range(HEADS_PER_GROUP // 2):
                sl = slice(p * LANE, (p + 1) * LANE)
                xp = xdt_b[:, sl]
                dyp = dyv[:, sl]
                vals, dx_parts = [], []
                for half in range(2):
                    in_half = (lane < P) if half == 0 else (lane >= P)
                    decay = _head_decay(q, acs, g, 2 * p + half)
                    m = decay * q["cb"]
                    dyh = jnp.where(in_half, dyp, 0.0).astype(BF16)
                    dm = jnp.where(q["causal"], _dot(dyh, xp, NT), 0.0)
                    dcb = dcb + dm * decay
                    dseg = dm * m
                    rs = jnp.sum(dseg, axis=1, keepdims=True)
                    cs = jnp.broadcast_to(_colsum(dseg), (L, L)).T[:, 0:1]
                    vals.append(rs - cs)
                    dx_parts.append(_dot(m.astype(BF16), dyp.astype(BF16), TN))
                dxdt_diag.append(jnp.where(lane < P, dx_parts[0], dx_parts[1]))
                dacs_head.append(jnp.where(lane == 0, vals[0], jnp.where(lane == P, vals[1], 0.0)))
            dxdt = dxdt + jnp.concatenate(dxdt_diag, axis=1)
            dacs = dacs + jnp.concatenate(dacs_head, axis=1)
            dacs = jnp.where(row_w == L - 1, dacs + dlast, dacs)
            dcb_b = dcb.astype(BF16)
            dcm = dcm + _dot(dcb_b, bm_b)
            dbm = dbm + _dot(dcb_b, cm_b, TN)
            dacs_h = _dot3(dacs, ex, NT)
            dda = _dot3l((row128 <= lane).astype(BF16), dacs_h)
            ddt = dda * q["a_row"] + _dot3(dxdt * xs, ex, NT)
            dxs = dxs + dxdt * q["dt_l"]
            ddtr = ddt * _sigmoid(dtr + bias)
            d_alog = _colsum(dda * q["dt"]) * q["a_row"]
            d_dh = _dot3(jnp.broadcast_to(d_dl, (8, GROUP_W)), ex, NT)[0:1, :]
            dxs_ref[:, gs] = dxs
            db_ref[:, ns] = dbm
            dc_ref[:, ns] = dcm
            ddt_ref[:, ns] = ddtr
            hv_ref[0:1, :] += d_dh
            hv_ref[1:2, :] += d_alog
            hv_ref[2:3, :] += _colsum(ddtr)

        for gg in range(SSD_PER_STEP):
            one_group(gg)

    rev = lambda c: nc - 1 - c
    xs = pl.BlockSpec((L, SSD_PER_STEP * GROUP_W), lambda g, c: (rev(c), g))
    nb = W // SSD_STATE
    bm = pl.BlockSpec((L, SSD_PER_STEP * SSD_STATE), lambda g, c: (rev(c), nb // SSD_PER_STEP + g))
    cm = pl.BlockSpec((L, SSD_PER_STEP * SSD_STATE), lambda g, c: (rev(c), (nb + SSD_GROUPS) // SSD_PER_STEP + g))
    dtr = pl.BlockSpec((L, LANE), lambda g, c: (rev(c), 0))
    st = pl.BlockSpec((None, SSD_PER_STEP, SSD_STATE, GROUP_W), lambda g, c: (rev(c), g, 0, 0))
    vec = pl.BlockSpec((1, LANE), lambda g, c: (0, 0))
    exs = pl.BlockSpec((SSD_PER_STEP, LANE, GROUP_W), lambda g, c: (g, 0, 0))
    dl = pl.BlockSpec((1, SSD_PER_STEP * GROUP_W), lambda g, c: (0, g))
    grp = pl.BlockSpec((L, SSD_PER_STEP * SSD_STATE), lambda g, c: (rev(c), g))
    return _call(body, name="ssd_scan_bwd", grid=(SSD_GROUPS // SSD_PER_STEP, nc),
                 in_specs=[xs, xs, bm, cm, dtr, st, vec, vec, exs, dl],
                 out_specs=[xs, grp, grp, grp, pl.BlockSpec((8, LANE), lambda g, c: (0, 0))],
                 out_shape=[jax.ShapeDtypeStruct((T, W), F32),
                            jax.ShapeDtypeStruct((T, SSD_GROUPS * SSD_STATE), F32),
                            jax.ShapeDtypeStruct((T, SSD_GROUPS * SSD_STATE), F32),
                            jax.ShapeDtypeStruct((T, SSD_GROUPS * LANE), F32),
                            jax.ShapeDtypeStruct((8, LANE), F32)],
                 scratch=[pltpu.VMEM((SSD_PER_STEP, SSD_STATE, GROUP_W), F32),
                          pltpu.VMEM((SSD_PER_STEP, LANE, L), F32)],
                 dimension_semantics=("arbitrary", "arbitrary"))(dy, xbc, xbc, xbc, proj_dt, states, bias, alog,
                                                                 ex, d_lanes)


SB_TQ = 256
SB_SUB = 128


def _sb_logits(q, kb, scale, mask):
    z = _dot(q, kb, NT) * scale
    lb = jnp.minimum(z, 0.0) - jnp.log(1.0 + jnp.exp(-jnp.abs(z)))
    lk = lb - z
    return lb, lk if mask is None else jnp.where(mask, lk, 0.0)


def _sb_weights(lb, lk, mask, run):
    n = SB_SUB
    strict = (_iota((n, n), 0) > _iota((n, n), 1)).astype(BF16)
    ws = [None] * (lb.shape[1] // n)
    for s in reversed(range(len(ws))):
        sl = slice(s * n, (s + 1) * n)
        w = jnp.exp(lb[:, sl] + (_dot2(lk[:, sl], strict) + run))
        ws[s] = w if mask is None else jnp.where(mask[:, sl], w, 0.0)
        run = run + jnp.sum(lk[:, sl], axis=1, keepdims=True)
    return jnp.concatenate(ws, axis=1), run


def _sb_diagonal(tq):
    return _iota((tq, tq), 1) < _iota((tq, tq), 0)


SB_LOG_CUT = -110.0


def _sb_more(it, qi, run):
    return jnp.logical_and(it <= qi, jnp.max(run) > SB_LOG_CUT)


SB_PAIR = 2
SB_PW = SB_PAIR * SB_DIM


def _sb_head_slices():
    return [slice(j * SB_DIM, (j + 1) * SB_DIM) for j in range(SB_PAIR)]


def _sb_fwd(qh, kh, proj, v_cb0, side=None):
    T, W = qh.shape
    tq = min(SB_TQ, T)
    scale = SB_DIM ** -0.5
    heads = _sb_head_slices()

    def body(q_ref, k_ref, v_ref, o_ref, ob_ref):
        qi = pl.program_id(1)
        qs = [q_ref[:, hs] for hs in heads]

        def block(it, carry, mask=None):
            kstart = pl.multiple_of((qi - it) * tq, tq)
            out = []
            for j, hs in enumerate(heads):
                run, acc = carry[2 * j], carry[2 * j + 1]
                lb, lk = _sb_logits(qs[j], k_ref[pl.ds(kstart, tq), hs], scale, mask)
                w, run = _sb_weights(lb, lk, mask, run)
                acc = acc + _dot2(w, v_ref[pl.ds(kstart, tq), hs].astype(BF16))
                out += [run, acc]
            return tuple(out)

        first = block(0, (jnp.zeros((tq, 1), F32), jnp.zeros((tq, SB_DIM), F32)) * SB_PAIR, _sb_diagonal(tq))
        more = lambda c: _sb_more(c[0], qi, functools.reduce(jnp.maximum, c[1::2]))
        out = lax.while_loop(more, lambda c: (c[0] + 1, *block(c[0], c[1:])), (jnp.int32(1), *first))
        for j, hs in enumerate(heads):
            o_ref[:, hs] = out[2 + 2 * j]
            ob_ref[:, hs] = out[2 + 2 * j].astype(BF16)

    tile = pl.BlockSpec((tq, SB_PW), lambda h, i: (i, h))
    return _call(body, name="sb_attn", grid=(W // SB_PW, T // tq),
                 in_specs=[tile, pl.BlockSpec((T, SB_PW), lambda h, i: (0, h)),
                           pl.BlockSpec((T, SB_PW), lambda h, i: (0, v_cb0 // SB_PAIR + h))],
                 out_specs=[tile, tile],
                 out_shape=[jax.ShapeDtypeStruct((T, W), F32), jax.ShapeDtypeStruct((T, W), BF16)], side=side,
                 dimension_semantics=("arbitrary", "arbitrary"))(qh, kh, proj)


def _sb_bwd(qh, kh, proj, v_cb0, o, dmerged, do_cb0, side=None):
    T, W = qh.shape
    tq = min(SB_TQ, T)
    n = SB_SUB
    scale = SB_DIM ** -0.5
    heads = _sb_head_slices()

    def body(q_ref, k_ref, v_ref, o_ref, do_ref, dq_ref, dk_ref, dv_ref):
        qi = pl.program_id(1)

        @pl.when(qi == 0)
        def _():
            dk_ref[...] = jnp.zeros_like(dk_ref)
            dv_ref[...] = jnp.zeros_like(dv_ref)

        qs = [q_ref[:, hs] for hs in heads]
        dos = [do_ref[:, hs].astype(BF16) for hs in heads]
        etots = [jnp.sum(d.astype(F32) * o_ref[:, hs], axis=1, keepdims=True) for d, hs in zip(dos, heads)]
        incl = (_iota((n, n), 0) >= _iota((n, n), 1)).astype(BF16)

        def block(it, carry, mask=None):
            kstart = pl.multiple_of((qi - it) * tq, tq)
            out = []
            for j, hs in enumerate(heads):
                run, erun, dq = carry[3 * j:3 * j + 3]
                kb = k_ref[pl.ds(kstart, tq), hs]
                vb = v_ref[pl.ds(kstart, tq), hs].astype(BF16)
                lb, lk = _sb_logits(qs[j], kb, scale, mask)
                w, run = _sb_weights(lb, lk, mask, run)
                e = _dot(dos[j], vb, NT) * w
                beta = jnp.exp(lb)
                dzs = [None] * (tq // n)
                for s in reversed(range(tq // n)):
                    sl = slice(s * n, (s + 1) * n)
                    before = etots[j] - erun - _dot3(e[:, sl], incl)
                    dz = e[:, sl] * (1.0 - beta[:, sl]) - before * beta[:, sl]
                    dzs[s] = dz if mask is None else jnp.where(mask[:, sl], dz, 0.0)
                    erun = erun + jnp.sum(e[:, sl], axis=1, keepdims=True)
                dz = (jnp.concatenate(dzs, axis=1) * scale).astype(BF16)
                dq = dq + _dot(dz, kb)
                dk_ref[pl.ds(kstart, tq), hs] += _dot(dz, qs[j], TN)
                dv_ref[pl.ds(kstart, tq), hs] += _dot(w.astype(BF16), dos[j], TN)
                out += [run, erun, dq]
            return tuple(out)

        zero = jnp.zeros((tq, 1), F32)
        first = block(0, (zero, zero, jnp.zeros((tq, SB_DIM), F32)) * SB_PAIR, _sb_diagonal(tq))
        more = lambda c: _sb_more(c[0], qi, functools.reduce(jnp.maximum, c[1::3]))
        out = lax.while_loop(more, lambda c: (c[0] + 1, *block(c[0], c[1:])), (jnp.int32(1), *first))
        for j, hs in enumerate(heads):
            dq_ref[:, hs] = out[3 + 3 * j]

    tile = pl.BlockSpec((tq, SB_PW), lambda h, i: (i, h))
    full = pl.BlockSpec((T, SB_PW), lambda h, i: (0, h))
    shp = jax.ShapeDtypeStruct((T, W), F32)
    return _call(body, name="sb_attn_bwd", grid=(W // SB_PW, T // tq),
                 in_specs=[tile, full, pl.BlockSpec((T, SB_PW), lambda h, i: (0, v_cb0 // SB_PAIR + h)), tile,
                           pl.BlockSpec((tq, SB_PW), lambda h, i: (i, do_cb0 // SB_PAIR + h))],
                 out_specs=[tile, full, full], out_shape=[shp, shp, shp], side=side,
                 dimension_semantics=("arbitrary", "arbitrary"))(qh, kh, proj, o, dmerged)


MM_TK = 2048
MM_TK_TOKENS = 4096


def _mlp_fwd(x, g, w_up, w_down, layer, sides=(None, None)):
    T, D = x.shape
    fs = w_up.shape[2]
    F = N_DEV * fs
    h = _rmsnorm(x, g, name=f"mlp{layer}_norm", dtype=BF16)

    def relu_sq(acc):
        u = jnp.maximum(acc, 0.0)
        return u, u * u

    up = _matmul(h, w_up, mode="nn", name=f"mlp{layer}_up", tm=1024, tn=fs, tk=D, epilogue=relu_sq,
                 out_dtypes=(BF16, BF16), mnk=(T, F, D), side=sides[0],
                 b_spec=pl.BlockSpec((None, D, fs), lambda i, j, k: (j, 0, 0)))
    (u, s), res_up = up if sides[0] is not None else (up, None)
    y = _matmul(s, w_down, mode="nn", name=f"mlp{layer}_down", tm=1024, tn=1024, tk=2 * fs, extras=(x,),
                epilogue=lambda acc, r: (acc + r,), mnk=(T, D, F), side=sides[1],
                b_spec=pl.BlockSpec((2, fs, 1024), lambda i, j, k: (k, 0, j)))
    y, res_down = y if sides[1] is not None else (y, None)
    return y, (h, u, s), (res_up, res_down)


def _mlp_bwd(dy, dy_b, x, g, w_up, w_down, saved, layer, pending=None):
    T, D = x.shape
    fs = w_up.shape[2]
    F = N_DEV * fs
    h, u, s = saved
    dw_down = _matmul(s, dy_b, mode="tn", name=f"mlp{layer}_dwdown", tm=1024, tn=1024, tk=MM_TK_TOKENS, side=pending)
    dw_down, pending_res = dw_down if pending is not None else (dw_down, None)
    dw_down = dw_down.reshape(N_DEV, -1, D)
    da, (r1_down,) = _matmul(dy_b, w_down, mode="nt", name=f"mlp{layer}_da", tm=1024, tn=fs, tk=D, extras=(u,),
                             epilogue=lambda acc, uv: (acc * (2.0 * uv.astype(F32)),), out_dtypes=(BF16,),
                             mnk=(T, F, D), side=_cores_job([dw_down]),
                             b_spec=pl.BlockSpec((None, fs, D), lambda i, j, k: (j, 0, 0)))
    pb_down, pm_down = _pair_sum(dw_down, r1_down, f"grad_pair_sum_w_down{layer}")
    dw_up = _matmul(h, da, mode="tn", name=f"mlp{layer}_dwup", tm=1024, tn=fs, tk=MM_TK_TOKENS, out_dims=(N_DEV, D, fs),
                    out_spec=pl.BlockSpec((None, 1024, fs), lambda i, j, k: (j, i, 0)))
    dh, (r2_down, r1_up) = _matmul(da, w_up, mode="nt", name=f"mlp{layer}_dh", tm=1024, tn=1024, tk=2 * fs,
                                   mnk=(T, D, F), side=_join(_chips_job([pb_down]), _cores_job([dw_up])),
                                   b_spec=pl.BlockSpec((2, 1024, fs), lambda i, j, k: (k, j, 0)))
    up_sums = _pair_sum(dw_up, r1_up, f"grad_pair_sum_w_up{layer}")
    dx, dx_b, dg = _rmsnorm_bwd(dh, x, g, dy, name=f"mlp{layer}_norm_bwd")
    return dx, dx_b, dg, (pm_down, r2_down), up_sums, pending_res


def _local_step(x, target, p, late):
    T, D = x.shape
    W = SSD_HEADS * SSD_HEAD_DIM
    g = {}
    add = lambda acc, r: (acc + r,)

    h0 = _rmsnorm(x, p["hyb_norm"], name="hyb_norm", dtype=BF16)
    proj, (w_out,) = _matmul(h0, p["w_main_t"], mode="nt", name="hyb_proj", tm=1024, tn=1024, tk=MM_TK,
                             side=_gather_job(late[2]))
    p = dict(p, w_out=w_out.reshape(-1, D))
    proj_dt = _matmul(h0, p["w_dt_t"], mode="nt", name="hyb_proj_dt", tm=1024, tn=128, tk=MM_TK)
    ex, d_lanes, alog, bias = _ssd_consts(p["ssd_d"], p["ssd_a_log"], p["ssd_dt_bias"])
    xbc = _conv_silu_fwd(proj, 4 * W, p["conv_w"].shape[1], p["conv_w"], p["conv_b"])
    (ypre, states), (w_up0,) = _ssd_fwd(xbc, proj_dt, bias, alog, ex, d_lanes, side=_gather_job(late[0][:1]))
    y_ssd = _gate_norm_fwd(ypre, proj, p["out_norm"])
    qh, kh = _qk_norm_fwd(proj, p["q_norm"], p["k_norm"], W)
    (y_sb, y_sb_b), (w_down0, pool_blocks) = _sb_fwd(qh, kh, proj, 3 * W // SB_DIM, side=_gather_job(late[0][1:]))
    pool_w = _whole(pool_blocks, "pool_w")[0]
    x1 = _matmul(y_ssd, p["w_out"], mode="nn", name="hyb_out_a", tm=1024, tn=1024, tk=MM_TK, extras=(x,), epilogue=add,
                 mnk=(T, D, W))
    x1 = _matmul(y_sb_b, p["w_out"], mode="nn", name="hyb_out_b", tm=1024, tn=1024, tk=W, extras=(x1,), epilogue=add,
                 mnk=(T, D, W), b_spec=pl.BlockSpec((W, 1024), lambda i, j, k: (1, j)))
    x2, mlp0, ((w_up1,), (w_down1,)) = _mlp_fwd(x1, p["mlp_norm"][0:1], w_up0, w_down0, 0,
                                                sides=(_gather_job(late[1][:1]), _gather_job(late[1][1:])))
    hp = _rmsnorm(x2, p["pool_norm"], name="pool_norm", dtype=F32)
    dpool = _pool_diff_fwd(hp)
    ypool, x3 = _pool_mm_fwd(dpool, pool_w, p["pool_b"], p["pool_scale"], x2)
    x4, mlp1, _ = _mlp_fwd(x3, p["mlp_norm"][1:2], w_up1, w_down1, 1)

    reduced = {}
    dy, dy_b, sq = _loss_grad(x4, target)
    dx3, dx3_b, dgm1, reduced["w_down1"], up1_sums, _ = _mlp_bwd(dy, dy_b, x3, p["mlp_norm"][1:2], w_up1, w_down1,
                                                                 mlp1, 1)
    dpre, dpool_scale, dpool_b = _pool_scale_bwd(dx3, ypool, p["pool_scale"])
    gw = {"pool_w": _pool_mm_dw(dpool, dpre)[None], "pool_b": dpool_b, "pool_scale": dpool_scale}
    dhp = _pool_diff_bwd(_pool_mm_dx(dpre, pool_w))
    dx2, dx2_b, gw["pool_norm"] = _rmsnorm_bwd(dhp, x2, p["pool_norm"], dx3, name="pool_norm_bwd")
    dx1, dx1_b, dgm0, reduced["w_down0"], up0_sums, (r2_up1,) = _mlp_bwd(
        dx2, dx2_b, x1, p["mlp_norm"][0:1], w_up0, w_down0, mlp0, 0, pending=_chips_job([up1_sums[0]]))
    reduced["w_up1"] = (up1_sums[1], r2_up1)
    g["mlp_norm"] = jnp.concatenate([dgm0, dgm1], axis=0)
    dw_out = jnp.concatenate([
        _matmul(y_ssd, dx1_b, mode="tn", name="hyb_dwout_a", tm=1024, tn=1024, tk=MM_TK_TOKENS),
        _matmul(y_sb_b, dx1_b, mode="tn", name="hyb_dwout_b", tm=1024, tn=1024, tk=MM_TK_TOKENS)], axis=0)
    dw_out = dw_out.reshape(N_DEV, -1, D)
    dpool_small = _shard_rows([_to_blocks(gw[n], n) for n in POOL_SHARDED], PACK_COLS)
    dmerged, (r2_up0, r1_out, r1_pool) = _matmul(
        dx1_b, p["w_out"], mode="nt", name="hyb_dmerged", tm=1024, tn=1024, tk=MM_TK,
        side=_join(_chips_job([up0_sums[0]]), _cores_job([dw_out, dpool_small])))
    reduced["w_up0"] = (up0_sums[1], r2_up0)
    out_sums = _pair_sum(dw_out, r1_out, "grad_pair_sum_w_out")
    pool_sums = _pair_sum(dpool_small, r1_pool, "grad_pair_sum_pool")
    (dqh, dkh, dv), (r2_out, r2_pool) = _sb_bwd(qh, kh, proj, 3 * W // SB_DIM, y_sb, dmerged, W // SB_DIM,
                                                side=_chips_job([out_sums[0], pool_sums[0]]))
    reduced["w_out"], reduced["pool"] = (out_sums[1], r2_out), (pool_sums[1], r2_pool)
    dq, dk, g["sb_q_norm"], g["sb_k_norm"] = _qk_norm_bwd(dqh, dkh, proj, p["q_norm"], p["k_norm"], W)
    dypre, dz, g["ssd_out_norm"] = _gate_norm_bwd(dmerged, ypre, proj, p["out_norm"])
    dxs, dbm, dcm, ddt4, hv = _ssd_bwd(dypre, xbc, proj_dt, states, bias, alog, ex, d_lanes)
    g["ssd_d"], g["ssd_a_log"], g["ssd_dt_bias"] = (hv[i:i + 1, :SSD_HEADS] for i in range(3))
    ddt = ddt4.reshape(T, SSD_GROUPS, LANE).sum(axis=1).astype(BF16)
    dxbc, dconv_w, g["ssd_conv_b"] = _conv_silu_bwd(jnp.concatenate([dxs, dbm, dcm], axis=1), proj, 4 * W,
                                                    p["conv_w"].shape[1], p["conv_w"], p["conv_b"])
    dproj = jnp.concatenate([dz, dq, dk, dv.astype(BF16), dxbc], axis=1)
    gm = _matmul(dproj, h0, mode="tn", name="hyb_dwin", tm=1024, tn=1024, tk=MM_TK_TOKENS)
    g_dt = _matmul(ddt, h0, mode="tn", name="hyb_dwdt", tm=128, tn=1024, tk=MM_TK)
    g_in_t = jnp.concatenate([gm[:W], gm[4 * W:], g_dt[:SSD_HEADS], gm[W:4 * W]], axis=0)
    last = [g_in_t.reshape(N_DEV, -1, D), _shard_rows([_to_blocks(dconv_w[:4][None], "ssd_conv_w")], LANE)]
    dh0, r1_last = _matmul(ddt, p["w_dt_t"], mode="nn", name="hyb_dh_dt", tm=1024, tn=1024, tk=128,
                           side=_cores_job(last))
    sums = [_pair_sum(a, r, f"grad_pair_sum_{t}") for a, r, t in zip(last, r1_last, ("w_in", "conv"))]
    dh0, r2_last = _matmul(dproj, p["w_main_t"], mode="nn", name="hyb_dh", tm=1024, tn=1024, tk=1024, extras=(dh0,),
                           epilogue=add, side=_chips_job([s[0] for s in sums]))
    reduced.update({t: (s[1], r2) for t, s, r2 in zip(("w_in", "conv"), sums, r2_last)})
    grad_x, _, g["hyb_norm"] = _rmsnorm_bwd(dh0, x, p["hyb_norm"], dx1, name="hyb_norm_bwd")
    return sq, grad_x, g, reduced


def _position():
    return lax.axis_index("x"), lax.axis_index("y"), lax.axis_index("c")


def _run_job(job, name):
    k_in, k_out = len(job.ins), len(job.outs)

    def body(*refs):
        parts = refs[:k_in], refs[k_in:k_in + k_out], refs[k_in + k_out:]
        job.start(*parts)
        if job.mid is not None:
            job.mid(*parts)
        job.finish(*parts)

    return pl.pallas_call(body, name=name, out_shape=list(job.outs), in_specs=[ANY] * k_in, out_specs=[ANY] * k_out,
                          scratch_shapes=list(job.sems))(*job.ins)


def _join(*jobs):
    def parts(ins, outs, sems):
        i = o = s = 0
        for j in jobs:
            yield j, (ins[i:i + len(j.ins)], outs[o:o + len(j.outs)], sems[s:s + len(j.sems)])
            i, o, s = i + len(j.ins), o + len(j.outs), s + len(j.sems)

    def start(*refs):
        for j, p in parts(*refs):
            j.start(*p)

    def mid(*refs):
        for j, p in parts(*refs):
            if j.mid is not None:
                j.mid(*p)

    def finish(*refs):
        for j, p in parts(*refs):
            j.finish(*p)

    return _Job(sum((j.ins for j in jobs), ()), sum((j.outs for j in jobs), ()), sum((j.sems for j in jobs), ()),
                start, mid if any(j.mid is not None for j in jobs) else None, finish)


def _gather_job(vs):
    n = len(vs)

    def plan(v_refs, out_refs, sems):
        send_sems, recv_sems, local_sems = sems
        x, y, c = _position()
        me, sibling = (x, y, c), (x, y, 1 - c)
        chips = [(1 - x, y), (x, 1 - y), (1 - x, 1 - y)]

        def rows(a, px, py, pc):
            return out_refs[a].at[4 * px + 2 * py + pc]

        def copy(a, k, block, to, src=None):
            return pltpu.make_async_remote_copy(
                src_ref=rows(a, *block) if src is None else src, dst_ref=rows(a, *block),
                send_sem=send_sems.at[7 * a + k], recv_sem=recv_sems.at[7 * a + k], device_id=to,
                device_id_type=MESH)

        mine = [pltpu.make_async_copy(v_refs[a], rows(a, *me), local_sems.at[a]) for a in range(n)]
        first = [copy(a, 0, me, sibling, src=v_refs[a]) for a in range(n)]
        first += [copy(a, 1 + j, me, (*chip, c), src=v_refs[a]) for a in range(n) for j, chip in enumerate(chips)]
        landed = [copy(a, 1 + j, (*chip, c), me) for j, chip in enumerate(chips) for a in range(n)]
        passed = [copy(a, 4 + j, (*chip, c), sibling) for j, chip in enumerate(chips) for a in range(n)]
        from_sibling = [copy(a, 0, sibling, me) for a in range(n)]
        from_sibling += [copy(a, 4 + j, (*chip, 1 - c), me) for a in range(n) for j, chip in enumerate(chips)]
        return mine, first, landed, passed, from_sibling

    def start(*refs):
        mine, first, _, _, _ = plan(*refs)
        for cp in mine + first:
            cp.start()

    def mid(*refs):
        _, _, landed, passed, _ = plan(*refs)
        for arrived, onward in zip(landed, passed):
            arrived.wait_recv()
            onward.start()

    def finish(*refs):
        mine, first, _, passed, from_sibling = plan(*refs)
        for cp in from_sibling:
            cp.wait_recv()
        for cp in first + passed:
            cp.wait_send()
        for cp in mine:
            cp.wait()

    return _Job(tuple(vs), tuple(jax.ShapeDtypeStruct((N_DEV,) + v.shape, v.dtype) for v in vs),
                (pltpu.SemaphoreType.DMA((7 * n,)), pltpu.SemaphoreType.DMA((7 * n,)), pltpu.SemaphoreType.DMA((n,))),
                start, mid, finish)


def _cores_job(gs):
    n = len(gs)

    def plan(g_refs, r_refs, sems):
        send_sems, recv_sems = sems
        x, y, c = _position()
        return [pltpu.make_async_remote_copy(
            src_ref=g_refs[a].at[2 * k + (1 - c)], dst_ref=r_refs[a].at[k], send_sem=send_sems.at[4 * a + k],
            recv_sem=recv_sems.at[4 * a + k], device_id=(x, y, 1 - c), device_id_type=MESH)
            for a in range(n) for k in range(4)]

    def start(*refs):
        for cp in plan(*refs):
            cp.start()

    def finish(*refs):
        copies = plan(*refs)
        for cp in copies:
            cp.wait_recv()
        for cp in copies:
            cp.wait_send()

    return _Job(tuple(gs), tuple(jax.ShapeDtypeStruct((4,) + g.shape[1:], g.dtype) for g in gs),
                (pltpu.SemaphoreType.DMA((4 * n,)), pltpu.SemaphoreType.DMA((4 * n,))), start, None, finish)


TILE_BYTES = 2 * 1024 * 1024


def _col_tile(R, C):
    tc = C
    while R * tc * 4 > TILE_BYTES and tc % (2 * LANE) == 0:
        tc //= 2
    return tc


def _pair_sum(gr, r1, name):
    _, R, C = gr.shape
    tc = _col_tile(R, C)
    x, y, c = _position()
    pos = jnp.stack([c, 2 * x + y]).astype(jnp.int32)

    def body(pos_ref, g_ref, r_ref, pb_ref, pm_ref):
        s = g_ref[...] + r_ref[...]
        pb_ref[...] = s.astype(BF16)

        @pl.when(pl.program_id(1) == pos_ref[1])
        def _():
            pm_ref[...] = s

    return _call(body, name=name, grid=(C // tc, 4), prefetch=1,
                 in_specs=[pl.BlockSpec((None, R, tc), lambda j, k, pos: (2 * k + pos[0], 0, j)),
                           pl.BlockSpec((None, R, tc), lambda j, k, pos: (k, 0, j))],
                 out_specs=[pl.BlockSpec((None, R, tc), lambda j, k, pos: (k, 0, j)),
                            pl.BlockSpec((R, tc), lambda j, k, pos: (0, j))],
                 out_shape=[jax.ShapeDtypeStruct((4, R, C), BF16), jax.ShapeDtypeStruct((R, C), F32)],
                 dimension_semantics=("arbitrary", "arbitrary"))(pos, gr, r1)


def _chips_job(pbs):
    n = len(pbs)

    def plan(p_refs, r_refs, sems):
        send_sems, recv_sems = sems
        x, y, c = _position()
        chips = [(1 - x, y), (x, 1 - y), (1 - x, 1 - y)]
        mine = 2 * x + y

        def copy(a, j, src_row, dst_row):
            cx, cy = chips[j]
            return pltpu.make_async_remote_copy(
                src_ref=p_refs[a].at[src_row], dst_ref=r_refs[a].at[dst_row], send_sem=send_sems.at[3 * a + j],
                recv_sem=recv_sems.at[3 * a + j], device_id=(cx, cy, c), device_id_type=MESH)

        sends = [copy(a, j, 2 * cx + cy, mine) for a in range(n) for j, (cx, cy) in enumerate(chips)]
        arrivals = [copy(a, j, mine, 2 * cx + cy) for a in range(n) for j, (cx, cy) in enumerate(chips)]
        return sends, arrivals

    def start(*refs):
        for cp in plan(*refs)[0]:
            cp.start()

    def finish(*refs):
        sends, arrivals = plan(*refs)
        for cp in arrivals:
            cp.wait_recv()
        for cp in sends:
            cp.wait_send()

    return _Job(tuple(pbs), tuple(jax.ShapeDtypeStruct(p.shape, p.dtype) for p in pbs),
                (pltpu.SemaphoreType.DMA((3 * n,)), pltpu.SemaphoreType.DMA((3 * n,))), start, None, finish)


def _adamw(w, grad, m, v):
    m = ADAM_B1 * m + (1.0 - ADAM_B1) * grad
    v = ADAM_B2 * v + (1.0 - ADAM_B2) * (grad * grad)
    m_hat = m / (1.0 - ADAM_B1 ** ADAM_STEP)
    v_hat = v / (1.0 - ADAM_B2 ** ADAM_STEP)
    delta = -ADAM_LR * (m_hat / (jnp.sqrt(v_hat) + ADAM_EPS) + ADAM_WD * w)
    return delta, m, v


def _other_chips():
    x, y, _ = _position()
    mine = 2 * x + y
    return jnp.stack([jnp.where(mine <= j, j + 1, j) for j in range(3)]).astype(jnp.int32)


def _adamw_sharded(pm, r2, w, m, v, name):
    R, C = pm.shape
    tc = _col_tile(R, C)
    update = w is not None

    def body(oth_ref, pm_ref, a_ref, b_ref, c_ref, *refs):
        grad = ((pm_ref[...] + a_ref[...].astype(F32)) + b_ref[...].astype(F32)) + c_ref[...].astype(F32)
        if update:
            w_ref, m_ref, v_ref, g_ref, d_ref, nm_ref, nv_ref = refs
            d, nm, nv = _adamw(w_ref[...], grad, m_ref[...], v_ref[...])
            g_ref[...], d_ref[...], nm_ref[...], nv_ref[...] = grad, d, nm, nv
        else:
            refs[0][...] = grad

    tile = pl.BlockSpec((R, tc), lambda j, oth: (0, j))
    other = [pl.BlockSpec((None, R, tc), functools.partial(lambda j, oth, q: (oth[q], 0, j), q=q)) for q in range(3)]
    shp = jax.ShapeDtypeStruct((R, C), F32)
    n_out = 4 if update else 1
    res = _call(body, name=name, grid=(C // tc,), prefetch=1,
                in_specs=[tile] + other + ([tile, tile, tile] if update else []), out_specs=[tile] * n_out,
                out_shape=[shp] * n_out, dimension_semantics=("arbitrary",))(
                    _other_chips(), pm, r2, r2, r2, *((w, m, v) if update else ()))
    return res if update else res[0]


def _adamw_plain(grad, w, m, v, name):
    R, C = w.shape
    tr = 256

    def body(g_ref, w_ref, m_ref, v_ref, d_ref, nm_ref, nv_ref):
        d_ref[...], nm_ref[...], nv_ref[...] = _adamw(w_ref[...], g_ref[...], m_ref[...], v_ref[...])

    tile = pl.BlockSpec((tr, C), lambda i: (i, 0))
    shp = jax.ShapeDtypeStruct((R, C), F32)
    return _call(body, name=name, grid=(R // tr,), in_specs=[tile] * 4, out_specs=[tile] * 3, out_shape=[shp] * 3,
                 dimension_semantics=("arbitrary",))(grad, w, m, v)


def _adamw_replicated(parts, w, m, v):
    _, R, C = parts.shape

    def body(p_ref, w_ref, m_ref, v_ref, g_ref, d_ref, nm_ref, nv_ref):
        grad = p_ref[0]
        for j in range(1, N_DEV):
            grad = grad + p_ref[j]
        d, nm, nv = _adamw(w_ref[...], grad, m_ref[...], v_ref[...])
        g_ref[...], d_ref[...], nm_ref[...], nv_ref[...] = grad, d, nm, nv

    tile = pl.BlockSpec((R, C), lambda i: (0, 0))
    shp = jax.ShapeDtypeStruct((R, C), F32)
    return _call(body, name="adamw_replicated", grid=(1,),
                 in_specs=[pl.BlockSpec((N_DEV, R, C), lambda i: (0, 0, 0)), tile, tile, tile],
                 out_specs=[tile] * 4, out_shape=[shp] * 4)(parts, w, m, v)


POOL_SHARDED = ("pool_w", "pool_norm", "pool_b", "pool_scale")
REPLICATED = ("hyb_norm", "ssd_conv_b", "ssd_dt_bias", "ssd_a_log", "ssd_d", "ssd_out_norm", "sb_q_norm",
              "sb_k_norm", "mlp_norm")
PACK_COLS = 1024


def _pack(arrays, cols, row_multiple, dtype):
    flat = jnp.concatenate([a.reshape(-1).astype(dtype) for a in arrays])
    n = flat.shape[0]
    total = -(-n // (cols * row_multiple)) * cols * row_multiple
    return jnp.pad(flat, (0, total - n)).reshape(total // cols, cols)


def _shard_rows(blocks, cols):
    flat = jnp.concatenate(blocks, axis=1)
    rows = -(-flat.shape[1] // (8 * cols)) * 8
    return jnp.pad(flat, ((0, 0), (0, rows * cols - flat.shape[1]))).reshape(N_DEV, rows, cols)


def _unpack(packed, shapes):
    flat = packed.reshape(packed.shape[:-2] + (-1,))
    out, off = [], 0
    for s in shapes:
        n = math.prod(s)
        out.append(flat[..., off:off + n].reshape(flat.shape[:-1] + tuple(s)))
        off += n
    return out


def _shard_axis(name):
    return {"hyb_w_in": 2, "hyb_w_out": 1, "mlp_w_up": 2, "mlp_w_down": 1, "pool_w": 2, "ssd_conv_w": 2,
            "pool_norm": 1, "pool_b": 1, "pool_scale": 1}[name]


def _whole(blocks, name):
    ax = _shard_axis(name)
    moved = jnp.moveaxis(blocks, 0, ax)
    s = moved.shape
    return moved.reshape(s[:ax] + (s[ax] * s[ax + 1],) + s[ax + 2:])


def _to_blocks(whole, name):
    ax = _shard_axis(name)
    s = whole.shape
    split = whole.reshape(s[:ax] + (N_DEV, s[ax] // N_DEV) + s[ax + 1:])
    return jnp.moveaxis(split, ax, 0).reshape(N_DEV, -1)


def kernel(x, hyb_norm, hyb_w_in, ssd_conv_w, ssd_conv_b, ssd_dt_bias, ssd_a_log, ssd_d, ssd_out_norm, sb_q_norm, sb_k_norm, hyb_w_out, pool_norm, pool_w, pool_b, pool_scale, mlp_norm, mlp_w_up, mlp_w_down, loss_target, m_hyb_norm, m_hyb_w_in, m_ssd_conv_w, m_ssd_conv_b, m_ssd_dt_bias, m_ssd_a_log, m_ssd_d, m_ssd_out_norm, m_sb_q_norm, m_sb_k_norm, m_hyb_w_out, m_pool_norm, m_pool_w, m_pool_b, m_pool_scale, m_mlp_norm, m_mlp_w_up, m_mlp_w_down, v_hyb_norm, v_hyb_w_in, v_ssd_conv_w, v_ssd_conv_b, v_ssd_dt_bias, v_ssd_a_log, v_ssd_d, v_ssd_out_norm, v_sb_q_norm, v_sb_k_norm, v_hyb_w_out, v_pool_norm, v_pool_w, v_pool_b, v_pool_scale, v_mlp_norm, v_mlp_w_up, v_mlp_w_down):
    args = dict(locals())
    names = ("hyb_norm", "hyb_w_in", "ssd_conv_w", "ssd_conv_b", "ssd_dt_bias", "ssd_a_log", "ssd_d", "ssd_out_norm",
             "sb_q_norm", "sb_k_norm", "hyb_w_out", "pool_norm", "pool_w", "pool_b", "pool_scale", "mlp_norm",
             "mlp_w_up", "mlp_w_down")
    wt = {n: args[n] for n in names}
    T, D = x.shape[1], x.shape[2]
    W = SSD_HEADS * SSD_HEAD_DIM

    conv_dim = ssd_conv_b.shape[-1]
    c1, c2 = W + conv_dim, W + conv_dim + SSD_HEADS
    vec_names = ("ssd_conv_w", "pool_norm", "pool_b", "pool_scale")

    gathered = _run_job(_gather_job([hyb_w_in[0].T.astype(BF16), _pack([wt[n] for n in vec_names], LANE, 8, F32)]),
                        "gather_hybrid_weights")
    in_t = gathered[0].reshape(-1, D)
    vec = {n: _whole(b, n) for n, b in zip(vec_names, _unpack(gathered[1], [wt[n].shape for n in vec_names]))}
    p = {
        "w_main_t": jnp.concatenate([in_t[:W], in_t[c2:], in_t[W:c1]], axis=0),
        "w_dt_t": jnp.pad(in_t[c1:c2], ((0, LANE - SSD_HEADS), (0, 0))),
        "conv_w": vec["ssd_conv_w"][0], "conv_b": ssd_conv_b,
        "pool_norm": vec["pool_norm"], "pool_b": vec["pool_b"], "pool_scale": vec["pool_scale"],
        "hyb_norm": hyb_norm, "mlp_norm": mlp_norm, "out_norm": ssd_out_norm, "q_norm": sb_q_norm,
        "k_norm": sb_k_norm, "ssd_d": ssd_d, "ssd_a_log": ssd_a_log, "ssd_dt_bias": ssd_dt_bias,
    }

    up, down = mlp_w_up.astype(BF16), mlp_w_down.astype(BF16)
    sq, grad_x, g, reduced = _local_step(x[0], loss_target[0], p,
                                         ([up[0], down[0], pool_w.astype(BF16)], [up[1], down[1]],
                                          [hyb_w_out[0].astype(BF16)]))
    loss = lax.psum(0.5 * jnp.sum(sq) / D, ("x", "y", "c"))

    res = {}
    grad_in = _adamw_sharded(*reduced["w_in"], None, None, None, "grad_sum_w_in").T
    res["hyb_w_in"] = [a[None] for a in (grad_in, *_adamw_plain(grad_in, hyb_w_in[0], m_hyb_w_in[0], v_hyb_w_in[0],
                                                                "adamw_w_in"))]
    res["hyb_w_out"] = [a[None] for a in _adamw_sharded(*reduced["w_out"], hyb_w_out[0], m_hyb_w_out[0],
                                                        v_hyb_w_out[0], "adamw_w_out")]
    for t, n in (("w_up", "mlp_w_up"), ("w_down", "mlp_w_down")):
        layers = [_adamw_sharded(*reduced[f"{t}{l}"], args[n][l], args["m_" + n][l], args["v_" + n][l],
                                 f"adamw_{n}{l}") for l in range(2)]
        res[n] = [jnp.stack([layers[0][k], layers[1][k]]) for k in range(4)]
    for t, group, cols in (("pool", POOL_SHARDED, PACK_COLS), ("conv", ("ssd_conv_w",), LANE)):
        packed = [_pack([args[pre + n] for n in group], cols, 8, F32) for pre in ("", "m_", "v_")]
        small = [_unpack(o, [wt[n].shape for n in group]) for o in _adamw_sharded(*reduced[t], *packed, f"adamw_{t}")]
        for i, n in enumerate(group):
            res[n] = [small[k][i] for k in range(4)]

    parts = _run_job(_gather_job([_pack([g[n] for n in REPLICATED], LANE, 8, F32)]), "gather_vector_grads")[0]
    packed = [_pack([args[pre + n] for n in REPLICATED], LANE, 8, F32) for pre in ("", "m_", "v_")]
    shapes = [wt[n].shape for n in REPLICATED]
    repl = [_unpack(o, shapes) for o in _adamw_replicated(parts, *packed)]
    for i, n in enumerate(REPLICATED):
        res[n] = [repl[k][i] for k in range(4)]

    outs = [res[n][k] for k in range(4) for n in names]
    return (loss, grad_x[None], *outs)
```

```python
import functools
import math
from typing import Callable, NamedTuple, Optional

import jax
import jax.numpy as jnp
from jax import lax
from jax.experimental import pallas as pl
from jax.experimental.pallas import tpu as pltpu

F32 = jnp.float32
BF16 = jnp.bfloat16
EPS = 1e-6
V7X_VMEM_LIMIT = 56 * 1024 * 1024
MESH = pl.DeviceIdType.MESH
ANY = pl.BlockSpec(memory_space=pl.ANY)
N_DEV = 8

SSD_HEADS = 32
SSD_HEAD_DIM = 64
SSD_STATE = 128
SSD_GROUPS = 4
SSD_CHUNK = 128
GROUP_W = SSD_HEADS * SSD_HEAD_DIM // SSD_GROUPS
HEADS_PER_GROUP = SSD_HEADS // SSD_GROUPS
SB_HEADS = 16
SB_DIM = 128
POOL_WINDOWS = (2, 4, 8, 16)
LANE = 128

ADAM_LR = 0.001
ADAM_B1 = 0.9
ADAM_B2 = 0.999
ADAM_EPS = 1e-08
ADAM_WD = 0.01
ADAM_STEP = 10

NN = (((1,), (0,)), ((), ()))
NT = (((1,), (1,)), ((), ()))
TN = (((0,), (0,)), ((), ()))

class _Job(NamedTuple):
    ins: tuple
    outs: tuple
    sems: tuple
    start: Callable
    mid: Optional[Callable]
    finish: Callable


def _call(body, *, name, grid, in_specs, out_specs, out_shape, scratch=(), prefetch=0, side=None, **params):
    if side is not None:
        single = not isinstance(out_shape, (list, tuple))
        out_specs = [out_specs] if single else list(out_specs)
        out_shape = [out_shape] if single else list(out_shape)
        n_in, n_out, n_scr = len(in_specs), len(out_shape), len(scratch)
        k_in, k_out = len(side.ins), len(side.outs)
        inner = body
        steps = math.prod(grid)

        def body(*refs):
            pre, rest = refs[:prefetch], refs[prefetch:]
            ins, s_in = rest[:n_in], rest[n_in:n_in + k_in]
            rest = rest[n_in + k_in:]
            outs, s_out = rest[:n_out], rest[n_out:n_out + k_out]
            rest = rest[n_out + k_out:]
            scr, s_sem = rest[:n_scr], rest[n_scr:]
            step = 0
            for axis, size in enumerate(grid):
                step = step * size + pl.program_id(axis)

            @pl.when(step == 0)
            def _():
                side.start(s_in, s_out, s_sem)

            inner(*pre, *ins, *outs, *scr)
            if side.mid is not None:
                @pl.when(step == (3 * steps) // 4)
                def _():
                    side.mid(s_in, s_out, s_sem)

            @pl.when(step == steps - 1)
            def _():
                side.finish(s_in, s_out, s_sem)

        params = dict(params, dimension_semantics=("arbitrary",) * len(grid))
        res = _call(body, name=name, grid=grid, in_specs=list(in_specs) + [ANY] * k_in,
                    out_specs=out_specs + [ANY] * k_out, out_shape=out_shape + list(side.outs),
                    scratch=list(scratch) + list(side.sems), prefetch=prefetch, **params)
        return lambda *args: (lambda r: ((r[0] if single else r[:n_out]), r[n_out:]))(res(*args, *side.ins))
    cp = pltpu.CompilerParams(vmem_limit_bytes=V7X_VMEM_LIMIT, **params)
    if prefetch:
        gs = pltpu.PrefetchScalarGridSpec(num_scalar_prefetch=prefetch, grid=grid, in_specs=in_specs,
                                          out_specs=out_specs, scratch_shapes=list(scratch))
        return pl.pallas_call(body, name=name, grid_spec=gs, out_shape=out_shape, compiler_params=cp)
    return pl.pallas_call(body, name=name, grid=grid, in_specs=in_specs, out_specs=out_specs,
                          out_shape=out_shape, scratch_shapes=list(scratch), compiler_params=cp)


def _dot(a, b, dims=NN):
    return lax.dot_general(a, b, dims, preferred_element_type=F32)


def _split3(x):
    hi = x.astype(BF16)
    r = x - hi.astype(F32)
    mid = r.astype(BF16)
    lo = (r - mid.astype(F32)).astype(BF16)
    return hi, mid, lo


def _dot3(x, m, dims=NN):
    hi, mid, lo = _split3(x)
    return _dot(hi, m, dims) + _dot(mid, m, dims) + _dot(lo, m, dims)


def _dot3l(m, x, dims=NN):
    hi, mid, lo = _split3(x)
    return _dot(m, hi, dims) + _dot(m, mid, dims) + _dot(m, lo, dims)


def _dot2(x, m):
    hi = x.astype(BF16)
    lo = (x - hi.astype(F32)).astype(BF16)
    return _dot(hi, m) + _dot(lo, m)


def _sigmoid(x):
    return 1.0 / (1.0 + jnp.exp(-x))


def _softplus(x):
    return jnp.maximum(x, 0.0) + jnp.log(1.0 + jnp.exp(-jnp.abs(x)))


def _iota(shape, dim):
    return lax.broadcasted_iota(jnp.int32, shape, dim)


def _matmul(a, b, *, mode, name, tm, tn, tk, extras=(), epilogue=None, out_dtypes=(F32,), mnk=None, b_spec=None,
            out_spec=None, out_dims=None, side=None):
    if mnk is not None:
        M, N, K = mnk
    elif mode == "tn":
        (K, M), N = a.shape, b.shape[1]
    else:
        (M, K), N = a.shape, b.shape[1 if mode == "nn" else 0]
    tm, tn, tk = min(tm, M), min(tn, N), min(tk, K)
    assert M % tm == 0 and N % tn == 0 and K % tk == 0, (name, M, N, K, tm, tn, tk)
    if mode == "nn":
        a_spec = pl.BlockSpec((tm, tk), lambda i, j, k: (i, k))
        b_spec = b_spec or pl.BlockSpec((tk, tn), lambda i, j, k: (k, j))
        dims = NN
    elif mode == "nt":
        a_spec = pl.BlockSpec((tm, tk), lambda i, j, k: (i, k))
        b_spec = b_spec or pl.BlockSpec((tn, tk), lambda i, j, k: (j, k))
        dims = NT
    else:
        a_spec = pl.BlockSpec((tk, tm), lambda i, j, k: (k, i))
        b_spec = b_spec or pl.BlockSpec((tk, tn), lambda i, j, k: (k, j))
        dims = TN
    nk = K // tk
    ex_specs = []
    for e in extras:
        if e.shape[0] == 1:
            ex_specs.append(pl.BlockSpec((1, tn), lambda i, j, k: (0, j)))
        else:
            ex_specs.append(pl.BlockSpec((tm, tn), lambda i, j, k: (i, j)))
    n_ex, n_out = len(extras), len(out_dtypes)

    def body(*refs):
        a_ref, b_ref = refs[0], refs[1]
        ex_refs = refs[2:2 + n_ex]
        o_refs = refs[2 + n_ex:2 + n_ex + n_out]

        def finish(r):
            outs = (r,) if epilogue is None else epilogue(r, *[e[...] for e in ex_refs])
            for o_ref, o in zip(o_refs, outs):
                o_ref[...] = o.astype(o_ref.dtype)

        b = b_ref[...]
        if b.ndim == 3:
            b = b.reshape(-1, b.shape[-1]) if mode == "nn" else jnp.concatenate([b[0], b[1]], axis=1)
        part = _dot(a_ref[...].astype(BF16), b.astype(BF16), dims)
        if nk == 1:
            finish(part)
            return
        acc = refs[2 + n_ex + n_out]
        k = pl.program_id(2)

        @pl.when(k == 0)
        def _():
            acc[...] = part

        @pl.when(jnp.logical_and(k > 0, k < nk - 1))
        def _():
            acc[...] += part

        @pl.when(k == nk - 1)
        def _():
            finish(acc[...] + part)

    out_shape = [jax.ShapeDtypeStruct(out_dims or (M, N), d) for d in out_dtypes]
    out_specs = [out_spec or pl.BlockSpec((tm, tn), lambda i, j, k: (i, j)) for _ in out_dtypes]
    res = _call(body, name=name, grid=(M // tm, N // tn, nk), in_specs=[a_spec, b_spec] + ex_specs,
                out_specs=out_specs, out_shape=out_shape, scratch=[pltpu.VMEM((tm, tn), F32)] if nk > 1 else [],
                dimension_semantics=("parallel", "parallel", "arbitrary"), side=side)(a, b, *extras)
    if side is not None:
        return (res[0] if n_out > 1 else res[0][0]), res[1]
    return res if n_out > 1 else res[0]


def _rowwise(fn, *, name, T, tm, tiles, vecs, out_tiles, out_vecs):
    n_t, n_v, n_ot, n_ov = len(tiles), len(vecs), len(out_tiles), len(out_vecs)

    def body(*refs):
        ins = [r[...] for r in refs[:n_t + n_v]]
        outs = fn(*ins)
        ot_refs = refs[n_t + n_v:n_t + n_v + n_ot]
        ov_refs = refs[n_t + n_v + n_ot:]
        for r, o in zip(ot_refs, outs[:n_ot]):
            r[...] = o.astype(r.dtype)
        if n_ov:
            first = pl.program_id(0) == 0

            @pl.when(first)
            def _():
                for r, o in zip(ov_refs, outs[n_ot:]):
                    r[...] = o

            @pl.when(jnp.logical_not(first))
            def _():
                for r, o in zip(ov_refs, outs[n_ot:]):
                    r[...] += o

    in_specs = [pl.BlockSpec((tm, w), functools.partial(lambda i, cb: (i, cb), cb=cb)) for _, w, cb in tiles]
    in_specs += [pl.BlockSpec(v.shape, lambda i: (0, 0)) for v in vecs]
    out_specs = [pl.BlockSpec((tm, w), lambda i: (i, 0)) for w, _ in out_tiles]
    out_specs += [pl.BlockSpec((r, w), lambda i: (0, 0)) for r, w in out_vecs]
    out_shape = [jax.ShapeDtypeStruct((T, w), d) for w, d in out_tiles]
    out_shape += [jax.ShapeDtypeStruct((r, w), F32) for r, w in out_vecs]
    return _call(body, name=name, grid=(T // tm,), in_specs=in_specs, out_specs=out_specs, out_shape=out_shape,
                 dimension_semantics=("arbitrary",))(*[t[0] for t in tiles], *vecs)


def _colsum(x):
    return jnp.sum(x, axis=0, keepdims=True)


def _rms_fwd(x, g):
    r = lax.rsqrt(jnp.mean(x * x, axis=-1, keepdims=True) + EPS)
    return x * r * g


def _rms_bwd(dh, x, g):
    r = lax.rsqrt(jnp.mean(x * x, axis=-1, keepdims=True) + EPS)
    xh = x * r
    dxh = dh * g
    dx = r * (dxh - xh * jnp.mean(dxh * xh, axis=-1, keepdims=True))
    return dx, _colsum(dh * xh)


def _rmsnorm(x, g, *, name, dtype):
    T, D = x.shape
    return _rowwise(lambda xv, gv: (_rms_fwd(xv, gv),), name=name, T=T, tm=256, tiles=[(x, D, 0)], vecs=[g],
                    out_tiles=[(D, dtype)], out_vecs=[])[0]


def _rmsnorm_bwd(dh, x, g, dres, *, name):
    T, D = x.shape

    def fn(dhv, xv, drv, gv):
        dx, dg = _rms_bwd(dhv, xv, gv)
        return drv + dx, drv + dx, dg

    return _rowwise(fn, name=name, T=T, tm=256, tiles=[(dh, D, 0), (x, D, 0), (dres, D, 0)], vecs=[g],
                    out_tiles=[(D, F32), (D, BF16)], out_vecs=[(1, D)])


def _group_slices(width, group):
    return [slice(i, i + group) for i in range(0, width, group)]


def _gate_norm_fwd(ypre, proj, gain):
    T, W = ypre.shape

    def fn(y, z, g):
        gated = y * (z * _sigmoid(z))
        return (jnp.concatenate([_rms_fwd(gated[:, s], g[:, s]) for s in _group_slices(W, GROUP_W)], axis=1),)

    return _rowwise(fn, name="ssd_gate_norm", T=T, tm=256, tiles=[(ypre, W, 0), (proj, W, 0)], vecs=[gain],
                    out_tiles=[(W, BF16)], out_vecs=[])[0]


def _gate_norm_bwd(dmerged, ypre, proj, gain):
    T, W = ypre.shape

    def fn(do, y, z, g):
        sg = _sigmoid(z)
        sz = z * sg
        gated = y * sz
        parts = [_rms_bwd(do[:, s], gated[:, s], g[:, s]) for s in _group_slices(W, GROUP_W)]
        dgated = jnp.concatenate([p[0] for p in parts], axis=1)
        dgain = jnp.concatenate([p[1] for p in parts], axis=1)
        return dgated * sz, dgated * y * (sg * (1.0 + z * (1.0 - sg))), dgain

    return _rowwise(fn, name="ssd_gate_norm_bwd", T=T, tm=256, tiles=[(dmerged, W, 0), (ypre, W, 0), (proj, W, 0)],
                    vecs=[gain], out_tiles=[(W, F32), (W, BF16)], out_vecs=[(1, W)])


def _qk_norm_fwd(proj, qg, kg, W):
    T = proj.shape[0]

    def fn(q, k, gq, gk):
        sl = _group_slices(W, SB_DIM)
        return (jnp.concatenate([_rms_fwd(q[:, s], gq) for s in sl], axis=1),
                jnp.concatenate([_rms_fwd(k[:, s], gk) for s in sl], axis=1))

    return _rowwise(fn, name="sb_qk_norm", T=T, tm=256, tiles=[(proj, W, 1), (proj, W, 2)], vecs=[qg, kg],
                    out_tiles=[(W, BF16), (W, BF16)], out_vecs=[])


def _qk_norm_bwd(dqh, dkh, proj, qg, kg, W):
    T = proj.shape[0]

    def fn(dq, dk, q, k, gq, gk):
        sl = _group_slices(W, SB_DIM)
        pq = [_rms_bwd(dq[:, s], q[:, s], gq) for s in sl]
        pk = [_rms_bwd(dk[:, s], k[:, s], gk) for s in sl]
        return (jnp.concatenate([p[0] for p in pq], axis=1), jnp.concatenate([p[0] for p in pk], axis=1),
                sum(p[1] for p in pq), sum(p[1] for p in pk))

    return _rowwise(fn, name="sb_qk_norm_bwd", T=T, tm=256,
                    tiles=[(dqh, W, 0), (dkh, W, 0), (proj, W, 1), (proj, W, 2)], vecs=[qg, kg],
                    out_tiles=[(W, BF16), (W, BF16)], out_vecs=[(1, SB_DIM), (1, SB_DIM)])


def _loss_grad(y, target):
    T, D = y.shape

    def fn(yv, tv):
        err = yv - tv
        return err * (1.0 / D), err * (1.0 / D), _colsum(err * err)

    return _rowwise(fn, name="loss_grad", T=T, tm=256, tiles=[(y, D, 0), (target, D, 0)], vecs=[],
                    out_tiles=[(D, F32), (D, BF16)], out_vecs=[(1, D)])


def _pool_scale_bwd(dx, ypre, scale):
    T, D = dx.shape

    def fn(d, yp, s):
        dpre = d * s
        return dpre, _colsum(d * yp), _colsum(dpre)

    return _rowwise(fn, name="pool_scale_bwd", T=T, tm=256, tiles=[(dx, D, 0), (ypre, D, 0)], vecs=[scale],
                    out_tiles=[(D, BF16)], out_vecs=[(1, D), (1, D)])


ROWS = 512


def _past(cur, prev, k):
    row = _iota(cur.shape, 0)
    rc = pltpu.roll(cur, k, 0)
    if prev is None:
        return jnp.where(row >= k, rc, 0.0)
    return jnp.where(row >= k, rc, pltpu.roll(prev, k, 0))


def _future(cur, nxt, k):
    n = cur.shape[0]
    row = _iota(cur.shape, 0)
    rc = pltpu.roll(cur, n - k, 0)
    if nxt is None:
        return jnp.where(row < n - k, rc, 0.0)
    return jnp.where(row < n - k, rc, pltpu.roll(nxt, n - k, 0))


def _chunk(ref, ci):
    return ref[ci * ROWS:(ci + 1) * ROWS, :]


def _conv_pre(x_ref, w, b, ci):
    cur = _chunk(x_ref, ci)
    prev = _chunk(x_ref, ci - 1) if ci > 0 else None
    taps = [_past(cur, prev, 3), _past(cur, prev, 2), _past(cur, prev, 1), cur]
    xc = b + sum(w[j:j + 1, :] * taps[j] for j in range(4))
    return xc, taps


CONV_COLS = 256


def _conv_silu_fwd(proj, col0, width, conv_w, conv_b):
    T = proj.shape[0]

    def body(x_ref, w_ref, b_ref, o_ref):
        w, b = w_ref[...], b_ref[...]
        for ci in range(T // ROWS):
            xc, _ = _conv_pre(x_ref, w, b, ci)
            o_ref[ci * ROWS:(ci + 1) * ROWS, :] = xc * _sigmoid(xc)

    cb0 = col0 // CONV_COLS
    return _call(body, name="ssd_conv_silu", grid=(width // CONV_COLS,),
                 in_specs=[pl.BlockSpec((T, CONV_COLS), lambda j: (0, cb0 + j)),
                           pl.BlockSpec((4, CONV_COLS), lambda j: (0, j)),
                           pl.BlockSpec((1, CONV_COLS), lambda j: (0, j))],
                 out_specs=pl.BlockSpec((T, CONV_COLS), lambda j: (0, j)),
                 out_shape=jax.ShapeDtypeStruct((T, width), F32))(proj, conv_w, conv_b)


def _conv_silu_bwd(dxa, proj, col0, width, conv_w, conv_b):
    T = proj.shape[0]
    nchunk = T // ROWS

    def body(d_ref, x_ref, w_ref, b_ref, dx_ref, dw_ref, db_ref, dxc_ref):
        w, b = w_ref[...], b_ref[...]
        dw = [jnp.zeros((1, CONV_COLS), F32) for _ in range(4)]
        db = jnp.zeros((1, CONV_COLS), F32)
        for ci in range(nchunk):
            xc, taps = _conv_pre(x_ref, w, b, ci)
            sg = _sigmoid(xc)
            dxc = _chunk(d_ref, ci) * (sg * (1.0 + xc * (1.0 - sg)))
            dxc_ref[ci * ROWS:(ci + 1) * ROWS, :] = dxc
            db = db + _colsum(dxc)
            dw = [dw[j] + _colsum(dxc * taps[j]) for j in range(4)]
        dw_ref[...] = jnp.concatenate(dw + [jnp.zeros((4, CONV_COLS), F32)], axis=0)
        db_ref[...] = db
        for ci in range(nchunk):
            cur = _chunk(dxc_ref, ci)
            nxt = _chunk(dxc_ref, ci + 1) if ci + 1 < nchunk else None
            dx = (w[3:4, :] * cur + w[2:3, :] * _future(cur, nxt, 1) + w[1:2, :] * _future(cur, nxt, 2)
                  + w[0:1, :] * _future(cur, nxt, 3))
            dx_ref[ci * ROWS:(ci + 1) * ROWS, :] = dx.astype(dx_ref.dtype)

    cb0 = col0 // CONV_COLS
    return _call(body, name="ssd_conv_silu_bwd", grid=(width // CONV_COLS,),
                 in_specs=[pl.BlockSpec((T, CONV_COLS), lambda j: (0, j)),
                           pl.BlockSpec((T, CONV_COLS), lambda j: (0, cb0 + j)),
                           pl.BlockSpec((4, CONV_COLS), lambda j: (0, j)),
                           pl.BlockSpec((1, CONV_COLS), lambda j: (0, j))],
                 out_specs=[pl.BlockSpec((T, CONV_COLS), lambda j: (0, j)),
                            pl.BlockSpec((8, CONV_COLS), lambda j: (0, j)),
                            pl.BlockSpec((1, CONV_COLS), lambda j: (0, j))],
                 out_shape=[jax.ShapeDtypeStruct((T, width), BF16), jax.ShapeDtypeStruct((8, width), F32),
                            jax.ShapeDtypeStruct((1, width), F32)],
                 scratch=[pltpu.VMEM((T, CONV_COLS), F32)])(dxa, proj, conv_w, conv_b)


def _window_count(ci, win, shape):
    t = (_iota(shape, 0) + ci * ROWS + 1).astype(F32)
    return jnp.minimum(t, float(win))


def _pool_diff_fwd(h):
    T, D = h.shape
    per_group = D // len(POOL_WINDOWS) // LANE

    def body(h_ref, o_ref):
        j = pl.program_id(0)
        for gi, win in enumerate(POOL_WINDOWS):
            @pl.when(j // per_group == gi)
            def _(win=win):
                for ci in range(T // ROWS):
                    cur = _chunk(h_ref, ci)
                    prev = _chunk(h_ref, ci - 1) if ci > 0 else None
                    s = cur
                    for k in range(1, win):
                        s = s + _past(cur, prev, k)
                    d = s / _window_count(ci, win, cur.shape) - cur
                    o_ref[ci * ROWS:(ci + 1) * ROWS, :] = d.astype(o_ref.dtype)

    return _call(body, name="pool_diff", grid=(D // LANE,), in_specs=[pl.BlockSpec((T, LANE), lambda j: (0, j))],
                 out_specs=pl.BlockSpec((T, LANE), lambda j: (0, j)),
                 out_shape=jax.ShapeDtypeStruct((T, D), BF16))(h)


def _pool_diff_bwd(dd):
    T, D = dd.shape
    per_group = D // len(POOL_WINDOWS) // LANE
    nchunk = T // ROWS

    def body(d_ref, o_ref):
        j = pl.program_id(0)
        for gi, win in enumerate(POOL_WINDOWS):
            @pl.when(j // per_group == gi)
            def _(win=win):
                for ci in range(nchunk):
                    cur = _chunk(d_ref, ci)
                    q = cur / _window_count(ci, win, cur.shape)
                    qn = None
                    if ci + 1 < nchunk:
                        qn = _chunk(d_ref, ci + 1) / _window_count(ci + 1, win, cur.shape)
                    s = q - cur
                    for k in range(1, win):
                        s = s + _future(q, qn, k)
                    o_ref[ci * ROWS:(ci + 1) * ROWS, :] = s

    return _call(body, name="pool_diff_bwd", grid=(D // LANE,), in_specs=[pl.BlockSpec((T, LANE), lambda j: (0, j))],
                 out_specs=pl.BlockSpec((T, LANE), lambda j: (0, j)),
                 out_shape=jax.ShapeDtypeStruct((T, D), F32))(dd)


def _pool_mm_fwd(d, w, b, scale, x):
    T, D = d.shape
    G = w.shape[1]
    tm = 512

    def body(d_ref, w_ref, b_ref, s_ref, x_ref, yp_ref, o_ref):
        yp = _dot(d_ref[...], w_ref[...]) + b_ref[...]
        yp_ref[...] = yp
        o_ref[...] = x_ref[...] + yp * s_ref[...]

    tile = pl.BlockSpec((tm, G), lambda i, g: (i, g))
    vec = pl.BlockSpec((1, G), lambda i, g: (0, g))
    return _call(body, name="pool_mm", grid=(T // tm, D // G),
                 in_specs=[tile, pl.BlockSpec((None, G, G), lambda i, g: (g, 0, 0)), vec, vec, tile],
                 out_specs=[tile, tile],
                 out_shape=[jax.ShapeDtypeStruct((T, D), F32), jax.ShapeDtypeStruct((T, D), F32)])(d, w, b, scale, x)


def _pool_mm_dx(dpre, w):
    T, D = dpre.shape
    G = w.shape[1]
    tm = 512

    def body(d_ref, w_ref, o_ref):
        o_ref[...] = _dot(d_ref[...], w_ref[...], NT)

    tile = pl.BlockSpec((tm, G), lambda i, g: (i, g))
    return _call(body, name="pool_mm_dx", grid=(T // tm, D // G),
                 in_specs=[tile, pl.BlockSpec((None, G, G), lambda i, g: (g, 0, 0))], out_specs=tile,
                 out_shape=jax.ShapeDtypeStruct((T, D), F32))(dpre, w)


def _pool_mm_dw(d, dpre):
    T, D = d.shape
    G = D // len(POOL_WINDOWS)
    tk = 512

    def body(d_ref, p_ref, o_ref):
        @pl.when(pl.program_id(1) == 0)
        def _():
            o_ref[...] = jnp.zeros_like(o_ref)

        o_ref[...] += _dot(d_ref[...], p_ref[...], TN)

    tile = pl.BlockSpec((tk, G), lambda g, k: (k, g))
    return _call(body, name="pool_mm_dw", grid=(D // G, T // tk), in_specs=[tile, tile],
                 out_specs=pl.BlockSpec((None, G, G), lambda g, k: (g, 0, 0)),
                 out_shape=jax.ShapeDtypeStruct((D // G, G, G), F32))(d, dpre)


def _ssd_consts(ssd_d, a_log, dt_bias):
    head = jnp.arange(LANE)[:, None]
    lane = jnp.arange(GROUP_W)[None, :]
    ex = jnp.stack([(head == g * HEADS_PER_GROUP + lane // SSD_HEAD_DIM) for g in range(SSD_GROUPS)])
    d_lanes = jnp.repeat(ssd_d.reshape(-1), SSD_HEAD_DIM).reshape(1, -1)
    pad = lambda v: jnp.pad(v.reshape(1, -1), ((0, 0), (0, LANE - SSD_HEADS)))
    return ex.astype(BF16), d_lanes, pad(a_log), pad(dt_bias)


def _ssd_chunk(xs, bm, cm, dtr, bias, alog, ex):
    L = SSD_CHUNK
    row, col = _iota((L, L), 0), _iota((L, L), 1)
    causal = col <= row
    ltri = causal.astype(BF16)
    a_row = -jnp.exp(alog)
    dt = _softplus(dtr + bias)
    da = dt * a_row
    dt_l = _dot3(dt, ex)
    da_l = _dot3(da, ex)
    acs_l = _dot3l(ltri, da_l)
    acs_r = _dot3(da, (row <= col).astype(BF16), TN)
    last_l = acs_l[L - 1:L, :]
    e_l = jnp.exp(last_l - acs_l)
    f_l = jnp.exp(acs_l)
    cd_l = jnp.exp(last_l)
    xdt = xs * dt_l
    cb = _dot(cm.astype(BF16), bm.astype(BF16), NT)
    return dict(causal=causal, dt=dt, da=da, dt_l=dt_l, acs_l=acs_l, acs_r=acs_r, e_l=e_l, f_l=f_l, cd_l=cd_l,
                xdt=xdt, cb=cb, a_row=a_row, ltri=ltri)


def _head_decay(q, acsrow_ref, g, r):
    colv = q["acs_l"][:, r * SSD_HEAD_DIM:r * SSD_HEAD_DIM + 1]
    rowv = acsrow_ref[pl.ds(g * HEADS_PER_GROUP + r, 1), :]
    return jnp.exp(jnp.where(q["causal"], colv - rowv, -1e30))


def _ssd_specs(T):
    L = SSD_CHUNK
    xs = pl.BlockSpec((L, GROUP_W), lambda g, c: (c, g))
    nb = SSD_HEADS * SSD_HEAD_DIM // SSD_STATE
    bm = pl.BlockSpec((L, SSD_STATE), lambda g, c: (c, nb + g))
    cm = pl.BlockSpec((L, SSD_STATE), lambda g, c: (c, nb + SSD_GROUPS + g))
    dtr = pl.BlockSpec((L, LANE), lambda g, c: (c, 0))
    vec = pl.BlockSpec((1, LANE), lambda g, c: (0, 0))
    ex = pl.BlockSpec((None, LANE, GROUP_W), lambda g, c: (g, 0, 0))
    dl = pl.BlockSpec((1, GROUP_W), lambda g, c: (0, g))
    return xs, bm, cm, dtr, vec, ex, dl


def _ssd_fwd(xbc, proj_dt, bias, alog, ex, d_lanes, side=None):
    T = xbc.shape[0]
    L, nc, W = SSD_CHUNK, T // SSD_CHUNK, SSD_HEADS * SSD_HEAD_DIM

    def body(xs_ref, b_ref, c_ref, dtr_ref, bias_ref, alog_ref, ex_ref, dl_ref, y_ref, st_ref, state, acsrow):
        g, c = pl.program_id(0), pl.program_id(1)

        @pl.when(c == 0)
        def _():
            state[...] = jnp.zeros_like(state)

        xs, bm, cm = xs_ref[...], b_ref[...], c_ref[...]
        q = _ssd_chunk(xs, bm, cm, dtr_ref[...], bias_ref[...], alog_ref[...], ex_ref[...])
        acsrow[...] = q["acs_r"]
        prev = state[...]
        st_ref[...] = prev
        xdt_b = q["xdt"].astype(BF16)
        yoff = q["f_l"] * _dot(cm.astype(BF16), prev.astype(BF16))
        lane = _iota((L, LANE), 1)
        for p in range(HEADS_PER_GROUP // 2):
            sl = slice(p * LANE, (p + 1) * LANE)
            ma = (_head_decay(q, acsrow, g, 2 * p) * q["cb"]).astype(BF16)
            mb = (_head_decay(q, acsrow, g, 2 * p + 1) * q["cb"]).astype(BF16)
            yd = jnp.where(lane < SSD_HEAD_DIM, _dot(ma, xdt_b[:, sl]), _dot(mb, xdt_b[:, sl]))
            y_ref[:, sl] = yd + yoff[:, sl] + dl_ref[:, sl] * xs[:, sl]
        st_new = _dot(bm.astype(BF16), (q["xdt"] * q["e_l"]).astype(BF16), TN)
        state[...] = q["cd_l"] * prev + st_new

    xs, bm, cm, dtr, vec, exs, dl = _ssd_specs(T)
    return _call(body, name="ssd_scan", grid=(SSD_GROUPS, nc), in_specs=[xs, bm, cm, dtr, vec, vec, exs, dl],
                 out_specs=[pl.BlockSpec((L, GROUP_W), lambda g, c: (c, g)),
                            pl.BlockSpec((None, None, SSD_STATE, GROUP_W), lambda g, c: (c, g, 0, 0))],
                 out_shape=[jax.ShapeDtypeStruct((T, W), F32),
                            jax.ShapeDtypeStruct((nc, SSD_GROUPS, SSD_STATE, GROUP_W), F32)],
                 scratch=[pltpu.VMEM((SSD_STATE, GROUP_W), F32), pltpu.VMEM((LANE, L), F32)], side=side,
                 dimension_semantics=("arbitrary", "arbitrary"))(xbc, xbc, xbc, proj_dt, bias, alog, ex, d_lanes)


SSD_PER_STEP = 2


def _ssd_bwd(dy, xbc, proj_dt, states, bias, alog, ex, d_lanes):
    T = xbc.shape[0]
    L, nc, W = SSD_CHUNK, T // SSD_CHUNK, SSD_HEADS * SSD_HEAD_DIM
    P = SSD_HEAD_DIM

    def body(dy_ref, xs_ref, b_ref, c_ref, dtr_ref, st_ref, bias_ref, alog_ref, ex_ref, dl_ref,
             dxs_ref, db_ref, dc_ref, ddt_ref, hv_ref, dstate, acsrow):
        g2, c = pl.program_id(0), pl.program_id(1)

        @pl.when(c == 0)
        def _():
            dstate[...] = jnp.zeros_like(dstate)

        @pl.when(jnp.logical_and(g2 == 0, c == 0))
        def _():
            hv_ref[...] = jnp.zeros_like(hv_ref)

        dtr, bias = dtr_ref[...], bias_ref[...]
        row128 = _iota((L, LANE), 0)
        lane = _iota((L, LANE), 1)
        row_w = _iota((L, GROUP_W), 0)

        def one_group(gg):
            g = SSD_PER_STEP * g2 + gg
            gs = slice(gg * GROUP_W, (gg + 1) * GROUP_W)
            ns = slice(gg * SSD_STATE, (gg + 1) * SSD_STATE)
            acs = acsrow.at[gg]
            xs, bm, cm, ex = xs_ref[:, gs], b_ref[:, ns], c_ref[:, ns], ex_ref[gg]
            q = _ssd_chunk(xs, bm, cm, dtr, bias, alog_ref[...], ex)
            acs[...] = q["acs_r"]
            dyv = dy_ref[:, gs]
            prev = st_ref[gg]
            dst = dstate[gg]
            bm_b, cm_b = bm.astype(BF16), cm.astype(BF16)

            dxs = dl_ref[:, gs] * dyv
            d_dl = _colsum(dyv * xs)
            gmat = _dot(cm_b, prev.astype(BF16))
            dg_b = (dyv * q["f_l"]).astype(BF16)
            dacs = dyv * q["f_l"] * gmat
            dcm = _dot(dg_b, prev.astype(BF16), NT)
            dprev = _dot(cm_b, dg_b, TN)
            dcd = _colsum(dst * prev)
            dlast = dcd * q["cd_l"]
            xe = q["xdt"] * q["e_l"]
            dxe = _dot(bm_b, dst.astype(BF16))
            dbm = _dot(xe.astype(BF16), dst.astype(BF16), NT)
            dxdt = dxe * q["e_l"]
            t1 = dxe * xe
            dacs = dacs - t1
            dlast = dlast + _colsum(t1)
            dstate[gg] = dprev + q["cd_l"] * dst
            xdt_b = q["xdt"].astype(BF16)
            dcb = jnp.zeros((L, L), F32)
            dacs_head = []
            dxdt_diag = []
            for p in range(HEADS_PER_GROUP // 2):
                sl = slice(p * LANE, (p + 1) * LANE)
                xp = xdt_b[:, sl]
                dyp = dyv[:, sl]
                vals, dx_parts = [], []
                for half in range(2):
                    in_half = (lane < P) if half == 0 else (lane >= P)
                    decay = _head_decay(q, acs, g, 2 * p + half)
                    m = decay * q["cb"]
                    dyh = jnp.where(in_half, dyp, 0.0).astype(BF16)
                    dm = jnp.where(q["causal"], _dot(dyh, xp, NT), 0.0)
                    dcb = dcb + dm * decay
                    dseg = dm * m
                    rs = jnp.sum(dseg, axis=1, keepdims=True)
                    cs = jnp.broadcast_to(_colsum(dseg), (L, L)).T[:, 0:1]
                    vals.append(rs - cs)
                    dx_parts.append(_dot(m.astype(BF16), dyp.astype(BF16), TN))
                dxdt_diag.append(jnp.where(lane < P, dx_parts[0], dx_parts[1]))
                dacs_head.append(jnp.where(lane == 0, vals[0], jnp.where(lane == P, vals[1], 0.0)))
            dxdt = dxdt + jnp.concatenate(dxdt_diag, axis=1)
            dacs = dacs + jnp.concatenate(dacs_head, axis=1)
            dacs = jnp.where(row_w == L - 1, dacs + dlast, dacs)
            dcb_b = dcb.astype(BF16)
            dcm = dcm + _dot(dcb_b, bm_b)
            dbm = dbm + _dot(dcb_b, cm_b, TN)
            dacs_h = _dot3(dacs, ex, NT)
            dda = _dot3l((row128 <= lane).astype(BF16), dacs_h)
            ddt = dda * q["a_row"] + _dot3(dxdt * xs, ex, NT)
            dxs = dxs + dxdt * q["dt_l"]
            ddtr = ddt * _sigmoid(dtr + bias)
            d_alog = _colsum(dda * q["dt"]) * q["a_row"]
            d_dh = _dot3(jnp.broadcast_to(d_dl, (8, GROUP_W)), ex, NT)[0:1, :]
            dxs_ref[:, gs] = dxs
            db_ref[:, ns] = dbm
            dc_ref[:, ns] = dcm
            ddt_ref[:, ns] = ddtr
            hv_ref[0:1, :] += d_dh
            hv_ref[1:2, :] += d_alog
            hv_ref[2:3, :] += _colsum(ddtr)

        for gg in range(SSD_PER_STEP):
            one_group(gg)

    rev = lambda c: nc - 1 - c
    xs = pl.BlockSpec((L, SSD_PER_STEP * GROUP_W), lambda g, c: (rev(c), g))
    nb = W // SSD_STATE
    bm = pl.BlockSpec((L, SSD_PER_STEP * SSD_STATE), lambda g, c: (rev(c), nb // SSD_PER_STEP + g))
    cm = pl.BlockSpec((L, SSD_PER_STEP * SSD_STATE), lambda g, c: (rev(c), (nb + SSD_GROUPS) // SSD_PER_STEP + g))
    dtr = pl.BlockSpec((L, LANE), lambda g, c: (rev(c), 0))
    st = pl.BlockSpec((None, SSD_PER_STEP, SSD_STATE, GROUP_W), lambda g, c: (rev(c), g, 0, 0))
    vec = pl.BlockSpec((1, LANE), lambda g, c: (0, 0))
    exs = pl.BlockSpec((SSD_PER_STEP, LANE, GROUP_W), lambda g, c: (g, 0, 0))
    dl = pl.BlockSpec((1, SSD_PER_STEP * GROUP_W), lambda g, c: (0, g))
    grp = pl.BlockSpec((L, SSD_PER_STEP * SSD_STATE), lambda g, c: (rev(c), g))
    return _call(body, name="ssd_scan_bwd", grid=(SSD_GROUPS // SSD_PER_STEP, nc),
                 in_specs=[xs, xs, bm, cm, dtr, st, vec, vec, exs, dl],
                 out_specs=[xs, grp, grp, grp, pl.BlockSpec((8, LANE), lambda g, c: (0, 0))],
                 out_shape=[jax.ShapeDtypeStruct((T, W), F32),
                            jax.ShapeDtypeStruct((T, SSD_GROUPS * SSD_STATE), F32),
                            jax.ShapeDtypeStruct((T, SSD_GROUPS * SSD_STATE), F32),
                            jax.ShapeDtypeStruct((T, SSD_GROUPS * LANE), F32),
                            jax.ShapeDtypeStruct((8, LANE), F32)],
                 scratch=[pltpu.VMEM((SSD_PER_STEP, SSD_STATE, GROUP_W), F32),
                          pltpu.VMEM((SSD_PER_STEP, LANE, L), F32)],
                 dimension_semantics=("arbitrary", "arbitrary"))(dy, xbc, xbc, xbc, proj_dt, states, bias, alog,
                                                                 ex, d_lanes)


SB_TQ = 256
SB_SUB = 128


def _sb_logits(q, kb, scale, mask):
    z = _dot(q, kb, NT) * scale
    lb = jnp.minimum(z, 0.0) - jnp.log(1.0 + jnp.exp(-jnp.abs(z)))
    lk = lb - z
    return lb, lk if mask is None else jnp.where(mask, lk, 0.0)


def _sb_weights(lb, lk, mask, run):
    n = SB_SUB
    strict = (_iota((n, n), 0) > _iota((n, n), 1)).astype(BF16)
    ws = [None] * (lb.shape[1] // n)
    for s in reversed(range(len(ws))):
        sl = slice(s * n, (s + 1) * n)
        w = jnp.exp(lb[:, sl] + (_dot2(lk[:, sl], strict) + run))
        ws[s] = w if mask is None else jnp.where(mask[:, sl], w, 0.0)
        run = run + jnp.sum(lk[:, sl], axis=1, keepdims=True)
    return jnp.concatenate(ws, axis=1), run


def _sb_diagonal(tq):
    return _iota((tq, tq), 1) < _iota((tq, tq), 0)


SB_LOG_CUT = -110.0


def _sb_more(it, qi, run):
    return jnp.logical_and(it <= qi, jnp.max(run) > SB_LOG_CUT)


SB_PAIR = 2
SB_PW = SB_PAIR * SB_DIM
SB_FWD_HEADS = 4


def _sb_head_slices():
    return [slice(j * SB_DIM, (j + 1) * SB_DIM) for j in range(SB_PAIR)]


def _sb_fwd(qh, kh, proj, v_cb0, side=None):
    T, W = qh.shape
    tq = min(SB_TQ, T)
    scale = SB_DIM ** -0.5
    n_heads = SB_FWD_HEADS
    pw = n_heads * SB_DIM
    heads = [slice(j * SB_DIM, (j + 1) * SB_DIM) for j in range(n_heads)]

    def body(q_ref, k_ref, v_ref, o_ref, ob_ref):
        qi = pl.program_id(1)
        qs = [q_ref[:, hs] for hs in heads]

        def block(it, carry, mask=None):
            kstart = pl.multiple_of((qi - it) * tq, tq)
            out = []
            for j, hs in enumerate(heads):
                run, acc = carry[2 * j], carry[2 * j + 1]
                lb, lk = _sb_logits(qs[j], k_ref[pl.ds(kstart, tq), hs], scale, mask)
                w, run = _sb_weights(lb, lk, mask, run)
                acc = acc + _dot2(w, v_ref[pl.ds(kstart, tq), hs].astype(BF16))
                out += [run, acc]
            return tuple(out)

        first = block(0, (jnp.zeros((tq, 1), F32), jnp.zeros((tq, SB_DIM), F32)) * n_heads, _sb_diagonal(tq))
        more = lambda c: _sb_more(c[0], qi, functools.reduce(jnp.maximum, c[1::2]))
        out = lax.while_loop(more, lambda c: (c[0] + 1, *block(c[0], c[1:])), (jnp.int32(1), *first))
        for j, hs in enumerate(heads):
            o_ref[:, hs] = out[2 + 2 * j]
            ob_ref[:, hs] = out[2 + 2 * j].astype(BF16)

    tile = pl.BlockSpec((tq, pw), lambda h, i: (i, h))
    return _call(body, name="sb_attn", grid=(W // pw, T // tq),
                 in_specs=[tile, pl.BlockSpec((T, pw), lambda h, i: (0, h)),
                           pl.BlockSpec((T, pw), lambda h, i: (0, v_cb0 // n_heads + h))],
                 out_specs=[tile, tile],
                 out_shape=[jax.ShapeDtypeStruct((T, W), F32), jax.ShapeDtypeStruct((T, W), BF16)], side=side,
                 dimension_semantics=("arbitrary", "arbitrary"))(qh, kh, proj)


def _sb_bwd(qh, kh, proj, v_cb0, o, dmerged, do_cb0, side=None):
    T, W = qh.shape
    tq = min(SB_TQ, T)
    n = SB_SUB
    scale = SB_DIM ** -0.5
    heads = _sb_head_slices()

    def body(q_ref, k_ref, v_ref, o_ref, do_ref, dq_ref, dk_ref, dv_ref):
        qi = pl.program_id(1)

        @pl.when(qi == 0)
        def _():
            dk_ref[...] = jnp.zeros_like(dk_ref)
            dv_ref[...] = jnp.zeros_like(dv_ref)

        qs = [q_ref[:, hs] for hs in heads]
        dos = [do_ref[:, hs].astype(BF16) for hs in heads]
        etots = [jnp.sum(d.astype(F32) * o_ref[:, hs], axis=1, keepdims=True) for d, hs in zip(dos, heads)]
        incl = (_iota((n, n), 0) >= _iota((n, n), 1)).astype(BF16)

        def block(it, carry, mask=None):
            kstart = pl.multiple_of((qi - it) * tq, tq)
            out = []
            for j, hs in enumerate(heads):
                run, erun, dq = carry[3 * j:3 * j + 3]
                kb = k_ref[pl.ds(kstart, tq), hs]
                vb = v_ref[pl.ds(kstart, tq), hs].astype(BF16)
                lb, lk = _sb_logits(qs[j], kb, scale, mask)
                w, run = _sb_weights(lb, lk, mask, run)
                e = _dot(dos[j], vb, NT) * w
                beta = jnp.exp(lb)
                dzs = [None] * (tq // n)
                for s in reversed(range(tq // n)):
                    sl = slice(s * n, (s + 1) * n)
                    before = etots[j] - erun - _dot3(e[:, sl], incl)
                    dz = e[:, sl] * (1.0 - beta[:, sl]) - before * beta[:, sl]
                    dzs[s] = dz if mask is None else jnp.where(mask[:, sl], dz, 0.0)
                    erun = erun + jnp.sum(e[:, sl], axis=1, keepdims=True)
                dz = (jnp.concatenate(dzs, axis=1) * scale).astype(BF16)
                dq = dq + _dot(dz, kb)
                dk_ref[pl.ds(kstart, tq), hs] += _dot(dz, qs[j], TN)
                dv_ref[pl.ds(kstart, tq), hs] += _dot(w.astype(BF16), dos[j], TN)
                out += [run, erun, dq]
            return tuple(out)

        zero = jnp.zeros((tq, 1), F32)
        first = block(0, (zero, zero, jnp.zeros((tq, SB_DIM), F32)) * SB_PAIR, _sb_diagonal(tq))
        more = lambda c: _sb_more(c[0], qi, functools.reduce(jnp.maximum, c[1::3]))
        out = lax.while_loop(more, lambda c: (c[0] + 1, *block(c[0], c[1:])), (jnp.int32(1), *first))
        for j, hs in enumerate(heads):
            dq_ref[:, hs] = out[3 + 3 * j]

    tile = pl.BlockSpec((tq, SB_PW), lambda h, i: (i, h))
    full = pl.BlockSpec((T, SB_PW), lambda h, i: (0, h))
    shp = jax.ShapeDtypeStruct((T, W), F32)
    return _call(body, name="sb_attn_bwd", grid=(W // SB_PW, T // tq),
                 in_specs=[tile, full, pl.BlockSpec((T, SB_PW), lambda h, i: (0, v_cb0 // SB_PAIR + h)), tile,
                           pl.BlockSpec((tq, SB_PW), lambda h, i: (i, do_cb0 // SB_PAIR + h))],
                 out_specs=[tile, full, full], out_shape=[shp, shp, shp], side=side,
                 dimension_semantics=("arbitrary", "arbitrary"))(qh, kh, proj, o, dmerged)


MM_TK = 2048
MM_TK_TOKENS = 4096


def _mlp_fwd(x, g, w_up, w_down, layer, sides=(None, None)):
    T, D = x.shape
    fs = w_up.shape[2]
    F = N_DEV * fs
    h = _rmsnorm(x, g, name=f"mlp{layer}_norm", dtype=BF16)

    def relu_sq(acc):
        u = jnp.maximum(acc, 0.0)
        return u, u * u

    up = _matmul(h, w_up, mode="nn", name=f"mlp{layer}_up", tm=1024, tn=fs, tk=D, epilogue=relu_sq,
                 out_dtypes=(BF16, BF16), mnk=(T, F, D), side=sides[0],
                 b_spec=pl.BlockSpec((None, D, fs), lambda i, j, k: (j, 0, 0)))
    (u, s), res_up = up if sides[0] is not None else (up, None)
    y = _matmul(s, w_down, mode="nn", name=f"mlp{layer}_down", tm=1024, tn=1024, tk=2 * fs, extras=(x,),
                epilogue=lambda acc, r: (acc + r,), mnk=(T, D, F), side=sides[1],
                b_spec=pl.BlockSpec((2, fs, 1024), lambda i, j, k: (k, 0, j)))
    y, res_down = y if sides[1] is not None else (y, None)
    return y, (h, u, s), (res_up, res_down)


def _mlp_bwd(dy, dy_b, x, g, w_up, w_down, saved, layer, pending=None):
    T, D = x.shape
    fs = w_up.shape[2]
    F = N_DEV * fs
    h, u, s = saved
    dw_down = _matmul(s, dy_b, mode="tn", name=f"mlp{layer}_dwdown", tm=1024, tn=1024, tk=MM_TK_TOKENS, side=pending)
    dw_down, pending_res = dw_down if pending is not None else (dw_down, None)
    dw_down = dw_down.reshape(N_DEV, -1, D)
    da, (r1_down,) = _matmul(dy_b, w_down, mode="nt", name=f"mlp{layer}_da", tm=1024, tn=fs, tk=D, extras=(u,),
                             epilogue=lambda acc, uv: (acc * (2.0 * uv.astype(F32)),), out_dtypes=(BF16,),
                             mnk=(T, F, D), side=_cores_job([dw_down]),
                             b_spec=pl.BlockSpec((None, fs, D), lambda i, j, k: (j, 0, 0)))
    pb_down, pm_down = _pair_sum(dw_down, r1_down, f"grad_pair_sum_w_down{layer}")
    dw_up = _matmul(h, da, mode="tn", name=f"mlp{layer}_dwup", tm=1024, tn=fs, tk=MM_TK_TOKENS, out_dims=(N_DEV, D, fs),
                    out_spec=pl.BlockSpec((None, 1024, fs), lambda i, j, k: (j, i, 0)))
    dh, (r2_down, r1_up) = _matmul(da, w_up, mode="nt", name=f"mlp{layer}_dh", tm=1024, tn=1024, tk=2 * fs,
                                   mnk=(T, D, F), side=_join(_chips_job([pb_down]), _cores_job([dw_up])),
                                   b_spec=pl.BlockSpec((2, 1024, fs), lambda i, j, k: (k, j, 0)))
    up_sums = _pair_sum(dw_up, r1_up, f"grad_pair_sum_w_up{layer}")
    dx, dx_b, dg = _rmsnorm_bwd(dh, x, g, dy, name=f"mlp{layer}_norm_bwd")
    return dx, dx_b, dg, (pm_down, r2_down), up_sums, pending_res


def _local_step(x, target, p, late):
    T, D = x.shape
    W = SSD_HEADS * SSD_HEAD_DIM
    g = {}
    add = lambda acc, r: (acc + r,)

    h0 = _rmsnorm(x, p["hyb_norm"], name="hyb_norm", dtype=BF16)
    proj, (w_out,) = _matmul(h0, p["w_main_t"], mode="nt", name="hyb_proj", tm=1024, tn=1024, tk=MM_TK,
                             side=_gather_job(late[2]))
    p = dict(p, w_out=w_out.reshape(-1, D))
    proj_dt = _matmul(h0, p["w_dt_t"], mode="nt", name="hyb_proj_dt", tm=1024, tn=128, tk=MM_TK)
    ex, d_lanes, alog, bias = _ssd_consts(p["ssd_d"], p["ssd_a_log"], p["ssd_dt_bias"])
    xbc = _conv_silu_fwd(proj, 4 * W, p["conv_w"].shape[1], p["conv_w"], p["conv_b"])
    (ypre, states), (w_up0,) = _ssd_fwd(xbc, proj_dt, bias, alog, ex, d_lanes, side=_gather_job(late[0][:1]))
    y_ssd = _gate_norm_fwd(ypre, proj, p["out_norm"])
    qh, kh = _qk_norm_fwd(proj, p["q_norm"], p["k_norm"], W)
    (y_sb, y_sb_b), (w_down0, pool_blocks) = _sb_fwd(qh, kh, proj, 3 * W // SB_DIM, side=_gather_job(late[0][1:]))
    pool_w = _whole(pool_blocks, "pool_w")[0]
    x1 = _matmul(y_ssd, p["w_out"], mode="nn", name="hyb_out_a", tm=1024, tn=1024, tk=MM_TK, extras=(x,), epilogue=add,
                 mnk=(T, D, W))
    x1 = _matmul(y_sb_b, p["w_out"], mode="nn", name="hyb_out_b", tm=1024, tn=1024, tk=W, extras=(x1,), epilogue=add,
                 mnk=(T, D, W), b_spec=pl.BlockSpec((W, 1024), lambda i, j, k: (1, j)))
    x2, mlp0, ((w_up1,), (w_down1,)) = _mlp_fwd(x1, p["mlp_norm"][0:1], w_up0, w_down0, 0,
                                                sides=(_gather_job(late[1][:1]), _gather_job(late[1][1:])))
    hp = _rmsnorm(x2, p["pool_norm"], name="pool_norm", dtype=F32)
    dpool = _pool_diff_fwd(hp)
    ypool, x3 = _pool_mm_fwd(dpool, pool_w, p["pool_b"], p["pool_scale"], x2)
    x4, mlp1, _ = _mlp_fwd(x3, p["mlp_norm"][1:2], w_up1, w_down1, 1)

    reduced = {}
    dy, dy_b, sq = _loss_grad(x4, target)
    dx3, dx3_b, dgm1, reduced["w_down1"], up1_sums, _ = _mlp_bwd(dy, dy_b, x3, p["mlp_norm"][1:2], w_up1, w_down1,
                                                                 mlp1, 1)
    dpre, dpool_scale, dpool_b = _pool_scale_bwd(dx3, ypool, p["pool_scale"])
    gw = {"pool_w": _pool_mm_dw(dpool, dpre)[None], "pool_b": dpool_b, "pool_scale": dpool_scale}
    dhp = _pool_diff_bwd(_pool_mm_dx(dpre, pool_w))
    dx2, dx2_b, gw["pool_norm"] = _rmsnorm_bwd(dhp, x2, p["pool_norm"], dx3, name="pool_norm_bwd")
    dx1, dx1_b, dgm0, reduced["w_down0"], up0_sums, (r2_up1,) = _mlp_bwd(
        dx2, dx2_b, x1, p["mlp_norm"][0:1], w_up0, w_down0, mlp0, 0, pending=_chips_job([up1_sums[0]]))
    reduced["w_up1"] = (up1_sums[1], r2_up1)
    g["mlp_norm"] = jnp.concatenate([dgm0, dgm1], axis=0)
    dw_out = jnp.concatenate([
        _matmul(y_ssd, dx1_b, mode="tn", name="hyb_dwout_a", tm=1024, tn=1024, tk=MM_TK_TOKENS),
        _matmul(y_sb_b, dx1_b, mode="tn", name="hyb_dwout_b", tm=1024, tn=1024, tk=MM_TK_TOKENS)], axis=0)
    dw_out = dw_out.reshape(N_DEV, -1, D)
    dpool_small = _shard_rows([_to_blocks(gw[n], n) for n in POOL_SHARDED], PACK_COLS)
    dmerged, (r2_up0, r1_out, r1_pool) = _matmul(
        dx1_b, p["w_out"], mode="nt", name="hyb_dmerged", tm=1024, tn=1024, tk=MM_TK,
        side=_join(_chips_job([up0_sums[0]]), _cores_job([dw_out, dpool_small])))
    reduced["w_up0"] = (up0_sums[1], r2_up0)
    out_sums = _pair_sum(dw_out, r1_out, "grad_pair_sum_w_out")
    pool_sums = _pair_sum(dpool_small, r1_pool, "grad_pair_sum_pool")
    (dqh, dkh, dv), (r2_out, r2_pool) = _sb_bwd(qh, kh, proj, 3 * W // SB_DIM, y_sb, dmerged, W // SB_DIM,
                                                side=_chips_job([out_sums[0], pool_sums[0]]))
    reduced["w_out"], reduced["pool"] = (out_sums[1], r2_out), (pool_sums[1], r2_pool)
    dq, dk, g["sb_q_norm"], g["sb_k_norm"] = _qk_norm_bwd(dqh, dkh, proj, p["q_norm"], p["k_norm"], W)
    dypre, dz, g["ssd_out_norm"] = _gate_norm_bwd(dmerged, ypre, proj, p["out_norm"])
    dxs, dbm, dcm, ddt4, hv = _ssd_bwd(dypre, xbc, proj_dt, states, bias, alog, ex, d_lanes)
    g["ssd_d"], g["ssd_a_log"], g["ssd_dt_bias"] = (hv[i:i + 1, :SSD_HEADS] for i in range(3))
    ddt = ddt4.reshape(T, SSD_GROUPS, LANE).sum(axis=1).astype(BF16)
    dxbc, dconv_w, g["ssd_conv_b"] = _conv_silu_bwd(jnp.concatenate([dxs, dbm, dcm], axis=1), proj, 4 * W,
                                                    p["conv_w"].shape[1], p["conv_w"], p["conv_b"])
    dproj = jnp.concatenate([dz, dq, dk, dv.astype(BF16), dxbc], axis=1)
    gm = _matmul(dproj, h0, mode="tn", name="hyb_dwin", tm=1024, tn=1024, tk=MM_TK_TOKENS)
    g_dt = _matmul(ddt, h0, mode="tn", name="hyb_dwdt", tm=128, tn=1024, tk=MM_TK)
    g_in_t = jnp.concatenate([gm[:W], gm[4 * W:], g_dt[:SSD_HEADS], gm[W:4 * W]], axis=0)
    last = [g_in_t.reshape(N_DEV, -1, D), _shard_rows([_to_blocks(dconv_w[:4][None], "ssd_conv_w")], LANE)]
    dh0, r1_last = _matmul(ddt, p["w_dt_t"], mode="nn", name="hyb_dh_dt", tm=1024, tn=1024, tk=128,
                           side=_cores_job(last))
    sums = [_pair_sum(a, r, f"grad_pair_sum_{t}") for a, r, t in zip(last, r1_last, ("w_in", "conv"))]
    dh0, r2_last = _matmul(dproj, p["w_main_t"], mode="nn", name="hyb_dh", tm=1024, tn=1024, tk=1024, extras=(dh0,),
                           epilogue=add, side=_chips_job([s[0] for s in sums]))
    reduced.update({t: (s[1], r2) for t, s, r2 in zip(("w_in", "conv"), sums, r2_last)})
    grad_x, _, g["hyb_norm"] = _rmsnorm_bwd(dh0, x, p["hyb_norm"], dx1, name="hyb_norm_bwd")
    return sq, grad_x, g, reduced


def _position():
    return lax.axis_index("x"), lax.axis_index("y"), lax.axis_index("c")


def _run_job(job, name):
    k_in, k_out = len(job.ins), len(job.outs)

    def body(*refs):
        parts = refs[:k_in], refs[k_in:k_in + k_out], refs[k_in + k_out:]
        job.start(*parts)
        if job.mid is not None:
            job.mid(*parts)
        job.finish(*parts)

    return pl.pallas_call(body, name=name, out_shape=list(job.outs), in_specs=[ANY] * k_in, out_specs=[ANY] * k_out,
                          scratch_shapes=list(job.sems))(*job.ins)


def _join(*jobs):
    def parts(ins, outs, sems):
        i = o = s = 0
        for j in jobs:
            yield j, (ins[i:i + len(j.ins)], outs[o:o + len(j.outs)], sems[s:s + len(j.sems)])
            i, o, s = i + len(j.ins), o + len(j.outs), s + len(j.sems)

    def start(*refs):
        for j, p in parts(*refs):
            j.start(*p)

    def mid(*refs):
        for j, p in parts(*refs):
            if j.mid is not None:
                j.mid(*p)

    def finish(*refs):
        for j, p in parts(*refs):
            j.finish(*p)

    return _Job(sum((j.ins for j in jobs), ()), sum((j.outs for j in jobs), ()), sum((j.sems for j in jobs), ()),
                start, mid if any(j.mid is not None for j in jobs) else None, finish)


def _gather_job(vs):
    n = len(vs)

    def plan(v_refs, out_refs, sems):
        send_sems, recv_sems, local_sems = sems
        x, y, c = _position()
        me, sibling = (x, y, c), (x, y, 1 - c)
        chips = [(1 - x, y), (x, 1 - y), (1 - x, 1 - y)]

        def rows(a, px, py, pc):
            return out_refs[a].at[4 * px + 2 * py + pc]

        def copy(a, k, block, to, src=None):
            return pltpu.make_async_remote_copy(
                src_ref=rows(a, *block) if src is None else src, dst_ref=rows(a, *block),
                send_sem=send_sems.at[7 * a + k], recv_sem=recv_sems.at[7 * a + k], device_id=to,
                device_id_type=MESH)

        mine = [pltpu.make_async_copy(v_refs[a], rows(a, *me), local_sems.at[a]) for a in range(n)]
        first = [copy(a, 0, me, sibling, src=v_refs[a]) for a in range(n)]
        first += [copy(a, 1 + j, me, (*chip, c), src=v_refs[a]) for a in range(n) for j, chip in enumerate(chips)]
        landed = [copy(a, 1 + j, (*chip, c), me) for j, chip in enumerate(chips) for a in range(n)]
        passed = [copy(a, 4 + j, (*chip, c), sibling) for j, chip in enumerate(chips) for a in range(n)]
        from_sibling = [copy(a, 0, sibling, me) for a in range(n)]
        from_sibling += [copy(a, 4 + j, (*chip, 1 - c), me) for a in range(n) for j, chip in enumerate(chips)]
        return mine, first, landed, passed, from_sibling

    def start(*refs):
        mine, first, _, _, _ = plan(*refs)
        for cp in mine + first:
            cp.start()

    def mid(*refs):
        _, _, landed, passed, _ = plan(*refs)
        for arrived, onward in zip(landed, passed):
            arrived.wait_recv()
            onward.start()

    def finish(*refs):
        mine, first, _, passed, from_sibling = plan(*refs)
        for cp in from_sibling:
            cp.wait_recv()
        for cp in first + passed:
            cp.wait_send()
        for cp in mine:
            cp.wait()

    return _Job(tuple(vs), tuple(jax.ShapeDtypeStruct((N_DEV,) + v.shape, v.dtype) for v in vs),
                (pltpu.SemaphoreType.DMA((7 * n,)), pltpu.SemaphoreType.DMA((7 * n,)), pltpu.SemaphoreType.DMA((n,))),
                start, mid, finish)


def _cores_job(gs):
    n = len(gs)

    def plan(g_refs, r_refs, sems):
        send_sems, recv_sems = sems
        x, y, c = _position()
        return [pltpu.make_async_remote_copy(
            src_ref=g_refs[a].at[2 * k + (1 - c)], dst_ref=r_refs[a].at[k], send_sem=send_sems.at[4 * a + k],
            recv_sem=recv_sems.at[4 * a + k], device_id=(x, y, 1 - c), device_id_type=MESH)
            for a in range(n) for k in range(4)]

    def start(*refs):
        for cp in plan(*refs):
            cp.start()

    def finish(*refs):
        copies = plan(*refs)
        for cp in copies:
            cp.wait_recv()
        for cp in copies:
            cp.wait_send()

    return _Job(tuple(gs), tuple(jax.ShapeDtypeStruct((4,) + g.shape[1:], g.dtype) for g in gs),
                (pltpu.SemaphoreType.DMA((4 * n,)), pltpu.SemaphoreType.DMA((4 * n,))), start, None, finish)


TILE_BYTES = 2 * 1024 * 1024


def _col_tile(R, C):
    tc = C
    while R * tc * 4 > TILE_BYTES and tc % (2 * LANE) == 0:
        tc //= 2
    return tc


def _pair_sum(gr, r1, name):
    _, R, C = gr.shape
    tc = _col_tile(R, C)
    x, y, c = _position()
    pos = jnp.stack([c, 2 * x + y]).astype(jnp.int32)

    def body(pos_ref, g_ref, r_ref, pb_ref, pm_ref):
        s = g_ref[...] + r_ref[...]
        pb_ref[...] = s.astype(BF16)

        @pl.when(pl.program_id(1) == pos_ref[1])
        def _():
            pm_ref[...] = s

    return _call(body, name=name, grid=(C // tc, 4), prefetch=1,
                 in_specs=[pl.BlockSpec((None, R, tc), lambda j, k, pos: (2 * k + pos[0], 0, j)),
                           pl.BlockSpec((None, R, tc), lambda j, k, pos: (k, 0, j))],
                 out_specs=[pl.BlockSpec((None, R, tc), lambda j, k, pos: (k, 0, j)),
                            pl.BlockSpec((R, tc), lambda j, k, pos: (0, j))],
                 out_shape=[jax.ShapeDtypeStruct((4, R, C), BF16), jax.ShapeDtypeStruct((R, C), F32)],
                 dimension_semantics=("arbitrary", "arbitrary"))(pos, gr, r1)


def _chips_job(pbs):
    n = len(pbs)

    def plan(p_refs, r_refs, sems):
        send_sems, recv_sems = sems
        x, y, c = _position()
        chips = [(1 - x, y), (x, 1 - y), (1 - x, 1 - y)]
        mine = 2 * x + y

        def copy(a, j, src_row, dst_row):
            cx, cy = chips[j]
            return pltpu.make_async_remote_copy(
                src_ref=p_refs[a].at[src_row], dst_ref=r_refs[a].at[dst_row], send_sem=send_sems.at[3 * a + j],
                recv_sem=recv_sems.at[3 * a + j], device_id=(cx, cy, c), device_id_type=MESH)

        sends = [copy(a, j, 2 * cx + cy, mine) for a in range(n) for j, (cx, cy) in enumerate(chips)]
        arrivals = [copy(a, j, mine, 2 * cx + cy) for a in range(n) for j, (cx, cy) in enumerate(chips)]
        return sends, arrivals

    def start(*refs):
        for cp in plan(*refs)[0]:
            cp.start()

    def finish(*refs):
        sends, arrivals = plan(*refs)
        for cp in arrivals:
            cp.wait_recv()
        for cp in sends:
            cp.wait_send()

    return _Job(tuple(pbs), tuple(jax.ShapeDtypeStruct(p.shape, p.dtype) for p in pbs),
                (pltpu.SemaphoreType.DMA((3 * n,)), pltpu.SemaphoreType.DMA((3 * n,))), start, None, finish)


def _adamw(w, grad, m, v):
    m = ADAM_B1 * m + (1.0 - ADAM_B1) * grad
    v = ADAM_B2 * v + (1.0 - ADAM_B2) * (grad * grad)
    m_hat = m / (1.0 - ADAM_B1 ** ADAM_STEP)
    v_hat = v / (1.0 - ADAM_B2 ** ADAM_STEP)
    delta = -ADAM_LR * (m_hat / (jnp.sqrt(v_hat) + ADAM_EPS) + ADAM_WD * w)
    return delta, m, v


def _other_chips():
    x, y, _ = _position()
    mine = 2 * x + y
    return jnp.stack([jnp.where(mine <= j, j + 1, j) for j in range(3)]).astype(jnp.int32)


def _adamw_sharded(pm, r2, w, m, v, name):
    R, C = pm.shape
    tc = _col_tile(R, C)
    update = w is not None

    def body(oth_ref, pm_ref, a_ref, b_ref, c_ref, *refs):
        grad = ((pm_ref[...] + a_ref[...].astype(F32)) + b_ref[...].astype(F32)) + c_ref[...].astype(F32)
        if update:
            w_ref, m_ref, v_ref, g_ref, d_ref, nm_ref, nv_ref = refs
            d, nm, nv = _adamw(w_ref[...], grad, m_ref[...], v_ref[...])
            g_ref[...], d_ref[...], nm_ref[...], nv_ref[...] = grad, d, nm, nv
        else:
            refs[0][...] = grad

    tile = pl.BlockSpec((R, tc), lambda j, oth: (0, j))
    other = [pl.BlockSpec((None, R, tc), functools.partial(lambda j, oth, q: (oth[q], 0, j), q=q)) for q in range(3)]
    shp = jax.ShapeDtypeStruct((R, C), F32)
    n_out = 4 if update else 1
    res = _call(body, name=name, grid=(C // tc,), prefetch=1,
                in_specs=[tile] + other + ([tile, tile, tile] if update else []), out_specs=[tile] * n_out,
                out_shape=[shp] * n_out, dimension_semantics=("arbitrary",))(
                    _other_chips(), pm, r2, r2, r2, *((w, m, v) if update else ()))
    return res if update else res[0]


def _adamw_plain(grad, w, m, v, name):
    R, C = w.shape
    tr = 256

    def body(g_ref, w_ref, m_ref, v_ref, d_ref, nm_ref, nv_ref):
        d_ref[...], nm_ref[...], nv_ref[...] = _adamw(w_ref[...], g_ref[...], m_ref[...], v_ref[...])

    tile = pl.BlockSpec((tr, C), lambda i: (i, 0))
    shp = jax.ShapeDtypeStruct((R, C), F32)
    return _call(body, name=name, grid=(R // tr,), in_specs=[tile] * 4, out_specs=[tile] * 3, out_shape=[shp] * 3,
                 dimension_semantics=("arbitrary",))(grad, w, m, v)


def _adamw_replicated(parts, w, m, v):
    _, R, C = parts.shape

    def body(p_ref, w_ref, m_ref, v_ref, g_ref, d_ref, nm_ref, nv_ref):
        grad = p_ref[0]
        for j in range(1, N_DEV):
            grad = grad + p_ref[j]
        d, nm, nv = _adamw(w_ref[...], grad, m_ref[...], v_ref[...])
        g_ref[...], d_ref[...], nm_ref[...], nv_ref[...] = grad, d, nm, nv

    tile = pl.BlockSpec((R, C), lambda i: (0, 0))
    shp = jax.ShapeDtypeStruct((R, C), F32)
    return _call(body, name="adamw_replicated", grid=(1,),
                 in_specs=[pl.BlockSpec((N_DEV, R, C), lambda i: (0, 0, 0)), tile, tile, tile],
                 out_specs=[tile] * 4, out_shape=[shp] * 4)(parts, w, m, v)


POOL_SHARDED = ("pool_w", "pool_norm", "pool_b", "pool_scale")
REPLICATED = ("hyb_norm", "ssd_conv_b", "ssd_dt_bias", "ssd_a_log", "ssd_d", "ssd_out_norm", "sb_q_norm",
              "sb_k_norm", "mlp_norm")
PACK_COLS = 1024


def _pack(arrays, cols, row_multiple, dtype):
    flat = jnp.concatenate([a.reshape(-1).astype(dtype) for a in arrays])
    n = flat.shape[0]
    total = -(-n // (cols * row_multiple)) * cols * row_multiple
    return jnp.pad(flat, (0, total - n)).reshape(total // cols, cols)


def _shard_rows(blocks, cols):
    flat = jnp.concatenate(blocks, axis=1)
    rows = -(-flat.shape[1] // (8 * cols)) * 8
    return jnp.pad(flat, ((0, 0), (0, rows * cols - flat.shape[1]))).reshape(N_DEV, rows, cols)


def _unpack(packed, shapes):
    flat = packed.reshape(packed.shape[:-2] + (-1,))
    out, off = [], 0
    for s in shapes:
        n = math.prod(s)
        out.append(flat[..., off:off + n].reshape(flat.shape[:-1] + tuple(s)))
        off += n
    return out


def _shard_axis(name):
    return {"hyb_w_in": 2, "hyb_w_out": 1, "mlp_w_up": 2, "mlp_w_down": 1, "pool_w": 2, "ssd_conv_w": 2,
            "pool_norm": 1, "pool_b": 1, "pool_scale": 1}[name]


def _whole(blocks, name):
    ax = _shard_axis(name)
    moved = jnp.moveaxis(blocks, 0, ax)
    s = moved.shape
    return moved.reshape(s[:ax] + (s[ax] * s[ax + 1],) + s[ax + 2:])


def _to_blocks(whole, name):
    ax = _shard_axis(name)
    s = whole.shape
    split = whole.reshape(s[:ax] + (N_DEV, s[ax] // N_DEV) + s[ax + 1:])
    return jnp.moveaxis(split, ax, 0).reshape(N_DEV, -1)


def kernel(x, hyb_norm, hyb_w_in, ssd_conv_w, ssd_conv_b, ssd_dt_bias, ssd_a_log, ssd_d, ssd_out_norm, sb_q_norm, sb_k_norm, hyb_w_out, pool_norm, pool_w, pool_b, pool_scale, mlp_norm, mlp_w_up, mlp_w_down, loss_target, m_hyb_norm, m_hyb_w_in, m_ssd_conv_w, m_ssd_conv_b, m_ssd_dt_bias, m_ssd_a_log, m_ssd_d, m_ssd_out_norm, m_sb_q_norm, m_sb_k_norm, m_hyb_w_out, m_pool_norm, m_pool_w, m_pool_b, m_pool_scale, m_mlp_norm, m_mlp_w_up, m_mlp_w_down, v_hyb_norm, v_hyb_w_in, v_ssd_conv_w, v_ssd_conv_b, v_ssd_dt_bias, v_ssd_a_log, v_ssd_d, v_ssd_out_norm, v_sb_q_norm, v_sb_k_norm, v_hyb_w_out, v_pool_norm, v_pool_w, v_pool_b, v_pool_scale, v_mlp_norm, v_mlp_w_up, v_mlp_w_down):
    args = dict(locals())
    names = ("hyb_norm", "hyb_w_in", "ssd_conv_w", "ssd_conv_b", "ssd_dt_bias", "ssd_a_log", "ssd_d", "ssd_out_norm",
             "sb_q_norm", "sb_k_norm", "hyb_w_out", "pool_norm", "pool_w", "pool_b", "pool_scale", "mlp_norm",
             "mlp_w_up", "mlp_w_down")
    wt = {n: args[n] for n in names}
    T, D = x.shape[1], x.shape[2]
    W = SSD_HEADS * SSD_HEAD_DIM

    conv_dim = ssd_conv_b.shape[-1]
    c1, c2 = W + conv_dim, W + conv_dim + SSD_HEADS
    vec_names = ("ssd_conv_w", "pool_norm", "pool_b", "pool_scale")

    gathered = _run_job(_gather_job([hyb_w_in[0].T.astype(BF16), _pack([wt[n] for n in vec_names], LANE, 8, F32)]),
                        "gather_hybrid_weights")
    in_t = gathered[0].reshape(-1, D)
    vec = {n: _whole(b, n) for n, b in zip(vec_names, _unpack(gathered[1], [wt[n].shape for n in vec_names]))}
    p = {
        "w_main_t": jnp.concatenate([in_t[:W], in_t[c2:], in_t[W:c1]], axis=0),
        "w_dt_t": jnp.pad(in_t[c1:c2], ((0, LANE - SSD_HEADS), (0, 0))),
        "conv_w": vec["ssd_conv_w"][0], "conv_b": ssd_conv_b,
        "pool_norm": vec["pool_norm"], "pool_b": vec["pool_b"], "pool_scale": vec["pool_scale"],
        "hyb_norm": hyb_norm, "mlp_norm": mlp_norm, "out_norm": ssd_out_norm, "q_norm": sb_q_norm,
        "k_norm": sb_k_norm, "ssd_d": ssd_d, "ssd_a_log": ssd_a_log, "ssd_dt_bias": ssd_dt_bias,
    }

    up, down = mlp_w_up.astype(BF16), mlp_w_down.astype(BF16)
    sq, grad_x, g, reduced = _local_step(x[0], loss_target[0], p,
                                         ([up[0], down[0], pool_w.astype(BF16)], [up[1], down[1]],
                                          [hyb_w_out[0].astype(BF16)]))
    loss = lax.psum(0.5 * jnp.sum(sq) / D, ("x", "y", "c"))

    res = {}
    grad_in = _adamw_sharded(*reduced["w_in"], None, None, None, "grad_sum_w_in").T
    res["hyb_w_in"] = [a[None] for a in (grad_in, *_adamw_plain(grad_in, hyb_w_in[0], m_hyb_w_in[0], v_hyb_w_in[0],
                                                                "adamw_w_in"))]
    res["hyb_w_out"] = [a[None] for a in _adamw_sharded(*reduced["w_out"], hyb_w_out[0], m_hyb_w_out[0],
                                                        v_hyb_w_out[0], "adamw_w_out")]
    for t, n in (("w_up", "mlp_w_up"), ("w_down", "mlp_w_down")):
        layers = [_adamw_sharded(*reduced[f"{t}{l}"], args[n][l], args["m_" + n][l], args["v_" + n][l],
                                 f"adamw_{n}{l}") for l in range(2)]
        res[n] = [jnp.stack([layers[0][k], layers[1][k]]) for k in range(4)]
    for t, group, cols in (("pool", POOL_SHARDED, PACK_COLS), ("conv", ("ssd_conv_w",), LANE)):
        packed = [_pack([args[pre + n] for n in group], cols, 8, F32) for pre in ("", "m_", "v_")]
        small = [_unpack(o, [wt[n].shape for n in group]) for o in _adamw_sharded(*reduced[t], *packed, f"adamw_{t}")]
        for i, n in enumerate(group):
            res[n] = [small[k][i] for k in range(4)]

    parts = _run_job(_gather_job([_pack([g[n] for n in REPLICATED], LANE, 8, F32)]), "gather_vector_grads")[0]
    packed = [_pack([args[pre + n] for n in REPLICATED], LANE, 8, F32) for pre in ("", "m_", "v_")]
    shapes = [wt[n].shape for n in REPLICATED]
    repl = [_unpack(o, shapes) for o in _adamw_replicated(parts, *packed)]
    for i, n in enumerate(REPLICATED):
        res[n] = [repl[k][i] for k in range(4)]

    outs = [res[n][k] for k in range(4) for n in names]
    return (loss, grad_x[None], *outs)
```
